```python
import jax, jax.numpy as jnp
from jax import lax
import numpy as np

D_MODEL = 1024
BATCH = 8
SEQ = 2048
DEPTH = 1
DEC_BATCH = 128
DEC_SEQ = 4
PAST_LEN = 16384
PAGE_SIZE = 128

N_Q_HEADS = 8
N_KV_HEADS = 2
HEAD_DIM = 64
Q_PER_KV = N_Q_HEADS // N_KV_HEADS
D_ATTN = N_Q_HEADS * HEAD_DIM
WINDOW = 128
ATTN_BLOCK = WINDOW
CHUNK = 128
D_SGU = D_MODEL // 2
N_SGU_GROUPS = 4
SGU_GROUP_DIM = D_SGU // N_SGU_GROUPS
PEER_HEADS = 8
N_SUBKEYS = 128
N_EXPERTS = N_SUBKEYS * N_SUBKEYS
PEER_TOPK = 16
PEER_KEY_DIM = 256
PEER_HALF = PEER_KEY_DIM // 2
PEER_BLOCK = 128
DN_ALPHA = (2.0 * DEPTH) ** 0.25
DN_BETA = (8.0 * DEPTH) ** -0.25
LN_EPS = 1e-5
NEG_INF = -1e30
D_KV = N_KV_HEADS * HEAD_DIM
IN_SPLITS = (D_ATTN, D_ATTN + D_KV, D_ATTN + 2 * D_KV, D_ATTN + 2 * D_KV + D_SGU,
             D_ATTN + 2 * D_KV + 2 * D_SGU, D_ATTN + 2 * D_KV + 2 * D_SGU + D_MODEL)
D_IN = D_ATTN + 2 * D_KV + 2 * D_SGU + 2 * D_MODEL

kernel_name = "hybrid_gmlp_swa_peer_step"


def layer_norm(x, g, b):
    xf = x.astype(jnp.float32)
    mu = jnp.mean(xf, -1, keepdims=True)
    var = jnp.mean(jnp.square(xf - mu), -1, keepdims=True)
    y = (xf - mu) * lax.rsqrt(var + LN_EPS) * g.astype(jnp.float32) + b.astype(jnp.float32)
    return y.astype(x.dtype)


def sink_softmax(logits, valid, sinks):
    logits = jnp.where(valid, logits, NEG_INF)
    s = sinks.astype(jnp.float32)[:, :, None, None]
    m = jnp.maximum(jnp.max(logits, -1, keepdims=True), s)
    p = jnp.exp(logits - m)
    return p / (jnp.sum(p, -1, keepdims=True) + jnp.exp(s - m))


def window_attn_prompt(q, k, v, sinks):
    B, S = q.shape[:2]
    nb = S // ATTN_BLOCK
    qb = q.reshape(B, nb, ATTN_BLOCK, N_KV_HEADS, Q_PER_KV, HEAD_DIM)
    pad = jnp.zeros((B, ATTN_BLOCK, N_KV_HEADS, HEAD_DIM), k.dtype)
    kb = jnp.concatenate([pad, k], 1).reshape(B, nb + 1, ATTN_BLOCK, N_KV_HEADS, HEAD_DIM)
    vb = jnp.concatenate([pad, v], 1).reshape(B, nb + 1, ATTN_BLOCK, N_KV_HEADS, HEAD_DIM)
    kband = jnp.concatenate([kb[:, :-1], kb[:, 1:]], 2)
    vband = jnp.concatenate([vb[:, :-1], vb[:, 1:]], 2)
    logits = jnp.einsum('bnqgrd,bnkgd->bngrqk', qb, kband,
                        preferred_element_type=jnp.float32) * (HEAD_DIM ** -0.5)
    qi = jnp.arange(ATTN_BLOCK)[:, None]
    kj = jnp.arange(2 * ATTN_BLOCK)[None, :]
    rel = qi + ATTN_BLOCK - kj
    blk = jnp.arange(nb)[:, None, None]
    valid = (rel >= 0) & (rel < WINDOW) & (blk * ATTN_BLOCK - ATTN_BLOCK + kj >= 0)
    p = sink_softmax(logits, valid[None, :, None, None], sinks.reshape(N_KV_HEADS, Q_PER_KV))
    o = jnp.einsum('bngrqk,bnkgd->bnqgrd', p.astype(v.dtype), vband)
    return o.reshape(B, S, D_ATTN)


def window_attn_sample(q, k, v, cache_k, cache_v, sinks):
    B, T = q.shape[:2]
    keys = jnp.concatenate([cache_k, k], 1)
    vals = jnp.concatenate([cache_v, v], 1)
    qg = q.reshape(B, T, N_KV_HEADS, Q_PER_KV, HEAD_DIM)
    logits = jnp.einsum('bqgrd,bkgd->bgrqk', qg, keys,
                        preferred_element_type=jnp.float32) * (HEAD_DIM ** -0.5)
    rel = jnp.arange(T)[:, None] + WINDOW - jnp.arange(WINDOW + T)[None, :]
    valid = (rel >= 0) & (rel < WINDOW)
    p = sink_softmax(logits, valid, sinks.reshape(N_KV_HEADS, Q_PER_KV))
    o = jnp.einsum('bgrqk,bkgd->bqgrd', p.astype(vals.dtype), vals)
    return o.reshape(B, T, D_ATTN), keys[:, -WINDOW:], vals[:, -WINDOW:]


def sgu_prepare(u, sv, ln_g, ln_b):
    B, T, _ = sv.shape
    u = jax.nn.gelu(u)
    sv = layer_norm(jax.nn.gelu(sv), ln_g, ln_b)
    return u, sv.reshape(B, T, N_SGU_GROUPS, SGU_GROUP_DIM)


def sgu_mix(u, v_chunks, w_s, b_s):
    L = v_chunks.shape[2]
    causal = jnp.tril(jnp.ones((L, L), dtype=bool))
    ws = jnp.where(causal, w_s[:, :L, :L], 0)
    s = jnp.einsum('gij,bcjgd->bcigd', ws, v_chunks) + b_s[:, :L].T[None, None, :, :, None]
    return u * s.reshape(u.shape)


def peer_block(xb, w_pq, sub_keys, expert_u, expert_v):
    T = xb.shape[0]
    q = (xb @ w_pq).reshape(T, PEER_HEADS, 2, PEER_HALF)
    scores = jnp.einsum('thcd,hcnd->thcn', q, sub_keys, preferred_element_type=jnp.float32)
    top_s, top_i = lax.top_k(scores, PEER_TOPK)
    cand = top_s[:, :, 0, :, None] + top_s[:, :, 1, None, :]
    cand_idx = top_i[:, :, 0, :, None] * N_SUBKEYS + top_i[:, :, 1, None, :]
    best_s, best_pos = lax.top_k(cand.reshape(T, PEER_HEADS, PEER_TOPK * PEER_TOPK), PEER_TOPK)
    expert = jnp.take_along_axis(cand_idx.reshape(T, PEER_HEADS, PEER_TOPK * PEER_TOPK), best_pos, -1)
    gates = jax.nn.softmax(best_s, -1).reshape(T, PEER_HEADS * PEER_TOPK)
    idx = expert.reshape(T, PEER_HEADS * PEER_TOPK)
    ue = jnp.take(expert_u, idx, axis=0)
    act = jax.nn.gelu(jnp.einsum('ted,td->te', ue, xb))
    ve = jnp.take(expert_v, idx, axis=0)
    coef = gates.astype(xb.dtype) * act
    return jnp.einsum('te,ted->td', coef, ve)


def peer(x, w_pq, sub_keys, expert_u, expert_v):
    B, S, D = x.shape
    flat = x.reshape(B * S, D)
    T = B * S
    nblk = -(-T // PEER_BLOCK)
    flat = jnp.pad(flat, ((0, nblk * PEER_BLOCK - T), (0, 0)))
    out = lax.map(lambda xb: peer_block(xb, w_pq, sub_keys, expert_u, expert_v),
                  flat.reshape(nblk, PEER_BLOCK, D))
    return out.reshape(nblk * PEER_BLOCK, D)[:T].reshape(B, S, D)


def layer(x, past_k, past_v, w_in, sinks, sgu_ln_g, sgu_ln_b, sgu_w, sgu_b,
          w_branch_sgu, w_branch_attn, w_out, ln1_g, ln1_b,
          peer_w_q, peer_sub_keys, peer_u, peer_v, ln2_g, ln2_b):
    B, T, _ = x.shape
    z = jnp.einsum('btd,de->bte', x, w_in)
    q, k, v, u, sv, ga, gb = jnp.split(z, IN_SPLITS, axis=-1)
    q = q.reshape(B, T, N_Q_HEADS, HEAD_DIM)
    k = k.reshape(B, T, N_KV_HEADS, HEAD_DIM)
    v = v.reshape(B, T, N_KV_HEADS, HEAD_DIM)
    u, svg = sgu_prepare(u, sv, sgu_ln_g, sgu_ln_b)
    if past_k is None:
        attn_o = window_attn_prompt(q, k, v, sinks)
        new_k, new_v = k[:, T - WINDOW:], v[:, T - WINDOW:]
        v_chunks = svg.reshape(B, T // CHUNK, CHUNK, N_SGU_GROUPS, SGU_GROUP_DIM)
        open_start = (T - 1) // CHUNK * CHUNK
    else:
        attn_o, new_k, new_v = window_attn_sample(q, k, v, past_k, past_v, sinks)
        v_chunks = svg[:, None]
        open_start = 0
    sgu_o = sgu_mix(u, v_chunks, sgu_w, sgu_b)
    merged = jax.nn.sigmoid(ga) * (sgu_o @ w_branch_sgu) + jax.nn.sigmoid(gb) * (attn_o @ w_branch_attn)
    h = layer_norm(DN_ALPHA * x + merged @ w_out, ln1_g, ln1_b)
    y = layer_norm(DN_ALPHA * h + peer(h, peer_w_q, peer_sub_keys, peer_u, peer_v), ln2_g, ln2_b)
    return y, new_k, new_v, svg[:, open_start:]


def setup_inputs(seed: int = 0) -> dict:
    key = jax.random.key(seed)
    ks = jax.random.split(key, 32)
    nrm = lambda k, shape, scale: jax.random.normal(k, shape, jnp.float32) * scale
    L = DEPTH
    inv_d = D_MODEL ** -0.5
    w_in = jnp.concatenate([
        nrm(ks[0], (L, D_MODEL, D_ATTN), inv_d),
        nrm(ks[1], (L, D_MODEL, D_KV), inv_d),
        nrm(ks[2], (L, D_MODEL, D_KV), inv_d * DN_BETA),
        nrm(ks[3], (L, D_MODEL, D_SGU), inv_d),
        nrm(ks[4], (L, D_MODEL, D_SGU), inv_d),
        nrm(ks[5], (L, D_MODEL, 2 * D_MODEL), inv_d),
    ], axis=-1)
    return {
        "x_prompt": nrm(ks[6], (BATCH, SEQ, D_MODEL), 1.0),
        "x_sample": nrm(ks[7], (DEC_BATCH, DEC_SEQ, D_MODEL), 1.0),
        "cache_k": nrm(ks[8], (L, DEC_BATCH, WINDOW, N_KV_HEADS, HEAD_DIM), 1.0),
        "cache_v": nrm(ks[9], (L, DEC_BATCH, WINDOW, N_KV_HEADS, HEAD_DIM), DN_BETA),
        "w_in": w_in,
        "sinks": nrm(ks[10], (L, N_Q_HEADS), 0.5),
        "sgu_ln_g": 1.0 + nrm(ks[11], (L, D_SGU), 0.02),
        "sgu_ln_b": nrm(ks[12], (L, D_SGU), 0.02),
        "sgu_w": nrm(ks[13], (L, N_SGU_GROUPS, CHUNK, CHUNK), CHUNK ** -0.5),
        "sgu_b": 1.0 + nrm(ks[14], (L, N_SGU_GROUPS, CHUNK), 0.1),
        "w_branch_sgu": nrm(ks[15], (L, D_SGU, D_MODEL), D_SGU ** -0.5),
        "w_branch_attn": nrm(ks[16], (L, D_ATTN, D_MODEL), D_ATTN ** -0.5),
        "w_out": nrm(ks[17], (L, D_MODEL, D_MODEL), inv_d * DN_BETA),
        "ln1_g": 1.0 + nrm(ks[18], (L, D_MODEL), 0.02),
        "ln1_b": nrm(ks[19], (L, D_MODEL), 0.02),
        "peer_w_q": nrm(ks[20], (L, D_MODEL, PEER_HEADS * PEER_KEY_DIM), inv_d),
        "peer_sub_keys": nrm(ks[21], (L, PEER_HEADS, 2, N_SUBKEYS, PEER_HALF), PEER_HALF ** -0.5),
        "peer_u": nrm(ks[22], (L, N_EXPERTS, D_MODEL), inv_d),
        "peer_v": nrm(ks[23], (L, N_EXPERTS, D_MODEL), DN_BETA * PEER_HEADS ** -0.5),
        "ln2_g": 1.0 + nrm(ks[24], (L, D_MODEL), 0.02),
        "ln2_b": nrm(ks[25], (L, D_MODEL), 0.02),
    }


def reference(x_prompt, x_sample, cache_k, cache_v, w_in, sinks, sgu_ln_g, sgu_ln_b, sgu_w, sgu_b,
              w_branch_sgu, w_branch_attn, w_out, ln1_g, ln1_b,
              peer_w_q, peer_sub_keys, peer_u, peer_v, ln2_g, ln2_b):
    yp, ys = x_prompt, x_sample
    kp, vp, ksm, vsm, svp, svs = [], [], [], [], [], []
    for l in range(DEPTH):
        params = (w_in[l], sinks[l], sgu_ln_g[l], sgu_ln_b[l], sgu_w[l], sgu_b[l],
                  w_branch_sgu[l], w_branch_attn[l], w_out[l], ln1_g[l], ln1_b[l],
                  peer_w_q[l], peer_sub_keys[l], peer_u[l], peer_v[l], ln2_g[l], ln2_b[l])
        yp, nk, nv, nsv = layer(yp, None, None, *params)
        kp.append(nk); vp.append(nv); svp.append(nsv)
        ys, nk, nv, nsv = layer(ys, cache_k[l], cache_v[l], *params)
        ksm.append(nk); vsm.append(nv); svs.append(nsv)
    return (yp, ys, jnp.stack(kp), jnp.stack(vp), jnp.stack(ksm), jnp.stack(vsm),
            jnp.stack(svp), jnp.stack(svs))
```

```python
import functools

import jax
import jax.numpy as jnp
from jax import lax
from jax.experimental import pallas as pl
from jax.experimental.pallas import tpu as pltpu

F32 = jnp.float32
BF16 = jnp.bfloat16

D_MODEL = 1024
N_Q_HEADS = 8
N_KV_HEADS = 2
HEAD_DIM = 64
Q_PER_KV = N_Q_HEADS // N_KV_HEADS
D_ATTN = N_Q_HEADS * HEAD_DIM
D_KV = N_KV_HEADS * HEAD_DIM
WINDOW = 128
CHUNK = 128
D_SGU = D_MODEL // 2
N_SGU_GROUPS = 4
SGU_GROUP_DIM = D_SGU // N_SGU_GROUPS
PEER_HEADS = 8
N_SUBKEYS = 128
N_EXPERTS = N_SUBKEYS * N_SUBKEYS
PEER_TOPK = 16
PEER_KEY_DIM = 256
PEER_HALF = PEER_KEY_DIM // 2
LN_EPS = 1e-5
NEG_INF = -1e30
D_IN = D_ATTN + 2 * D_KV + 2 * D_SGU + 2 * D_MODEL
_OFF_Q = 0
_OFF_K = D_ATTN
_OFF_V = _OFF_K + D_KV
_OFF_U = _OFF_V + D_KV
_OFF_SV = _OFF_U + D_SGU
_OFF_GA = _OFF_SV + D_SGU
_OFF_GB = _OFF_GA + D_MODEL

LANES = 128
VMEM_LIMIT_BYTES = 56 * 1024 * 1024

PROMPT_BLOCK = 512
SAMPLE_ATTN_BATCH = 8
PEER_TOKEN_BLOCK = 512
PEER_EXPERT_BLOCK = 512


def _gelu(x):
    return 0.5 * x * (1.0 + jnp.tanh(0.7978845608028654 * (x + 0.044715 * (x * x * x))))


def _sigmoid(x):
    return 1.0 / (1.0 + jnp.exp(-x))


def _layer_norm(x, g, b):
    mu = jnp.mean(x, -1, keepdims=True)
    xc = x - mu
    var = jnp.mean(xc * xc, -1, keepdims=True)
    return xc * lax.rsqrt(var + LN_EPS) * g + b


def _dot(a, b):
    return jnp.dot(a, b, preferred_element_type=F32)


def _dot_nt(a, b):
    return lax.dot_general(a, b, (((1,), (1,)), ((), ())), preferred_element_type=F32)


def _sink_softmax_pv(logits, sink_col, vals):
    m = jnp.maximum(jnp.max(logits, -1, keepdims=True), sink_col)
    p = jnp.exp(logits - m)
    denom = jnp.sum(p, -1, keepdims=True) + jnp.exp(sink_col - m)
    return _dot(p.astype(BF16), vals) / denom


def _prompt_mixer_kernel(sinks_ref, x_ref, w_in_ref, sgu_g_ref, sgu_b_ref, sgu_w_ref, sgu_bcol_ref,
                         w_bs_ref, w_ba_ref, w_out_ref, ln1_g_ref, ln1_b_ref,
                         h_ref, k_ref, v_ref, sv_ref,
                         kext_ref, vext_ref, sgu_o_ref, attn_o_ref, *, alpha, ts):
    s = pl.program_id(1)
    last = pl.num_programs(1) - 1
    nblk = ts // WINDOW

    x = x_ref[...]
    z = _dot(x.astype(BF16), w_in_ref[...])
    k = z[:, _OFF_K:_OFF_V]
    v = z[:, _OFF_V:_OFF_U]
    u = _gelu(z[:, _OFF_U:_OFF_SV])
    svn = _layer_norm(_gelu(z[:, _OFF_SV:_OFF_GA]), sgu_g_ref[...], sgu_b_ref[...])

    @pl.when(s == 0)
    def _():
        kext_ref[0:WINDOW, :] = jnp.zeros((WINDOW, D_KV), BF16)
        vext_ref[0:WINDOW, :] = jnp.zeros((WINDOW, D_KV), BF16)

    kext_ref[WINDOW:, :] = k.astype(BF16)
    vext_ref[WINDOW:, :] = v.astype(BF16)

    @pl.when(s == last)
    def _():
        k_ref[...] = k[ts - WINDOW:, :]
        v_ref[...] = v[ts - WINDOW:, :]
        sv_ref[...] = svn[ts - CHUNK:, :]

    row = lax.broadcasted_iota(jnp.int32, (CHUNK, CHUNK), 0)
    col = lax.broadcasted_iota(jnp.int32, (CHUNK, CHUNK), 1)
    for g in range(N_SGU_GROUPS):
        wt = jnp.where(col <= row, sgu_w_ref[g], 0.0).astype(BF16)
        bcol = sgu_bcol_ref[:, g:g + 1]
        gs = slice(g * SGU_GROUP_DIM, (g + 1) * SGU_GROUP_DIM)
        for c in range(ts // CHUNK):
            rs = slice(c * CHUNK, (c + 1) * CHUNK)
            mix = _dot(wt, svn[rs, gs].astype(BF16)) + bcol
            sgu_o_ref[rs, gs] = (u[rs, gs] * mix).astype(BF16)

    nq = Q_PER_KV * WINDOW
    qi = lax.broadcasted_iota(jnp.int32, (nq, 2 * WINDOW), 0) % WINDOW
    kj = lax.broadcasted_iota(jnp.int32, (nq, 2 * WINDOW), 1)
    rel = qi + WINDOW - kj
    bias = jnp.where((rel >= 0) & (rel < WINDOW), 0.0, NEG_INF).astype(F32)
    bias_first = jnp.where(s > 0, bias, jnp.where(kj < WINDOW, NEG_INF, bias))
    scale = HEAD_DIM ** -0.5
    for i in range(nblk):
        qs = slice(i * WINDOW, (i + 1) * WINDOW)
        heads = [None] * N_Q_HEADS
        for g in range(N_KV_HEADS):
            ds = slice(g * HEAD_DIM, (g + 1) * HEAD_DIM)
            kb = kext_ref[i * WINDOW:(i + 2) * WINDOW, ds]
            vb = vext_ref[i * WINDOW:(i + 2) * WINDOW, ds]
            q4 = jnp.concatenate(
                [z[qs, (g * Q_PER_KV + r) * HEAD_DIM:(g * Q_PER_KV + r + 1) * HEAD_DIM]
                 for r in range(Q_PER_KV)], axis=0).astype(BF16)
            logits = _dot_nt(q4, kb) * scale + (bias_first if i == 0 else bias)
            sink_col = jnp.concatenate(
                [jnp.full((WINDOW, 1), sinks_ref[g * Q_PER_KV + r], F32) for r in range(Q_PER_KV)], axis=0)
            o4 = _sink_softmax_pv(logits, sink_col, vb)
            for r in range(Q_PER_KV):
                heads[g * Q_PER_KV + r] = o4[r * WINDOW:(r + 1) * WINDOW, :]
        attn_o_ref[qs, :] = jnp.concatenate(heads, axis=1).astype(BF16)

    kext_ref[0:WINDOW, :] = kext_ref[ts:ts + WINDOW, :]
    vext_ref[0:WINDOW, :] = vext_ref[ts:ts + WINDOW, :]

    merged = (_sigmoid(z[:, _OFF_GA:_OFF_GB]) * _dot(sgu_o_ref[...], w_bs_ref[...])
              + _sigmoid(z[:, _OFF_GB:]) * _dot(attn_o_ref[...], w_ba_ref[...]))
    y = _dot(merged.astype(BF16), w_out_ref[...])
    h_ref[...] = _layer_norm(alpha * x + y, ln1_g_ref[...], ln1_b_ref[...])


def _full(shape):
    return pl.BlockSpec(shape, lambda *_: (0,) * len(shape))


def _prompt_mixer(x, p, alpha, ts=PROMPT_BLOCK):
    B, S, D = x.shape
    ts = min(ts, S)
    assert S % ts == 0 and ts % WINDOW == 0 and ts % CHUNK == 0
    grid = (B, S // ts)
    smem = pl.BlockSpec(memory_space=pltpu.SMEM)
    in_specs = [
        smem,
        pl.BlockSpec((None, ts, D), lambda b, s: (b, s, 0)),
        _full((D, D_IN)), _full((1, D_SGU)), _full((1, D_SGU)),
        _full((N_SGU_GROUPS, CHUNK, CHUNK)), _full((CHUNK, N_SGU_GROUPS)),
        _full((D_SGU, D)), _full((D_ATTN, D)), _full((D, D)), _full((1, D)), _full((1, D)),
    ]
    out_specs = [
        pl.BlockSpec((None, ts, D), lambda b, s: (b, s, 0)),
        pl.BlockSpec((None, WINDOW, D_KV), lambda b, s: (b, 0, 0)),
        pl.BlockSpec((None, WINDOW, D_KV), lambda b, s: (b, 0, 0)),
        pl.BlockSpec((None, CHUNK, D_SGU), lambda b, s: (b, 0, 0)),
    ]
    out_shape = [
        jax.ShapeDtypeStruct((B, S, D), F32),
        jax.ShapeDtypeStruct((B, WINDOW, D_KV), F32),
        jax.ShapeDtypeStruct((B, WINDOW, D_KV), F32),
        jax.ShapeDtypeStruct((B, CHUNK, D_SGU), F32),
    ]
    scratch = [
        pltpu.VMEM((ts + WINDOW, D_KV), BF16),
        pltpu.VMEM((ts + WINDOW, D_KV), BF16),
        pltpu.VMEM((ts, D_SGU), BF16),
        pltpu.VMEM((ts, D_ATTN), BF16),
    ]
    return pl.pallas_call(
        functools.partial(_prompt_mixer_kernel, alpha=alpha, ts=ts),
        grid=grid, in_specs=in_specs, out_specs=out_specs, out_shape=out_shape,
        scratch_shapes=scratch, name="prompt_mixer",
        compiler_params=pltpu.CompilerParams(
            dimension_semantics=("arbitrary", "arbitrary"), vmem_limit_bytes=VMEM_LIMIT_BYTES),
    )(p["sinks"], x, p["w_in"], p["sgu_ln_g"], p["sgu_ln_b"], p["sgu_w"], p["sgu_bcol"],
      p["w_bs"], p["w_ba"], p["w_out"], p["ln1_g"], p["ln1_b"])


def _sample_pre_kernel(sgu_w_ref, sgu_b_ref, x_ref, w_in_ref, sgu_g_ref, sgu_bb_ref, w_bs_ref,
                       q_ref, k_ref, v_ref, sv_ref, part_ref, gate_b_ref, *, t_new):
    x = x_ref[...]
    n = x.shape[0]
    z = _dot(x.astype(BF16), w_in_ref[...])
    q_ref[...] = z[:, _OFF_Q:_OFF_K]
    k_ref[...] = z[:, _OFF_K:_OFF_V]
    v_ref[...] = z[:, _OFF_V:_OFF_U]
    u = _gelu(z[:, _OFF_U:_OFF_SV])
    svn = _layer_norm(_gelu(z[:, _OFF_SV:_OFF_GA]), sgu_g_ref[...], sgu_bb_ref[...])
    sv_ref[...] = svn
    t_of_row = lax.broadcasted_iota(jnp.int32, (n, SGU_GROUP_DIM), 0) % t_new
    pieces = []
    for g in range(N_SGU_GROUPS):
        vg = svn[:, g * SGU_GROUP_DIM:(g + 1) * SGU_GROUP_DIM]
        mix = jnp.zeros((n, SGU_GROUP_DIM), F32)
        for t in range(t_new):
            mix = jnp.where(t_of_row == t, sgu_b_ref[g, t], mix)
        for d in range(t_new):
            coef = jnp.zeros((n, SGU_GROUP_DIM), F32)
            for t in range(d, t_new):
                coef = jnp.where(t_of_row == t, sgu_w_ref[g, t * t_new + t - d], coef)
            shifted = vg if d == 0 else pltpu.roll(vg, d, 0)
            mix = mix + coef * shifted
        pieces.append(u[:, g * SGU_GROUP_DIM:(g + 1) * SGU_GROUP_DIM] * mix)
    sgu_o = jnp.concatenate(pieces, axis=1).astype(BF16)
    part_ref[...] = _sigmoid(z[:, _OFF_GA:_OFF_GB]) * _dot(sgu_o, w_bs_ref[...])
    gate_b_ref[...] = _sigmoid(z[:, _OFF_GB:])


def _sample_pre(x2d, p, t_new):
    n, D = x2d.shape
    smem = pl.BlockSpec(memory_space=pltpu.SMEM)
    out_shape = [
        jax.ShapeDtypeStruct((n, D_ATTN), F32),
        jax.ShapeDtypeStruct((n, D_KV), F32),
        jax.ShapeDtypeStruct((n, D_KV), F32),
        jax.ShapeDtypeStruct((n, D_SGU), F32),
        jax.ShapeDtypeStruct((n, D), F32),
        jax.ShapeDtypeStruct((n, D), F32),
    ]
    return pl.pallas_call(
        functools.partial(_sample_pre_kernel, t_new=t_new),
        in_specs=[smem, smem] + [pl.BlockSpec(memory_space=pltpu.VMEM)] * 5,
        out_specs=[pl.BlockSpec(memory_space=pltpu.VMEM)] * 6,
        out_shape=out_shape, name="sample_pre",
        compiler_params=pltpu.CompilerParams(vmem_limit_bytes=VMEM_LIMIT_BYTES),
    )(p["sgu_w"][:, :t_new, :t_new].reshape(N_SGU_GROUPS, t_new * t_new), p["sgu_b"],
      x2d, p["w_in"], p["sgu_ln_g"], p["sgu_ln_b"], p["w_bs"])


def _sample_attn_kernel(sinks_ref, q_ref, kn_ref, vn_ref, ck_ref, cv_ref,
                        o_ref, nk_ref, nv_ref, kbuf_ref, vbuf_ref, *, t_new, bb):
    nq = Q_PER_KV * t_new
    nkeys = WINDOW + 8
    row = lax.broadcasted_iota(jnp.int32, (nq, nkeys), 0)
    kj = lax.broadcasted_iota(jnp.int32, (nq, nkeys), 1)
    t_q = row % t_new
    bias = jnp.where((kj > t_q) & (kj <= t_q + WINDOW), 0.0, NEG_INF).astype(F32)
    r_of_row = lax.broadcasted_iota(jnp.int32, (nq, 1), 0) // t_new
    scale = HEAD_DIM ** -0.5
    pad = jnp.zeros((8 - t_new, D_KV), F32)
    for b in range(bb):
        kbuf_ref[0:WINDOW, :] = ck_ref[b]
        kbuf_ref[WINDOW:WINDOW + t_new, :] = kn_ref[b]
        kbuf_ref[WINDOW + t_new:, :] = pad
        vbuf_ref[0:WINDOW, :] = cv_ref[b]
        vbuf_ref[WINDOW:WINDOW + t_new, :] = vn_ref[b]
        vbuf_ref[WINDOW + t_new:, :] = pad
        nk_ref[b] = kbuf_ref[t_new:t_new + WINDOW, :]
        nv_ref[b] = vbuf_ref[t_new:t_new + WINDOW, :]
        for g in range(N_KV_HEADS):
            ds = slice(g * HEAD_DIM, (g + 1) * HEAD_DIM)
            kg = kbuf_ref[:, ds].astype(BF16)
            vg = vbuf_ref[:, ds].astype(BF16)
            logits = _dot_nt(q_ref[b, g].astype(BF16), kg) * scale + bias
            sink_col = jnp.zeros((nq, 1), F32)
            for r in range(Q_PER_KV):
                sink_col = jnp.where(r_of_row == r, sinks_ref[g * Q_PER_KV + r], sink_col)
            o_ref[b, g] = _sink_softmax_pv(logits, sink_col, vg)


def _sample_attn(q_grouped, k_new, v_new, cache_k, cache_v, sinks, t_new, bb=SAMPLE_ATTN_BATCH):
    nb = q_grouped.shape[0]
    bb = min(bb, nb)
    assert nb % bb == 0 and t_new <= 8
    nq = Q_PER_KV * t_new
    smem = pl.BlockSpec(memory_space=pltpu.SMEM)
    in_specs = [
        smem,
        pl.BlockSpec((bb, N_KV_HEADS, nq, HEAD_DIM), lambda i: (i, 0, 0, 0)),
        pl.BlockSpec((bb, t_new, D_KV), lambda i: (i, 0, 0)),
        pl.BlockSpec((bb, t_new, D_KV), lambda i: (i, 0, 0)),
        pl.BlockSpec((bb, WINDOW, D_KV), lambda i: (i, 0, 0)),
        pl.BlockSpec((bb, WINDOW, D_KV), lambda i: (i, 0, 0)),
    ]
    out_specs = [
        pl.BlockSpec((bb, N_KV_HEADS, nq, HEAD_DIM), lambda i: (i, 0, 0, 0)),
        pl.BlockSpec((bb, WINDOW, D_KV), lambda i: (i, 0, 0)),
        pl.BlockSpec((bb, WINDOW, D_KV), lambda i: (i, 0, 0)),
    ]
    out_shape = [
        jax.ShapeDtypeStruct((nb, N_KV_HEADS, nq, HEAD_DIM), F32),
        jax.ShapeDtypeStruct((nb, WINDOW, D_KV), F32),
        jax.ShapeDtypeStruct((nb, WINDOW, D_KV), F32),
    ]
    return pl.pallas_call(
        functools.partial(_sample_attn_kernel, t_new=t_new, bb=bb),
        grid=(nb // bb,), in_specs=in_specs, out_specs=out_specs, out_shape=out_shape,
        scratch_shapes=[pltpu.VMEM((WINDOW + 8, D_KV), F32), pltpu.VMEM((WINDOW + 8, D_KV), F32)],
        name="sample_attn",
        compiler_params=pltpu.CompilerParams(dimension_semantics=("arbitrary",)),
    )(sinks, q_grouped, k_new, v_new, cache_k, cache_v)


def _sample_post_kernel(x_ref, part_ref, gate_b_ref, o_ref, w_ba_ref, w_out_ref, ln1_g_ref, ln1_b_ref,
                        h_ref, *, alpha):
    merged = part_ref[...] + gate_b_ref[...] * _dot(o_ref[...].astype(BF16), w_ba_ref[...])
    y = _dot(merged.astype(BF16), w_out_ref[...])
    h_ref[...] = _layer_norm(alpha * x_ref[...] + y, ln1_g_ref[...], ln1_b_ref[...])


def _sample_post(x2d, part, gate_b, o2d, p, alpha):
    return pl.pallas_call(
        functools.partial(_sample_post_kernel, alpha=alpha),
        in_specs=[pl.BlockSpec(memory_space=pltpu.VMEM)] * 8,
        out_specs=pl.BlockSpec(memory_space=pltpu.VMEM),
        out_shape=jax.ShapeDtypeStruct(x2d.shape, F32), name="sample_post",
        compiler_params=pltpu.CompilerParams(vmem_limit_bytes=VMEM_LIMIT_BYTES),
    )(x2d, part, gate_b, o2d, p["w_ba"], p["w_out"], p["ln1_g"], p["ln1_b"])


def _top_values(work, count):
    vals = []
    for i in range(count):
        top = jnp.max(work, axis=0, keepdims=True)
        vals.append(top)
        if i + 1 < count:
            work = jnp.where(work == top, -jnp.inf, work)
    return vals


def _candidate_rows(a_rows, b_rows):
    a_all = jnp.concatenate(a_rows, axis=0)
    a_low = a_all[0:8]
    rank = lax.broadcasted_iota(jnp.int32, a_low.shape, 0)
    tiles = [a_all + b_rows[0]]
    for j in range(1, PEER_TOPK):
        keep = PEER_TOPK // (j + 1)
        tiles.append(jnp.where(rank < keep, a_low + b_rows[j], -jnp.inf))
    return jnp.concatenate(tiles, axis=0)


def _peer_kernel(h_ref, wq_ref, sk_ref, u_ref, vt_ref, ln_g_ref, ln_b_ref, y_ref,
                 ht_ref, s2_ref, e2_ref, th_ref, e1_ref, coef_ref, acc_ref, *, alpha, tb, ce):
    j = pl.program_id(1)
    ntb = tb // LANES
    n1_per_step = ce // N_SUBKEYS

    @pl.when(j == 0)
    def _():
        ht = h_ref[...].T.astype(BF16)
        ht_ref[...] = ht
        qt = _dot(wq_ref[...], ht)
        for hd in range(PEER_HEADS):
            for t in range(ntb):
                ls = slice(t * LANES, (t + 1) * LANES)
                s = []
                for c in range(2):
                    r0 = (hd * 2 + c) * PEER_HALF
                    s.append(_dot(sk_ref[hd * 2 + c], qt[r0:r0 + PEER_HALF, ls].astype(BF16)))
                a_rows = _top_values(s[0], PEER_TOPK)
                b_rows = _top_values(s[1], PEER_TOPK)
                best = _top_values(_candidate_rows(a_rows, b_rows), PEER_TOPK + 1)
                norm = jnp.zeros_like(best[0])
                for i in range(PEER_TOPK):
                    norm = norm + jnp.exp(best[i] - best[0])
                cut = 0.5 * (best[PEER_TOPK - 1] + best[PEER_TOPK])
                cut = jnp.where(best[PEER_TOPK] == -jnp.inf, best[PEER_TOPK - 1], cut)
                s2_ref[hd, t] = s[1]
                e2_ref[hd, t] = jnp.exp(s[1] - b_rows[0])
                th_ref[hd, t] = cut - s[0]
                e1_ref[hd, t] = jnp.exp(s[0] - a_rows[0]) / norm
        acc_ref[...] = jnp.zeros_like(acc_ref)

    act = _dot(u_ref[...], ht_ref[...])
    for n in range(n1_per_step):
        n1 = j * n1_per_step + n
        rs = slice(n * N_SUBKEYS, (n + 1) * N_SUBKEYS)
        for t in range(ntb):
            ls = slice(t * LANES, (t + 1) * LANES)
            gate = jnp.zeros((N_SUBKEYS, LANES), F32)
            for hd in range(PEER_HEADS):
                th = th_ref[hd, t, pl.ds(n1, 1), :]
                e1 = e1_ref[hd, t, pl.ds(n1, 1), :]
                gate = gate + jnp.where(s2_ref[hd, t] >= th, e2_ref[hd, t], 0.0) * e1
            coef_ref[rs, ls] = (gate * _gelu(act[rs, ls])).astype(BF16)
    acc_ref[...] += _dot(vt_ref[...], coef_ref[...])

    @pl.when(j == pl.num_programs(1) - 1)
    def _():
        y_ref[...] = _layer_norm(alpha * h_ref[...] + acc_ref[...].T, ln_g_ref[...], ln_b_ref[...])


def _peer(h2d, p, alpha, tb=PEER_TOKEN_BLOCK, ce=PEER_EXPERT_BLOCK):
    n, D = h2d.shape
    tb = min(tb, n)
    assert n % tb == 0 and tb % LANES == 0 and N_EXPERTS % ce == 0 and ce % N_SUBKEYS == 0
    ntb = tb // LANES
    qdim = PEER_HEADS * PEER_KEY_DIM
    in_specs = [
        pl.BlockSpec((tb, D), lambda i, j: (i, 0)),
        _full((qdim, D)),
        _full((PEER_HEADS * 2, N_SUBKEYS, PEER_HALF)),
        pl.BlockSpec((ce, D), lambda i, j: (j, 0)),
        pl.BlockSpec((D, ce), lambda i, j: (0, j)),
        _full((1, D)), _full((1, D)),
    ]
    per_head = (PEER_HEADS, ntb, N_SUBKEYS, LANES)
    scratch = [
        pltpu.VMEM((D, tb), BF16),
        pltpu.VMEM(per_head, F32), pltpu.VMEM(per_head, F32),
        pltpu.VMEM(per_head, F32), pltpu.VMEM(per_head, F32),
        pltpu.VMEM((ce, tb), BF16),
        pltpu.VMEM((D, tb), F32),
    ]
    return pl.pallas_call(
        functools.partial(_peer_kernel, alpha=alpha, tb=tb, ce=ce),
        grid=(n // tb, N_EXPERTS // ce), in_specs=in_specs,
        out_specs=pl.BlockSpec((tb, D), lambda i, j: (i, 0)),
        out_shape=jax.ShapeDtypeStruct((n, D), F32), scratch_shapes=scratch, name="peer",
        compiler_params=pltpu.CompilerParams(
            dimension_semantics=("arbitrary", "arbitrary"), vmem_limit_bytes=VMEM_LIMIT_BYTES),
    )(h2d, p["peer_wq_t"], p["peer_sk"], p["peer_u"], p["peer_vt"], p["ln2_g"], p["ln2_b"])


def _layer_params(l, w_in, sinks, sgu_ln_g, sgu_ln_b, sgu_w, sgu_b, w_branch_sgu, w_branch_attn, w_out,
                  ln1_g, ln1_b, peer_w_q, peer_sub_keys, peer_u, peer_v, ln2_g, ln2_b):
    return {
        "w_in": w_in[l].astype(BF16),
        "sinks": sinks[l],
        "sgu_ln_g": sgu_ln_g[l][None, :], "sgu_ln_b": sgu_ln_b[l][None, :],
        "sgu_w": sgu_w[l],
        "sgu_b": sgu_b[l], "sgu_bcol": sgu_b[l].T,
        "w_bs": w_branch_sgu[l].astype(BF16), "w_ba": w_branch_attn[l].astype(BF16),
        "w_out": w_out[l].astype(BF16),
        "ln1_g": ln1_g[l][None, :], "ln1_b": ln1_b[l][None, :],
        "peer_wq_t": peer_w_q[l].T.astype(BF16),
        "peer_sk": peer_sub_keys[l].reshape(PEER_HEADS * 2, N_SUBKEYS, PEER_HALF).astype(BF16),
        "peer_u": peer_u[l].astype(BF16),
        "peer_vt": peer_v[l].T.astype(BF16),
        "ln2_g": ln2_g[l][None, :], "ln2_b": ln2_b[l][None, :],
    }


def _prompt_layer(x, p, alpha):
    B, S, D = x.shape
    h, k_last, v_last, sv_last = _prompt_mixer(x, p, alpha)
    y = _peer(h.reshape(B * S, D), p, alpha).reshape(B, S, D)
    return (y, k_last.reshape(B, WINDOW, N_KV_HEADS, HEAD_DIM), v_last.reshape(B, WINDOW, N_KV_HEADS, HEAD_DIM),
            sv_last.reshape(B, CHUNK, N_SGU_GROUPS, SGU_GROUP_DIM))


def _sample_layer(x, cache_k, cache_v, p, alpha):
    B, T, D = x.shape
    x2d = x.reshape(B * T, D)
    q, k_new, v_new, svn, part, gate_b = _sample_pre(x2d, p, T)
    qg = q.reshape(B, T, N_KV_HEADS, Q_PER_KV, HEAD_DIM).transpose(0, 2, 3, 1, 4)
    qg = qg.reshape(B, N_KV_HEADS, Q_PER_KV * T, HEAD_DIM)
    og, new_k, new_v = _sample_attn(qg, k_new.reshape(B, T, D_KV), v_new.reshape(B, T, D_KV),
                                    cache_k.reshape(B, WINDOW, D_KV), cache_v.reshape(B, WINDOW, D_KV),
                                    p["sinks"], T)
    o2d = og.reshape(B, N_KV_HEADS, Q_PER_KV, T, HEAD_DIM).transpose(0, 3, 1, 2, 4).reshape(B * T, D_ATTN)
    h = _sample_post(x2d, part, gate_b, o2d, p, alpha)
    y = _peer(h, p, alpha).reshape(B, T, D)
    return (y, new_k.reshape(B, WINDOW, N_KV_HEADS, HEAD_DIM), new_v.reshape(B, WINDOW, N_KV_HEADS, HEAD_DIM),
            svn.reshape(B, T, N_SGU_GROUPS, SGU_GROUP_DIM))


def kernel(x_prompt, x_sample, cache_k, cache_v, w_in, sinks, sgu_ln_g, sgu_ln_b, sgu_w, sgu_b, w_branch_sgu, w_branch_attn, w_out, ln1_g, ln1_b, peer_w_q, peer_sub_keys, peer_u, peer_v, ln2_g, ln2_b):
    depth = w_in.shape[0]
    alpha = (2.0 * depth) ** 0.25
    yp, ys = x_prompt, x_sample
    outs = [[] for _ in range(6)]
    for l in range(depth):
        p = _layer_params(l, w_in, sinks, sgu_ln_g, sgu_ln_b, sgu_w, sgu_b, w_branch_sgu, w_branch_attn,
                          w_out, ln1_g, ln1_b, peer_w_q, peer_sub_keys, peer_u, peer_v, ln2_g, ln2_b)
        yp, kp, vp, svp = _prompt_layer(yp, p, alpha)
        ys, ksm, vsm, svs = _sample_layer(ys, cache_k[l], cache_v[l], p, alpha)
        for acc, val in zip(outs, (kp, vp, ksm, vsm, svp, svs)):
            acc.append(val)
    return (yp, ys) + tuple(jnp.stack(o) for o in outs)
```

```python
import functools

import jax
import jax.numpy as jnp
from jax import lax
from jax.experimental import pallas as pl
from jax.experimental.pallas import tpu as pltpu

F32 = jnp.float32
BF16 = jnp.bfloat16

D_MODEL = 1024
N_Q_HEADS = 8
N_KV_HEADS = 2
HEAD_DIM = 64
Q_PER_KV = N_Q_HEADS // N_KV_HEADS
D_ATTN = N_Q_HEADS * HEAD_DIM
D_KV = N_KV_HEADS * HEAD_DIM
WINDOW = 128
CHUNK = 128
D_SGU = D_MODEL // 2
N_SGU_GROUPS = 4
SGU_GROUP_DIM = D_SGU // N_SGU_GROUPS
PEER_HEADS = 8
N_SUBKEYS = 128
N_EXPERTS = N_SUBKEYS * N_SUBKEYS
PEER_TOPK = 16
PEER_KEY_DIM = 256
PEER_HALF = PEER_KEY_DIM // 2
LN_EPS = 1e-5
NEG_INF = -1e30
D_IN = D_ATTN + 2 * D_KV + 2 * D_SGU + 2 * D_MODEL
_OFF_Q = 0
_OFF_K = D_ATTN
_OFF_V = _OFF_K + D_KV
_OFF_U = _OFF_V + D_KV
_OFF_SV = _OFF_U + D_SGU
_OFF_GA = _OFF_SV + D_SGU
_OFF_GB = _OFF_GA + D_MODEL

LANES = 128
VMEM_LIMIT_BYTES = 56 * 1024 * 1024

PROMPT_BLOCK = 512
SAMPLE_ATTN_BATCH = 8
PEER_TOKEN_BLOCK = 512
PEER_EXPERT_BLOCK = 1024


def _gelu(x):
    return 0.5 * x * (1.0 + jnp.tanh(0.7978845608028654 * (x + 0.044715 * (x * x * x))))


def _sigmoid(x):
    return 1.0 / (1.0 + jnp.exp(-x))


def _layer_norm(x, g, b):
    mu = jnp.mean(x, -1, keepdims=True)
    xc = x - mu
    var = jnp.mean(xc * xc, -1, keepdims=True)
    return xc * lax.rsqrt(var + LN_EPS) * g + b


def _dot(a, b):
    return jnp.dot(a, b, preferred_element_type=F32)


def _dot_nt(a, b):
    return lax.dot_general(a, b, (((1,), (1,)), ((), ())), preferred_element_type=F32)


def _sink_softmax_pv(logits, sink_col, vals):
    m = jnp.maximum(jnp.max(logits, -1, keepdims=True), sink_col)
    p = jnp.exp(logits - m)
    denom = jnp.sum(p, -1, keepdims=True) + jnp.exp(sink_col - m)
    return _dot(p.astype(BF16), vals) / denom


def _prompt_mixer_kernel(sinks_ref, x_ref, w_in_ref, sgu_g_ref, sgu_b_ref, sgu_w_ref, sgu_bcol_ref,
                         w_bs_ref, w_ba_ref, w_out_ref, ln1_g_ref, ln1_b_ref,
                         h_ref, k_ref, v_ref, sv_ref,
                         kext_ref, vext_ref, sgu_o_ref, attn_o_ref, *, alpha, ts):
    s = pl.program_id(1)
    last = pl.num_programs(1) - 1
    nblk = ts // WINDOW

    x = x_ref[...]
    z = _dot(x.astype(BF16), w_in_ref[...])
    k = z[:, _OFF_K:_OFF_V]
    v = z[:, _OFF_V:_OFF_U]
    u = _gelu(z[:, _OFF_U:_OFF_SV])
    svn = _layer_norm(_gelu(z[:, _OFF_SV:_OFF_GA]), sgu_g_ref[...], sgu_b_ref[...])

    @pl.when(s == 0)
    def _():
        kext_ref[0:WINDOW, :] = jnp.zeros((WINDOW, D_KV), BF16)
        vext_ref[0:WINDOW, :] = jnp.zeros((WINDOW, D_KV), BF16)

    kext_ref[WINDOW:, :] = k.astype(BF16)
    vext_ref[WINDOW:, :] = v.astype(BF16)

    @pl.when(s == last)
    def _():
        k_ref[...] = k[ts - WINDOW:, :]
        v_ref[...] = v[ts - WINDOW:, :]
        sv_ref[...] = svn[ts - CHUNK:, :]

    row = lax.broadcasted_iota(jnp.int32, (CHUNK, CHUNK), 0)
    col = lax.broadcasted_iota(jnp.int32, (CHUNK, CHUNK), 1)
    for g in range(N_SGU_GROUPS):
        wt = jnp.where(col <= row, sgu_w_ref[g], 0.0).astype(BF16)
        bcol = sgu_bcol_ref[:, g:g + 1]
        gs = slice(g * SGU_GROUP_DIM, (g + 1) * SGU_GROUP_DIM)
        for c in range(ts // CHUNK):
            rs = slice(c * CHUNK, (c + 1) * CHUNK)
            mix = _dot(wt, svn[rs, gs].astype(BF16)) + bcol
            sgu_o_ref[rs, gs] = (u[rs, gs] * mix).astype(BF16)

    nq = Q_PER_KV * WINDOW
    qi = lax.broadcasted_iota(jnp.int32, (nq, 2 * WINDOW), 0) % WINDOW
    kj = lax.broadcasted_iota(jnp.int32, (nq, 2 * WINDOW), 1)
    rel = qi + WINDOW - kj
    bias = jnp.where((rel >= 0) & (rel < WINDOW), 0.0, NEG_INF).astype(F32)
    bias_first = jnp.where(s > 0, bias, jnp.where(kj < WINDOW, NEG_INF, bias))
    scale = HEAD_DIM ** -0.5
    for i in range(nblk):
        qs = slice(i * WINDOW, (i + 1) * WINDOW)
        heads = [None] * N_Q_HEADS
        for g in range(N_KV_HEADS):
            ds = slice(g * HEAD_DIM, (g + 1) * HEAD_DIM)
            kb = kext_ref[i * WINDOW:(i + 2) * WINDOW, ds]
            vb = vext_ref[i * WINDOW:(i + 2) * WINDOW, ds]
            q4 = jnp.concatenate(
                [z[qs, (g * Q_PER_KV + r) * HEAD_DIM:(g * Q_PER_KV + r + 1) * HEAD_DIM]
                 for r in range(Q_PER_KV)], axis=0).astype(BF16)
            logits = _dot_nt(q4, kb) * scale + (bias_first if i == 0 else bias)
            sink_col = jnp.concatenate(
                [jnp.full((WINDOW, 1), sinks_ref[g * Q_PER_KV + r], F32) for r in range(Q_PER_KV)], axis=0)
            o4 = _sink_softmax_pv(logits, sink_col, vb)
            for r in range(Q_PER_KV):
                heads[g * Q_PER_KV + r] = o4[r * WINDOW:(r + 1) * WINDOW, :]
        attn_o_ref[qs, :] = jnp.concatenate(heads, axis=1).astype(BF16)

    kext_ref[0:WINDOW, :] = kext_ref[ts:ts + WINDOW, :]
    vext_ref[0:WINDOW, :] = vext_ref[ts:ts + WINDOW, :]

    merged = (_sigmoid(z[:, _OFF_GA:_OFF_GB]) * _dot(sgu_o_ref[...], w_bs_ref[...])
              + _sigmoid(z[:, _OFF_GB:]) * _dot(attn_o_ref[...], w_ba_ref[...]))
    y = _dot(merged.astype(BF16), w_out_ref[...])
    h_ref[...] = _layer_norm(alpha * x + y, ln1_g_ref[...], ln1_b_ref[...])


def _full(shape):
    return pl.BlockSpec(shape, lambda *_: (0,) * len(shape))


def _prompt_mixer(x, p, alpha, ts=PROMPT_BLOCK):
    B, S, D = x.shape
    ts = min(ts, S)
    assert S % ts == 0 and ts % WINDOW == 0 and ts % CHUNK == 0
    grid = (B, S // ts)
    smem = pl.BlockSpec(memory_space=pltpu.SMEM)
    in_specs = [
        smem,
        pl.BlockSpec((None, ts, D), lambda b, s: (b, s, 0)),
        _full((D, D_IN)), _full((1, D_SGU)), _full((1, D_SGU)),
        _full((N_SGU_GROUPS, CHUNK, CHUNK)), _full((CHUNK, N_SGU_GROUPS)),
        _full((D_SGU, D)), _full((D_ATTN, D)), _full((D, D)), _full((1, D)), _full((1, D)),
    ]
    out_specs = [
        pl.BlockSpec((None, ts, D), lambda b, s: (b, s, 0)),
        pl.BlockSpec((None, WINDOW, D_KV), lambda b, s: (b, 0, 0)),
        pl.BlockSpec((None, WINDOW, D_KV), lambda b, s: (b, 0, 0)),
        pl.BlockSpec((None, CHUNK, D_SGU), lambda b, s: (b, 0, 0)),
    ]
    out_shape = [
        jax.ShapeDtypeStruct((B, S, D), F32),
        jax.ShapeDtypeStruct((B, WINDOW, D_KV), F32),
        jax.ShapeDtypeStruct((B, WINDOW, D_KV), F32),
        jax.ShapeDtypeStruct((B, CHUNK, D_SGU), F32),
    ]
    scratch = [
        pltpu.VMEM((ts + WINDOW, D_KV), BF16),
        pltpu.VMEM((ts + WINDOW, D_KV), BF16),
        pltpu.VMEM((ts, D_SGU), BF16),
        pltpu.VMEM((ts, D_ATTN), BF16),
    ]
    return pl.pallas_call(
        functools.partial(_prompt_mixer_kernel, alpha=alpha, ts=ts),
        grid=grid, in_specs=in_specs, out_specs=out_specs, out_shape=out_shape,
        scratch_shapes=scratch, name="prompt_mixer",
        compiler_params=pltpu.CompilerParams(
            dimension_semantics=("arbitrary", "arbitrary"), vmem_limit_bytes=VMEM_LIMIT_BYTES),
    )(p["sinks"], x, p["w_in"], p["sgu_ln_g"], p["sgu_ln_b"], p["sgu_w"], p["sgu_bcol"],
      p["w_bs"], p["w_ba"], p["w_out"], p["ln1_g"], p["ln1_b"])


def _sample_pre_kernel(sgu_w_ref, sgu_b_ref, x_ref, w_in_ref, sgu_g_ref, sgu_bb_ref, w_bs_ref,
                       q_ref, k_ref, v_ref, sv_ref, part_ref, gate_b_ref, *, t_new):
    x = x_ref[...]
    n = x.shape[0]
    z = _dot(x.astype(BF16), w_in_ref[...])
    q_ref[...] = z[:, _OFF_Q:_OFF_K]
    k_ref[...] = z[:, _OFF_K:_OFF_V]
    v_ref[...] = z[:, _OFF_V:_OFF_U]
    u = _gelu(z[:, _OFF_U:_OFF_SV])
    svn = _layer_norm(_gelu(z[:, _OFF_SV:_OFF_GA]), sgu_g_ref[...], sgu_bb_ref[...])
    sv_ref[...] = svn
    t_of_row = lax.broadcasted_iota(jnp.int32, (n, SGU_GROUP_DIM), 0) % t_new
    pieces = []
    for g in range(N_SGU_GROUPS):
        vg = svn[:, g * SGU_GROUP_DIM:(g + 1) * SGU_GROUP_DIM]
        mix = jnp.zeros((n, SGU_GROUP_DIM), F32)
        for t in range(t_new):
            mix = jnp.where(t_of_row == t, sgu_b_ref[g, t], mix)
        for d in range(t_new):
            coef = jnp.zeros((n, SGU_GROUP_DIM), F32)
            for t in range(d, t_new):
                coef = jnp.where(t_of_row == t, sgu_w_ref[g, t * t_new + t - d], coef)
            shifted = vg if d == 0 else pltpu.roll(vg, d, 0)
            mix = mix + coef * shifted
        pieces.append(u[:, g * SGU_GROUP_DIM:(g + 1) * SGU_GROUP_DIM] * mix)
    sgu_o = jnp.concatenate(pieces, axis=1).astype(BF16)
    part_ref[...] = _sigmoid(z[:, _OFF_GA:_OFF_GB]) * _dot(sgu_o, w_bs_ref[...])
    gate_b_ref[...] = _sigmoid(z[:, _OFF_GB:])


def _sample_pre(x2d, p, t_new):
    n, D = x2d.shape
    smem = pl.BlockSpec(memory_space=pltpu.SMEM)
    out_shape = [
        jax.ShapeDtypeStruct((n, D_ATTN), F32),
        jax.ShapeDtypeStruct((n, D_KV), F32),
        jax.ShapeDtypeStruct((n, D_KV), F32),
        jax.ShapeDtypeStruct((n, D_SGU), F32),
        jax.ShapeDtypeStruct((n, D), F32),
        jax.ShapeDtypeStruct((n, D), F32),
    ]
    return pl.pallas_call(
        functools.partial(_sample_pre_kernel, t_new=t_new),
        in_specs=[smem, smem] + [pl.BlockSpec(memory_space=pltpu.VMEM)] * 5,
        out_specs=[pl.BlockSpec(memory_space=pltpu.VMEM)] * 6,
        out_shape=out_shape, name="sample_pre",
        compiler_params=pltpu.CompilerParams(vmem_limit_bytes=VMEM_LIMIT_BYTES),
    )(p["sgu_w"][:, :t_new, :t_new].reshape(N_SGU_GROUPS, t_new * t_new), p["sgu_b"],
      x2d, p["w_in"], p["sgu_ln_g"], p["sgu_ln_b"], p["w_bs"])


def _sample_attn_kernel(sinks_ref, q_ref, kn_ref, vn_ref, ck_ref, cv_ref,
                        o_ref, nk_ref, nv_ref, kbuf_ref, vbuf_ref, *, t_new, bb):
    nq = Q_PER_KV * t_new
    nkeys = WINDOW + 8
    row = lax.broadcasted_iota(jnp.int32, (nq, nkeys), 0)
    kj = lax.broadcasted_iota(jnp.int32, (nq, nkeys), 1)
    t_q = row % t_new
    bias = jnp.where((kj > t_q) & (kj <= t_q + WINDOW), 0.0, NEG_INF).astype(F32)
    r_of_row = lax.broadcasted_iota(jnp.int32, (nq, 1), 0) // t_new
    scale = HEAD_DIM ** -0.5
    pad = jnp.zeros((8 - t_new, D_KV), F32)
    for b in range(bb):
        kbuf_ref[0:WINDOW, :] = ck_ref[b]
        kbuf_ref[WINDOW:WINDOW + t_new, :] = kn_ref[b]
        kbuf_ref[WINDOW + t_new:, :] = pad
        vbuf_ref[0:WINDOW, :] = cv_ref[b]
        vbuf_ref[WINDOW:WINDOW + t_new, :] = vn_ref[b]
        vbuf_ref[WINDOW + t_new:, :] = pad
        nk_ref[b] = kbuf_ref[t_new:t_new + WINDOW, :]
        nv_ref[b] = vbuf_ref[t_new:t_new + WINDOW, :]
        for g in range(N_KV_HEADS):
            ds = slice(g * HEAD_DIM, (g + 1) * HEAD_DIM)
            kg = kbuf_ref[:, ds].astype(BF16)
            vg = vbuf_ref[:, ds].astype(BF16)
            logits = _dot_nt(q_ref[b, g].astype(BF16), kg) * scale + bias
            sink_col = jnp.zeros((nq, 1), F32)
            for r in range(Q_PER_KV):
                sink_col = jnp.where(r_of_row == r, sinks_ref[g * Q_PER_KV + r], sink_col)
            o_ref[b, g] = _sink_softmax_pv(logits, sink_col, vg)


def _sample_attn(q_grouped, k_new, v_new, cache_k, cache_v, sinks, t_new, bb=SAMPLE_ATTN_BATCH):
    nb = q_grouped.shape[0]
    bb = min(bb, nb)
    assert nb % bb == 0 and t_new <= 8
    nq = Q_PER_KV * t_new
    smem = pl.BlockSpec(memory_space=pltpu.SMEM)
    in_specs = [
        smem,
        pl.BlockSpec((bb, N_KV_HEADS, nq, HEAD_DIM), lambda i: (i, 0, 0, 0)),
        pl.BlockSpec((bb, t_new, D_KV), lambda i: (i, 0, 0)),
        pl.BlockSpec((bb, t_new, D_KV), lambda i: (i, 0, 0)),
        pl.BlockSpec((bb, WINDOW, D_KV), lambda i: (i, 0, 0)),
        pl.BlockSpec((bb, WINDOW, D_KV), lambda i: (i, 0, 0)),
    ]
    out_specs = [
        pl.BlockSpec((bb, N_KV_HEADS, nq, HEAD_DIM), lambda i: (i, 0, 0, 0)),
        pl.BlockSpec((bb, WINDOW, D_KV), lambda i: (i, 0, 0)),
        pl.BlockSpec((bb, WINDOW, D_KV), lambda i: (i, 0, 0)),
    ]
    out_shape = [
        jax.ShapeDtypeStruct((nb, N_KV_HEADS, nq, HEAD_DIM), F32),
        jax.ShapeDtypeStruct((nb, WINDOW, D_KV), F32),
        jax.ShapeDtypeStruct((nb, WINDOW, D_KV), F32),
    ]
    return pl.pallas_call(
        functools.partial(_sample_attn_kernel, t_new=t_new, bb=bb),
        grid=(nb // bb,), in_specs=in_specs, out_specs=out_specs, out_shape=out_shape,
        scratch_shapes=[pltpu.VMEM((WINDOW + 8, D_KV), F32), pltpu.VMEM((WINDOW + 8, D_KV), F32)],
        name="sample_attn",
        compiler_params=pltpu.CompilerParams(dimension_semantics=("arbitrary",)),
    )(sinks, q_grouped, k_new, v_new, cache_k, cache_v)


def _sample_post_kernel(x_ref, part_ref, gate_b_ref, o_ref, w_ba_ref, w_out_ref, ln1_g_ref, ln1_b_ref,
                        h_ref, *, alpha):
    merged = part_ref[...] + gate_b_ref[...] * _dot(o_ref[...].astype(BF16), w_ba_ref[...])
    y = _dot(merged.astype(BF16), w_out_ref[...])
    h_ref[...] = _layer_norm(alpha * x_ref[...] + y, ln1_g_ref[...], ln1_b_ref[...])


def _sample_post(x2d, part, gate_b, o2d, p, alpha):
    return pl.pallas_call(
        functools.partial(_sample_post_kernel, alpha=alpha),
        in_specs=[pl.BlockSpec(memory_space=pltpu.VMEM)] * 8,
        out_specs=pl.BlockSpec(memory_space=pltpu.VMEM),
        out_shape=jax.ShapeDtypeStruct(x2d.shape, F32), name="sample_post",
        compiler_params=pltpu.CompilerParams(vmem_limit_bytes=VMEM_LIMIT_BYTES),
    )(x2d, part, gate_b, o2d, p["w_ba"], p["w_out"], p["ln1_g"], p["ln1_b"])


def _top_values(work, count, with_rank=False):
    vals = []
    rank = jnp.full(work.shape, float(count), F32) if with_rank else None
    for i in range(count):
        top = jnp.max(work, axis=0, keepdims=True)
        vals.append(top)
        hit = work == top
        if with_rank:
            rank = jnp.where(hit, float(i), rank)
        if i + 1 < count:
            work = jnp.where(hit, -jnp.inf, work)
    return (vals, rank) if with_rank else vals


def _candidate_rows(a_rows, b_rows):
    assert PEER_TOPK == 16
    a_all = jnp.concatenate(a_rows, axis=0)
    a_low = a_all[0:8]
    row = lax.broadcasted_iota(jnp.int32, a_low.shape, 0)

    def pair(j):
        return a_low + b_rows[j]

    def corner(j):
        return a_rows[0] + b_rows[j]

    t3 = jnp.where(row >= 5, pltpu.roll(pair(4), 5, 0), pair(2))
    t4 = jnp.where(row >= 6, pltpu.roll(pair(6), 6, 0),
                   jnp.where(row >= 4, pltpu.roll(pair(5), 4, 0), pair(3)))
    t5 = pair(7)
    for k in range(6):
        t5 = jnp.where(row >= 2 + k, corner(8 + k), t5)
    t6 = jnp.where(row >= 2, -jnp.inf, jnp.where(row >= 1, corner(15), corner(14)))
    return jnp.concatenate([a_all + b_rows[0], pair(1), t3, t4, t5, t6], axis=0)


def _bf16_rows(row, rows):
    return jnp.broadcast_to(row.astype(BF16), (rows, row.shape[1]))


def _peer_select(s1, s2):
    a_rows = _top_values(s1, PEER_TOPK)
    b_rows, rank2 = _top_values(s2, PEER_TOPK, with_rank=True)
    best = _top_values(_candidate_rows(a_rows, b_rows), PEER_TOPK)
    norm = jnp.zeros_like(best[0])
    for i in range(PEER_TOPK):
        norm = norm + jnp.exp(best[i] - best[0])
    count = jnp.zeros_like(s1)
    for jj in range(PEER_TOPK):
        count = jnp.where(s1 + b_rows[jj] >= best[PEER_TOPK - 1], float(jj + 1), count)
    e1 = jnp.exp(s1 - a_rows[0]) / norm
    e2 = jnp.exp(s2 - b_rows[0])
    return rank2, e2, count, e1


def _peer_gate_tile(act_ref, coef_ref, first_n1, n, t, sel_refs):
    rank2_ref, e2_ref, count_ref, e1_ref = sel_refs
    ls = slice(t * LANES, (t + 1) * LANES)
    rs = slice(n * N_SUBKEYS, (n + 1) * N_SUBKEYS)
    n1 = first_n1 + n
    gate = jnp.zeros((N_SUBKEYS, LANES), BF16)
    for hd in range(PEER_HEADS):
        count = _bf16_rows(count_ref[hd, t, pl.ds(n1, 1), :], N_SUBKEYS)
        e1 = _bf16_rows(e1_ref[hd, t, pl.ds(n1, 1), :], N_SUBKEYS)
        gate = gate + jnp.where(rank2_ref[hd, t] < count, e2_ref[hd, t], jnp.zeros((), BF16)) * e1
    coef_ref[rs, ls] = gate * _gelu(act_ref[rs, ls])


def _peer_kernel(h_ref, wq_ref, sk_ref, u_ref, vt_ref, ln_g_ref, ln_b_ref, y_ref,
                 ht_ref, qt_ref, rank2_ref, e2_ref, count_ref, e1_ref, acc_ref,
                 act0_ref, act1_ref, coef0_ref, coef1_ref, *, alpha, tb, se):
    j = pl.program_id(1)
    ntb = tb // LANES
    nblocks = N_EXPERTS // se
    sel_refs = (rank2_ref, e2_ref, count_ref, e1_ref)
    act_refs = (act0_ref, act1_ref)
    coef_refs = (coef0_ref, coef1_ref)

    @pl.when(j == 0)
    def _():
        ht = h_ref[...].T.astype(BF16)
        ht_ref[...] = ht
        qt_ref[...] = _dot(wq_ref[...], ht).astype(BF16)

        def per_head(hd, carry):
            for t in range(ntb):
                ls = slice(t * LANES, (t + 1) * LANES)
                s = []
                for c in range(2):
                    r0 = pl.multiple_of((hd * 2 + c) * PEER_HALF, PEER_HALF)
                    s.append(_dot(sk_ref[hd * 2 + c], qt_ref[pl.ds(r0, PEER_HALF), ls]))
                rank2, e2, count, e1 = _peer_select(s[0], s[1])
                rank2_ref[hd, t] = rank2.astype(BF16)
                e2_ref[hd, t] = e2.astype(BF16)
                count_ref[hd, t] = count
                e1_ref[hd, t] = e1
            return carry

        lax.fori_loop(0, PEER_HEADS, per_head, 0)
        acc_ref[...] = jnp.zeros_like(acc_ref)

    def step_body(q, with_act, with_gate, with_combine):
        half_t, half_e, half_d = tb // 2, se // 2, acc_ref.shape[0] // 2
        pieces = []
        for hs in (slice(0, half_t), slice(half_t, tb)):
            for lo in range(2):
                if with_act:
                    pieces.append(("act", hs, slice(lo * half_e, (lo + 1) * half_e)))
                if with_combine:
                    pieces.append(("combine", hs, slice(lo * half_d, (lo + 1) * half_d)))
        tiles = [(n, t) for t in range(ntb) for n in range(se // N_SUBKEYS)] if with_gate else []
        per_piece = -(-len(tiles) // max(len(pieces), 1))
        for k, (kind, hs, es) in enumerate(pieces):
            for n, t in tiles[k * per_piece:(k + 1) * per_piece]:
                _peer_gate_tile(act_refs[1 - q], coef_refs[1 - q], (j - 1) * (se // N_SUBKEYS), n, t, sel_refs)
            if kind == "act":
                act_refs[q][es, hs] = _dot(u_ref[es, :], ht_ref[:, hs]).astype(BF16)
            else:
                acc_ref[es, hs] += _dot(vt_ref[es, :], coef_refs[q][:, hs])
        for n, t in tiles[len(pieces) * per_piece:]:
            _peer_gate_tile(act_refs[1 - q], coef_refs[1 - q], (j - 1) * (se // N_SUBKEYS), n, t, sel_refs)

    assert nblocks % 2 == 0 and nblocks >= 4
    pl.when(j == 0)(functools.partial(step_body, 0, True, False, False))
    pl.when(j == 1)(functools.partial(step_body, 1, True, True, False))
    for q in range(2):
        pl.when((j >= 2) & (j < nblocks) & (j % 2 == q))(functools.partial(step_body, q, True, True, True))
    pl.when(j == nblocks)(functools.partial(step_body, 0, False, True, True))
    pl.when(j == nblocks + 1)(functools.partial(step_body, 1, False, False, True))

    @pl.when(j == pl.num_programs(1) - 1)
    def _():
        y_ref[...] = _layer_norm(alpha * h_ref[...] + acc_ref[...].T, ln_g_ref[...], ln_b_ref[...])


def _peer(h2d, p, alpha, tb=PEER_TOKEN_BLOCK, se=PEER_EXPERT_BLOCK):
    n, D = h2d.shape
    tb = min(tb, n)
    assert n % tb == 0 and tb % LANES == 0 and N_EXPERTS % se == 0 and se % N_SUBKEYS == 0
    ntb = tb // LANES
    qdim = PEER_HEADS * PEER_KEY_DIM
    nblocks = N_EXPERTS // se
    in_specs = [
        pl.BlockSpec((tb, D), lambda i, j: (i, 0)),
        _full((qdim, D)),
        _full((PEER_HEADS * 2, N_SUBKEYS, PEER_HALF)),
        pl.BlockSpec((se, D), lambda i, j: (jnp.minimum(j, nblocks - 1), 0)),
        pl.BlockSpec((D, se), lambda i, j: (0, jnp.clip(j - 2, 0, nblocks - 1))),
        _full((1, D)), _full((1, D)),
    ]
    per_head = (PEER_HEADS, ntb, N_SUBKEYS, LANES)
    scratch = [
        pltpu.VMEM((D, tb), BF16),
        pltpu.VMEM((qdim, tb), BF16),
        pltpu.VMEM(per_head, BF16), pltpu.VMEM(per_head, BF16),
        pltpu.VMEM(per_head, F32), pltpu.VMEM(per_head, F32),
        pltpu.VMEM((D, tb), F32),
    ] + [pltpu.VMEM((se, tb), BF16)] * 4
    return pl.pallas_call(
        functools.partial(_peer_kernel, alpha=alpha, tb=tb, se=se),
        grid=(n // tb, nblocks + 2), in_specs=in_specs,
        out_specs=pl.BlockSpec((tb, D), lambda i, j: (i, 0)),
        out_shape=jax.ShapeDtypeStruct((n, D), F32), scratch_shapes=scratch, name="peer",
        compiler_params=pltpu.CompilerParams(
            dimension_semantics=("arbitrary", "arbitrary"), vmem_limit_bytes=VMEM_LIMIT_BYTES),
    )(h2d, p["peer_wq_t"], p["peer_sk"], p["peer_u"], p["peer_vt"], p["ln2_g"], p["ln2_b"])


def _layer_params(l, w_in, sinks, sgu_ln_g, sgu_ln_b, sgu_w, sgu_b, w_branch_sgu, w_branch_attn, w_out,
                  ln1_g, ln1_b, peer_w_q, peer_sub_keys, peer_u, peer_v, ln2_g, ln2_b):
    return {
        "w_in": w_in[l].astype(BF16),
        "sinks": sinks[l],
        "sgu_ln_g": sgu_ln_g[l][None, :], "sgu_ln_b": sgu_ln_b[l][None, :],
        "sgu_w": sgu_w[l],
        "sgu_b": sgu_b[l], "sgu_bcol": sgu_b[l].T,
        "w_bs": w_branch_sgu[l].astype(BF16), "w_ba": w_branch_attn[l].astype(BF16),
        "w_out": w_out[l].astype(BF16),
        "ln1_g": ln1_g[l][None, :], "ln1_b": ln1_b[l][None, :],
        "peer_wq_t": peer_w_q[l].T.astype(BF16),
        "peer_sk": peer_sub_keys[l].reshape(PEER_HEADS * 2, N_SUBKEYS, PEER_HALF).astype(BF16),
        "peer_u": peer_u[l].astype(BF16),
        "peer_vt": peer_v[l].T.astype(BF16),
        "ln2_g": ln2_g[l][None, :], "ln2_b": ln2_b[l][None, :],
    }


def _prompt_layer(x, p, alpha):
    B, S, D = x.shape
    h, k_last, v_last, sv_last = _prompt_mixer(x, p, alpha)
    y = _peer(h.reshape(B * S, D), p, alpha).reshape(B, S, D)
    return (y, k_last.reshape(B, WINDOW, N_KV_HEADS, HEAD_DIM), v_last.reshape(B, WINDOW, N_KV_HEADS, HEAD_DIM),
            sv_last.reshape(B, CHUNK, N_SGU_GROUPS, SGU_GROUP_DIM))


def _sample_layer(x, cache_k, cache_v, p, alpha):
    B, T, D = x.shape
    x2d = x.reshape(B * T, D)
    q, k_new, v_new, svn, part, gate_b = _sample_pre(x2d, p, T)
    qg = q.reshape(B, T, N_KV_HEADS, Q_PER_KV, HEAD_DIM).transpose(0, 2, 3, 1, 4)
    qg = qg.reshape(B, N_KV_HEADS, Q_PER_KV * T, HEAD_DIM)
    og, new_k, new_v = _sample_attn(qg, k_new.reshape(B, T, D_KV), v_new.reshape(B, T, D_KV),
                                    cache_k.reshape(B, WINDOW, D_KV), cache_v.reshape(B, WINDOW, D_KV),
                                    p["sinks"], T)
    o2d = og.reshape(B, N_KV_HEADS, Q_PER_KV, T, HEAD_DIM).transpose(0, 3, 1, 2, 4).reshape(B * T, D_ATTN)
    h = _sample_post(x2d, part, gate_b, o2d, p, alpha)
    y = _peer(h, p, alpha).reshape(B, T, D)
    return (y, new_k.reshape(B, WINDOW, N_KV_HEADS, HEAD_DIM), new_v.reshape(B, WINDOW, N_KV_HEADS, HEAD_DIM),
            svn.reshape(B, T, N_SGU_GROUPS, SGU_GROUP_DIM))


def kernel(x_prompt, x_sample, cache_k, cache_v, w_in, sinks, sgu_ln_g, sgu_ln_b, sgu_w, sgu_b, w_branch_sgu, w_branch_attn, w_out, ln1_g, ln1_b, peer_w_q, peer_sub_keys, peer_u, peer_v, ln2_g, ln2_b):
    depth = w_in.shape[0]
    alpha = (2.0 * depth) ** 0.25
    yp, ys = x_prompt, x_sample
    outs = [[] for _ in range(6)]
    for l in range(depth):
        p = _layer_params(l, w_in, sinks, sgu_ln_g, sgu_ln_b, sgu_w, sgu_b, w_branch_sgu, w_branch_attn,
                          w_out, ln1_g, ln1_b, peer_w_q, peer_sub_keys, peer_u, peer_v, ln2_g, ln2_b)
        yp, kp, vp, svp = _prompt_layer(yp, p, alpha)
        ys, ksm, vsm, svs = _sample_layer(ys, cache_k[l], cache_v[l], p, alpha)
        for acc, val in zip(outs, (kp, vp, ksm, vsm, svp, svs)):
            acc.append(val)
    return (yp, ys) + tuple(jnp.stack(o) for o in outs)
```

```python
import functools

import jax
import jax.numpy as jnp
from jax import lax
from jax.experimental import pallas as pl
from jax.experimental.pallas import tpu as pltpu

F32 = jnp.float32
BF16 = jnp.bfloat16

D_MODEL = 1024
N_Q_HEADS = 8
N_KV_HEADS = 2
HEAD_DIM = 64
Q_PER_KV = N_Q_HEADS // N_KV_HEADS
D_ATTN = N_Q_HEADS * HEAD_DIM
D_KV = N_KV_HEADS * HEAD_DIM
WINDOW = 128
CHUNK = 128
D_SGU = D_MODEL // 2
N_SGU_GROUPS = 4
SGU_GROUP_DIM = D_SGU // N_SGU_GROUPS
PEER_HEADS = 8
N_SUBKEYS = 128
N_EXPERTS = N_SUBKEYS * N_SUBKEYS
PEER_TOPK = 16
PEER_KEY_DIM = 256
PEER_HALF = PEER_KEY_DIM // 2
LN_EPS = 1e-5
NEG_INF = -1e30
D_IN = D_ATTN + 2 * D_KV + 2 * D_SGU + 2 * D_MODEL
_OFF_Q = 0
_OFF_K = D_ATTN
_OFF_V = _OFF_K + D_KV
_OFF_U = _OFF_V + D_KV
_OFF_SV = _OFF_U + D_SGU
_OFF_GA = _OFF_SV + D_SGU
_OFF_GB = _OFF_GA + D_MODEL

LANES = 128
VMEM_LIMIT_BYTES = 56 * 1024 * 1024

PROMPT_BLOCK = 512
SAMPLE_ATTN_BATCH = 8
PEER_TOKEN_BLOCK = 1024
PEER_EXPERT_BLOCK = 512
PEER_ROW_PIECES = 4


def _gelu(x):
    return 0.5 * x * (1.0 + jnp.tanh(0.7978845608028654 * (x + 0.044715 * (x * x * x))))


def _sigmoid(x):
    return 1.0 / (1.0 + jnp.exp(-x))


def _layer_norm(x, g, b):
    mu = jnp.mean(x, -1, keepdims=True)
    xc = x - mu
    var = jnp.mean(xc * xc, -1, keepdims=True)
    return xc * lax.rsqrt(var + LN_EPS) * g + b


def _dot(a, b):
    return jnp.dot(a, b, preferred_element_type=F32)


def _dot_nt(a, b):
    return lax.dot_general(a, b, (((1,), (1,)), ((), ())), preferred_element_type=F32)


def _sink_softmax_pv(logits, sink_col, vals):
    m = jnp.maximum(jnp.max(logits, -1, keepdims=True), sink_col)
    p = jnp.exp(logits - m)
    denom = jnp.sum(p, -1, keepdims=True) + jnp.exp(sink_col - m)
    return _dot(p.astype(BF16), vals) / denom


def _prompt_mixer_kernel(sinks_ref, x_ref, w_in_ref, sgu_g_ref, sgu_b_ref, sgu_w_ref, sgu_bcol_ref,
                         w_bs_ref, w_ba_ref, w_out_ref, ln1_g_ref, ln1_b_ref,
                         h_ref, k_ref, v_ref, sv_ref,
                         kext_ref, vext_ref, sgu_o_ref, attn_o_ref, *, alpha, ts):
    s = pl.program_id(1)
    last = pl.num_programs(1) - 1
    nblk = ts // WINDOW

    x = x_ref[...]
    z = _dot(x.astype(BF16), w_in_ref[...])
    k = z[:, _OFF_K:_OFF_V]
    v = z[:, _OFF_V:_OFF_U]
    u = _gelu(z[:, _OFF_U:_OFF_SV])
    svn = _layer_norm(_gelu(z[:, _OFF_SV:_OFF_GA]), sgu_g_ref[...], sgu_b_ref[...])

    @pl.when(s == 0)
    def _():
        kext_ref[0:WINDOW, :] = jnp.zeros((WINDOW, D_KV), BF16)
        vext_ref[0:WINDOW, :] = jnp.zeros((WINDOW, D_KV), BF16)

    kext_ref[WINDOW:, :] = k.astype(BF16)
    vext_ref[WINDOW:, :] = v.astype(BF16)

    @pl.when(s == last)
    def _():
        k_ref[...] = k[ts - WINDOW:, :]
        v_ref[...] = v[ts - WINDOW:, :]
        sv_ref[...] = svn[ts - CHUNK:, :]

    row = lax.broadcasted_iota(jnp.int32, (CHUNK, CHUNK), 0)
    col = lax.broadcasted_iota(jnp.int32, (CHUNK, CHUNK), 1)
    for g in range(N_SGU_GROUPS):
        wt = jnp.where(col <= row, sgu_w_ref[g], 0.0).astype(BF16)
        bcol = sgu_bcol_ref[:, g:g + 1]
        gs = slice(g * SGU_GROUP_DIM, (g + 1) * SGU_GROUP_DIM)
        for c in range(ts // CHUNK):
            rs = slice(c * CHUNK, (c + 1) * CHUNK)
            mix = _dot(wt, svn[rs, gs].astype(BF16)) + bcol
            sgu_o_ref[rs, gs] = (u[rs, gs] * mix).astype(BF16)

    nq = Q_PER_KV * WINDOW
    qi = lax.broadcasted_iota(jnp.int32, (nq, 2 * WINDOW), 0) % WINDOW
    kj = lax.broadcasted_iota(jnp.int32, (nq, 2 * WINDOW), 1)
    rel = qi + WINDOW - kj
    bias = jnp.where((rel >= 0) & (rel < WINDOW), 0.0, NEG_INF).astype(F32)
    bias_first = jnp.where(s > 0, bias, jnp.where(kj < WINDOW, NEG_INF, bias))
    scale = HEAD_DIM ** -0.5
    for i in range(nblk):
        qs = slice(i * WINDOW, (i + 1) * WINDOW)
        heads = [None] * N_Q_HEADS
        for g in range(N_KV_HEADS):
            ds = slice(g * HEAD_DIM, (g + 1) * HEAD_DIM)
            kb = kext_ref[i * WINDOW:(i + 2) * WINDOW, ds]
            vb = vext_ref[i * WINDOW:(i + 2) * WINDOW, ds]
            q4 = jnp.concatenate(
                [z[qs, (g * Q_PER_KV + r) * HEAD_DIM:(g * Q_PER_KV + r + 1) * HEAD_DIM]
                 for r in range(Q_PER_KV)], axis=0).astype(BF16)
            logits = _dot_nt(q4, kb) * scale + (bias_first if i == 0 else bias)
            sink_col = jnp.concatenate(
                [jnp.full((WINDOW, 1), sinks_ref[g * Q_PER_KV + r], F32) for r in range(Q_PER_KV)], axis=0)
            o4 = _sink_softmax_pv(logits, sink_col, vb)
            for r in range(Q_PER_KV):
                heads[g * Q_PER_KV + r] = o4[r * WINDOW:(r + 1) * WINDOW, :]
        attn_o_ref[qs, :] = jnp.concatenate(heads, axis=1).astype(BF16)

    kext_ref[0:WINDOW, :] = kext_ref[ts:ts + WINDOW, :]
    vext_ref[0:WINDOW, :] = vext_ref[ts:ts + WINDOW, :]

    merged = (_sigmoid(z[:, _OFF_GA:_OFF_GB]) * _dot(sgu_o_ref[...], w_bs_ref[...])
              + _sigmoid(z[:, _OFF_GB:]) * _dot(attn_o_ref[...], w_ba_ref[...]))
    y = _dot(merged.astype(BF16), w_out_ref[...])
    h_ref[...] = _layer_norm(alpha * x + y, ln1_g_ref[...], ln1_b_ref[...])


def _full(shape):
    return pl.BlockSpec(shape, lambda *_: (0,) * len(shape), pipeline_mode=pl.Buffered(1))


def _prompt_mixer(x, p, alpha, ts=PROMPT_BLOCK):
    B, S, D = x.shape
    ts = min(ts, S)
    assert S % ts == 0 and ts % WINDOW == 0 and ts % CHUNK == 0
    grid = (B, S // ts)
    smem = pl.BlockSpec(memory_space=pltpu.SMEM)
    in_specs = [
        smem,
        pl.BlockSpec((None, ts, D), lambda b, s: (b, s, 0)),
        _full((D, D_IN)), _full((1, D_SGU)), _full((1, D_SGU)),
        _full((N_SGU_GROUPS, CHUNK, CHUNK)), _full((CHUNK, N_SGU_GROUPS)),
        _full((D_SGU, D)), _full((D_ATTN, D)), _full((D, D)), _full((1, D)), _full((1, D)),
    ]
    out_specs = [
        pl.BlockSpec((None, ts, D), lambda b, s: (b, s, 0)),
        pl.BlockSpec((None, WINDOW, D_KV), lambda b, s: (b, 0, 0)),
        pl.BlockSpec((None, WINDOW, D_KV), lambda b, s: (b, 0, 0)),
        pl.BlockSpec((None, CHUNK, D_SGU), lambda b, s: (b, 0, 0)),
    ]
    out_shape = [
        jax.ShapeDtypeStruct((B, S, D), F32),
        jax.ShapeDtypeStruct((B, WINDOW, D_KV), F32),
        jax.ShapeDtypeStruct((B, WINDOW, D_KV), F32),
        jax.ShapeDtypeStruct((B, CHUNK, D_SGU), F32),
    ]
    scratch = [
        pltpu.VMEM((ts + WINDOW, D_KV), BF16),
        pltpu.VMEM((ts + WINDOW, D_KV), BF16),
        pltpu.VMEM((ts, D_SGU), BF16),
        pltpu.VMEM((ts, D_ATTN), BF16),
    ]
    return pl.pallas_call(
        functools.partial(_prompt_mixer_kernel, alpha=alpha, ts=ts),
        grid=grid, in_specs=in_specs, out_specs=out_specs, out_shape=out_shape,
        scratch_shapes=scratch, name="prompt_mixer",
        compiler_params=pltpu.CompilerParams(
            dimension_semantics=("arbitrary", "arbitrary"), vmem_limit_bytes=VMEM_LIMIT_BYTES),
    )(p["sinks"], x, p["w_in"], p["sgu_ln_g"], p["sgu_ln_b"], p["sgu_w"], p["sgu_bcol"],
      p["w_bs"], p["w_ba"], p["w_out"], p["ln1_g"], p["ln1_b"])


def _sample_pre_kernel(sgu_w_ref, sgu_b_ref, x_ref, w_in_ref, sgu_g_ref, sgu_bb_ref, w_bs_ref,
                       q_ref, k_ref, v_ref, sv_ref, part_ref, gate_b_ref, *, t_new):
    x = x_ref[...]
    n = x.shape[0]
    z = _dot(x.astype(BF16), w_in_ref[...])
    q_ref[...] = z[:, _OFF_Q:_OFF_K]
    k_ref[...] = z[:, _OFF_K:_OFF_V]
    v_ref[...] = z[:, _OFF_V:_OFF_U]
    u = _gelu(z[:, _OFF_U:_OFF_SV])
    svn = _layer_norm(_gelu(z[:, _OFF_SV:_OFF_GA]), sgu_g_ref[...], sgu_bb_ref[...])
    sv_ref[...] = svn
    t_of_row = lax.broadcasted_iota(jnp.int32, (n, SGU_GROUP_DIM), 0) % t_new
    pieces = []
    for g in range(N_SGU_GROUPS):
        vg = svn[:, g * SGU_GROUP_DIM:(g + 1) * SGU_GROUP_DIM]
        mix = jnp.zeros((n, SGU_GROUP_DIM), F32)
        for t in range(t_new):
            mix = jnp.where(t_of_row == t, sgu_b_ref[g, t], mix)
        for d in range(t_new):
            coef = jnp.zeros((n, SGU_GROUP_DIM), F32)
            for t in range(d, t_new):
                coef = jnp.where(t_of_row == t, sgu_w_ref[g, t * t_new + t - d], coef)
            shifted = vg if d == 0 else pltpu.roll(vg, d, 0)
            mix = mix + coef * shifted
        pieces.append(u[:, g * SGU_GROUP_DIM:(g + 1) * SGU_GROUP_DIM] * mix)
    sgu_o = jnp.concatenate(pieces, axis=1).astype(BF16)
    part_ref[...] = _sigmoid(z[:, _OFF_GA:_OFF_GB]) * _dot(sgu_o, w_bs_ref[...])
    gate_b_ref[...] = _sigmoid(z[:, _OFF_GB:])


def _sample_pre(x2d, p, t_new):
    n, D = x2d.shape
    smem = pl.BlockSpec(memory_space=pltpu.SMEM)
    out_shape = [
        jax.ShapeDtypeStruct((n, D_ATTN), F32),
        jax.ShapeDtypeStruct((n, D_KV), F32),
        jax.ShapeDtypeStruct((n, D_KV), F32),
        jax.ShapeDtypeStruct((n, D_SGU), F32),
        jax.ShapeDtypeStruct((n, D), F32),
        jax.ShapeDtypeStruct((n, D), F32),
    ]
    return pl.pallas_call(
        functools.partial(_sample_pre_kernel, t_new=t_new),
        in_specs=[smem, smem] + [pl.BlockSpec(memory_space=pltpu.VMEM)] * 5,
        out_specs=[pl.BlockSpec(memory_space=pltpu.VMEM)] * 6,
        out_shape=out_shape, name="sample_pre",
        compiler_params=pltpu.CompilerParams(vmem_limit_bytes=VMEM_LIMIT_BYTES),
    )(p["sgu_w"][:, :t_new, :t_new].reshape(N_SGU_GROUPS, t_new * t_new), p["sgu_b"],
      x2d, p["w_in"], p["sgu_ln_g"], p["sgu_ln_b"], p["w_bs"])


def _sample_attn_kernel(sinks_ref, q_ref, kn_ref, vn_ref, ck_ref, cv_ref,
                        o_ref, nk_ref, nv_ref, kbuf_ref, vbuf_ref, *, t_new, bb):
    nq = Q_PER_KV * t_new
    nkeys = WINDOW + 8
    row = lax.broadcasted_iota(jnp.int32, (nq, nkeys), 0)
    kj = lax.broadcasted_iota(jnp.int32, (nq, nkeys), 1)
    t_q = row % t_new
    bias = jnp.where((kj > t_q) & (kj <= t_q + WINDOW), 0.0, NEG_INF).astype(F32)
    r_of_row = lax.broadcasted_iota(jnp.int32, (nq, 1), 0) // t_new
    scale = HEAD_DIM ** -0.5
    pad = jnp.zeros((8 - t_new, D_KV), F32)
    for b in range(bb):
        kbuf_ref[0:WINDOW, :] = ck_ref[b]
        kbuf_ref[WINDOW:WINDOW + t_new, :] = kn_ref[b]
        kbuf_ref[WINDOW + t_new:, :] = pad
        vbuf_ref[0:WINDOW, :] = cv_ref[b]
        vbuf_ref[WINDOW:WINDOW + t_new, :] = vn_ref[b]
        vbuf_ref[WINDOW + t_new:, :] = pad
        nk_ref[b] = kbuf_ref[t_new:t_new + WINDOW, :]
        nv_ref[b] = vbuf_ref[t_new:t_new + WINDOW, :]
        for g in range(N_KV_HEADS):
            ds = slice(g * HEAD_DIM, (g + 1) * HEAD_DIM)
            kg = kbuf_ref[:, ds].astype(BF16)
            vg = vbuf_ref[:, ds].astype(BF16)
            logits = _dot_nt(q_ref[b, g].astype(BF16), kg) * scale + bias
            sink_col = jnp.zeros((nq, 1), F32)
            for r in range(Q_PER_KV):
                sink_col = jnp.where(r_of_row == r, sinks_ref[g * Q_PER_KV + r], sink_col)
            o_ref[b, g] = _sink_softmax_pv(logits, sink_col, vg)


def _sample_attn(q_grouped, k_new, v_new, cache_k, cache_v, sinks, t_new, bb=SAMPLE_ATTN_BATCH):
    nb = q_grouped.shape[0]
    bb = min(bb, nb)
    assert nb % bb == 0 and t_new <= 8
    nq = Q_PER_KV * t_new
    smem = pl.BlockSpec(memory_space=pltpu.SMEM)
    in_specs = [
        smem,
        pl.BlockSpec((bb, N_KV_HEADS, nq, HEAD_DIM), lambda i: (i, 0, 0, 0)),
        pl.BlockSpec((bb, t_new, D_KV), lambda i: (i, 0, 0)),
        pl.BlockSpec((bb, t_new, D_KV), lambda i: (i, 0, 0)),
        pl.BlockSpec((bb, WINDOW, D_KV), lambda i: (i, 0, 0)),
        pl.BlockSpec((bb, WINDOW, D_KV), lambda i: (i, 0, 0)),
    ]
    out_specs = [
        pl.BlockSpec((bb, N_KV_HEADS, nq, HEAD_DIM), lambda i: (i, 0, 0, 0)),
        pl.BlockSpec((bb, WINDOW, D_KV), lambda i: (i, 0, 0)),
        pl.BlockSpec((bb, WINDOW, D_KV), lambda i: (i, 0, 0)),
    ]
    out_shape = [
        jax.ShapeDtypeStruct((nb, N_KV_HEADS, nq, HEAD_DIM), F32),
        jax.ShapeDtypeStruct((nb, WINDOW, D_KV), F32),
        jax.ShapeDtypeStruct((nb, WINDOW, D_KV), F32),
    ]
    return pl.pallas_call(
        functools.partial(_sample_attn_kernel, t_new=t_new, bb=bb),
        grid=(nb // bb,), in_specs=in_specs, out_specs=out_specs, out_shape=out_shape,
        scratch_shapes=[pltpu.VMEM((WINDOW + 8, D_KV), F32), pltpu.VMEM((WINDOW + 8, D_KV), F32)],
        name="sample_attn",
        compiler_params=pltpu.CompilerParams(dimension_semantics=("arbitrary",)),
    )(sinks, q_grouped, k_new, v_new, cache_k, cache_v)


def _sample_post_kernel(x_ref, part_ref, gate_b_ref, o_ref, w_ba_ref, w_out_ref, ln1_g_ref, ln1_b_ref,
                        h_ref, *, alpha):
    merged = part_ref[...] + gate_b_ref[...] * _dot(o_ref[...].astype(BF16), w_ba_ref[...])
    y = _dot(merged.astype(BF16), w_out_ref[...])
    h_ref[...] = _layer_norm(alpha * x_ref[...] + y, ln1_g_ref[...], ln1_b_ref[...])


def _sample_post(x2d, part, gate_b, o2d, p, alpha):
    return pl.pallas_call(
        functools.partial(_sample_post_kernel, alpha=alpha),
        in_specs=[pl.BlockSpec(memory_space=pltpu.VMEM)] * 8,
        out_specs=pl.BlockSpec(memory_space=pltpu.VMEM),
        out_shape=jax.ShapeDtypeStruct(x2d.shape, F32), name="sample_post",
        compiler_params=pltpu.CompilerParams(vmem_limit_bytes=VMEM_LIMIT_BYTES),
    )(x2d, part, gate_b, o2d, p["w_ba"], p["w_out"], p["ln1_g"], p["ln1_b"])


def _top_values(work, count, with_rank=False):
    vals = []
    rank = jnp.full(work.shape, float(count), F32) if with_rank else None
    for i in range(count):
        top = jnp.max(work, axis=0, keepdims=True)
        vals.append(top)
        hit = work == top
        if with_rank:
            rank = jnp.where(hit, float(i), rank)
        if i + 1 < count:
            work = jnp.where(hit, -jnp.inf, work)
    return (vals, rank) if with_rank else vals


def _candidate_rows(a_rows, b_rows):
    assert PEER_TOPK == 16
    a_all = jnp.concatenate(a_rows, axis=0)
    a_low = a_all[0:8]
    row = lax.broadcasted_iota(jnp.int32, a_low.shape, 0)

    def pair(j):
        return a_low + b_rows[j]

    def corner(j):
        return a_rows[0] + b_rows[j]

    t3 = jnp.where(row >= 5, pltpu.roll(pair(4), 5, 0), pair(2))
    t4 = jnp.where(row >= 6, pltpu.roll(pair(6), 6, 0),
                   jnp.where(row >= 4, pltpu.roll(pair(5), 4, 0), pair(3)))
    t5 = pair(7)
    for k in range(6):
        t5 = jnp.where(row >= 2 + k, corner(8 + k), t5)
    t6 = jnp.where(row >= 2, -jnp.inf, jnp.where(row >= 1, corner(15), corner(14)))
    return jnp.concatenate([a_all + b_rows[0], pair(1), t3, t4, t5, t6], axis=0)


def _bf16_rows(row, rows):
    return jnp.broadcast_to(row.astype(BF16), (rows, row.shape[1]))


def _peer_select(s1, s2):
    a_rows = _top_values(s1, PEER_TOPK)
    b_rows, rank2 = _top_values(s2, PEER_TOPK, with_rank=True)
    best = _top_values(_candidate_rows(a_rows, b_rows), PEER_TOPK)
    norm = jnp.zeros_like(best[0])
    for i in range(PEER_TOPK):
        norm = norm + jnp.exp(best[i] - best[0])
    count = jnp.zeros_like(s1)
    for jj in range(PEER_TOPK):
        count = jnp.where(s1 + b_rows[jj] >= best[PEER_TOPK - 1], float(jj + 1), count)
    e1 = jnp.exp(s1 - a_rows[0]) / norm
    e2 = jnp.exp(s2 - b_rows[0])
    return rank2, e2, count, e1


def _peer_gate_tile(act_ref, coef_ref, first_n1, n, t, sel_refs):
    rank2_ref, e2_ref, count_ref, e1_ref = sel_refs
    ls = slice(t * LANES, (t + 1) * LANES)
    rs = slice(n * N_SUBKEYS, (n + 1) * N_SUBKEYS)
    n1 = first_n1 + n
    gate = jnp.zeros((N_SUBKEYS, LANES), BF16)
    for hd in range(PEER_HEADS):
        count = _bf16_rows(count_ref[hd, t, pl.ds(n1, 1), :], N_SUBKEYS)
        e1 = _bf16_rows(e1_ref[hd, t, pl.ds(n1, 1), :], N_SUBKEYS)
        gate = gate + jnp.where(rank2_ref[hd, t] < count, e2_ref[hd, t], jnp.zeros((), BF16)) * e1
    coef_ref[rs, ls] = gate * _gelu(act_ref[rs, ls])


def _peer_kernel(h_ref, wq_ref, sk_ref, u_ref, vt_ref, ln_g_ref, ln_b_ref, y_ref,
                 ht_ref, qt_ref, rank2_ref, e2_ref, count_ref, e1_ref, acc_ref,
                 act0_ref, act1_ref, coef0_ref, coef1_ref, *, alpha, tb, se):
    j = pl.program_id(1)
    ntb = tb // LANES
    nblocks = N_EXPERTS // se
    sel_refs = (rank2_ref, e2_ref, count_ref, e1_ref)
    act_refs = (act0_ref, act1_ref)
    coef_refs = (coef0_ref, coef1_ref)

    @pl.when(j == 0)
    def _():
        ht = h_ref[...].T.astype(BF16)
        ht_ref[...] = ht
        qt_ref[...] = _dot(wq_ref[...], ht).astype(BF16)

        def per_head(hd, carry):
            for t in range(ntb):
                ls = slice(t * LANES, (t + 1) * LANES)
                s = []
                for c in range(2):
                    r0 = pl.multiple_of((hd * 2 + c) * PEER_HALF, PEER_HALF)
                    s.append(_dot(sk_ref[hd * 2 + c], qt_ref[pl.ds(r0, PEER_HALF), ls]))
                rank2, e2, count, e1 = _peer_select(s[0], s[1])
                rank2_ref[hd, t] = rank2.astype(BF16)
                e2_ref[hd, t] = e2.astype(BF16)
                count_ref[hd, t] = count
                e1_ref[hd, t] = e1
            return carry

        lax.fori_loop(0, PEER_HEADS, per_head, 0)
        acc_ref[...] = jnp.zeros_like(acc_ref)

    def step_body(q, with_act, with_gate, with_combine):
        half_t, quarter_e, quarter_d = tb // 2, se // PEER_ROW_PIECES, acc_ref.shape[0] // PEER_ROW_PIECES
        pieces = []
        for hs in (slice(0, half_t), slice(half_t, tb)):
            for lo in range(PEER_ROW_PIECES):
                if with_act:
                    pieces.append(("act", hs, slice(lo * quarter_e, (lo + 1) * quarter_e)))
                if with_combine:
                    pieces.append(("combine", hs, slice(lo * quarter_d, (lo + 1) * quarter_d)))
        tiles = [(n, t) for t in range(ntb) for n in range(se // N_SUBKEYS)] if with_gate else []
        per_piece = -(-len(tiles) // max(len(pieces), 1))
        for k, (kind, hs, es) in enumerate(pieces):
            for n, t in tiles[k * per_piece:(k + 1) * per_piece]:
                _peer_gate_tile(act_refs[1 - q], coef_refs[1 - q], (j - 1) * (se // N_SUBKEYS), n, t, sel_refs)
            if kind == "act":
                act_refs[q][es, hs] = _dot(u_ref[es, :], ht_ref[:, hs]).astype(BF16)
            else:
                acc_ref[es, hs] += _dot(vt_ref[es, :], coef_refs[q][:, hs])
        for n, t in tiles[len(pieces) * per_piece:]:
            _peer_gate_tile(act_refs[1 - q], coef_refs[1 - q], (j - 1) * (se // N_SUBKEYS), n, t, sel_refs)

    assert nblocks % 2 == 0 and nblocks >= 4
    pl.when(j == 0)(functools.partial(step_body, 0, True, False, False))
    pl.when(j == 1)(functools.partial(step_body, 1, True, True, False))
    for q in range(2):
        pl.when((j >= 2) & (j < nblocks) & (j % 2 == q))(functools.partial(step_body, q, True, True, True))
    pl.when(j == nblocks)(functools.partial(step_body, 0, False, True, True))
    pl.when(j == nblocks + 1)(functools.partial(step_body, 1, False, False, True))

    @pl.when(j == pl.num_programs(1) - 1)
    def _():
        y_ref[...] = _layer_norm(alpha * h_ref[...] + acc_ref[...].T, ln_g_ref[...], ln_b_ref[...])


def _peer(h2d, p, alpha, tb=PEER_TOKEN_BLOCK):
    n, D = h2d.shape
    tb = min(tb, n)
    se = p["peer_vt"].shape[2]
    assert n % tb == 0 and tb % LANES == 0 and N_EXPERTS % se == 0 and se % N_SUBKEYS == 0
    ntb = tb // LANES
    qdim = PEER_HEADS * PEER_KEY_DIM
    nblocks = N_EXPERTS // se
    in_specs = [
        pl.BlockSpec((tb, D), lambda i, j: (i, 0)),
        _full((qdim, D)),
        _full((PEER_HEADS * 2, N_SUBKEYS, PEER_HALF)),
        pl.BlockSpec((se, D), lambda i, j: (jnp.minimum(j, nblocks - 1), 0)),
        pl.BlockSpec((None, D, se), lambda i, j: (jnp.clip(j - 2, 0, nblocks - 1), 0, 0)),
        _full((1, D)), _full((1, D)),
    ]
    per_head = (PEER_HEADS, ntb, N_SUBKEYS, LANES)
    scratch = [
        pltpu.VMEM((D, tb), BF16),
        pltpu.VMEM((qdim, tb), BF16),
        pltpu.VMEM(per_head, BF16), pltpu.VMEM(per_head, BF16),
        pltpu.VMEM(per_head, F32), pltpu.VMEM(per_head, F32),
        pltpu.VMEM((D, tb), F32),
    ] + [pltpu.VMEM((se, tb), BF16)] * 4
    return pl.pallas_call(
        functools.partial(_peer_kernel, alpha=alpha, tb=tb, se=se),
        grid=(n // tb, nblocks + 2), in_specs=in_specs,
        out_specs=pl.BlockSpec((tb, D), lambda i, j: (i, 0), pipeline_mode=pl.Buffered(1)),
        out_shape=jax.ShapeDtypeStruct((n, D), F32), scratch_shapes=scratch, name="peer",
        compiler_params=pltpu.CompilerParams(
            dimension_semantics=("arbitrary", "arbitrary"), vmem_limit_bytes=VMEM_LIMIT_BYTES),
    )(h2d, p["peer_wq_t"], p["peer_sk"], p["peer_u"], p["peer_vt"], p["ln2_g"], p["ln2_b"])


def _layer_params(l, w_in, sinks, sgu_ln_g, sgu_ln_b, sgu_w, sgu_b, w_branch_sgu, w_branch_attn, w_out,
                  ln1_g, ln1_b, peer_w_q, peer_sub_keys, peer_u, peer_v, ln2_g, ln2_b):
    return {
        "w_in": w_in[l].astype(BF16),
        "sinks": sinks[l],
        "sgu_ln_g": sgu_ln_g[l][None, :], "sgu_ln_b": sgu_ln_b[l][None, :],
        "sgu_w": sgu_w[l],
        "sgu_b": sgu_b[l], "sgu_bcol": sgu_b[l].T,
        "w_bs": w_branch_sgu[l].astype(BF16), "w_ba": w_branch_attn[l].astype(BF16),
        "w_out": w_out[l].astype(BF16),
        "ln1_g": ln1_g[l][None, :], "ln1_b": ln1_b[l][None, :],
        "peer_wq_t": peer_w_q[l].T.astype(BF16),
        "peer_sk": peer_sub_keys[l].reshape(PEER_HEADS * 2, N_SUBKEYS, PEER_HALF).astype(BF16),
        "peer_u": peer_u[l].astype(BF16),
        "peer_vt": peer_v[l].reshape(N_EXPERTS // PEER_EXPERT_BLOCK, PEER_EXPERT_BLOCK, D_MODEL)
                            .transpose(0, 2, 1).astype(BF16),
        "ln2_g": ln2_g[l][None, :], "ln2_b": ln2_b[l][None, :],
    }


def _prompt_layer(x, p, alpha):
    B, S, D = x.shape
    h, k_last, v_last, sv_last = _prompt_mixer(x, p, alpha)
    y = _peer(h.reshape(B * S, D), p, alpha).reshape(B, S, D)
    return (y, k_last.reshape(B, WINDOW, N_KV_HEADS, HEAD_DIM), v_last.reshape(B, WINDOW, N_KV_HEADS, HEAD_DIM),
            sv_last.reshape(B, CHUNK, N_SGU_GROUPS, SGU_GROUP_DIM))


def _sample_layer(x, cache_k, cache_v, p, alpha):
    B, T, D = x.shape
    x2d = x.reshape(B * T, D)
    q, k_new, v_new, svn, part, gate_b = _sample_pre(x2d, p, T)
    qg = q.reshape(B, T, N_KV_HEADS, Q_PER_KV, HEAD_DIM).transpose(0, 2, 3, 1, 4)
    qg = qg.reshape(B, N_KV_HEADS, Q_PER_KV * T, HEAD_DIM)
    og, new_k, new_v = _sample_attn(qg, k_new.reshape(B, T, D_KV), v_new.reshape(B, T, D_KV),
                                    cache_k.reshape(B, WINDOW, D_KV), cache_v.reshape(B, WINDOW, D_KV),
                                    p["sinks"], T)
    o2d = og.reshape(B, N_KV_HEADS, Q_PER_KV, T, HEAD_DIM).transpose(0, 3, 1, 2, 4).reshape(B * T, D_ATTN)
    h = _sample_post(x2d, part, gate_b, o2d, p, alpha)
    y = _peer(h, p, alpha).reshape(B, T, D)
    return (y, new_k.reshape(B, WINDOW, N_KV_HEADS, HEAD_DIM), new_v.reshape(B, WINDOW, N_KV_HEADS, HEAD_DIM),
            svn.reshape(B, T, N_SGU_GROUPS, SGU_GROUP_DIM))


def kernel(x_prompt, x_sample, cache_k, cache_v, w_in, sinks, sgu_ln_g, sgu_ln_b, sgu_w, sgu_b, w_branch_sgu, w_branch_attn, w_out, ln1_g, ln1_b, peer_w_q, peer_sub_keys, peer_u, peer_v, ln2_g, ln2_b):
    depth = w_in.shape[0]
    alpha = (2.0 * depth) ** 0.25
    yp, ys = x_prompt, x_sample
    outs = [[] for _ in range(6)]
    for l in range(depth):
        p = _layer_params(l, w_in, sinks, sgu_ln_g, sgu_ln_b, sgu_w, sgu_b, w_branch_sgu, w_branch_attn,
                          w_out, ln1_g, ln1_b, peer_w_q, peer_sub_keys, peer_u, peer_v, ln2_g, ln2_b)
        yp, kp, vp, svp = _prompt_layer(yp, p, alpha)
        ys, ksm, vsm, svs = _sample_layer(ys, cache_k[l], cache_v[l], p, alpha)
        for acc, val in zip(outs, (kp, vp, ksm, vsm, svp, svs)):
            acc.append(val)
    return (yp, ys) + tuple(jnp.stack(o) for o in outs)
```

```python
import functools

import jax
import jax.numpy as jnp
from jax import lax
from jax.experimental import pallas as pl
from jax.experimental.pallas import tpu as pltpu

F32 = jnp.float32
BF16 = jnp.bfloat16

D_MODEL = 1024
N_Q_HEADS = 8
N_KV_HEADS = 2
HEAD_DIM = 64
Q_PER_KV = N_Q_HEADS // N_KV_HEADS
D_ATTN = N_Q_HEADS * HEAD_DIM
D_KV = N_KV_HEADS * HEAD_DIM
WINDOW = 128
CHUNK = 128
D_SGU = D_MODEL // 2
N_SGU_GROUPS = 4
SGU_GROUP_DIM = D_SGU // N_SGU_GROUPS
PEER_HEADS = 8
N_SUBKEYS = 128
N_EXPERTS = N_SUBKEYS * N_SUBKEYS
PEER_TOPK = 16
PEER_KEY_DIM = 256
PEER_HALF = PEER_KEY_DIM // 2
LN_EPS = 1e-5
NEG_INF = -1e30
D_IN = D_ATTN + 2 * D_KV + 2 * D_SGU + 2 * D_MODEL
_OFF_Q = 0
_OFF_K = D_ATTN
_OFF_V = _OFF_K + D_KV
_OFF_U = _OFF_V + D_KV
_OFF_SV = _OFF_U + D_SGU
_OFF_GA = _OFF_SV + D_SGU
_OFF_GB = _OFF_GA + D_MODEL

LANES = 128
VMEM_LIMIT_BYTES = 56 * 1024 * 1024

PROMPT_BLOCK = 512
SAMPLE_ATTN_BATCH = 8
PEER_TOKEN_BLOCK = 1024
PEER_EXPERT_BLOCK = 512
PEER_TOKEN_PIECES = 2
PEER_ROW_PIECES = 1


def _gelu(x):
    return 0.5 * x * (1.0 + jnp.tanh(0.7978845608028654 * (x + 0.044715 * (x * x * x))))


def _sigmoid(x):
    return 1.0 / (1.0 + jnp.exp(-x))


def _layer_norm(x, g, b):
    mu = jnp.mean(x, -1, keepdims=True)
    xc = x - mu
    var = jnp.mean(xc * xc, -1, keepdims=True)
    return xc * lax.rsqrt(var + LN_EPS) * g + b


def _dot(a, b):
    return jnp.dot(a, b, preferred_element_type=F32)


def _dot_nt(a, b):
    return lax.dot_general(a, b, (((1,), (1,)), ((), ())), preferred_element_type=F32)


def _sink_softmax_pv(logits, sink_col, vals):
    m = jnp.maximum(jnp.max(logits, -1, keepdims=True), sink_col)
    p = jnp.exp(logits - m)
    denom = jnp.sum(p, -1, keepdims=True) + jnp.exp(sink_col - m)
    return _dot(p.astype(BF16), vals) / denom


def _prompt_mixer_kernel(sinks_ref, x_ref, w_in_ref, sgu_g_ref, sgu_b_ref, sgu_w_ref, sgu_bcol_ref,
                         w_bs_ref, w_ba_ref, w_out_ref, ln1_g_ref, ln1_b_ref,
                         h_ref, k_ref, v_ref, sv_ref,
                         kext_ref, vext_ref, sgu_o_ref, attn_o_ref, *, alpha, ts):
    s = pl.program_id(1)
    last = pl.num_programs(1) - 1
    nblk = ts // WINDOW

    x = x_ref[...]
    z = _dot(x.astype(BF16), w_in_ref[...])
    k = z[:, _OFF_K:_OFF_V]
    v = z[:, _OFF_V:_OFF_U]
    u = _gelu(z[:, _OFF_U:_OFF_SV])
    svn = _layer_norm(_gelu(z[:, _OFF_SV:_OFF_GA]), sgu_g_ref[...], sgu_b_ref[...])

    @pl.when(s == 0)
    def _():
        kext_ref[0:WINDOW, :] = jnp.zeros((WINDOW, D_KV), BF16)
        vext_ref[0:WINDOW, :] = jnp.zeros((WINDOW, D_KV), BF16)

    kext_ref[WINDOW:, :] = k.astype(BF16)
    vext_ref[WINDOW:, :] = v.astype(BF16)

    @pl.when(s == last)
    def _():
        k_ref[...] = k[ts - WINDOW:, :]
        v_ref[...] = v[ts - WINDOW:, :]
        sv_ref[...] = svn[ts - CHUNK:, :]

    row = lax.broadcasted_iota(jnp.int32, (CHUNK, CHUNK), 0)
    col = lax.broadcasted_iota(jnp.int32, (CHUNK, CHUNK), 1)
    for g in range(N_SGU_GROUPS):
        wt = jnp.where(col <= row, sgu_w_ref[g], 0.0).astype(BF16)
        bcol = sgu_bcol_ref[:, g:g + 1]
        gs = slice(g * SGU_GROUP_DIM, (g + 1) * SGU_GROUP_DIM)
        for c in range(ts // CHUNK):
            rs = slice(c * CHUNK, (c + 1) * CHUNK)
            mix = _dot(wt, svn[rs, gs].astype(BF16)) + bcol
            sgu_o_ref[rs, gs] = (u[rs, gs] * mix).astype(BF16)

    nq = Q_PER_KV * WINDOW
    qi = lax.broadcasted_iota(jnp.int32, (nq, 2 * WINDOW), 0) % WINDOW
    kj = lax.broadcasted_iota(jnp.int32, (nq, 2 * WINDOW), 1)
    rel = qi + WINDOW - kj
    bias = jnp.where((rel >= 0) & (rel < WINDOW), 0.0, NEG_INF).astype(F32)
    bias_first = jnp.where(s > 0, bias, jnp.where(kj < WINDOW, NEG_INF, bias))
    scale = HEAD_DIM ** -0.5
    for i in range(nblk):
        qs = slice(i * WINDOW, (i + 1) * WINDOW)
        heads = [None] * N_Q_HEADS
        for g in range(N_KV_HEADS):
            ds = slice(g * HEAD_DIM, (g + 1) * HEAD_DIM)
            kb = kext_ref[i * WINDOW:(i + 2) * WINDOW, ds]
            vb = vext_ref[i * WINDOW:(i + 2) * WINDOW, ds]
            q4 = jnp.concatenate(
                [z[qs, (g * Q_PER_KV + r) * HEAD_DIM:(g * Q_PER_KV + r + 1) * HEAD_DIM]
                 for r in range(Q_PER_KV)], axis=0).astype(BF16)
            logits = _dot_nt(q4, kb) * scale + (bias_first if i == 0 else bias)
            sink_col = jnp.concatenate(
                [jnp.full((WINDOW, 1), sinks_ref[g * Q_PER_KV + r], F32) for r in range(Q_PER_KV)], axis=0)
            o4 = _sink_softmax_pv(logits, sink_col, vb)
            for r in range(Q_PER_KV):
                heads[g * Q_PER_KV + r] = o4[r * WINDOW:(r + 1) * WINDOW, :]
        attn_o_ref[qs, :] = jnp.concatenate(heads, axis=1).astype(BF16)

    kext_ref[0:WINDOW, :] = kext_ref[ts:ts + WINDOW, :]
    vext_ref[0:WINDOW, :] = vext_ref[ts:ts + WINDOW, :]

    merged = (_sigmoid(z[:, _OFF_GA:_OFF_GB]) * _dot(sgu_o_ref[...], w_bs_ref[...])
              + _sigmoid(z[:, _OFF_GB:]) * _dot(attn_o_ref[...], w_ba_ref[...]))
    y = _dot(merged.astype(BF16), w_out_ref[...])
    h_ref[...] = _layer_norm(alpha * x + y, ln1_g_ref[...], ln1_b_ref[...])


def _full(shape):
    return pl.BlockSpec(shape, lambda *_: (0,) * len(shape), pipeline_mode=pl.Buffered(1))


def _prompt_mixer(x, p, alpha, ts=PROMPT_BLOCK):
    B, S, D = x.shape
    ts = min(ts, S)
    assert S % ts == 0 and ts % WINDOW == 0 and ts % CHUNK == 0
    grid = (B, S // ts)
    smem = pl.BlockSpec(memory_space=pltpu.SMEM)
    in_specs = [
        smem,
        pl.BlockSpec((None, ts, D), lambda b, s: (b, s, 0)),
        _full((D, D_IN)), _full((1, D_SGU)), _full((1, D_SGU)),
        _full((N_SGU_GROUPS, CHUNK, CHUNK)), _full((CHUNK, N_SGU_GROUPS)),
        _full((D_SGU, D)), _full((D_ATTN, D)), _full((D, D)), _full((1, D)), _full((1, D)),
    ]
    out_specs = [
        pl.BlockSpec((None, ts, D), lambda b, s: (b, s, 0)),
        pl.BlockSpec((None, WINDOW, D_KV), lambda b, s: (b, 0, 0)),
        pl.BlockSpec((None, WINDOW, D_KV), lambda b, s: (b, 0, 0)),
        pl.BlockSpec((None, CHUNK, D_SGU), lambda b, s: (b, 0, 0)),
    ]
    out_shape = [
        jax.ShapeDtypeStruct((B, S, D), F32),
        jax.ShapeDtypeStruct((B, WINDOW, D_KV), F32),
        jax.ShapeDtypeStruct((B, WINDOW, D_KV), F32),
        jax.ShapeDtypeStruct((B, CHUNK, D_SGU), F32),
    ]
    scratch = [
        pltpu.VMEM((ts + WINDOW, D_KV), BF16),
        pltpu.VMEM((ts + WINDOW, D_KV), BF16),
        pltpu.VMEM((ts, D_SGU), BF16),
        pltpu.VMEM((ts, D_ATTN), BF16),
    ]
    return pl.pallas_call(
        functools.partial(_prompt_mixer_kernel, alpha=alpha, ts=ts),
        grid=grid, in_specs=in_specs, out_specs=out_specs, out_shape=out_shape,
        scratch_shapes=scratch, name="prompt_mixer",
        compiler_params=pltpu.CompilerParams(
            dimension_semantics=("arbitrary", "arbitrary"), vmem_limit_bytes=VMEM_LIMIT_BYTES),
    )(p["sinks"], x, p["w_in"], p["sgu_ln_g"], p["sgu_ln_b"], p["sgu_w"], p["sgu_bcol"],
      p["w_bs"], p["w_ba"], p["w_out"], p["ln1_g"], p["ln1_b"])


def _sample_pre_kernel(sgu_w_ref, sgu_b_ref, x_ref, w_in_ref, sgu_g_ref, sgu_bb_ref, w_bs_ref,
                       q_ref, k_ref, v_ref, sv_ref, part_ref, gate_b_ref, *, t_new):
    x = x_ref[...]
    n = x.shape[0]
    z = _dot(x.astype(BF16), w_in_ref[...])
    q_ref[...] = z[:, _OFF_Q:_OFF_K]
    k_ref[...] = z[:, _OFF_K:_OFF_V]
    v_ref[...] = z[:, _OFF_V:_OFF_U]
    u = _gelu(z[:, _OFF_U:_OFF_SV])
    svn = _layer_norm(_gelu(z[:, _OFF_SV:_OFF_GA]), sgu_g_ref[...], sgu_bb_ref[...])
    sv_ref[...] = svn
    t_of_row = lax.broadcasted_iota(jnp.int32, (n, SGU_GROUP_DIM), 0) % t_new
    pieces = []
    for g in range(N_SGU_GROUPS):
        vg = svn[:, g * SGU_GROUP_DIM:(g + 1) * SGU_GROUP_DIM]
        mix = jnp.zeros((n, SGU_GROUP_DIM), F32)
        for t in range(t_new):
            mix = jnp.where(t_of_row == t, sgu_b_ref[g, t], mix)
        for d in range(t_new):
            coef = jnp.zeros((n, SGU_GROUP_DIM), F32)
            for t in range(d, t_new):
                coef = jnp.where(t_of_row == t, sgu_w_ref[g, t * t_new + t - d], coef)
            shifted = vg if d == 0 else pltpu.roll(vg, d, 0)
            mix = mix + coef * shifted
        pieces.append(u[:, g * SGU_GROUP_DIM:(g + 1) * SGU_GROUP_DIM] * mix)
    sgu_o = jnp.concatenate(pieces, axis=1).astype(BF16)
    part_ref[...] = _sigmoid(z[:, _OFF_GA:_OFF_GB]) * _dot(sgu_o, w_bs_ref[...])
    gate_b_ref[...] = _sigmoid(z[:, _OFF_GB:])


def _sample_pre(x2d, p, t_new):
    n, D = x2d.shape
    smem = pl.BlockSpec(memory_space=pltpu.SMEM)
    out_shape = [
        jax.ShapeDtypeStruct((n, D_ATTN), F32),
        jax.ShapeDtypeStruct((n, D_KV), F32),
        jax.ShapeDtypeStruct((n, D_KV), F32),
        jax.ShapeDtypeStruct((n, D_SGU), F32),
        jax.ShapeDtypeStruct((n, D), F32),
        jax.ShapeDtypeStruct((n, D), F32),
    ]
    return pl.pallas_call(
        functools.partial(_sample_pre_kernel, t_new=t_new),
        in_specs=[smem, smem] + [pl.BlockSpec(memory_space=pltpu.VMEM)] * 5,
        out_specs=[pl.BlockSpec(memory_space=pltpu.VMEM)] * 6,
        out_shape=out_shape, name="sample_pre",
        compiler_params=pltpu.CompilerParams(vmem_limit_bytes=VMEM_LIMIT_BYTES),
    )(p["sgu_w"][:, :t_new, :t_new].reshape(N_SGU_GROUPS, t_new * t_new), p["sgu_b"],
      x2d, p["w_in"], p["sgu_ln_g"], p["sgu_ln_b"], p["w_bs"])


def _sample_attn_kernel(sinks_ref, q_ref, kn_ref, vn_ref, ck_ref, cv_ref,
                        o_ref, nk_ref, nv_ref, kbuf_ref, vbuf_ref, *, t_new, bb):
    nq = Q_PER_KV * t_new
    nkeys = WINDOW + 8
    row = lax.broadcasted_iota(jnp.int32, (nq, nkeys), 0)
    kj = lax.broadcasted_iota(jnp.int32, (nq, nkeys), 1)
    t_q = row % t_new
    bias = jnp.where((kj > t_q) & (kj <= t_q + WINDOW), 0.0, NEG_INF).astype(F32)
    r_of_row = lax.broadcasted_iota(jnp.int32, (nq, 1), 0) // t_new
    scale = HEAD_DIM ** -0.5
    pad = jnp.zeros((8 - t_new, D_KV), F32)
    for b in range(bb):
        kbuf_ref[0:WINDOW, :] = ck_ref[b]
        kbuf_ref[WINDOW:WINDOW + t_new, :] = kn_ref[b]
        kbuf_ref[WINDOW + t_new:, :] = pad
        vbuf_ref[0:WINDOW, :] = cv_ref[b]
        vbuf_ref[WINDOW:WINDOW + t_new, :] = vn_ref[b]
        vbuf_ref[WINDOW + t_new:, :] = pad
        nk_ref[b] = kbuf_ref[t_new:t_new + WINDOW, :]
        nv_ref[b] = vbuf_ref[t_new:t_new + WINDOW, :]
        for g in range(N_KV_HEADS):
            ds = slice(g * HEAD_DIM, (g + 1) * HEAD_DIM)
            kg = kbuf_ref[:, ds].astype(BF16)
            vg = vbuf_ref[:, ds].astype(BF16)
            logits = _dot_nt(q_ref[b, g].astype(BF16), kg) * scale + bias
            sink_col = jnp.zeros((nq, 1), F32)
            for r in range(Q_PER_KV):
                sink_col = jnp.where(r_of_row == r, sinks_ref[g * Q_PER_KV + r], sink_col)
            o_ref[b, g] = _sink_softmax_pv(logits, sink_col, vg)


def _sample_attn(q_grouped, k_new, v_new, cache_k, cache_v, sinks, t_new, bb=SAMPLE_ATTN_BATCH):
    nb = q_grouped.shape[0]
    bb = min(bb, nb)
    assert nb % bb == 0 and t_new <= 8
    nq = Q_PER_KV * t_new
    smem = pl.BlockSpec(memory_space=pltpu.SMEM)
    in_specs = [
        smem,
        pl.BlockSpec((bb, N_KV_HEADS, nq, HEAD_DIM), lambda i: (i, 0, 0, 0)),
        pl.BlockSpec((bb, t_new, D_KV), lambda i: (i, 0, 0)),
        pl.BlockSpec((bb, t_new, D_KV), lambda i: (i, 0, 0)),
        pl.BlockSpec((bb, WINDOW, D_KV), lambda i: (i, 0, 0)),
        pl.BlockSpec((bb, WINDOW, D_KV), lambda i: (i, 0, 0)),
    ]
    out_specs = [
        pl.BlockSpec((bb, N_KV_HEADS, nq, HEAD_DIM), lambda i: (i, 0, 0, 0)),
        pl.BlockSpec((bb, WINDOW, D_KV), lambda i: (i, 0, 0)),
        pl.BlockSpec((bb, WINDOW, D_KV), lambda i: (i, 0, 0)),
    ]
    out_shape = [
        jax.ShapeDtypeStruct((nb, N_KV_HEADS, nq, HEAD_DIM), F32),
        jax.ShapeDtypeStruct((nb, WINDOW, D_KV), F32),
        jax.ShapeDtypeStruct((nb, WINDOW, D_KV), F32),
    ]
    return pl.pallas_call(
        functools.partial(_sample_attn_kernel, t_new=t_new, bb=bb),
        grid=(nb // bb,), in_specs=in_specs, out_specs=out_specs, out_shape=out_shape,
        scratch_shapes=[pltpu.VMEM((WINDOW + 8, D_KV), F32), pltpu.VMEM((WINDOW + 8, D_KV), F32)],
        name="sample_attn",
        compiler_params=pltpu.CompilerParams(dimension_semantics=("arbitrary",)),
    )(sinks, q_grouped, k_new, v_new, cache_k, cache_v)


def _sample_post_kernel(x_ref, part_ref, gate_b_ref, o_ref, w_ba_ref, w_out_ref, ln1_g_ref, ln1_b_ref,
                        h_ref, *, alpha):
    merged = part_ref[...] + gate_b_ref[...] * _dot(o_ref[...].astype(BF16), w_ba_ref[...])
    y = _dot(merged.astype(BF16), w_out_ref[...])
    h_ref[...] = _layer_norm(alpha * x_ref[...] + y, ln1_g_ref[...], ln1_b_ref[...])


def _sample_post(x2d, part, gate_b, o2d, p, alpha):
    return pl.pallas_call(
        functools.partial(_sample_post_kernel, alpha=alpha),
        in_specs=[pl.BlockSpec(memory_space=pltpu.VMEM)] * 8,
        out_specs=pl.BlockSpec(memory_space=pltpu.VMEM),
        out_shape=jax.ShapeDtypeStruct(x2d.shape, F32), name="sample_post",
        compiler_params=pltpu.CompilerParams(vmem_limit_bytes=VMEM_LIMIT_BYTES),
    )(x2d, part, gate_b, o2d, p["w_ba"], p["w_out"], p["ln1_g"], p["ln1_b"])


def _top_values(work, count, with_rank=False):
    vals = []
    rank = jnp.full(work.shape, float(count), F32) if with_rank else None
    for i in range(count):
        top = jnp.max(work, axis=0, keepdims=True)
        vals.append(top)
        hit = work == top
        if with_rank:
            rank = jnp.where(hit, float(i), rank)
        if i + 1 < count:
            work = jnp.where(hit, -jnp.inf, work)
    return (vals, rank) if with_rank else vals


def _candidate_rows(a_rows, b_rows):
    assert PEER_TOPK == 16
    a_all = jnp.concatenate(a_rows, axis=0)
    a_low = a_all[0:8]
    row = lax.broadcasted_iota(jnp.int32, a_low.shape, 0)

    def pair(j):
        return a_low + b_rows[j]

    def corner(j):
        return a_rows[0] + b_rows[j]

    t3 = jnp.where(row >= 5, pltpu.roll(pair(4), 5, 0), pair(2))
    t4 = jnp.where(row >= 6, pltpu.roll(pair(6), 6, 0),
                   jnp.where(row >= 4, pltpu.roll(pair(5), 4, 0), pair(3)))
    t5 = pair(7)
    for k in range(6):
        t5 = jnp.where(row >= 2 + k, corner(8 + k), t5)
    t6 = jnp.where(row >= 2, -jnp.inf, jnp.where(row >= 1, corner(15), corner(14)))
    return jnp.concatenate([a_all + b_rows[0], pair(1), t3, t4, t5, t6], axis=0)


def _bf16_rows(row, rows):
    return jnp.broadcast_to(row.astype(BF16), (rows, row.shape[1]))


def _peer_select(s1, s2):
    a_rows = _top_values(s1, PEER_TOPK)
    b_rows, rank2 = _top_values(s2, PEER_TOPK, with_rank=True)
    best = _top_values(_candidate_rows(a_rows, b_rows), PEER_TOPK)
    norm = jnp.zeros_like(best[0])
    for i in range(PEER_TOPK):
        norm = norm + jnp.exp(best[i] - best[0])
    count = jnp.zeros_like(s1)
    for jj in range(PEER_TOPK):
        count = jnp.where(s1 + b_rows[jj] >= best[PEER_TOPK - 1], float(jj + 1), count)
    e1 = jnp.exp(s1 - a_rows[0]) / norm
    e2 = jnp.exp(s2 - b_rows[0])
    return rank2, e2, count, e1


def _peer_gate_tile(act_ref, coef_ref, first_n1, n, t, sel_refs):
    rank2_ref, e2_ref, count_ref, e1_ref = sel_refs
    ls = slice(t * LANES, (t + 1) * LANES)
    rs = slice(n * N_SUBKEYS, (n + 1) * N_SUBKEYS)
    n1 = first_n1 + n
    gate = jnp.zeros((N_SUBKEYS, LANES), BF16)
    for hd in range(PEER_HEADS):
        count = _bf16_rows(count_ref[hd, t, pl.ds(n1, 1), :], N_SUBKEYS)
        e1 = _bf16_rows(e1_ref[hd, t, pl.ds(n1, 1), :], N_SUBKEYS)
        gate = gate + jnp.where(rank2_ref[hd, t] < count, e2_ref[hd, t], jnp.zeros((), BF16)) * e1
    coef_ref[rs, ls] = gate * _gelu(act_ref[rs, ls])


def _peer_kernel(h_ref, wq_ref, sk_ref, u_ref, vt_ref, ln_g_ref, ln_b_ref, y_ref,
                 ht_ref, qt_ref, rank2_ref, e2_ref, count_ref, e1_ref, acc_ref,
                 act0_ref, act1_ref, coef0_ref, coef1_ref, *, alpha, tb, se):
    j = pl.program_id(1)
    ntb = tb // LANES
    nblocks = N_EXPERTS // se
    sel_refs = (rank2_ref, e2_ref, count_ref, e1_ref)
    act_refs = (act0_ref, act1_ref)
    coef_refs = (coef0_ref, coef1_ref)

    @pl.when(j == 0)
    def _():
        ht = h_ref[...].T.astype(BF16)
        ht_ref[...] = ht
        qt_ref[...] = _dot(wq_ref[...], ht).astype(BF16)

        def per_head(hd, carry):
            for t in range(ntb):
                ls = slice(t * LANES, (t + 1) * LANES)
                s = []
                for c in range(2):
                    r0 = pl.multiple_of((hd * 2 + c) * PEER_HALF, PEER_HALF)
                    s.append(_dot(sk_ref[hd * 2 + c], qt_ref[pl.ds(r0, PEER_HALF), ls]))
                rank2, e2, count, e1 = _peer_select(s[0], s[1])
                rank2_ref[hd, t] = rank2.astype(BF16)
                e2_ref[hd, t] = e2.astype(BF16)
                count_ref[hd, t] = count
                e1_ref[hd, t] = e1
            return carry

        lax.fori_loop(0, PEER_HEADS, per_head, 0)
        acc_ref[...] = jnp.zeros_like(acc_ref)

    def step_body(q, with_act, with_gate, with_combine):
        piece_t, piece_e, piece_d = tb // PEER_TOKEN_PIECES, se // PEER_ROW_PIECES, acc_ref.shape[0] // PEER_ROW_PIECES
        pieces = []
        for tp in range(PEER_TOKEN_PIECES):
            hs = slice(tp * piece_t, (tp + 1) * piece_t)
            for lo in range(PEER_ROW_PIECES):
                if with_act:
                    pieces.append(("act", hs, slice(lo * piece_e, (lo + 1) * piece_e)))
                if with_combine:
                    pieces.append(("combine", hs, slice(lo * piece_d, (lo + 1) * piece_d)))
        tiles = [(n, t) for t in range(ntb) for n in range(se // N_SUBKEYS)] if with_gate else []
        per_piece = -(-len(tiles) // max(len(pieces), 1))
        for k, (kind, hs, es) in enumerate(pieces):
            for n, t in tiles[k * per_piece:(k + 1) * per_piece]:
                _peer_gate_tile(act_refs[1 - q], coef_refs[1 - q], (j - 1) * (se // N_SUBKEYS), n, t, sel_refs)
            if kind == "act":
                act_refs[q][es, hs] = _dot(u_ref[es, :], ht_ref[:, hs]).astype(BF16)
            else:
                acc_ref[es, hs] += _dot(vt_ref[es, :], coef_refs[q][:, hs])
        for n, t in tiles[len(pieces) * per_piece:]:
            _peer_gate_tile(act_refs[1 - q], coef_refs[1 - q], (j - 1) * (se // N_SUBKEYS), n, t, sel_refs)

    assert nblocks % 2 == 0 and nblocks >= 4
    pl.when(j == 0)(functools.partial(step_body, 0, True, False, False))
    pl.when(j == 1)(functools.partial(step_body, 1, True, True, False))
    for q in range(2):
        pl.when((j >= 2) & (j < nblocks) & (j % 2 == q))(functools.partial(step_body, q, True, True, True))
    pl.when(j == nblocks)(functools.partial(step_body, 0, False, True, True))
    pl.when(j == nblocks + 1)(functools.partial(step_body, 1, False, False, True))

    @pl.when(j == pl.num_programs(1) - 1)
    def _():
        y_ref[...] = _layer_norm(alpha * h_ref[...] + acc_ref[...].T, ln_g_ref[...], ln_b_ref[...])


def _peer(h2d, p, alpha, tb=PEER_TOKEN_BLOCK):
    n, D = h2d.shape
    tb = min(tb, n)
    se = p["peer_vt"].shape[2]
    assert n % tb == 0 and tb % LANES == 0 and N_EXPERTS % se == 0 and se % N_SUBKEYS == 0
    ntb = tb // LANES
    qdim = PEER_HEADS * PEER_KEY_DIM
    nblocks = N_EXPERTS // se
    in_specs = [
        pl.BlockSpec((tb, D), lambda i, j: (i, 0)),
        _full((qdim, D)),
        _full((PEER_HEADS * 2, N_SUBKEYS, PEER_HALF)),
        pl.BlockSpec((se, D), lambda i, j: (jnp.minimum(j, nblocks - 1), 0)),
        pl.BlockSpec((None, D, se), lambda i, j: (jnp.clip(j - 2, 0, nblocks - 1), 0, 0)),
        _full((1, D)), _full((1, D)),
    ]
    per_head = (PEER_HEADS, ntb, N_SUBKEYS, LANES)
    scratch = [
        pltpu.VMEM((D, tb), BF16),
        pltpu.VMEM((qdim, tb), BF16),
        pltpu.VMEM(per_head, BF16), pltpu.VMEM(per_head, BF16),
        pltpu.VMEM(per_head, F32), pltpu.VMEM(per_head, F32),
        pltpu.VMEM((D, tb), F32),
    ] + [pltpu.VMEM((se, tb), BF16)] * 4
    return pl.pallas_call(
        functools.partial(_peer_kernel, alpha=alpha, tb=tb, se=se),
        grid=(n // tb, nblocks + 2), in_specs=in_specs,
        out_specs=pl.BlockSpec((tb, D), lambda i, j: (i, 0), pipeline_mode=pl.Buffered(1)),
        out_shape=jax.ShapeDtypeStruct((n, D), F32), scratch_shapes=scratch, name="peer",
        compiler_params=pltpu.CompilerParams(
            dimension_semantics=("arbitrary", "arbitrary"), vmem_limit_bytes=VMEM_LIMIT_BYTES),
    )(h2d, p["peer_wq_t"], p["peer_sk"], p["peer_u"], p["peer_vt"], p["ln2_g"], p["ln2_b"])


def _layer_params(l, w_in, sinks, sgu_ln_g, sgu_ln_b, sgu_w, sgu_b, w_branch_sgu, w_branch_attn, w_out,
                  ln1_g, ln1_b, peer_w_q, peer_sub_keys, peer_u, peer_v, ln2_g, ln2_b):
    return {
        "w_in": w_in[l].astype(BF16),
        "sinks": sinks[l],
        "sgu_ln_g": sgu_ln_g[l][None, :], "sgu_ln_b": sgu_ln_b[l][None, :],
        "sgu_w": sgu_w[l],
        "sgu_b": sgu_b[l], "sgu_bcol": sgu_b[l].T,
        "w_bs": w_branch_sgu[l].astype(BF16), "w_ba": w_branch_attn[l].astype(BF16),
        "w_out": w_out[l].astype(BF16),
        "ln1_g": ln1_g[l][None, :], "ln1_b": ln1_b[l][None, :],
        "peer_wq_t": peer_w_q[l].T.astype(BF16),
        "peer_sk": peer_sub_keys[l].reshape(PEER_HEADS * 2, N_SUBKEYS, PEER_HALF).astype(BF16),
        "peer_u": peer_u[l].astype(BF16),
        "peer_vt": peer_v[l].reshape(N_EXPERTS // PEER_EXPERT_BLOCK, PEER_EXPERT_BLOCK, D_MODEL)
                            .transpose(0, 2, 1).astype(BF16),
        "ln2_g": ln2_g[l][None, :], "ln2_b": ln2_b[l][None, :],
    }


def _prompt_layer(x, p, alpha):
    B, S, D = x.shape
    h, k_last, v_last, sv_last = _prompt_mixer(x, p, alpha)
    y = _peer(h.reshape(B * S, D), p, alpha).reshape(B, S, D)
    return (y, k_last.reshape(B, WINDOW, N_KV_HEADS, HEAD_DIM), v_last.reshape(B, WINDOW, N_KV_HEADS, HEAD_DIM),
            sv_last.reshape(B, CHUNK, N_SGU_GROUPS, SGU_GROUP_DIM))


def _sample_layer(x, cache_k, cache_v, p, alpha):
    B, T, D = x.shape
    x2d = x.reshape(B * T, D)
    q, k_new, v_new, svn, part, gate_b = _sample_pre(x2d, p, T)
    qg = q.reshape(B, T, N_KV_HEADS, Q_PER_KV, HEAD_DIM).transpose(0, 2, 3, 1, 4)
    qg = qg.reshape(B, N_KV_HEADS, Q_PER_KV * T, HEAD_DIM)
    og, new_k, new_v = _sample_attn(qg, k_new.reshape(B, T, D_KV), v_new.reshape(B, T, D_KV),
                                    cache_k.reshape(B, WINDOW, D_KV), cache_v.reshape(B, WINDOW, D_KV),
                                    p["sinks"], T)
    o2d = og.reshape(B, N_KV_HEADS, Q_PER_KV, T, HEAD_DIM).transpose(0, 3, 1, 2, 4).reshape(B * T, D_ATTN)
    h = _sample_post(x2d, part, gate_b, o2d, p, alpha)
    y = _peer(h, p, alpha).reshape(B, T, D)
    return (y, new_k.reshape(B, WINDOW, N_KV_HEADS, HEAD_DIM), new_v.reshape(B, WINDOW, N_KV_HEADS, HEAD_DIM),
            svn.reshape(B, T, N_SGU_GROUPS, SGU_GROUP_DIM))


def kernel(x_prompt, x_sample, cache_k, cache_v, w_in, sinks, sgu_ln_g, sgu_ln_b, sgu_w, sgu_b, w_branch_sgu, w_branch_attn, w_out, ln1_g, ln1_b, peer_w_q, peer_sub_keys, peer_u, peer_v, ln2_g, ln2_b):
    depth = w_in.shape[0]
    alpha = (2.0 * depth) ** 0.25
    yp, ys = x_prompt, x_sample
    outs = [[] for _ in range(6)]
    for l in range(depth):
        p = _layer_params(l, w_in, sinks, sgu_ln_g, sgu_ln_b, sgu_w, sgu_b, w_branch_sgu, w_branch_attn,
                          w_out, ln1_g, ln1_b, peer_w_q, peer_sub_keys, peer_u, peer_v, ln2_g, ln2_b)
        yp, kp, vp, svp = _prompt_layer(yp, p, alpha)
        ys, ksm, vsm, svs = _sample_layer(ys, cache_k[l], cache_v[l], p, alpha)
        for acc, val in zip(outs, (kp, vp, ksm, vsm, svp, svs)):
            acc.append(val)
    return (yp, ys) + tuple(jnp.stack(o) for o in outs)
```

```python
import functools

import jax
import jax.numpy as jnp
from jax import lax
from jax.experimental import pallas as pl
from jax.experimental.pallas import tpu as pltpu

F32 = jnp.float32
BF16 = jnp.bfloat16

D_MODEL = 1024
N_Q_HEADS = 8
N_KV_HEADS = 2
HEAD_DIM = 64
Q_PER_KV = N_Q_HEADS // N_KV_HEADS
D_ATTN = N_Q_HEADS * HEAD_DIM
D_KV = N_KV_HEADS * HEAD_DIM
WINDOW = 128
CHUNK = 128
D_SGU = D_MODEL // 2
N_SGU_GROUPS = 4
SGU_GROUP_DIM = D_SGU // N_SGU_GROUPS
PEER_HEADS = 8
N_SUBKEYS = 128
N_EXPERTS = N_SUBKEYS * N_SUBKEYS
PEER_TOPK = 16
PEER_KEY_DIM = 256
PEER_HALF = PEER_KEY_DIM // 2
LN_EPS = 1e-5
NEG_INF = -1e30
D_IN = D_ATTN + 2 * D_KV + 2 * D_SGU + 2 * D_MODEL
_OFF_Q = 0
_OFF_K = D_ATTN
_OFF_V = _OFF_K + D_KV
_OFF_U = _OFF_V + D_KV
_OFF_SV = _OFF_U + D_SGU
_OFF_GA = _OFF_SV + D_SGU
_OFF_GB = _OFF_GA + D_MODEL

LANES = 128
VMEM_LIMIT_BYTES = 56 * 1024 * 1024

PROMPT_BLOCK = 512
SAMPLE_ATTN_BATCH = 8
PEER_TOKEN_BLOCK = 1024
PEER_EXPERT_BLOCK = 512
PEER_GATE_GROUP = 4
PEER_TOKEN_PIECES = 2
PEER_ROW_PIECES = 1


def _gelu(x):
    return 0.5 * x * (1.0 + jnp.tanh(0.7978845608028654 * (x + 0.044715 * (x * x * x))))


def _sigmoid(x):
    return 1.0 / (1.0 + jnp.exp(-x))


def _layer_norm(x, g, b):
    mu = jnp.mean(x, -1, keepdims=True)
    xc = x - mu
    var = jnp.mean(xc * xc, -1, keepdims=True)
    return xc * lax.rsqrt(var + LN_EPS) * g + b


def _dot(a, b):
    return jnp.dot(a, b, preferred_element_type=F32)


def _dot_nt(a, b):
    return lax.dot_general(a, b, (((1,), (1,)), ((), ())), preferred_element_type=F32)


def _sink_softmax_pv(logits, sink_col, vals):
    m = jnp.maximum(jnp.max(logits, -1, keepdims=True), sink_col)
    p = jnp.exp(logits - m)
    denom = jnp.sum(p, -1, keepdims=True) + jnp.exp(sink_col - m)
    return _dot(p.astype(BF16), vals) / denom


def _prompt_mixer_kernel(sinks_ref, x_ref, w_in_ref, sgu_g_ref, sgu_b_ref, sgu_w_ref, sgu_bcol_ref,
                         w_bs_ref, w_ba_ref, w_out_ref, ln1_g_ref, ln1_b_ref,
                         h_ref, k_ref, v_ref, sv_ref,
                         kext_ref, vext_ref, sgu_o_ref, attn_o_ref, *, alpha, ts):
    s = pl.program_id(1)
    last = pl.num_programs(1) - 1
    nblk = ts // WINDOW

    x = x_ref[...]
    z = _dot(x.astype(BF16), w_in_ref[...])
    k = z[:, _OFF_K:_OFF_V]
    v = z[:, _OFF_V:_OFF_U]
    u = _gelu(z[:, _OFF_U:_OFF_SV])
    svn = _layer_norm(_gelu(z[:, _OFF_SV:_OFF_GA]), sgu_g_ref[...], sgu_b_ref[...])

    @pl.when(s == 0)
    def _():
        kext_ref[0:WINDOW, :] = jnp.zeros((WINDOW, D_KV), BF16)
        vext_ref[0:WINDOW, :] = jnp.zeros((WINDOW, D_KV), BF16)

    kext_ref[WINDOW:, :] = k.astype(BF16)
    vext_ref[WINDOW:, :] = v.astype(BF16)

    @pl.when(s == last)
    def _():
        k_ref[...] = k[ts - WINDOW:, :]
        v_ref[...] = v[ts - WINDOW:, :]
        sv_ref[...] = svn[ts - CHUNK:, :]

    row = lax.broadcasted_iota(jnp.int32, (CHUNK, CHUNK), 0)
    col = lax.broadcasted_iota(jnp.int32, (CHUNK, CHUNK), 1)
    for g in range(N_SGU_GROUPS):
        wt = jnp.where(col <= row, sgu_w_ref[g], 0.0).astype(BF16)
        bcol = sgu_bcol_ref[:, g:g + 1]
        gs = slice(g * SGU_GROUP_DIM, (g + 1) * SGU_GROUP_DIM)
        for c in range(ts // CHUNK):
            rs = slice(c * CHUNK, (c + 1) * CHUNK)
            mix = _dot(wt, svn[rs, gs].astype(BF16)) + bcol
            sgu_o_ref[rs, gs] = (u[rs, gs] * mix).astype(BF16)

    nq = Q_PER_KV * WINDOW
    qi = lax.broadcasted_iota(jnp.int32, (nq, 2 * WINDOW), 0) % WINDOW
    kj = lax.broadcasted_iota(jnp.int32, (nq, 2 * WINDOW), 1)
    rel = qi + WINDOW - kj
    bias = jnp.where((rel >= 0) & (rel < WINDOW), 0.0, NEG_INF).astype(F32)
    bias_first = jnp.where(s > 0, bias, jnp.where(kj < WINDOW, NEG_INF, bias))
    scale = HEAD_DIM ** -0.5
    for i in range(nblk):
        qs = slice(i * WINDOW, (i + 1) * WINDOW)
        heads = [None] * N_Q_HEADS
        for g in range(N_KV_HEADS):
            ds = slice(g * HEAD_DIM, (g + 1) * HEAD_DIM)
            kb = kext_ref[i * WINDOW:(i + 2) * WINDOW, ds]
            vb = vext_ref[i * WINDOW:(i + 2) * WINDOW, ds]
            q4 = jnp.concatenate(
                [z[qs, (g * Q_PER_KV + r) * HEAD_DIM:(g * Q_PER_KV + r + 1) * HEAD_DIM]
                 for r in range(Q_PER_KV)], axis=0).astype(BF16)
            logits = _dot_nt(q4, kb) * scale + (bias_first if i == 0 else bias)
            sink_col = jnp.concatenate(
                [jnp.full((WINDOW, 1), sinks_ref[g * Q_PER_KV + r], F32) for r in range(Q_PER_KV)], axis=0)
            o4 = _sink_softmax_pv(logits, sink_col, vb)
            for r in range(Q_PER_KV):
                heads[g * Q_PER_KV + r] = o4[r * WINDOW:(r + 1) * WINDOW, :]
        attn_o_ref[qs, :] = jnp.concatenate(heads, axis=1).astype(BF16)

    kext_ref[0:WINDOW, :] = kext_ref[ts:ts + WINDOW, :]
    vext_ref[0:WINDOW, :] = vext_ref[ts:ts + WINDOW, :]

    merged = (_sigmoid(z[:, _OFF_GA:_OFF_GB]) * _dot(sgu_o_ref[...], w_bs_ref[...])
              + _sigmoid(z[:, _OFF_GB:]) * _dot(attn_o_ref[...], w_ba_ref[...]))
    y = _dot(merged.astype(BF16), w_out_ref[...])
    h_ref[...] = _layer_norm(alpha * x + y, ln1_g_ref[...], ln1_b_ref[...])


def _full(shape):
    return pl.BlockSpec(shape, lambda *_: (0,) * len(shape), pipeline_mode=pl.Buffered(1))


def _prompt_mixer(x, p, alpha, ts=PROMPT_BLOCK):
    B, S, D = x.shape
    ts = min(ts, S)
    assert S % ts == 0 and ts % WINDOW == 0 and ts % CHUNK == 0
    grid = (B, S // ts)
    smem = pl.BlockSpec(memory_space=pltpu.SMEM)
    in_specs = [
        smem,
        pl.BlockSpec((None, ts, D), lambda b, s: (b, s, 0)),
        _full((D, D_IN)), _full((1, D_SGU)), _full((1, D_SGU)),
        _full((N_SGU_GROUPS, CHUNK, CHUNK)), _full((CHUNK, N_SGU_GROUPS)),
        _full((D_SGU, D)), _full((D_ATTN, D)), _full((D, D)), _full((1, D)), _full((1, D)),
    ]
    out_specs = [
        pl.BlockSpec((None, ts, D), lambda b, s: (b, s, 0)),
        pl.BlockSpec((None, WINDOW, D_KV), lambda b, s: (b, 0, 0)),
        pl.BlockSpec((None, WINDOW, D_KV), lambda b, s: (b, 0, 0)),
        pl.BlockSpec((None, CHUNK, D_SGU), lambda b, s: (b, 0, 0)),
    ]
    out_shape = [
        jax.ShapeDtypeStruct((B, S, D), F32),
        jax.ShapeDtypeStruct((B, WINDOW, D_KV), F32),
        jax.ShapeDtypeStruct((B, WINDOW, D_KV), F32),
        jax.ShapeDtypeStruct((B, CHUNK, D_SGU), F32),
    ]
    scratch = [
        pltpu.VMEM((ts + WINDOW, D_KV), BF16),
        pltpu.VMEM((ts + WINDOW, D_KV), BF16),
        pltpu.VMEM((ts, D_SGU), BF16),
        pltpu.VMEM((ts, D_ATTN), BF16),
    ]
    return pl.pallas_call(
        functools.partial(_prompt_mixer_kernel, alpha=alpha, ts=ts),
        grid=grid, in_specs=in_specs, out_specs=out_specs, out_shape=out_shape,
        scratch_shapes=scratch, name="prompt_mixer",
        compiler_params=pltpu.CompilerParams(
            dimension_semantics=("arbitrary", "arbitrary"), vmem_limit_bytes=VMEM_LIMIT_BYTES),
    )(p["sinks"], x, p["w_in"], p["sgu_ln_g"], p["sgu_ln_b"], p["sgu_w"], p["sgu_bcol"],
      p["w_bs"], p["w_ba"], p["w_out"], p["ln1_g"], p["ln1_b"])


def _sample_pre_kernel(sgu_w_ref, sgu_b_ref, x_ref, w_in_ref, sgu_g_ref, sgu_bb_ref, w_bs_ref,
                       q_ref, k_ref, v_ref, sv_ref, part_ref, gate_b_ref, *, t_new):
    x = x_ref[...]
    n = x.shape[0]
    z = _dot(x.astype(BF16), w_in_ref[...])
    q_ref[...] = z[:, _OFF_Q:_OFF_K]
    k_ref[...] = z[:, _OFF_K:_OFF_V]
    v_ref[...] = z[:, _OFF_V:_OFF_U]
    u = _gelu(z[:, _OFF_U:_OFF_SV])
    svn = _layer_norm(_gelu(z[:, _OFF_SV:_OFF_GA]), sgu_g_ref[...], sgu_bb_ref[...])
    sv_ref[...] = svn
    t_of_row = lax.broadcasted_iota(jnp.int32, (n, SGU_GROUP_DIM), 0) % t_new
    pieces = []
    for g in range(N_SGU_GROUPS):
        vg = svn[:, g * SGU_GROUP_DIM:(g + 1) * SGU_GROUP_DIM]
        mix = jnp.zeros((n, SGU_GROUP_DIM), F32)
        for t in range(t_new):
            mix = jnp.where(t_of_row == t, sgu_b_ref[g, t], mix)
        for d in range(t_new):
            coef = jnp.zeros((n, SGU_GROUP_DIM), F32)
            for t in range(d, t_new):
                coef = jnp.where(t_of_row == t, sgu_w_ref[g, t * t_new + t - d], coef)
            shifted = vg if d == 0 else pltpu.roll(vg, d, 0)
            mix = mix + coef * shifted
        pieces.append(u[:, g * SGU_GROUP_DIM:(g + 1) * SGU_GROUP_DIM] * mix)
    sgu_o = jnp.concatenate(pieces, axis=1).astype(BF16)
    part_ref[...] = _sigmoid(z[:, _OFF_GA:_OFF_GB]) * _dot(sgu_o, w_bs_ref[...])
    gate_b_ref[...] = _sigmoid(z[:, _OFF_GB:])


def _sample_pre(x2d, p, t_new):
    n, D = x2d.shape
    smem = pl.BlockSpec(memory_space=pltpu.SMEM)
    out_shape = [
        jax.ShapeDtypeStruct((n, D_ATTN), F32),
        jax.ShapeDtypeStruct((n, D_KV), F32),
        jax.ShapeDtypeStruct((n, D_KV), F32),
        jax.ShapeDtypeStruct((n, D_SGU), F32),
        jax.ShapeDtypeStruct((n, D), F32),
        jax.ShapeDtypeStruct((n, D), F32),
    ]
    return pl.pallas_call(
        functools.partial(_sample_pre_kernel, t_new=t_new),
        in_specs=[smem, smem] + [pl.BlockSpec(memory_space=pltpu.VMEM)] * 5,
        out_specs=[pl.BlockSpec(memory_space=pltpu.VMEM)] * 6,
        out_shape=out_shape, name="sample_pre",
        compiler_params=pltpu.CompilerParams(vmem_limit_bytes=VMEM_LIMIT_BYTES),
    )(p["sgu_w"][:, :t_new, :t_new].reshape(N_SGU_GROUPS, t_new * t_new), p["sgu_b"],
      x2d, p["w_in"], p["sgu_ln_g"], p["sgu_ln_b"], p["w_bs"])


def _sample_attn_kernel(sinks_ref, q_ref, kn_ref, vn_ref, ck_ref, cv_ref,
                        o_ref, nk_ref, nv_ref, kbuf_ref, vbuf_ref, *, t_new, bb):
    nq = Q_PER_KV * t_new
    nkeys = WINDOW + 8
    row = lax.broadcasted_iota(jnp.int32, (nq, nkeys), 0)
    kj = lax.broadcasted_iota(jnp.int32, (nq, nkeys), 1)
    t_q = row % t_new
    bias = jnp.where((kj > t_q) & (kj <= t_q + WINDOW), 0.0, NEG_INF).astype(F32)
    r_of_row = lax.broadcasted_iota(jnp.int32, (nq, 1), 0) // t_new
    scale = HEAD_DIM ** -0.5
    pad = jnp.zeros((8 - t_new, D_KV), F32)
    for b in range(bb):
        kbuf_ref[0:WINDOW, :] = ck_ref[b]
        kbuf_ref[WINDOW:WINDOW + t_new, :] = kn_ref[b]
        kbuf_ref[WINDOW + t_new:, :] = pad
        vbuf_ref[0:WINDOW, :] = cv_ref[b]
        vbuf_ref[WINDOW:WINDOW + t_new, :] = vn_ref[b]
        vbuf_ref[WINDOW + t_new:, :] = pad
        nk_ref[b] = kbuf_ref[t_new:t_new + WINDOW, :]
        nv_ref[b] = vbuf_ref[t_new:t_new + WINDOW, :]
        for g in range(N_KV_HEADS):
            ds = slice(g * HEAD_DIM, (g + 1) * HEAD_DIM)
            kg = kbuf_ref[:, ds].astype(BF16)
            vg = vbuf_ref[:, ds].astype(BF16)
            logits = _dot_nt(q_ref[b, g].astype(BF16), kg) * scale + bias
            sink_col = jnp.zeros((nq, 1), F32)
            for r in range(Q_PER_KV):
                sink_col = jnp.where(r_of_row == r, sinks_ref[g * Q_PER_KV + r], sink_col)
            o_ref[b, g] = _sink_softmax_pv(logits, sink_col, vg)


def _sample_attn(q_grouped, k_new, v_new, cache_k, cache_v, sinks, t_new, bb=SAMPLE_ATTN_BATCH):
    nb = q_grouped.shape[0]
    bb = min(bb, nb)
    assert nb % bb == 0 and t_new <= 8
    nq = Q_PER_KV * t_new
    smem = pl.BlockSpec(memory_space=pltpu.SMEM)
    in_specs = [
        smem,
        pl.BlockSpec((bb, N_KV_HEADS, nq, HEAD_DIM), lambda i: (i, 0, 0, 0)),
        pl.BlockSpec((bb, t_new, D_KV), lambda i: (i, 0, 0)),
        pl.BlockSpec((bb, t_new, D_KV), lambda i: (i, 0, 0)),
        pl.BlockSpec((bb, WINDOW, D_KV), lambda i: (i, 0, 0)),
        pl.BlockSpec((bb, WINDOW, D_KV), lambda i: (i, 0, 0)),
    ]
    out_specs = [
        pl.BlockSpec((bb, N_KV_HEADS, nq, HEAD_DIM), lambda i: (i, 0, 0, 0)),
        pl.BlockSpec((bb, WINDOW, D_KV), lambda i: (i, 0, 0)),
        pl.BlockSpec((bb, WINDOW, D_KV), lambda i: (i, 0, 0)),
    ]
    out_shape = [
        jax.ShapeDtypeStruct((nb, N_KV_HEADS, nq, HEAD_DIM), F32),
        jax.ShapeDtypeStruct((nb, WINDOW, D_KV), F32),
        jax.ShapeDtypeStruct((nb, WINDOW, D_KV), F32),
    ]
    return pl.pallas_call(
        functools.partial(_sample_attn_kernel, t_new=t_new, bb=bb),
        grid=(nb // bb,), in_specs=in_specs, out_specs=out_specs, out_shape=out_shape,
        scratch_shapes=[pltpu.VMEM((WINDOW + 8, D_KV), F32), pltpu.VMEM((WINDOW + 8, D_KV), F32)],
        name="sample_attn",
        compiler_params=pltpu.CompilerParams(dimension_semantics=("arbitrary",)),
    )(sinks, q_grouped, k_new, v_new, cache_k, cache_v)


def _sample_post_kernel(x_ref, part_ref, gate_b_ref, o_ref, w_ba_ref, w_out_ref, ln1_g_ref, ln1_b_ref,
                        h_ref, *, alpha):
    merged = part_ref[...] + gate_b_ref[...] * _dot(o_ref[...].astype(BF16), w_ba_ref[...])
    y = _dot(merged.astype(BF16), w_out_ref[...])
    h_ref[...] = _layer_norm(alpha * x_ref[...] + y, ln1_g_ref[...], ln1_b_ref[...])


def _sample_post(x2d, part, gate_b, o2d, p, alpha):
    return pl.pallas_call(
        functools.partial(_sample_post_kernel, alpha=alpha),
        in_specs=[pl.BlockSpec(memory_space=pltpu.VMEM)] * 8,
        out_specs=pl.BlockSpec(memory_space=pltpu.VMEM),
        out_shape=jax.ShapeDtypeStruct(x2d.shape, F32), name="sample_post",
        compiler_params=pltpu.CompilerParams(vmem_limit_bytes=VMEM_LIMIT_BYTES),
    )(x2d, part, gate_b, o2d, p["w_ba"], p["w_out"], p["ln1_g"], p["ln1_b"])


def _top_values(work, count, with_rank=False):
    vals = []
    rank = jnp.full(work.shape, float(count), F32) if with_rank else None
    for i in range(count):
        top = jnp.max(work, axis=0, keepdims=True)
        vals.append(top)
        hit = work == top
        if with_rank:
            rank = jnp.where(hit, float(i), rank)
        if i + 1 < count:
            work = jnp.where(hit, -jnp.inf, work)
    return (vals, rank) if with_rank else vals


def _candidate_rows(a_rows, b_rows):
    assert PEER_TOPK == 16
    a_all = jnp.concatenate(a_rows, axis=0)
    a_low = a_all[0:8]
    row = lax.broadcasted_iota(jnp.int32, a_low.shape, 0)

    def pair(j):
        return a_low + b_rows[j]

    def corner(j):
        return a_rows[0] + b_rows[j]

    t3 = jnp.where(row >= 5, pltpu.roll(pair(4), 5, 0), pair(2))
    t4 = jnp.where(row >= 6, pltpu.roll(pair(6), 6, 0),
                   jnp.where(row >= 4, pltpu.roll(pair(5), 4, 0), pair(3)))
    t5 = pair(7)
    for k in range(6):
        t5 = jnp.where(row >= 2 + k, corner(8 + k), t5)
    t6 = jnp.where(row >= 2, -jnp.inf, jnp.where(row >= 1, corner(15), corner(14)))
    return jnp.concatenate([a_all + b_rows[0], pair(1), t3, t4, t5, t6], axis=0)


def _bf16_rows(row, rows):
    return jnp.broadcast_to(row.astype(BF16), (rows, row.shape[1]))


def _peer_select(s1, s2):
    a_rows = _top_values(s1, PEER_TOPK)
    b_rows, rank2 = _top_values(s2, PEER_TOPK, with_rank=True)
    best = _top_values(_candidate_rows(a_rows, b_rows), PEER_TOPK)
    norm = jnp.zeros_like(best[0])
    for i in range(PEER_TOPK):
        norm = norm + jnp.exp(best[i] - best[0])
    count = jnp.zeros_like(s1)
    for jj in range(PEER_TOPK):
        count = jnp.where(s1 + b_rows[jj] >= best[PEER_TOPK - 1], float(jj + 1), count)
    e1 = jnp.exp(s1 - a_rows[0]) / norm
    e2 = jnp.exp(s2 - b_rows[0])
    return rank2, e2, count, e1


def _peer_gate_tile(act_ref, coef_ref, first_n1, ns, t, sel_refs):
    rank2_ref, e2_ref, count_ref, e1_ref = sel_refs
    ls = slice(t * LANES, (t + 1) * LANES)
    gates = [jnp.zeros((N_SUBKEYS, LANES), BF16) for _ in ns]
    for hd in range(PEER_HEADS):
        rank2 = rank2_ref[hd, t]
        e2 = e2_ref[hd, t]
        for i, n in enumerate(ns):
            count = _bf16_rows(count_ref[hd, t, pl.ds(first_n1 + n, 1), :], N_SUBKEYS)
            e1 = _bf16_rows(e1_ref[hd, t, pl.ds(first_n1 + n, 1), :], N_SUBKEYS)
            gates[i] = gates[i] + jnp.where(rank2 < count, e2, jnp.zeros((), BF16)) * e1
    for i, n in enumerate(ns):
        rs = slice(n * N_SUBKEYS, (n + 1) * N_SUBKEYS)
        coef_ref[rs, ls] = gates[i] * _gelu(act_ref[rs, ls])


def _peer_kernel(h_ref, wq_ref, sk_ref, u_ref, vt_ref, ln_g_ref, ln_b_ref, y_ref,
                 ht_ref, qt_ref, rank2_ref, e2_ref, count_ref, e1_ref, acc_ref,
                 act0_ref, act1_ref, coef0_ref, coef1_ref, *, alpha, tb, se):
    j = pl.program_id(1)
    ntb = tb // LANES
    nblocks = N_EXPERTS // se
    sel_refs = (rank2_ref, e2_ref, count_ref, e1_ref)
    act_refs = (act0_ref, act1_ref)
    coef_refs = (coef0_ref, coef1_ref)

    @pl.when(j == 0)
    def _():
        ht = h_ref[...].T.astype(BF16)
        ht_ref[...] = ht
        qt_ref[...] = _dot(wq_ref[...], ht).astype(BF16)

        def per_head(hd, carry):
            for t in range(ntb):
                ls = slice(t * LANES, (t + 1) * LANES)
                s = []
                for c in range(2):
                    r0 = pl.multiple_of((hd * 2 + c) * PEER_HALF, PEER_HALF)
                    s.append(_dot(sk_ref[hd * 2 + c], qt_ref[pl.ds(r0, PEER_HALF), ls]))
                rank2, e2, count, e1 = _peer_select(s[0], s[1])
                rank2_ref[hd, t] = rank2.astype(BF16)
                e2_ref[hd, t] = e2.astype(BF16)
                count_ref[hd, t] = count
                e1_ref[hd, t] = e1
            return carry

        lax.fori_loop(0, PEER_HEADS, per_head, 0)
        acc_ref[...] = jnp.zeros_like(acc_ref)

    def step_body(q, with_act, with_gate, with_combine):
        piece_t, piece_e, piece_d = tb // PEER_TOKEN_PIECES, se // PEER_ROW_PIECES, acc_ref.shape[0] // PEER_ROW_PIECES
        pieces = []
        for tp in range(PEER_TOKEN_PIECES):
            hs = slice(tp * piece_t, (tp + 1) * piece_t)
            for lo in range(PEER_ROW_PIECES):
                if with_act:
                    pieces.append(("act", hs, slice(lo * piece_e, (lo + 1) * piece_e)))
                if with_combine:
                    pieces.append(("combine", hs, slice(lo * piece_d, (lo + 1) * piece_d)))
        n_sub = se // N_SUBKEYS
        groups = [tuple(range(g, min(g + PEER_GATE_GROUP, n_sub))) for g in range(0, n_sub, PEER_GATE_GROUP)]
        tiles = [(ns, t) for t in range(ntb) for ns in groups] if with_gate else []
        per_piece = -(-len(tiles) // max(len(pieces), 1))
        for k, (kind, hs, es) in enumerate(pieces):
            for n, t in tiles[k * per_piece:(k + 1) * per_piece]:
                _peer_gate_tile(act_refs[1 - q], coef_refs[1 - q], (j - 1) * (se // N_SUBKEYS), n, t, sel_refs)
            if kind == "act":
                act_refs[q][es, hs] = _dot(u_ref[es, :], ht_ref[:, hs]).astype(BF16)
            else:
                acc_ref[es, hs] += _dot(vt_ref[es, :], coef_refs[q][:, hs])
        for n, t in tiles[len(pieces) * per_piece:]:
            _peer_gate_tile(act_refs[1 - q], coef_refs[1 - q], (j - 1) * (se // N_SUBKEYS), n, t, sel_refs)

    assert nblocks % 2 == 0 and nblocks >= 4
    pl.when(j == 0)(functools.partial(step_body, 0, True, False, False))
    pl.when(j == 1)(functools.partial(step_body, 1, True, True, False))
    for q in range(2):
        pl.when((j >= 2) & (j < nblocks) & (j % 2 == q))(functools.partial(step_body, q, True, True, True))
    pl.when(j == nblocks)(functools.partial(step_body, 0, False, True, True))
    pl.when(j == nblocks + 1)(functools.partial(step_body, 1, False, False, True))

    @pl.when(j == pl.num_programs(1) - 1)
    def _():
        y_ref[...] = _layer_norm(alpha * h_ref[...] + acc_ref[...].T, ln_g_ref[...], ln_b_ref[...])


def _peer(h2d, p, alpha, tb=PEER_TOKEN_BLOCK):
    n, D = h2d.shape
    tb = min(tb, n)
    se = p["peer_vt"].shape[2]
    assert n % tb == 0 and tb % LANES == 0 and N_EXPERTS % se == 0 and se % N_SUBKEYS == 0
    ntb = tb // LANES
    qdim = PEER_HEADS * PEER_KEY_DIM
    nblocks = N_EXPERTS // se
    in_specs = [
        pl.BlockSpec((tb, D), lambda i, j: (i, 0)),
        _full((qdim, D)),
        _full((PEER_HEADS * 2, N_SUBKEYS, PEER_HALF)),
        pl.BlockSpec((se, D), lambda i, j: (jnp.minimum(j, nblocks - 1), 0)),
        pl.BlockSpec((None, D, se), lambda i, j: (jnp.clip(j - 2, 0, nblocks - 1), 0, 0)),
        _full((1, D)), _full((1, D)),
    ]
    per_head = (PEER_HEADS, ntb, N_SUBKEYS, LANES)
    scratch = [
        pltpu.VMEM((D, tb), BF16),
        pltpu.VMEM((qdim, tb), BF16),
        pltpu.VMEM(per_head, BF16), pltpu.VMEM(per_head, BF16),
        pltpu.VMEM(per_head, F32), pltpu.VMEM(per_head, F32),
        pltpu.VMEM((D, tb), F32),
    ] + [pltpu.VMEM((se, tb), BF16)] * 4
    return pl.pallas_call(
        functools.partial(_peer_kernel, alpha=alpha, tb=tb, se=se),
        grid=(n // tb, nblocks + 2), in_specs=in_specs,
        out_specs=pl.BlockSpec((tb, D), lambda i, j: (i, 0), pipeline_mode=pl.Buffered(1)),
        out_shape=jax.ShapeDtypeStruct((n, D), F32), scratch_shapes=scratch, name="peer",
        compiler_params=pltpu.CompilerParams(
            dimension_semantics=("arbitrary", "arbitrary"), vmem_limit_bytes=VMEM_LIMIT_BYTES),
    )(h2d, p["peer_wq_t"], p["peer_sk"], p["peer_u"], p["peer_vt"], p["ln2_g"], p["ln2_b"])


def _layer_params(l, w_in, sinks, sgu_ln_g, sgu_ln_b, sgu_w, sgu_b, w_branch_sgu, w_branch_attn, w_out,
                  ln1_g, ln1_b, peer_w_q, peer_sub_keys, peer_u, peer_v, ln2_g, ln2_b):
    return {
        "w_in": w_in[l].astype(BF16),
        "sinks": sinks[l],
        "sgu_ln_g": sgu_ln_g[l][None, :], "sgu_ln_b": sgu_ln_b[l][None, :],
        "sgu_w": sgu_w[l],
        "sgu_b": sgu_b[l], "sgu_bcol": sgu_b[l].T,
        "w_bs": w_branch_sgu[l].astype(BF16), "w_ba": w_branch_attn[l].astype(BF16),
        "w_out": w_out[l].astype(BF16),
        "ln1_g": ln1_g[l][None, :], "ln1_b": ln1_b[l][None, :],
        "peer_wq_t": peer_w_q[l].T.astype(BF16),
        "peer_sk": peer_sub_keys[l].reshape(PEER_HEADS * 2, N_SUBKEYS, PEER_HALF).astype(BF16),
        "peer_u": peer_u[l].astype(BF16),
        "peer_vt": peer_v[l].reshape(N_EXPERTS // PEER_EXPERT_BLOCK, PEER_EXPERT_BLOCK, D_MODEL)
                            .transpose(0, 2, 1).astype(BF16),
        "ln2_g": ln2_g[l][None, :], "ln2_b": ln2_b[l][None, :],
    }


def _prompt_layer(x, p, alpha):
    B, S, D = x.shape
    h, k_last, v_last, sv_last = _prompt_mixer(x, p, alpha)
    y = _peer(h.reshape(B * S, D), p, alpha).reshape(B, S, D)
    return (y, k_last.reshape(B, WINDOW, N_KV_HEADS, HEAD_DIM), v_last.reshape(B, WINDOW, N_KV_HEADS, HEAD_DIM),
            sv_last.reshape(B, CHUNK, N_SGU_GROUPS, SGU_GROUP_DIM))


def _sample_layer(x, cache_k, cache_v, p, alpha):
    B, T, D = x.shape
    x2d = x.reshape(B * T, D)
    q, k_new, v_new, svn, part, gate_b = _sample_pre(x2d, p, T)
    qg = q.reshape(B, T, N_KV_HEADS, Q_PER_KV, HEAD_DIM).transpose(0, 2, 3, 1, 4)
    qg = qg.reshape(B, N_KV_HEADS, Q_PER_KV * T, HEAD_DIM)
    og, new_k, new_v = _sample_attn(qg, k_new.reshape(B, T, D_KV), v_new.reshape(B, T, D_KV),
                                    cache_k.reshape(B, WINDOW, D_KV), cache_v.reshape(B, WINDOW, D_KV),
                                    p["sinks"], T)
    o2d = og.reshape(B, N_KV_HEADS, Q_PER_KV, T, HEAD_DIM).transpose(0, 3, 1, 2, 4).reshape(B * T, D_ATTN)
    h = _sample_post(x2d, part, gate_b, o2d, p, alpha)
    y = _peer(h, p, alpha).reshape(B, T, D)
    return (y, new_k.reshape(B, WINDOW, N_KV_HEADS, HEAD_DIM), new_v.reshape(B, WINDOW, N_KV_HEADS, HEAD_DIM),
            svn.reshape(B, T, N_SGU_GROUPS, SGU_GROUP_DIM))


def kernel(x_prompt, x_sample, cache_k, cache_v, w_in, sinks, sgu_ln_g, sgu_ln_b, sgu_w, sgu_b, w_branch_sgu, w_branch_attn, w_out, ln1_g, ln1_b, peer_w_q, peer_sub_keys, peer_u, peer_v, ln2_g, ln2_b):
    depth = w_in.shape[0]
    alpha = (2.0 * depth) ** 0.25
    yp, ys = x_prompt, x_sample
    outs = [[] for _ in range(6)]
    for l in range(depth):
        p = _layer_params(l, w_in, sinks, sgu_ln_g, sgu_ln_b, sgu_w, sgu_b, w_branch_sgu, w_branch_attn,
                          w_out, ln1_g, ln1_b, peer_w_q, peer_sub_keys, peer_u, peer_v, ln2_g, ln2_b)
        yp, kp, vp, svp = _prompt_layer(yp, p, alpha)
        ys, ksm, vsm, svs = _sample_layer(ys, cache_k[l], cache_v[l], p, alpha)
        for acc, val in zip(outs, (kp, vp, ksm, vsm, svp, svs)):
            acc.append(val)
    return (yp, ys) + tuple(jnp.stack(o) for o in outs)
```

```python
import functools

import jax
import jax.numpy as jnp
from jax import lax
from jax.experimental import pallas as pl
from jax.experimental.pallas import tpu as pltpu

F32 = jnp.float32
BF16 = jnp.bfloat16

D_MODEL = 1024
N_Q_HEADS = 8
N_KV_HEADS = 2
HEAD_DIM = 64
Q_PER_KV = N_Q_HEADS // N_KV_HEADS
D_ATTN = N_Q_HEADS * HEAD_DIM
D_KV = N_KV_HEADS * HEAD_DIM
WINDOW = 128
CHUNK = 128
D_SGU = D_MODEL // 2
N_SGU_GROUPS = 4
SGU_GROUP_DIM = D_SGU // N_SGU_GROUPS
PEER_HEADS = 8
N_SUBKEYS = 128
N_EXPERTS = N_SUBKEYS * N_SUBKEYS
PEER_TOPK = 16
PEER_KEY_DIM = 256
PEER_HALF = PEER_KEY_DIM // 2
LN_EPS = 1e-5
NEG_INF = -1e30
D_IN = D_ATTN + 2 * D_KV + 2 * D_SGU + 2 * D_MODEL
_OFF_Q = 0
_OFF_K = D_ATTN
_OFF_V = _OFF_K + D_KV
_OFF_U = _OFF_V + D_KV
_OFF_SV = _OFF_U + D_SGU
_OFF_GA = _OFF_SV + D_SGU
_OFF_GB = _OFF_GA + D_MODEL

LANES = 128
VMEM_LIMIT_BYTES = 56 * 1024 * 1024

PROMPT_BLOCK = 512
SAMPLE_ATTN_BATCH = 8
PEER_TOKEN_BLOCK = 1024
PEER_EXPERT_BLOCK = 512
PEER_GATE_GROUP = 1
PEER_TOKEN_PIECES = 2
PEER_ROW_PIECES = 2


def _gelu(x):
    return 0.5 * x * (1.0 + jnp.tanh(0.7978845608028654 * (x + 0.044715 * (x * x * x))))


def _sigmoid(x):
    return 1.0 / (1.0 + jnp.exp(-x))


def _layer_norm(x, g, b):
    mu = jnp.mean(x, -1, keepdims=True)
    xc = x - mu
    var = jnp.mean(xc * xc, -1, keepdims=True)
    return xc * lax.rsqrt(var + LN_EPS) * g + b


def _dot(a, b):
    return jnp.dot(a, b, preferred_element_type=F32)


def _dot_nt(a, b):
    return lax.dot_general(a, b, (((1,), (1,)), ((), ())), preferred_element_type=F32)


def _sink_softmax_pv(logits, sink_col, vals):
    m = jnp.maximum(jnp.max(logits, -1, keepdims=True), sink_col)
    p = jnp.exp(logits - m)
    denom = jnp.sum(p, -1, keepdims=True) + jnp.exp(sink_col - m)
    return _dot(p.astype(BF16), vals) / denom


def _prompt_mixer_kernel(sinks_ref, x_ref, w_in_ref, sgu_g_ref, sgu_b_ref, sgu_w_ref, sgu_bcol_ref,
                         w_bs_ref, w_ba_ref, w_out_ref, ln1_g_ref, ln1_b_ref,
                         h_ref, k_ref, v_ref, sv_ref,
                         kext_ref, vext_ref, sgu_o_ref, attn_o_ref, *, alpha, ts):
    s = pl.program_id(1)
    last = pl.num_programs(1) - 1
    nblk = ts // WINDOW

    x = x_ref[...]
    z = _dot(x.astype(BF16), w_in_ref[...])
    k = z[:, _OFF_K:_OFF_V]
    v = z[:, _OFF_V:_OFF_U]
    u = _gelu(z[:, _OFF_U:_OFF_SV])
    svn = _layer_norm(_gelu(z[:, _OFF_SV:_OFF_GA]), sgu_g_ref[...], sgu_b_ref[...])

    @pl.when(s == 0)
    def _():
        kext_ref[0:WINDOW, :] = jnp.zeros((WINDOW, D_KV), BF16)
        vext_ref[0:WINDOW, :] = jnp.zeros((WINDOW, D_KV), BF16)

    kext_ref[WINDOW:, :] = k.astype(BF16)
    vext_ref[WINDOW:, :] = v.astype(BF16)

    @pl.when(s == last)
    def _():
        k_ref[...] = k[ts - WINDOW:, :]
        v_ref[...] = v[ts - WINDOW:, :]
        sv_ref[...] = svn[ts - CHUNK:, :]

    row = lax.broadcasted_iota(jnp.int32, (CHUNK, CHUNK), 0)
    col = lax.broadcasted_iota(jnp.int32, (CHUNK, CHUNK), 1)
    for g in range(N_SGU_GROUPS):
        wt = jnp.where(col <= row, sgu_w_ref[g], 0.0).astype(BF16)
        bcol = sgu_bcol_ref[:, g:g + 1]
        gs = slice(g * SGU_GROUP_DIM, (g + 1) * SGU_GROUP_DIM)
        for c in range(ts // CHUNK):
            rs = slice(c * CHUNK, (c + 1) * CHUNK)
            mix = _dot(wt, svn[rs, gs].astype(BF16)) + bcol
            sgu_o_ref[rs, gs] = (u[rs, gs] * mix).astype(BF16)

    nq = Q_PER_KV * WINDOW
    qi = lax.broadcasted_iota(jnp.int32, (nq, 2 * WINDOW), 0) % WINDOW
    kj = lax.broadcasted_iota(jnp.int32, (nq, 2 * WINDOW), 1)
    rel = qi + WINDOW - kj
    bias = jnp.where((rel >= 0) & (rel < WINDOW), 0.0, NEG_INF).astype(F32)
    bias_first = jnp.where(s > 0, bias, jnp.where(kj < WINDOW, NEG_INF, bias))
    scale = HEAD_DIM ** -0.5
    for i in range(nblk):
        qs = slice(i * WINDOW, (i + 1) * WINDOW)
        heads = [None] * N_Q_HEADS
        for g in range(N_KV_HEADS):
            ds = slice(g * HEAD_DIM, (g + 1) * HEAD_DIM)
            kb = kext_ref[i * WINDOW:(i + 2) * WINDOW, ds]
            vb = vext_ref[i * WINDOW:(i + 2) * WINDOW, ds]
            q4 = jnp.concatenate(
                [z[qs, (g * Q_PER_KV + r) * HEAD_DIM:(g * Q_PER_KV + r + 1) * HEAD_DIM]
                 for r in range(Q_PER_KV)], axis=0).astype(BF16)
            logits = _dot_nt(q4, kb) * scale + (bias_first if i == 0 else bias)
            sink_col = jnp.concatenate(
                [jnp.full((WINDOW, 1), sinks_ref[g * Q_PER_KV + r], F32) for r in range(Q_PER_KV)], axis=0)
            o4 = _sink_softmax_pv(logits, sink_col, vb)
            for r in range(Q_PER_KV):
                heads[g * Q_PER_KV + r] = o4[r * WINDOW:(r + 1) * WINDOW, :]
        attn_o_ref[qs, :] = jnp.concatenate(heads, axis=1).astype(BF16)

    kext_ref[0:WINDOW, :] = kext_ref[ts:ts + WINDOW, :]
    vext_ref[0:WINDOW, :] = vext_ref[ts:ts + WINDOW, :]

    merged = (_sigmoid(z[:, _OFF_GA:_OFF_GB]) * _dot(sgu_o_ref[...], w_bs_ref[...])
              + _sigmoid(z[:, _OFF_GB:]) * _dot(attn_o_ref[...], w_ba_ref[...]))
    y = _dot(merged.astype(BF16), w_out_ref[...])
    h_ref[...] = _layer_norm(alpha * x + y, ln1_g_ref[...], ln1_b_ref[...])


def _full(shape):
    return pl.BlockSpec(shape, lambda *_: (0,) * len(shape), pipeline_mode=pl.Buffered(1))


def _prompt_mixer(x, p, alpha, ts=PROMPT_BLOCK):
    B, S, D = x.shape
    ts = min(ts, S)
    assert S % ts == 0 and ts % WINDOW == 0 and ts % CHUNK == 0
    grid = (B, S // ts)
    smem = pl.BlockSpec(memory_space=pltpu.SMEM)
    in_specs = [
        smem,
        pl.BlockSpec((None, ts, D), lambda b, s: (b, s, 0)),
        _full((D, D_IN)), _full((1, D_SGU)), _full((1, D_SGU)),
        _full((N_SGU_GROUPS, CHUNK, CHUNK)), _full((CHUNK, N_SGU_GROUPS)),
        _full((D_SGU, D)), _full((D_ATTN, D)), _full((D, D)), _full((1, D)), _full((1, D)),
    ]
    out_specs = [
        pl.BlockSpec((None, ts, D), lambda b, s: (b, s, 0)),
        pl.BlockSpec((None, WINDOW, D_KV), lambda b, s: (b, 0, 0)),
        pl.BlockSpec((None, WINDOW, D_KV), lambda b, s: (b, 0, 0)),
        pl.BlockSpec((None, CHUNK, D_SGU), lambda b, s: (b, 0, 0)),
    ]
    out_shape = [
        jax.ShapeDtypeStruct((B, S, D), F32),
        jax.ShapeDtypeStruct((B, WINDOW, D_KV), F32),
        jax.ShapeDtypeStruct((B, WINDOW, D_KV), F32),
        jax.ShapeDtypeStruct((B, CHUNK, D_SGU), F32),
    ]
    scratch = [
        pltpu.VMEM((ts + WINDOW, D_KV), BF16),
        pltpu.VMEM((ts + WINDOW, D_KV), BF16),
        pltpu.VMEM((ts, D_SGU), BF16),
        pltpu.VMEM((ts, D_ATTN), BF16),
    ]
    return pl.pallas_call(
        functools.partial(_prompt_mixer_kernel, alpha=alpha, ts=ts),
        grid=grid, in_specs=in_specs, out_specs=out_specs, out_shape=out_shape,
        scratch_shapes=scratch, name="prompt_mixer",
        compiler_params=pltpu.CompilerParams(
            dimension_semantics=("arbitrary", "arbitrary"), vmem_limit_bytes=VMEM_LIMIT_BYTES),
    )(p["sinks"], x, p["w_in"], p["sgu_ln_g"], p["sgu_ln_b"], p["sgu_w"], p["sgu_bcol"],
      p["w_bs"], p["w_ba"], p["w_out"], p["ln1_g"], p["ln1_b"])


def _sample_pre_kernel(sgu_w_ref, sgu_b_ref, x_ref, w_in_ref, sgu_g_ref, sgu_bb_ref, w_bs_ref,
                       q_ref, k_ref, v_ref, sv_ref, part_ref, gate_b_ref, *, t_new):
    x = x_ref[...]
    n = x.shape[0]
    z = _dot(x.astype(BF16), w_in_ref[...])
    q_ref[...] = z[:, _OFF_Q:_OFF_K]
    k_ref[...] = z[:, _OFF_K:_OFF_V]
    v_ref[...] = z[:, _OFF_V:_OFF_U]
    u = _gelu(z[:, _OFF_U:_OFF_SV])
    svn = _layer_norm(_gelu(z[:, _OFF_SV:_OFF_GA]), sgu_g_ref[...], sgu_bb_ref[...])
    sv_ref[...] = svn
    t_of_row = lax.broadcasted_iota(jnp.int32, (n, SGU_GROUP_DIM), 0) % t_new
    pieces = []
    for g in range(N_SGU_GROUPS):
        vg = svn[:, g * SGU_GROUP_DIM:(g + 1) * SGU_GROUP_DIM]
        mix = jnp.zeros((n, SGU_GROUP_DIM), F32)
        for t in range(t_new):
            mix = jnp.where(t_of_row == t, sgu_b_ref[g, t], mix)
        for d in range(t_new):
            coef = jnp.zeros((n, SGU_GROUP_DIM), F32)
            for t in range(d, t_new):
                coef = jnp.where(t_of_row == t, sgu_w_ref[g, t * t_new + t - d], coef)
            shifted = vg if d == 0 else pltpu.roll(vg, d, 0)
            mix = mix + coef * shifted
        pieces.append(u[:, g * SGU_GROUP_DIM:(g + 1) * SGU_GROUP_DIM] * mix)
    sgu_o = jnp.concatenate(pieces, axis=1).astype(BF16)
    part_ref[...] = _sigmoid(z[:, _OFF_GA:_OFF_GB]) * _dot(sgu_o, w_bs_ref[...])
    gate_b_ref[...] = _sigmoid(z[:, _OFF_GB:])


def _sample_pre(x2d, p, t_new):
    n, D = x2d.shape
    smem = pl.BlockSpec(memory_space=pltpu.SMEM)
    out_shape = [
        jax.ShapeDtypeStruct((n, D_ATTN), F32),
        jax.ShapeDtypeStruct((n, D_KV), F32),
        jax.ShapeDtypeStruct((n, D_KV), F32),
        jax.ShapeDtypeStruct((n, D_SGU), F32),
        jax.ShapeDtypeStruct((n, D), F32),
        jax.ShapeDtypeStruct((n, D), F32),
    ]
    return pl.pallas_call(
        functools.partial(_sample_pre_kernel, t_new=t_new),
        in_specs=[smem, smem] + [pl.BlockSpec(memory_space=pltpu.VMEM)] * 5,
        out_specs=[pl.BlockSpec(memory_space=pltpu.VMEM)] * 6,
        out_shape=out_shape, name="sample_pre",
        compiler_params=pltpu.CompilerParams(vmem_limit_bytes=VMEM_LIMIT_BYTES),
    )(p["sgu_w"][:, :t_new, :t_new].reshape(N_SGU_GROUPS, t_new * t_new), p["sgu_b"],
      x2d, p["w_in"], p["sgu_ln_g"], p["sgu_ln_b"], p["w_bs"])


def _sample_attn_kernel(sinks_ref, q_ref, kn_ref, vn_ref, ck_ref, cv_ref,
                        o_ref, nk_ref, nv_ref, kbuf_ref, vbuf_ref, *, t_new, bb):
    nq = Q_PER_KV * t_new
    nkeys = WINDOW + 8
    row = lax.broadcasted_iota(jnp.int32, (nq, nkeys), 0)
    kj = lax.broadcasted_iota(jnp.int32, (nq, nkeys), 1)
    t_q = row % t_new
    bias = jnp.where((kj > t_q) & (kj <= t_q + WINDOW), 0.0, NEG_INF).astype(F32)
    r_of_row = lax.broadcasted_iota(jnp.int32, (nq, 1), 0) // t_new
    scale = HEAD_DIM ** -0.5
    pad = jnp.zeros((8 - t_new, D_KV), F32)
    for b in range(bb):
        kbuf_ref[0:WINDOW, :] = ck_ref[b]
        kbuf_ref[WINDOW:WINDOW + t_new, :] = kn_ref[b]
        kbuf_ref[WINDOW + t_new:, :] = pad
        vbuf_ref[0:WINDOW, :] = cv_ref[b]
        vbuf_ref[WINDOW:WINDOW + t_new, :] = vn_ref[b]
        vbuf_ref[WINDOW + t_new:, :] = pad
        nk_ref[b] = kbuf_ref[t_new:t_new + WINDOW, :]
        nv_ref[b] = vbuf_ref[t_new:t_new + WINDOW, :]
        for g in range(N_KV_HEADS):
            ds = slice(g * HEAD_DIM, (g + 1) * HEAD_DIM)
            kg = kbuf_ref[:, ds].astype(BF16)
            vg = vbuf_ref[:, ds].astype(BF16)
            logits = _dot_nt(q_ref[b, g].astype(BF16), kg) * scale + bias
            sink_col = jnp.zeros((nq, 1), F32)
            for r in range(Q_PER_KV):
                sink_col = jnp.where(r_of_row == r, sinks_ref[g * Q_PER_KV + r], sink_col)
            o_ref[b, g] = _sink_softmax_pv(logits, sink_col, vg)


def _sample_attn(q_grouped, k_new, v_new, cache_k, cache_v, sinks, t_new, bb=SAMPLE_ATTN_BATCH):
    nb = q_grouped.shape[0]
    bb = min(bb, nb)
    assert nb % bb == 0 and t_new <= 8
    nq = Q_PER_KV * t_new
    smem = pl.BlockSpec(memory_space=pltpu.SMEM)
    in_specs = [
        smem,
        pl.BlockSpec((bb, N_KV_HEADS, nq, HEAD_DIM), lambda i: (i, 0, 0, 0)),
        pl.BlockSpec((bb, t_new, D_KV), lambda i: (i, 0, 0)),
        pl.BlockSpec((bb, t_new, D_KV), lambda i: (i, 0, 0)),
        pl.BlockSpec((bb, WINDOW, D_KV), lambda i: (i, 0, 0)),
        pl.BlockSpec((bb, WINDOW, D_KV), lambda i: (i, 0, 0)),
    ]
    out_specs = [
        pl.BlockSpec((bb, N_KV_HEADS, nq, HEAD_DIM), lambda i: (i, 0, 0, 0)),
        pl.BlockSpec((bb, WINDOW, D_KV), lambda i: (i, 0, 0)),
        pl.BlockSpec((bb, WINDOW, D_KV), lambda i: (i, 0, 0)),
    ]
    out_shape = [
        jax.ShapeDtypeStruct((nb, N_KV_HEADS, nq, HEAD_DIM), F32),
        jax.ShapeDtypeStruct((nb, WINDOW, D_KV), F32),
        jax.ShapeDtypeStruct((nb, WINDOW, D_KV), F32),
    ]
    return pl.pallas_call(
        functools.partial(_sample_attn_kernel, t_new=t_new, bb=bb),
        grid=(nb // bb,), in_specs=in_specs, out_specs=out_specs, out_shape=out_shape,
        scratch_shapes=[pltpu.VMEM((WINDOW + 8, D_KV), F32), pltpu.VMEM((WINDOW + 8, D_KV), F32)],
        name="sample_attn",
        compiler_params=pltpu.CompilerParams(dimension_semantics=("arbitrary",)),
    )(sinks, q_grouped, k_new, v_new, cache_k, cache_v)


def _sample_post_kernel(x_ref, part_ref, gate_b_ref, o_ref, w_ba_ref, w_out_ref, ln1_g_ref, ln1_b_ref,
                        h_ref, *, alpha):
    merged = part_ref[...] + gate_b_ref[...] * _dot(o_ref[...].astype(BF16), w_ba_ref[...])
    y = _dot(merged.astype(BF16), w_out_ref[...])
    h_ref[...] = _layer_norm(alpha * x_ref[...] + y, ln1_g_ref[...], ln1_b_ref[...])


def _sample_post(x2d, part, gate_b, o2d, p, alpha):
    return pl.pallas_call(
        functools.partial(_sample_post_kernel, alpha=alpha),
        in_specs=[pl.BlockSpec(memory_space=pltpu.VMEM)] * 8,
        out_specs=pl.BlockSpec(memory_space=pltpu.VMEM),
        out_shape=jax.ShapeDtypeStruct(x2d.shape, F32), name="sample_post",
        compiler_params=pltpu.CompilerParams(vmem_limit_bytes=VMEM_LIMIT_BYTES),
    )(x2d, part, gate_b, o2d, p["w_ba"], p["w_out"], p["ln1_g"], p["ln1_b"])


def _top_values(work, count, with_rank=False):
    vals = []
    rank = jnp.full(work.shape, float(count), F32) if with_rank else None
    for i in range(count):
        top = jnp.max(work, axis=0, keepdims=True)
        vals.append(top)
        hit = work == top
        if with_rank:
            rank = jnp.where(hit, float(i), rank)
        if i + 1 < count:
            work = jnp.where(hit, -jnp.inf, work)
    return (vals, rank) if with_rank else vals


def _candidate_rows(a_rows, b_rows):
    assert PEER_TOPK == 16
    a_all = jnp.concatenate(a_rows, axis=0)
    a_low = a_all[0:8]
    row = lax.broadcasted_iota(jnp.int32, a_low.shape, 0)

    def pair(j):
        return a_low + b_rows[j]

    def corner(j):
        return a_rows[0] + b_rows[j]

    t3 = jnp.where(row >= 5, pltpu.roll(pair(4), 5, 0), pair(2))
    t4 = jnp.where(row >= 6, pltpu.roll(pair(6), 6, 0),
                   jnp.where(row >= 4, pltpu.roll(pair(5), 4, 0), pair(3)))
    t5 = pair(7)
    for k in range(6):
        t5 = jnp.where(row >= 2 + k, corner(8 + k), t5)
    t6 = jnp.where(row >= 2, -jnp.inf, jnp.where(row >= 1, corner(15), corner(14)))
    return jnp.concatenate([a_all + b_rows[0], pair(1), t3, t4, t5, t6], axis=0)


def _bf16_rows(row, rows):
    return jnp.broadcast_to(row.astype(BF16), (rows, row.shape[1]))


def _peer_select(s1, s2):
    a_rows = _top_values(s1, PEER_TOPK)
    b_rows, rank2 = _top_values(s2, PEER_TOPK, with_rank=True)
    best = _top_values(_candidate_rows(a_rows, b_rows), PEER_TOPK)
    norm = jnp.zeros_like(best[0])
    for i in range(PEER_TOPK):
        norm = norm + jnp.exp(best[i] - best[0])
    count = jnp.zeros_like(s1)
    for jj in range(PEER_TOPK):
        count = jnp.where(s1 + b_rows[jj] >= best[PEER_TOPK - 1], float(jj + 1), count)
    e1 = jnp.exp(s1 - a_rows[0]) / norm
    e2 = jnp.exp(s2 - b_rows[0])
    return rank2, e2, count, e1


def _peer_gate_tile(act_ref, coef_ref, first_n1, ns, t, sel_refs):
    rank2_ref, e2_ref, count_ref, e1_ref = sel_refs
    ls = slice(t * LANES, (t + 1) * LANES)
    gates = [jnp.zeros((N_SUBKEYS, LANES), BF16) for _ in ns]
    for hd in range(PEER_HEADS):
        rank2 = rank2_ref[hd, t]
        e2 = e2_ref[hd, t]
        for i, n in enumerate(ns):
            count = _bf16_rows(count_ref[hd, t, pl.ds(first_n1 + n, 1), :], N_SUBKEYS)
            e1 = _bf16_rows(e1_ref[hd, t, pl.ds(first_n1 + n, 1), :], N_SUBKEYS)
            gates[i] = gates[i] + jnp.where(rank2 < count, e2, jnp.zeros((), BF16)) * e1
    for i, n in enumerate(ns):
        rs = slice(n * N_SUBKEYS, (n + 1) * N_SUBKEYS)
        coef_ref[rs, ls] = gates[i] * _gelu(act_ref[rs, ls])


def _peer_kernel(h_ref, wq_ref, sk_ref, u_ref, vt_ref, ln_g_ref, ln_b_ref, y_ref,
                 ht_ref, qt_ref, rank2_ref, e2_ref, count_ref, e1_ref, acc_ref,
                 act0_ref, act1_ref, coef0_ref, coef1_ref, *, alpha, tb, se):
    j = pl.program_id(1)
    ntb = tb // LANES
    nblocks = N_EXPERTS // se
    sel_refs = (rank2_ref, e2_ref, count_ref, e1_ref)
    act_refs = (act0_ref, act1_ref)
    coef_refs = (coef0_ref, coef1_ref)

    @pl.when(j == 0)
    def _():
        ht = h_ref[...].T.astype(BF16)
        ht_ref[...] = ht
        qt_ref[...] = _dot(wq_ref[...], ht).astype(BF16)

        def per_head(hd, carry):
            for t in range(ntb):
                ls = slice(t * LANES, (t + 1) * LANES)
                s = []
                for c in range(2):
                    r0 = pl.multiple_of((hd * 2 + c) * PEER_HALF, PEER_HALF)
                    s.append(_dot(sk_ref[hd * 2 + c], qt_ref[pl.ds(r0, PEER_HALF), ls]))
                rank2, e2, count, e1 = _peer_select(s[0], s[1])
                rank2_ref[hd, t] = rank2.astype(BF16)
                e2_ref[hd, t] = e2.astype(BF16)
                count_ref[hd, t] = count
                e1_ref[hd, t] = e1
            return carry

        lax.fori_loop(0, PEER_HEADS, per_head, 0)
        acc_ref[...] = jnp.zeros_like(acc_ref)

    def step_body(q, with_act, with_gate, with_combine):
        piece_t, piece_e, piece_d = tb // PEER_TOKEN_PIECES, se // PEER_ROW_PIECES, acc_ref.shape[0] // PEER_ROW_PIECES
        pieces = []
        for tp in range(PEER_TOKEN_PIECES):
            hs = slice(tp * piece_t, (tp + 1) * piece_t)
            for lo in range(PEER_ROW_PIECES):
                if with_act:
                    pieces.append(("act", hs, slice(lo * piece_e, (lo + 1) * piece_e)))
                if with_combine:
                    pieces.append(("combine", hs, slice(lo * piece_d, (lo + 1) * piece_d)))
        n_sub = se // N_SUBKEYS
        groups = [tuple(range(g, min(g + PEER_GATE_GROUP, n_sub))) for g in range(0, n_sub, PEER_GATE_GROUP)]
        tiles = [(ns, t) for t in range(ntb) for ns in groups] if with_gate else []
        per_piece = -(-len(tiles) // max(len(pieces), 1))
        for k, (kind, hs, es) in enumerate(pieces):
            for n, t in tiles[k * per_piece:(k + 1) * per_piece]:
                _peer_gate_tile(act_refs[1 - q], coef_refs[1 - q], (j - 1) * (se // N_SUBKEYS), n, t, sel_refs)
            if kind == "act":
                act_refs[q][es, hs] = _dot(u_ref[es, :], ht_ref[:, hs]).astype(BF16)
            else:
                acc_ref[es, hs] += _dot(vt_ref[es, :], coef_refs[q][:, hs])
        for n, t in tiles[len(pieces) * per_piece:]:
            _peer_gate_tile(act_refs[1 - q], coef_refs[1 - q], (j - 1) * (se // N_SUBKEYS), n, t, sel_refs)

    assert nblocks % 2 == 0 and nblocks >= 4
    pl.when(j == 0)(functools.partial(step_body, 0, True, False, False))
    pl.when(j == 1)(functools.partial(step_body, 1, True, True, False))
    for q in range(2):
        pl.when((j >= 2) & (j < nblocks) & (j % 2 == q))(functools.partial(step_body, q, True, True, True))
    pl.when(j == nblocks)(functools.partial(step_body, 0, False, True, True))
    pl.when(j == nblocks + 1)(functools.partial(step_body, 1, False, False, True))

    @pl.when(j == pl.num_programs(1) - 1)
    def _():
        y_ref[...] = _layer_norm(alpha * h_ref[...] + acc_ref[...].T, ln_g_ref[...], ln_b_ref[...])


def _peer(h2d, p, alpha, tb=PEER_TOKEN_BLOCK):
    n, D = h2d.shape
    tb = min(tb, n)
    se = p["peer_vt"].shape[2]
    assert n % tb == 0 and tb % LANES == 0 and N_EXPERTS % se == 0 and se % N_SUBKEYS == 0
    ntb = tb // LANES
    qdim = PEER_HEADS * PEER_KEY_DIM
    nblocks = N_EXPERTS // se
    in_specs = [
        pl.BlockSpec((tb, D), lambda i, j: (i, 0)),
        _full((qdim, D)),
        _full((PEER_HEADS * 2, N_SUBKEYS, PEER_HALF)),
        pl.BlockSpec((se, D), lambda i, j: (jnp.minimum(j, nblocks - 1), 0)),
        pl.BlockSpec((None, D, se), lambda i, j: (jnp.clip(j - 2, 0, nblocks - 1), 0, 0)),
        _full((1, D)), _full((1, D)),
    ]
    per_head = (PEER_HEADS, ntb, N_SUBKEYS, LANES)
    scratch = [
        pltpu.VMEM((D, tb), BF16),
        pltpu.VMEM((qdim, tb), BF16),
        pltpu.VMEM(per_head, BF16), pltpu.VMEM(per_head, BF16),
        pltpu.VMEM(per_head, F32), pltpu.VMEM(per_head, F32),
        pltpu.VMEM((D, tb), F32),
    ] + [pltpu.VMEM((se, tb), BF16)] * 4
    return pl.pallas_call(
        functools.partial(_peer_kernel, alpha=alpha, tb=tb, se=se),
        grid=(n // tb, nblocks + 2), in_specs=in_specs,
        out_specs=pl.BlockSpec((tb, D), lambda i, j: (i, 0), pipeline_mode=pl.Buffered(1)),
        out_shape=jax.ShapeDtypeStruct((n, D), F32), scratch_shapes=scratch, name="peer",
        compiler_params=pltpu.CompilerParams(
            dimension_semantics=("arbitrary", "arbitrary"), vmem_limit_bytes=VMEM_LIMIT_BYTES),
    )(h2d, p["peer_wq_t"], p["peer_sk"], p["peer_u"], p["peer_vt"], p["ln2_g"], p["ln2_b"])


def _layer_params(l, w_in, sinks, sgu_ln_g, sgu_ln_b, sgu_w, sgu_b, w_branch_sgu, w_branch_attn, w_out,
                  ln1_g, ln1_b, peer_w_q, peer_sub_keys, peer_u, peer_v, ln2_g, ln2_b):
    return {
        "w_in": w_in[l].astype(BF16),
        "sinks": sinks[l],
        "sgu_ln_g": sgu_ln_g[l][None, :], "sgu_ln_b": sgu_ln_b[l][None, :],
        "sgu_w": sgu_w[l],
        "sgu_b": sgu_b[l], "sgu_bcol": sgu_b[l].T,
        "w_bs": w_branch_sgu[l].astype(BF16), "w_ba": w_branch_attn[l].astype(BF16),
        "w_out": w_out[l].astype(BF16),
        "ln1_g": ln1_g[l][None, :], "ln1_b": ln1_b[l][None, :],
        "peer_wq_t": peer_w_q[l].T.astype(BF16),
        "peer_sk": peer_sub_keys[l].reshape(PEER_HEADS * 2, N_SUBKEYS, PEER_HALF).astype(BF16),
        "peer_u": peer_u[l].astype(BF16),
        "peer_vt": peer_v[l].reshape(N_EXPERTS // PEER_EXPERT_BLOCK, PEER_EXPERT_BLOCK, D_MODEL)
                            .transpose(0, 2, 1).astype(BF16),
        "ln2_g": ln2_g[l][None, :], "ln2_b": ln2_b[l][None, :],
    }


def _prompt_layer(x, p, alpha):
    B, S, D = x.shape
    h, k_last, v_last, sv_last = _prompt_mixer(x, p, alpha)
    y = _peer(h.reshape(B * S, D), p, alpha).reshape(B, S, D)
    return (y, k_last.reshape(B, WINDOW, N_KV_HEADS, HEAD_DIM), v_last.reshape(B, WINDOW, N_KV_HEADS, HEAD_DIM),
            sv_last.reshape(B, CHUNK, N_SGU_GROUPS, SGU_GROUP_DIM))


def _sample_layer(x, cache_k, cache_v, p, alpha):
    B, T, D = x.shape
    x2d = x.reshape(B * T, D)
    q, k_new, v_new, svn, part, gate_b = _sample_pre(x2d, p, T)
    qg = q.reshape(B, T, N_KV_HEADS, Q_PER_KV, HEAD_DIM).transpose(0, 2, 3, 1, 4)
    qg = qg.reshape(B, N_KV_HEADS, Q_PER_KV * T, HEAD_DIM)
    og, new_k, new_v = _sample_attn(qg, k_new.reshape(B, T, D_KV), v_new.reshape(B, T, D_KV),
                                    cache_k.reshape(B, WINDOW, D_KV), cache_v.reshape(B, WINDOW, D_KV),
                                    p["sinks"], T)
    o2d = og.reshape(B, N_KV_HEADS, Q_PER_KV, T, HEAD_DIM).transpose(0, 3, 1, 2, 4).reshape(B * T, D_ATTN)
    h = _sample_post(x2d, part, gate_b, o2d, p, alpha)
    y = _peer(h, p, alpha).reshape(B, T, D)
    return (y, new_k.reshape(B, WINDOW, N_KV_HEADS, HEAD_DIM), new_v.reshape(B, WINDOW, N_KV_HEADS, HEAD_DIM),
            svn.reshape(B, T, N_SGU_GROUPS, SGU_GROUP_DIM))


def kernel(x_prompt, x_sample, cache_k, cache_v, w_in, sinks, sgu_ln_g, sgu_ln_b, sgu_w, sgu_b, w_branch_sgu, w_branch_attn, w_out, ln1_g, ln1_b, peer_w_q, peer_sub_keys, peer_u, peer_v, ln2_g, ln2_b):
    depth = w_in.shape[0]
    alpha = (2.0 * depth) ** 0.25
    yp, ys = x_prompt, x_sample
    outs = [[] for _ in range(6)]
    for l in range(depth):
        p = _layer_params(l, w_in, sinks, sgu_ln_g, sgu_ln_b, sgu_w, sgu_b, w_branch_sgu, w_branch_attn,
                          w_out, ln1_g, ln1_b, peer_w_q, peer_sub_keys, peer_u, peer_v, ln2_g, ln2_b)
        yp, kp, vp, svp = _prompt_layer(yp, p, alpha)
        ys, ksm, vsm, svs = _sample_layer(ys, cache_k[l], cache_v[l], p, alpha)
        for acc, val in zip(outs, (kp, vp, ksm, vsm, svp, svs)):
            acc.append(val)
    return (yp, ys) + tuple(jnp.stack(o) for o in outs)
```

```python
import functools

import jax
import jax.numpy as jnp
from jax import lax
from jax.experimental import pallas as pl
from jax.experimental.pallas import tpu as pltpu

F32 = jnp.float32
BF16 = jnp.bfloat16

D_MODEL = 1024
N_Q_HEADS = 8
N_KV_HEADS = 2
HEAD_DIM = 64
Q_PER_KV = N_Q_HEADS // N_KV_HEADS
D_ATTN = N_Q_HEADS * HEAD_DIM
D_KV = N_KV_HEADS * HEAD_DIM
WINDOW = 128
CHUNK = 128
D_SGU = D_MODEL // 2
N_SGU_GROUPS = 4
SGU_GROUP_DIM = D_SGU // N_SGU_GROUPS
PEER_HEADS = 8
N_SUBKEYS = 128
N_EXPERTS = N_SUBKEYS * N_SUBKEYS
PEER_TOPK = 16
PEER_KEY_DIM = 256
PEER_HALF = PEER_KEY_DIM // 2
LN_EPS = 1e-5
NEG_INF = -1e30
D_IN = D_ATTN + 2 * D_KV + 2 * D_SGU + 2 * D_MODEL
_OFF_Q = 0
_OFF_K = D_ATTN
_OFF_V = _OFF_K + D_KV
_OFF_U = _OFF_V + D_KV
_OFF_SV = _OFF_U + D_SGU
_OFF_GA = _OFF_SV + D_SGU
_OFF_GB = _OFF_GA + D_MODEL

LANES = 128
VMEM_LIMIT_BYTES = 56 * 1024 * 1024

PROMPT_BLOCK = 512
SAMPLE_ATTN_BATCH = 8
PEER_TOKEN_BLOCK = 1024
PEER_EXPERT_BLOCK = 512
PEER_GATE_GROUP = 1
PEER_TOKEN_PIECES = 2
PEER_ROW_PIECES = 2


def _gelu(x):
    return 0.5 * x * (1.0 + jnp.tanh(0.7978845608028654 * (x + 0.044715 * (x * x * x))))


def _sigmoid(x):
    return 1.0 / (1.0 + jnp.exp(-x))


def _layer_norm(x, g, b):
    mu = jnp.mean(x, -1, keepdims=True)
    xc = x - mu
    var = jnp.mean(xc * xc, -1, keepdims=True)
    return xc * lax.rsqrt(var + LN_EPS) * g + b


def _dot(a, b):
    return jnp.dot(a, b, preferred_element_type=F32)


def _dot_nt(a, b):
    return lax.dot_general(a, b, (((1,), (1,)), ((), ())), preferred_element_type=F32)


def _sink_softmax_pv(logits, sink_col, vals):
    m = jnp.maximum(jnp.max(logits, -1, keepdims=True), sink_col)
    p = jnp.exp(logits - m)
    denom = jnp.sum(p, -1, keepdims=True) + jnp.exp(sink_col - m)
    return _dot(p.astype(BF16), vals) / denom


def _prompt_mixer_kernel(sinks_ref, x_ref, w_in_ref, sgu_g_ref, sgu_b_ref, sgu_w_ref, sgu_bcol_ref,
                         w_bs_ref, w_ba_ref, w_out_ref, ln1_g_ref, ln1_b_ref,
                         h_ref, k_ref, v_ref, sv_ref,
                         kext_ref, vext_ref, sgu_o_ref, attn_o_ref, *, alpha, ts):
    s = pl.program_id(1)
    last = pl.num_programs(1) - 1
    nblk = ts // WINDOW

    x = x_ref[...]
    z = _dot(x.astype(BF16), w_in_ref[...])
    k = z[:, _OFF_K:_OFF_V]
    v = z[:, _OFF_V:_OFF_U]
    u = _gelu(z[:, _OFF_U:_OFF_SV])
    svn = _layer_norm(_gelu(z[:, _OFF_SV:_OFF_GA]), sgu_g_ref[...], sgu_b_ref[...])

    @pl.when(s == 0)
    def _():
        kext_ref[0:WINDOW, :] = jnp.zeros((WINDOW, D_KV), BF16)
        vext_ref[0:WINDOW, :] = jnp.zeros((WINDOW, D_KV), BF16)

    kext_ref[WINDOW:, :] = k.astype(BF16)
    vext_ref[WINDOW:, :] = v.astype(BF16)

    @pl.when(s == last)
    def _():
        k_ref[...] = k[ts - WINDOW:, :]
        v_ref[...] = v[ts - WINDOW:, :]
        sv_ref[...] = svn[ts - CHUNK:, :]

    row = lax.broadcasted_iota(jnp.int32, (CHUNK, CHUNK), 0)
    col = lax.broadcasted_iota(jnp.int32, (CHUNK, CHUNK), 1)
    for g in range(N_SGU_GROUPS):
        wt = jnp.where(col <= row, sgu_w_ref[g], 0.0).astype(BF16)
        bcol = sgu_bcol_ref[:, g:g + 1]
        gs = slice(g * SGU_GROUP_DIM, (g + 1) * SGU_GROUP_DIM)
        for c in range(ts // CHUNK):
            rs = slice(c * CHUNK, (c + 1) * CHUNK)
            mix = _dot(wt, svn[rs, gs].astype(BF16)) + bcol
            sgu_o_ref[rs, gs] = (u[rs, gs] * mix).astype(BF16)

    nq = Q_PER_KV * WINDOW
    qi = lax.broadcasted_iota(jnp.int32, (nq, 2 * WINDOW), 0) % WINDOW
    kj = lax.broadcasted_iota(jnp.int32, (nq, 2 * WINDOW), 1)
    rel = qi + WINDOW - kj
    bias = jnp.where((rel >= 0) & (rel < WINDOW), 0.0, NEG_INF).astype(F32)
    bias_first = jnp.where(s > 0, bias, jnp.where(kj < WINDOW, NEG_INF, bias))
    scale = HEAD_DIM ** -0.5
    for i in range(nblk):
        qs = slice(i * WINDOW, (i + 1) * WINDOW)
        heads = [None] * N_Q_HEADS
        for g in range(N_KV_HEADS):
            ds = slice(g * HEAD_DIM, (g + 1) * HEAD_DIM)
            kb = kext_ref[i * WINDOW:(i + 2) * WINDOW, ds]
            vb = vext_ref[i * WINDOW:(i + 2) * WINDOW, ds]
            q4 = jnp.concatenate(
                [z[qs, (g * Q_PER_KV + r) * HEAD_DIM:(g * Q_PER_KV + r + 1) * HEAD_DIM]
                 for r in range(Q_PER_KV)], axis=0).astype(BF16)
            logits = _dot_nt(q4, kb) * scale + (bias_first if i == 0 else bias)
            sink_col = jnp.concatenate(
                [jnp.full((WINDOW, 1), sinks_ref[g * Q_PER_KV + r], F32) for r in range(Q_PER_KV)], axis=0)
            o4 = _sink_softmax_pv(logits, sink_col, vb)
            for r in range(Q_PER_KV):
                heads[g * Q_PER_KV + r] = o4[r * WINDOW:(r + 1) * WINDOW, :]
        attn_o_ref[qs, :] = jnp.concatenate(heads, axis=1).astype(BF16)

    kext_ref[0:WINDOW, :] = kext_ref[ts:ts + WINDOW, :]
    vext_ref[0:WINDOW, :] = vext_ref[ts:ts + WINDOW, :]

    merged = (_sigmoid(z[:, _OFF_GA:_OFF_GB]) * _dot(sgu_o_ref[...], w_bs_ref[...])
              + _sigmoid(z[:, _OFF_GB:]) * _dot(attn_o_ref[...], w_ba_ref[...]))
    y = _dot(merged.astype(BF16), w_out_ref[...])
    h_ref[...] = _layer_norm(alpha * x + y, ln1_g_ref[...], ln1_b_ref[...])


def _full(shape):
    return pl.BlockSpec(shape, lambda *_: (0,) * len(shape), pipeline_mode=pl.Buffered(1))


def _prompt_mixer(x, p, alpha, ts=PROMPT_BLOCK):
    B, S, D = x.shape
    ts = min(ts, S)
    assert S % ts == 0 and ts % WINDOW == 0 and ts % CHUNK == 0
    grid = (B, S // ts)
    smem = pl.BlockSpec(memory_space=pltpu.SMEM)
    in_specs = [
        smem,
        pl.BlockSpec((None, ts, D), lambda b, s: (b, s, 0)),
        _full((D, D_IN)), _full((1, D_SGU)), _full((1, D_SGU)),
        _full((N_SGU_GROUPS, CHUNK, CHUNK)), _full((CHUNK, N_SGU_GROUPS)),
        _full((D_SGU, D)), _full((D_ATTN, D)), _full((D, D)), _full((1, D)), _full((1, D)),
    ]
    out_specs = [
        pl.BlockSpec((None, ts, D), lambda b, s: (b, s, 0)),
        pl.BlockSpec((None, WINDOW, D_KV), lambda b, s: (b, 0, 0)),
        pl.BlockSpec((None, WINDOW, D_KV), lambda b, s: (b, 0, 0)),
        pl.BlockSpec((None, CHUNK, D_SGU), lambda b, s: (b, 0, 0)),
    ]
    out_shape = [
        jax.ShapeDtypeStruct((B, S, D), F32),
        jax.ShapeDtypeStruct((B, WINDOW, D_KV), F32),
        jax.ShapeDtypeStruct((B, WINDOW, D_KV), F32),
        jax.ShapeDtypeStruct((B, CHUNK, D_SGU), F32),
    ]
    scratch = [
        pltpu.VMEM((ts + WINDOW, D_KV), BF16),
        pltpu.VMEM((ts + WINDOW, D_KV), BF16),
        pltpu.VMEM((ts, D_SGU), BF16),
        pltpu.VMEM((ts, D_ATTN), BF16),
    ]
    return pl.pallas_call(
        functools.partial(_prompt_mixer_kernel, alpha=alpha, ts=ts),
        grid=grid, in_specs=in_specs, out_specs=out_specs, out_shape=out_shape,
        scratch_shapes=scratch, name="prompt_mixer",
        compiler_params=pltpu.CompilerParams(
            dimension_semantics=("arbitrary", "arbitrary"), vmem_limit_bytes=VMEM_LIMIT_BYTES),
    )(p["sinks"], x, p["w_in"], p["sgu_ln_g"], p["sgu_ln_b"], p["sgu_w"], p["sgu_bcol"],
      p["w_bs"], p["w_ba"], p["w_out"], p["ln1_g"], p["ln1_b"])


def _sample_pre_kernel(sgu_w_ref, sgu_b_ref, x_ref, w_in_ref, sgu_g_ref, sgu_bb_ref, w_bs_ref,
                       q_ref, k_ref, v_ref, sv_ref, part_ref, gate_b_ref, *, t_new):
    x = x_ref[...]
    n = x.shape[0]
    z = _dot(x.astype(BF16), w_in_ref[...])
    q_ref[...] = z[:, _OFF_Q:_OFF_K]
    k_ref[...] = z[:, _OFF_K:_OFF_V]
    v_ref[...] = z[:, _OFF_V:_OFF_U]
    u = _gelu(z[:, _OFF_U:_OFF_SV])
    svn = _layer_norm(_gelu(z[:, _OFF_SV:_OFF_GA]), sgu_g_ref[...], sgu_bb_ref[...])
    sv_ref[...] = svn
    t_of_row = lax.broadcasted_iota(jnp.int32, (n, SGU_GROUP_DIM), 0) % t_new
    pieces = []
    for g in range(N_SGU_GROUPS):
        vg = svn[:, g * SGU_GROUP_DIM:(g + 1) * SGU_GROUP_DIM]
        mix = jnp.zeros((n, SGU_GROUP_DIM), F32)
        for t in range(t_new):
            mix = jnp.where(t_of_row == t, sgu_b_ref[g, t], mix)
        for d in range(t_new):
            coef = jnp.zeros((n, SGU_GROUP_DIM), F32)
            for t in range(d, t_new):
                coef = jnp.where(t_of_row == t, sgu_w_ref[g, t * t_new + t - d], coef)
            shifted = vg if d == 0 else pltpu.roll(vg, d, 0)
            mix = mix + coef * shifted
        pieces.append(u[:, g * SGU_GROUP_DIM:(g + 1) * SGU_GROUP_DIM] * mix)
    sgu_o = jnp.concatenate(pieces, axis=1).astype(BF16)
    part_ref[...] = _sigmoid(z[:, _OFF_GA:_OFF_GB]) * _dot(sgu_o, w_bs_ref[...])
    gate_b_ref[...] = _sigmoid(z[:, _OFF_GB:])


def _sample_pre(x2d, p, t_new):
    n, D = x2d.shape
    smem = pl.BlockSpec(memory_space=pltpu.SMEM)
    out_shape = [
        jax.ShapeDtypeStruct((n, D_ATTN), F32),
        jax.ShapeDtypeStruct((n, D_KV), F32),
        jax.ShapeDtypeStruct((n, D_KV), F32),
        jax.ShapeDtypeStruct((n, D_SGU), F32),
        jax.ShapeDtypeStruct((n, D), F32),
        jax.ShapeDtypeStruct((n, D), F32),
    ]
    return pl.pallas_call(
        functools.partial(_sample_pre_kernel, t_new=t_new),
        in_specs=[smem, smem] + [pl.BlockSpec(memory_space=pltpu.VMEM)] * 5,
        out_specs=[pl.BlockSpec(memory_space=pltpu.VMEM)] * 6,
        out_shape=out_shape, name="sample_pre",
        compiler_params=pltpu.CompilerParams(vmem_limit_bytes=VMEM_LIMIT_BYTES),
    )(p["sgu_w"][:, :t_new, :t_new].reshape(N_SGU_GROUPS, t_new * t_new), p["sgu_b"],
      x2d, p["w_in"], p["sgu_ln_g"], p["sgu_ln_b"], p["w_bs"])


def _sample_attn_kernel(sinks_ref, q_ref, kn_ref, vn_ref, ck_ref, cv_ref,
                        o_ref, nk_ref, nv_ref, kbuf_ref, vbuf_ref, *, t_new, bb):
    nq = Q_PER_KV * t_new
    nkeys = WINDOW + 8
    row = lax.broadcasted_iota(jnp.int32, (nq, nkeys), 0)
    kj = lax.broadcasted_iota(jnp.int32, (nq, nkeys), 1)
    t_q = row % t_new
    bias = jnp.where((kj > t_q) & (kj <= t_q + WINDOW), 0.0, NEG_INF).astype(F32)
    r_of_row = lax.broadcasted_iota(jnp.int32, (nq, 1), 0) // t_new
    scale = HEAD_DIM ** -0.5
    pad = jnp.zeros((8 - t_new, D_KV), F32)
    for b in range(bb):
        kbuf_ref[0:WINDOW, :] = ck_ref[b]
        kbuf_ref[WINDOW:WINDOW + t_new, :] = kn_ref[b]
        kbuf_ref[WINDOW + t_new:, :] = pad
        vbuf_ref[0:WINDOW, :] = cv_ref[b]
        vbuf_ref[WINDOW:WINDOW + t_new, :] = vn_ref[b]
        vbuf_ref[WINDOW + t_new:, :] = pad
        nk_ref[b] = kbuf_ref[t_new:t_new + WINDOW, :]
        nv_ref[b] = vbuf_ref[t_new:t_new + WINDOW, :]
        for g in range(N_KV_HEADS):
            ds = slice(g * HEAD_DIM, (g + 1) * HEAD_DIM)
            kg = kbuf_ref[:, ds].astype(BF16)
            vg = vbuf_ref[:, ds].astype(BF16)
            logits = _dot_nt(q_ref[b, g].astype(BF16), kg) * scale + bias
            sink_col = jnp.zeros((nq, 1), F32)
            for r in range(Q_PER_KV):
                sink_col = jnp.where(r_of_row == r, sinks_ref[g * Q_PER_KV + r], sink_col)
            o_ref[b, g] = _sink_softmax_pv(logits, sink_col, vg)


def _sample_attn(q_grouped, k_new, v_new, cache_k, cache_v, sinks, t_new, bb=SAMPLE_ATTN_BATCH):
    nb = q_grouped.shape[0]
    bb = min(bb, nb)
    assert nb % bb == 0 and t_new <= 8
    nq = Q_PER_KV * t_new
    smem = pl.BlockSpec(memory_space=pltpu.SMEM)
    in_specs = [
        smem,
        pl.BlockSpec((bb, N_KV_HEADS, nq, HEAD_DIM), lambda i: (i, 0, 0, 0)),
        pl.BlockSpec((bb, t_new, D_KV), lambda i: (i, 0, 0)),
        pl.BlockSpec((bb, t_new, D_KV), lambda i: (i, 0, 0)),
        pl.BlockSpec((bb, WINDOW, D_KV), lambda i: (i, 0, 0)),
        pl.BlockSpec((bb, WINDOW, D_KV), lambda i: (i, 0, 0)),
    ]
    out_specs = [
        pl.BlockSpec((bb, N_KV_HEADS, nq, HEAD_DIM), lambda i: (i, 0, 0, 0)),
        pl.BlockSpec((bb, WINDOW, D_KV), lambda i: (i, 0, 0)),
        pl.BlockSpec((bb, WINDOW, D_KV), lambda i: (i, 0, 0)),
    ]
    out_shape = [
        jax.ShapeDtypeStruct((nb, N_KV_HEADS, nq, HEAD_DIM), F32),
        jax.ShapeDtypeStruct((nb, WINDOW, D_KV), F32),
        jax.ShapeDtypeStruct((nb, WINDOW, D_KV), F32),
    ]
    return pl.pallas_call(
        functools.partial(_sample_attn_kernel, t_new=t_new, bb=bb),
        grid=(nb // bb,), in_specs=in_specs, out_specs=out_specs, out_shape=out_shape,
        scratch_shapes=[pltpu.VMEM((WINDOW + 8, D_KV), F32), pltpu.VMEM((WINDOW + 8, D_KV), F32)],
        name="sample_attn",
        compiler_params=pltpu.CompilerParams(dimension_semantics=("arbitrary",)),
    )(sinks, q_grouped, k_new, v_new, cache_k, cache_v)


def _sample_post_kernel(x_ref, part_ref, gate_b_ref, o_ref, w_ba_ref, w_out_ref, ln1_g_ref, ln1_b_ref,
                        h_ref, *, alpha):
    merged = part_ref[...] + gate_b_ref[...] * _dot(o_ref[...].astype(BF16), w_ba_ref[...])
    y = _dot(merged.astype(BF16), w_out_ref[...])
    h_ref[...] = _layer_norm(alpha * x_ref[...] + y, ln1_g_ref[...], ln1_b_ref[...])


def _sample_post(x2d, part, gate_b, o2d, p, alpha):
    return pl.pallas_call(
        functools.partial(_sample_post_kernel, alpha=alpha),
        in_specs=[pl.BlockSpec(memory_space=pltpu.VMEM)] * 8,
        out_specs=pl.BlockSpec(memory_space=pltpu.VMEM),
        out_shape=jax.ShapeDtypeStruct(x2d.shape, F32), name="sample_post",
        compiler_params=pltpu.CompilerParams(vmem_limit_bytes=VMEM_LIMIT_BYTES),
    )(x2d, part, gate_b, o2d, p["w_ba"], p["w_out"], p["ln1_g"], p["ln1_b"])


def _top_values(work, count, with_rank=False):
    vals = []
    rank = jnp.full(work.shape, float(count), F32) if with_rank else None
    for i in range(count):
        top = jnp.max(work, axis=0, keepdims=True)
        vals.append(top)
        hit = work == top
        if with_rank:
            rank = jnp.where(hit, float(i), rank)
        if i + 1 < count:
            work = jnp.where(hit, -jnp.inf, work)
    return (vals, rank) if with_rank else vals


def _candidate_rows(a_rows, b_rows):
    assert PEER_TOPK == 16
    a_all = jnp.concatenate(a_rows, axis=0)
    a_low = a_all[0:8]
    row = lax.broadcasted_iota(jnp.int32, a_low.shape, 0)

    def pair(j):
        return a_low + b_rows[j]

    def corner(j):
        return a_rows[0] + b_rows[j]

    t3 = jnp.where(row >= 5, pltpu.roll(pair(4), 5, 0), pair(2))
    t4 = jnp.where(row >= 6, pltpu.roll(pair(6), 6, 0),
                   jnp.where(row >= 4, pltpu.roll(pair(5), 4, 0), pair(3)))
    t5 = pair(7)
    for k in range(6):
        t5 = jnp.where(row >= 2 + k, corner(8 + k), t5)
    t6 = jnp.where(row >= 2, -jnp.inf, jnp.where(row >= 1, corner(15), corner(14)))
    return jnp.concatenate([a_all + b_rows[0], pair(1), t3, t4, t5, t6], axis=0)


def _bf16_rows(row, rows):
    return jnp.broadcast_to(row.astype(BF16), (rows, row.shape[1]))


def _peer_select(s1, s2):
    a_rows = _top_values(s1, PEER_TOPK)
    b_rows, rank2 = _top_values(s2, PEER_TOPK, with_rank=True)
    best = _top_values(_candidate_rows(a_rows, b_rows), PEER_TOPK)
    norm = jnp.zeros_like(best[0])
    for i in range(PEER_TOPK):
        norm = norm + jnp.exp(best[i] - best[0])
    count = jnp.zeros_like(s1)
    for jj in range(PEER_TOPK):
        count = jnp.where(s1 + b_rows[jj] >= best[PEER_TOPK - 1], float(jj + 1), count)
    e1 = jnp.exp(s1 - a_rows[0]) / norm
    e2 = jnp.exp(s2 - b_rows[0])
    return rank2, e2, count, e1


def _peer_gate_tile(act_ref, coef_ref, first_n1, ns, t, sel_refs):
    rank2_ref, e2_ref, count_ref, e1_ref = sel_refs
    ls = slice(t * LANES, (t + 1) * LANES)
    gates = [jnp.zeros((N_SUBKEYS, LANES), BF16) for _ in ns]
    for hd in range(PEER_HEADS):
        rank2 = rank2_ref[hd, t]
        e2 = e2_ref[hd, t, 16:, :]
        for i, n in enumerate(ns):
            count = _bf16_rows(count_ref[hd, t, pl.ds(first_n1 + n, 1), :], N_SUBKEYS)
            e1 = _bf16_rows(e1_ref[hd, t, pl.ds(first_n1 + n, 1), :], N_SUBKEYS)
            gates[i] = gates[i] + jnp.where(rank2 < count, e2, jnp.zeros((), BF16)) * e1
    for i, n in enumerate(ns):
        rs = slice(n * N_SUBKEYS, (n + 1) * N_SUBKEYS)
        coef_ref[rs, ls] = gates[i] * _gelu(act_ref[rs, ls])


def _peer_kernel(h_ref, wq_ref, sk_ref, u_ref, vt_ref, ln_g_ref, ln_b_ref, y_ref,
                 ht_ref, qt_ref, rank2_ref, e2_ref, count_ref, e1_ref, acc_ref,
                 act0_ref, act1_ref, coef0_ref, coef1_ref, *, alpha, tb, se):
    j = pl.program_id(1)
    ntb = tb // LANES
    nblocks = N_EXPERTS // se
    sel_refs = (rank2_ref, e2_ref, count_ref, e1_ref)
    act_refs = (act0_ref, act1_ref)
    coef_refs = (coef0_ref, coef1_ref)

    @pl.when(j == 0)
    def _():
        ht = h_ref[...].T.astype(BF16)
        ht_ref[...] = ht
        qt_ref[...] = _dot(wq_ref[...], ht).astype(BF16)

        def per_head(hd, carry):
            for t in range(ntb):
                ls = slice(t * LANES, (t + 1) * LANES)
                s = []
                for c in range(2):
                    r0 = pl.multiple_of((hd * 2 + c) * PEER_HALF, PEER_HALF)
                    s.append(_dot(sk_ref[hd * 2 + c], qt_ref[pl.ds(r0, PEER_HALF), ls]))
                rank2, e2, count, e1 = _peer_select(s[0], s[1])
                rank2_ref[hd, t] = rank2.astype(BF16)
                e2_ref[hd, t, 16:, :] = e2.astype(BF16)
                count_ref[hd, t] = count
                e1_ref[hd, t] = e1
            return carry

        lax.fori_loop(0, PEER_HEADS, per_head, 0)
        acc_ref[...] = jnp.zeros_like(acc_ref)

    def step_body(q, with_act, with_gate, with_combine):
        piece_t, piece_e, piece_d = tb // PEER_TOKEN_PIECES, se // PEER_ROW_PIECES, acc_ref.shape[0] // PEER_ROW_PIECES
        pieces = []
        for tp in range(PEER_TOKEN_PIECES):
            hs = slice(tp * piece_t, (tp + 1) * piece_t)
            for lo in range(PEER_ROW_PIECES):
                if with_act:
                    pieces.append(("act", hs, slice(lo * piece_e, (lo + 1) * piece_e)))
                if with_combine:
                    pieces.append(("combine", hs, slice(lo * piece_d, (lo + 1) * piece_d)))
        n_sub = se // N_SUBKEYS
        groups = [tuple(range(g, min(g + PEER_GATE_GROUP, n_sub))) for g in range(0, n_sub, PEER_GATE_GROUP)]
        tiles = [(ns, t) for t in range(ntb) for ns in groups] if with_gate else []
        per_piece = -(-len(tiles) // max(len(pieces), 1))
        for k, (kind, hs, es) in enumerate(pieces):
            for n, t in tiles[k * per_piece:(k + 1) * per_piece]:
                _peer_gate_tile(act_refs[1 - q], coef_refs[1 - q], (j - 1) * (se // N_SUBKEYS), n, t, sel_refs)
            if kind == "act":
                act_refs[q][es, hs] = _dot(u_ref[es, :], ht_ref[:, hs]).astype(BF16)
            else:
                acc_ref[es, hs] += _dot(vt_ref[es, :], coef_refs[q][:, hs])
        for n, t in tiles[len(pieces) * per_piece:]:
            _peer_gate_tile(act_refs[1 - q], coef_refs[1 - q], (j - 1) * (se // N_SUBKEYS), n, t, sel_refs)

    assert nblocks % 2 == 0 and nblocks >= 4
    pl.when(j == 0)(functools.partial(step_body, 0, True, False, False))
    pl.when(j == 1)(functools.partial(step_body, 1, True, True, False))
    for q in range(2):
        pl.when((j >= 2) & (j < nblocks) & (j % 2 == q))(functools.partial(step_body, q, True, True, True))
    pl.when(j == nblocks)(functools.partial(step_body, 0, False, True, True))
    pl.when(j == nblocks + 1)(functools.partial(step_body, 1, False, False, True))

    @pl.when(j == pl.num_programs(1) - 1)
    def _():
        y_ref[...] = _layer_norm(alpha * h_ref[...] + acc_ref[...].T, ln_g_ref[...], ln_b_ref[...])


def _peer(h2d, p, alpha, tb=PEER_TOKEN_BLOCK):
    n, D = h2d.shape
    tb = min(tb, n)
    se = p["peer_vt"].shape[2]
    assert n % tb == 0 and tb % LANES == 0 and N_EXPERTS % se == 0 and se % N_SUBKEYS == 0
    ntb = tb // LANES
    qdim = PEER_HEADS * PEER_KEY_DIM
    nblocks = N_EXPERTS // se
    in_specs = [
        pl.BlockSpec((tb, D), lambda i, j: (i, 0)),
        _full((qdim, D)),
        _full((PEER_HEADS * 2, N_SUBKEYS, PEER_HALF)),
        pl.BlockSpec((se, D), lambda i, j: (jnp.minimum(j, nblocks - 1), 0)),
        pl.BlockSpec((None, D, se), lambda i, j: (jnp.clip(j - 2, 0, nblocks - 1), 0, 0)),
        _full((1, D)), _full((1, D)),
    ]
    per_head = (PEER_HEADS, ntb, N_SUBKEYS, LANES)
    scratch = [
        pltpu.VMEM((D, tb), BF16),
        pltpu.VMEM((qdim, tb), BF16),
        pltpu.VMEM(per_head, BF16), pltpu.VMEM((PEER_HEADS, ntb, N_SUBKEYS + 16, LANES), BF16),
        pltpu.VMEM(per_head, F32), pltpu.VMEM(per_head, F32),
        pltpu.VMEM((D, tb), F32),
    ] + [pltpu.VMEM((se, tb), BF16)] * 4
    return pl.pallas_call(
        functools.partial(_peer_kernel, alpha=alpha, tb=tb, se=se),
        grid=(n // tb, nblocks + 2), in_specs=in_specs,
        out_specs=pl.BlockSpec((tb, D), lambda i, j: (i, 0), pipeline_mode=pl.Buffered(1)),
        out_shape=jax.ShapeDtypeStruct((n, D), F32), scratch_shapes=scratch, name="peer",
        compiler_params=pltpu.CompilerParams(
            dimension_semantics=("arbitrary", "arbitrary"), vmem_limit_bytes=VMEM_LIMIT_BYTES),
    )(h2d, p["peer_wq_t"], p["peer_sk"], p["peer_u"], p["peer_vt"], p["ln2_g"], p["ln2_b"])


def _layer_params(l, w_in, sinks, sgu_ln_g, sgu_ln_b, sgu_w, sgu_b, w_branch_sgu, w_branch_attn, w_out,
                  ln1_g, ln1_b, peer_w_q, peer_sub_keys, peer_u, peer_v, ln2_g, ln2_b):
    return {
        "w_in": w_in[l].astype(BF16),
        "sinks": sinks[l],
        "sgu_ln_g": sgu_ln_g[l][None, :], "sgu_ln_b": sgu_ln_b[l][None, :],
        "sgu_w": sgu_w[l],
        "sgu_b": sgu_b[l], "sgu_bcol": sgu_b[l].T,
        "w_bs": w_branch_sgu[l].astype(BF16), "w_ba": w_branch_attn[l].astype(BF16),
        "w_out": w_out[l].astype(BF16),
        "ln1_g": ln1_g[l][None, :], "ln1_b": ln1_b[l][None, :],
        "peer_wq_t": peer_w_q[l].T.astype(BF16),
        "peer_sk": peer_sub_keys[l].reshape(PEER_HEADS * 2, N_SUBKEYS, PEER_HALF).astype(BF16),
        "peer_u": peer_u[l].astype(BF16),
        "peer_vt": peer_v[l].reshape(N_EXPERTS // PEER_EXPERT_BLOCK, PEER_EXPERT_BLOCK, D_MODEL)
                            .transpose(0, 2, 1).astype(BF16),
        "ln2_g": ln2_g[l][None, :], "ln2_b": ln2_b[l][None, :],
    }


def _prompt_layer(x, p, alpha):
    B, S, D = x.shape
    h, k_last, v_last, sv_last = _prompt_mixer(x, p, alpha)
    y = _peer(h.reshape(B * S, D), p, alpha).reshape(B, S, D)
    return (y, k_last.reshape(B, WINDOW, N_KV_HEADS, HEAD_DIM), v_last.reshape(B, WINDOW, N_KV_HEADS, HEAD_DIM),
            sv_last.reshape(B, CHUNK, N_SGU_GROUPS, SGU_GROUP_DIM))


def _sample_layer(x, cache_k, cache_v, p, alpha):
    B, T, D = x.shape
    x2d = x.reshape(B * T, D)
    q, k_new, v_new, svn, part, gate_b = _sample_pre(x2d, p, T)
    qg = q.reshape(B, T, N_KV_HEADS, Q_PER_KV, HEAD_DIM).transpose(0, 2, 3, 1, 4)
    qg = qg.reshape(B, N_KV_HEADS, Q_PER_KV * T, HEAD_DIM)
    og, new_k, new_v = _sample_attn(qg, k_new.reshape(B, T, D_KV), v_new.reshape(B, T, D_KV),
                                    cache_k.reshape(B, WINDOW, D_KV), cache_v.reshape(B, WINDOW, D_KV),
                                    p["sinks"], T)
    o2d = og.reshape(B, N_KV_HEADS, Q_PER_KV, T, HEAD_DIM).transpose(0, 3, 1, 2, 4).reshape(B * T, D_ATTN)
    h = _sample_post(x2d, part, gate_b, o2d, p, alpha)
    y = _peer(h, p, alpha).reshape(B, T, D)
    return (y, new_k.reshape(B, WINDOW, N_KV_HEADS, HEAD_DIM), new_v.reshape(B, WINDOW, N_KV_HEADS, HEAD_DIM),
            svn.reshape(B, T, N_SGU_GROUPS, SGU_GROUP_DIM))


def kernel(x_prompt, x_sample, cache_k, cache_v, w_in, sinks, sgu_ln_g, sgu_ln_b, sgu_w, sgu_b, w_branch_sgu, w_branch_attn, w_out, ln1_g, ln1_b, peer_w_q, peer_sub_keys, peer_u, peer_v, ln2_g, ln2_b):
    depth = w_in.shape[0]
    alpha = (2.0 * depth) ** 0.25
    yp, ys = x_prompt, x_sample
    outs = [[] for _ in range(6)]
    for l in range(depth):
        p = _layer_params(l, w_in, sinks, sgu_ln_g, sgu_ln_b, sgu_w, sgu_b, w_branch_sgu, w_branch_attn,
                          w_out, ln1_g, ln1_b, peer_w_q, peer_sub_keys, peer_u, peer_v, ln2_g, ln2_b)
        yp, kp, vp, svp = _prompt_layer(yp, p, alpha)
        ys, ksm, vsm, svs = _sample_layer(ys, cache_k[l], cache_v[l], p, alpha)
        for acc, val in zip(outs, (kp, vp, ksm, vsm, svp, svs)):
            acc.append(val)
    return (yp, ys) + tuple(jnp.stack(o) for o in outs)
```

```python
import functools

import jax
import jax.numpy as jnp
from jax import lax
from jax.experimental import pallas as pl
from jax.experimental.pallas import tpu as pltpu

F32 = jnp.float32
BF16 = jnp.bfloat16

D_MODEL = 1024
N_Q_HEADS = 8
N_KV_HEADS = 2
HEAD_DIM = 64
Q_PER_KV = N_Q_HEADS // N_KV_HEADS
D_ATTN = N_Q_HEADS * HEAD_DIM
D_KV = N_KV_HEADS * HEAD_DIM
WINDOW = 128
CHUNK = 128
D_SGU = D_MODEL // 2
N_SGU_GROUPS = 4
SGU_GROUP_DIM = D_SGU // N_SGU_GROUPS
PEER_HEADS = 8
N_SUBKEYS = 128
N_EXPERTS = N_SUBKEYS * N_SUBKEYS
PEER_TOPK = 16
PEER_KEY_DIM = 256
PEER_HALF = PEER_KEY_DIM // 2
LN_EPS = 1e-5
NEG_INF = -1e30
D_IN = D_ATTN + 2 * D_KV + 2 * D_SGU + 2 * D_MODEL
_OFF_Q = 0
_OFF_K = D_ATTN
_OFF_V = _OFF_K + D_KV
_OFF_U = _OFF_V + D_KV
_OFF_SV = _OFF_U + D_SGU
_OFF_GA = _OFF_SV + D_SGU
_OFF_GB = _OFF_GA + D_MODEL

LANES = 128
VMEM_LIMIT_BYTES = 56 * 1024 * 1024

PROMPT_BLOCK = 512
SAMPLE_ATTN_BATCH = 8
PEER_TOKEN_BLOCK = 1024
PEER_EXPERT_BLOCK = 512
PEER_GATE_GROUP = 1
PEER_TOKEN_PIECES = 2
PEER_ROW_PIECES = 1


def _gelu(x):
    return 0.5 * x * (1.0 + jnp.tanh(0.7978845608028654 * (x + 0.044715 * (x * x * x))))


def _sigmoid(x):
    return 1.0 / (1.0 + jnp.exp(-x))


def _layer_norm(x, g, b):
    mu = jnp.mean(x, -1, keepdims=True)
    xc = x - mu
    var = jnp.mean(xc * xc, -1, keepdims=True)
    return xc * lax.rsqrt(var + LN_EPS) * g + b


def _dot(a, b):
    return jnp.dot(a, b, preferred_element_type=F32)


def _dot_nt(a, b):
    return lax.dot_general(a, b, (((1,), (1,)), ((), ())), preferred_element_type=F32)


def _sink_softmax_pv(logits, sink_col, vals):
    m = jnp.maximum(jnp.max(logits, -1, keepdims=True), sink_col)
    p = jnp.exp(logits - m)
    denom = jnp.sum(p, -1, keepdims=True) + jnp.exp(sink_col - m)
    return _dot(p.astype(BF16), vals) / denom


def _prompt_mixer_kernel(sinks_ref, x_ref, w_in_ref, sgu_g_ref, sgu_b_ref, sgu_w_ref, sgu_bcol_ref,
                         w_bs_ref, w_ba_ref, w_out_ref, ln1_g_ref, ln1_b_ref,
                         h_ref, k_ref, v_ref, sv_ref,
                         kext_ref, vext_ref, sgu_o_ref, attn_o_ref, *, alpha, ts):
    s = pl.program_id(1)
    last = pl.num_programs(1) - 1
    nblk = ts // WINDOW

    x = x_ref[...]
    z = _dot(x.astype(BF16), w_in_ref[...])
    k = z[:, _OFF_K:_OFF_V]
    v = z[:, _OFF_V:_OFF_U]
    u = _gelu(z[:, _OFF_U:_OFF_SV])
    svn = _layer_norm(_gelu(z[:, _OFF_SV:_OFF_GA]), sgu_g_ref[...], sgu_b_ref[...])

    @pl.when(s == 0)
    def _():
        kext_ref[0:WINDOW, :] = jnp.zeros((WINDOW, D_KV), BF16)
        vext_ref[0:WINDOW, :] = jnp.zeros((WINDOW, D_KV), BF16)

    kext_ref[WINDOW:, :] = k.astype(BF16)
    vext_ref[WINDOW:, :] = v.astype(BF16)

    @pl.when(s == last)
    def _():
        k_ref[...] = k[ts - WINDOW:, :]
        v_ref[...] = v[ts - WINDOW:, :]
        sv_ref[...] = svn[ts - CHUNK:, :]

    row = lax.broadcasted_iota(jnp.int32, (CHUNK, CHUNK), 0)
    col = lax.broadcasted_iota(jnp.int32, (CHUNK, CHUNK), 1)
    for g in range(N_SGU_GROUPS):
        wt = jnp.where(col <= row, sgu_w_ref[g], 0.0).astype(BF16)
        bcol = sgu_bcol_ref[:, g:g + 1]
        gs = slice(g * SGU_GROUP_DIM, (g + 1) * SGU_GROUP_DIM)
        for c in range(ts // CHUNK):
            rs = slice(c * CHUNK, (c + 1) * CHUNK)
            mix = _dot(wt, svn[rs, gs].astype(BF16)) + bcol
            sgu_o_ref[rs, gs] = (u[rs, gs] * mix).astype(BF16)

    nq = Q_PER_KV * WINDOW
    kj = lax.broadcasted_iota(jnp.int32, (2 * WINDOW, nq), 0)
    qi = lax.broadcasted_iota(jnp.int32, (2 * WINDOW, nq), 1) % WINDOW
    rel = qi + WINDOW - kj
    bias = jnp.where((rel >= 0) & (rel < WINDOW), 0.0, NEG_INF).astype(F32)
    bias_first = jnp.where(s > 0, bias, jnp.where(kj < WINDOW, NEG_INF, bias))
    scale = HEAD_DIM ** -0.5
    for i in range(nblk):
        qs = slice(i * WINDOW, (i + 1) * WINDOW)
        outs = []
        for g in range(N_KV_HEADS):
            ds = slice(g * HEAD_DIM, (g + 1) * HEAD_DIM)
            kb = kext_ref[i * WINDOW:(i + 2) * WINDOW, ds]
            vb = vext_ref[i * WINDOW:(i + 2) * WINDOW, ds]
            q4 = jnp.concatenate(
                [z[qs, (g * Q_PER_KV + r) * HEAD_DIM:(g * Q_PER_KV + r + 1) * HEAD_DIM]
                 for r in range(Q_PER_KV)], axis=0).astype(BF16)
            logits = _dot_nt(kb, q4) * scale + (bias_first if i == 0 else bias)
            sink = jnp.concatenate(
                [jnp.full((1, WINDOW), sinks_ref[g * Q_PER_KV + r], F32) for r in range(Q_PER_KV)], axis=1)
            m = jnp.maximum(jnp.max(logits, 0, keepdims=True), sink)
            p = jnp.exp(logits - m)
            denom = jnp.sum(p, 0, keepdims=True) + jnp.exp(sink - m)
            pv = lax.dot_general(vb, p.astype(BF16), (((0,), (0,)), ((), ())), preferred_element_type=F32)
            outs.append(pv / denom)
        o_all = jnp.concatenate(outs, axis=0).T
        heads = [o_all[r * WINDOW:(r + 1) * WINDOW, g * HEAD_DIM:(g + 1) * HEAD_DIM]
                 for g in range(N_KV_HEADS) for r in range(Q_PER_KV)]
        attn_o_ref[qs, :] = jnp.concatenate(heads, axis=1).astype(BF16)

    kext_ref[0:WINDOW, :] = kext_ref[ts:ts + WINDOW, :]
    vext_ref[0:WINDOW, :] = vext_ref[ts:ts + WINDOW, :]

    merged = (_sigmoid(z[:, _OFF_GA:_OFF_GB]) * _dot(sgu_o_ref[...], w_bs_ref[...])
              + _sigmoid(z[:, _OFF_GB:]) * _dot(attn_o_ref[...], w_ba_ref[...]))
    y = _dot(merged.astype(BF16), w_out_ref[...])
    h_ref[...] = _layer_norm(alpha * x + y, ln1_g_ref[...], ln1_b_ref[...])


def _full(shape):
    return pl.BlockSpec(shape, lambda *_: (0,) * len(shape), pipeline_mode=pl.Buffered(1))


def _prompt_mixer(x, p, alpha, ts=PROMPT_BLOCK):
    B, S, D = x.shape
    ts = min(ts, S)
    assert S % ts == 0 and ts % WINDOW == 0 and ts % CHUNK == 0
    grid = (B, S // ts)
    smem = pl.BlockSpec(memory_space=pltpu.SMEM)
    in_specs = [
        smem,
        pl.BlockSpec((None, ts, D), lambda b, s: (b, s, 0)),
        _full((D, D_IN)), _full((1, D_SGU)), _full((1, D_SGU)),
        _full((N_SGU_GROUPS, CHUNK, CHUNK)), _full((CHUNK, N_SGU_GROUPS)),
        _full((D_SGU, D)), _full((D_ATTN, D)), _full((D, D)), _full((1, D)), _full((1, D)),
    ]
    out_specs = [
        pl.BlockSpec((None, ts, D), lambda b, s: (b, s, 0)),
        pl.BlockSpec((None, WINDOW, D_KV), lambda b, s: (b, 0, 0)),
        pl.BlockSpec((None, WINDOW, D_KV), lambda b, s: (b, 0, 0)),
        pl.BlockSpec((None, CHUNK, D_SGU), lambda b, s: (b, 0, 0)),
    ]
    out_shape = [
        jax.ShapeDtypeStruct((B, S, D), F32),
        jax.ShapeDtypeStruct((B, WINDOW, D_KV), F32),
        jax.ShapeDtypeStruct((B, WINDOW, D_KV), F32),
        jax.ShapeDtypeStruct((B, CHUNK, D_SGU), F32),
    ]
    scratch = [
        pltpu.VMEM((ts + WINDOW, D_KV), BF16),
        pltpu.VMEM((ts + WINDOW, D_KV), BF16),
        pltpu.VMEM((ts, D_SGU), BF16),
        pltpu.VMEM((ts, D_ATTN), BF16),
    ]
    return pl.pallas_call(
        functools.partial(_prompt_mixer_kernel, alpha=alpha, ts=ts),
        grid=grid, in_specs=in_specs, out_specs=out_specs, out_shape=out_shape,
        scratch_shapes=scratch, name="prompt_mixer",
        compiler_params=pltpu.CompilerParams(
            dimension_semantics=("arbitrary", "arbitrary"), vmem_limit_bytes=VMEM_LIMIT_BYTES),
    )(p["sinks"], x, p["w_in"], p["sgu_ln_g"], p["sgu_ln_b"], p["sgu_w"], p["sgu_bcol"],
      p["w_bs"], p["w_ba"], p["w_out"], p["ln1_g"], p["ln1_b"])


def _sample_pre_kernel(sgu_w_ref, sgu_b_ref, x_ref, w_in_ref, sgu_g_ref, sgu_bb_ref, w_bs_ref,
                       q_ref, k_ref, v_ref, sv_ref, part_ref, gate_b_ref, *, t_new):
    x = x_ref[...]
    n = x.shape[0]
    z = _dot(x.astype(BF16), w_in_ref[...])
    q_ref[...] = z[:, _OFF_Q:_OFF_K]
    k_ref[...] = z[:, _OFF_K:_OFF_V]
    v_ref[...] = z[:, _OFF_V:_OFF_U]
    u = _gelu(z[:, _OFF_U:_OFF_SV])
    svn = _layer_norm(_gelu(z[:, _OFF_SV:_OFF_GA]), sgu_g_ref[...], sgu_bb_ref[...])
    sv_ref[...] = svn
    t_of_row = lax.broadcasted_iota(jnp.int32, (n, SGU_GROUP_DIM), 0) % t_new
    pieces = []
    for g in range(N_SGU_GROUPS):
        vg = svn[:, g * SGU_GROUP_DIM:(g + 1) * SGU_GROUP_DIM]
        mix = jnp.zeros((n, SGU_GROUP_DIM), F32)
        for t in range(t_new):
            mix = jnp.where(t_of_row == t, sgu_b_ref[g, t], mix)
        for d in range(t_new):
            coef = jnp.zeros((n, SGU_GROUP_DIM), F32)
            for t in range(d, t_new):
                coef = jnp.where(t_of_row == t, sgu_w_ref[g, t * t_new + t - d], coef)
            shifted = vg if d == 0 else pltpu.roll(vg, d, 0)
            mix = mix + coef * shifted
        pieces.append(u[:, g * SGU_GROUP_DIM:(g + 1) * SGU_GROUP_DIM] * mix)
    sgu_o = jnp.concatenate(pieces, axis=1).astype(BF16)
    part_ref[...] = _sigmoid(z[:, _OFF_GA:_OFF_GB]) * _dot(sgu_o, w_bs_ref[...])
    gate_b_ref[...] = _sigmoid(z[:, _OFF_GB:])


def _sample_pre(x2d, p, t_new):
    n, D = x2d.shape
    smem = pl.BlockSpec(memory_space=pltpu.SMEM)
    out_shape = [
        jax.ShapeDtypeStruct((n, D_ATTN), F32),
        jax.ShapeDtypeStruct((n, D_KV), F32),
        jax.ShapeDtypeStruct((n, D_KV), F32),
        jax.ShapeDtypeStruct((n, D_SGU), F32),
        jax.ShapeDtypeStruct((n, D), F32),
        jax.ShapeDtypeStruct((n, D), F32),
    ]
    return pl.pallas_call(
        functools.partial(_sample_pre_kernel, t_new=t_new),
        in_specs=[smem, smem] + [pl.BlockSpec(memory_space=pltpu.VMEM)] * 5,
        out_specs=[pl.BlockSpec(memory_space=pltpu.VMEM)] * 6,
        out_shape=out_shape, name="sample_pre",
        compiler_params=pltpu.CompilerParams(vmem_limit_bytes=VMEM_LIMIT_BYTES),
    )(p["sgu_w"][:, :t_new, :t_new].reshape(N_SGU_GROUPS, t_new * t_new), p["sgu_b"],
      x2d, p["w_in"], p["sgu_ln_g"], p["sgu_ln_b"], p["w_bs"])


def _sample_attn_kernel(sinks_ref, q_ref, kn_ref, vn_ref, ck_ref, cv_ref,
                        o_ref, nk_ref, nv_ref, kbuf_ref, vbuf_ref, *, t_new, bb):
    nq = Q_PER_KV * t_new
    nkeys = WINDOW + 8
    row = lax.broadcasted_iota(jnp.int32, (nq, nkeys), 0)
    kj = lax.broadcasted_iota(jnp.int32, (nq, nkeys), 1)
    t_q = row % t_new
    bias = jnp.where((kj > t_q) & (kj <= t_q + WINDOW), 0.0, NEG_INF).astype(F32)
    r_of_row = lax.broadcasted_iota(jnp.int32, (nq, 1), 0) // t_new
    scale = HEAD_DIM ** -0.5
    pad = jnp.zeros((8 - t_new, D_KV), F32)
    for b in range(bb):
        kbuf_ref[0:WINDOW, :] = ck_ref[b]
        kbuf_ref[WINDOW:WINDOW + t_new, :] = kn_ref[b]
        kbuf_ref[WINDOW + t_new:, :] = pad
        vbuf_ref[0:WINDOW, :] = cv_ref[b]
        vbuf_ref[WINDOW:WINDOW + t_new, :] = vn_ref[b]
        vbuf_ref[WINDOW + t_new:, :] = pad
        nk_ref[b] = kbuf_ref[t_new:t_new + WINDOW, :]
        nv_ref[b] = vbuf_ref[t_new:t_new + WINDOW, :]
        for g in range(N_KV_HEADS):
            ds = slice(g * HEAD_DIM, (g + 1) * HEAD_DIM)
            kg = kbuf_ref[:, ds].astype(BF16)
            vg = vbuf_ref[:, ds].astype(BF16)
            logits = _dot_nt(q_ref[b, g].astype(BF16), kg) * scale + bias
            sink_col = jnp.zeros((nq, 1), F32)
            for r in range(Q_PER_KV):
                sink_col = jnp.where(r_of_row == r, sinks_ref[g * Q_PER_KV + r], sink_col)
            o_ref[b, g] = _sink_softmax_pv(logits, sink_col, vg)


def _sample_attn(q_grouped, k_new, v_new, cache_k, cache_v, sinks, t_new, bb=SAMPLE_ATTN_BATCH):
    nb = q_grouped.shape[0]
    bb = min(bb, nb)
    assert nb % bb == 0 and t_new <= 8
    nq = Q_PER_KV * t_new
    smem = pl.BlockSpec(memory_space=pltpu.SMEM)
    in_specs = [
        smem,
        pl.BlockSpec((bb, N_KV_HEADS, nq, HEAD_DIM), lambda i: (i, 0, 0, 0)),
        pl.BlockSpec((bb, t_new, D_KV), lambda i: (i, 0, 0)),
        pl.BlockSpec((bb, t_new, D_KV), lambda i: (i, 0, 0)),
        pl.BlockSpec((bb, WINDOW, D_KV), lambda i: (i, 0, 0)),
        pl.BlockSpec((bb, WINDOW, D_KV), lambda i: (i, 0, 0)),
    ]
    out_specs = [
        pl.BlockSpec((bb, N_KV_HEADS, nq, HEAD_DIM), lambda i: (i, 0, 0, 0)),
        pl.BlockSpec((bb, WINDOW, D_KV), lambda i: (i, 0, 0)),
        pl.BlockSpec((bb, WINDOW, D_KV), lambda i: (i, 0, 0)),
    ]
    out_shape = [
        jax.ShapeDtypeStruct((nb, N_KV_HEADS, nq, HEAD_DIM), F32),
        jax.ShapeDtypeStruct((nb, WINDOW, D_KV), F32),
        jax.ShapeDtypeStruct((nb, WINDOW, D_KV), F32),
    ]
    return pl.pallas_call(
        functools.partial(_sample_attn_kernel, t_new=t_new, bb=bb),
        grid=(nb // bb,), in_specs=in_specs, out_specs=out_specs, out_shape=out_shape,
        scratch_shapes=[pltpu.VMEM((WINDOW + 8, D_KV), F32), pltpu.VMEM((WINDOW + 8, D_KV), F32)],
        name="sample_attn",
        compiler_params=pltpu.CompilerParams(dimension_semantics=("arbitrary",)),
    )(sinks, q_grouped, k_new, v_new, cache_k, cache_v)


def _sample_post_kernel(x_ref, part_ref, gate_b_ref, o_ref, w_ba_ref, w_out_ref, ln1_g_ref, ln1_b_ref,
                        h_ref, *, alpha):
    merged = part_ref[...] + gate_b_ref[...] * _dot(o_ref[...].astype(BF16), w_ba_ref[...])
    y = _dot(merged.astype(BF16), w_out_ref[...])
    h_ref[...] = _layer_norm(alpha * x_ref[...] + y, ln1_g_ref[...], ln1_b_ref[...])


def _sample_post(x2d, part, gate_b, o2d, p, alpha):
    return pl.pallas_call(
        functools.partial(_sample_post_kernel, alpha=alpha),
        in_specs=[pl.BlockSpec(memory_space=pltpu.VMEM)] * 8,
        out_specs=pl.BlockSpec(memory_space=pltpu.VMEM),
        out_shape=jax.ShapeDtypeStruct(x2d.shape, F32), name="sample_post",
        compiler_params=pltpu.CompilerParams(vmem_limit_bytes=VMEM_LIMIT_BYTES),
    )(x2d, part, gate_b, o2d, p["w_ba"], p["w_out"], p["ln1_g"], p["ln1_b"])


def _top_values(work, count, with_rank=False):
    vals = []
    rank = jnp.full(work.shape, float(count), F32) if with_rank else None
    for i in range(count):
        top = jnp.max(work, axis=0, keepdims=True)
        vals.append(top)
        hit = work == top
        if with_rank:
            rank = jnp.where(hit, float(i), rank)
        if i + 1 < count:
            work = jnp.where(hit, -jnp.inf, work)
    return (vals, rank) if with_rank else vals


def _candidate_rows(a_rows, b_rows):
    assert PEER_TOPK == 16
    a_all = jnp.concatenate(a_rows, axis=0)
    a_low = a_all[0:8]
    row = lax.broadcasted_iota(jnp.int32, a_low.shape, 0)

    def pair(j):
        return a_low + b_rows[j]

    def corner(j):
        return a_rows[0] + b_rows[j]

    t3 = jnp.where(row >= 5, pltpu.roll(pair(4), 5, 0), pair(2))
    t4 = jnp.where(row >= 6, pltpu.roll(pair(6), 6, 0),
                   jnp.where(row >= 4, pltpu.roll(pair(5), 4, 0), pair(3)))
    t5 = pair(7)
    for k in range(6):
        t5 = jnp.where(row >= 2 + k, corner(8 + k), t5)
    t6 = jnp.where(row >= 2, -jnp.inf, jnp.where(row >= 1, corner(15), corner(14)))
    return jnp.concatenate([a_all + b_rows[0], pair(1), t3, t4, t5, t6], axis=0)


def _bf16_rows(row, rows):
    return jnp.broadcast_to(row.astype(BF16), (rows, row.shape[1]))


def _peer_select(s1, s2):
    a_rows = _top_values(s1, PEER_TOPK)
    b_rows, rank2 = _top_values(s2, PEER_TOPK, with_rank=True)
    best = _top_values(_candidate_rows(a_rows, b_rows), PEER_TOPK)
    norm = jnp.zeros_like(best[0])
    for i in range(PEER_TOPK):
        norm = norm + jnp.exp(best[i] - best[0])
    count = jnp.zeros_like(s1)
    for jj in range(PEER_TOPK):
        count = jnp.where(s1 + b_rows[jj] >= best[PEER_TOPK - 1], float(jj + 1), count)
    e1 = jnp.exp(s1 - a_rows[0]) / norm
    e2 = jnp.exp(s2 - b_rows[0])
    return rank2, e2, count, e1


def _peer_gate_tile(act_ref, coef_ref, first_n1, ns, t, sel_refs):
    rank2_ref, e2_ref, count_ref, e1_ref = sel_refs
    ls = slice(t * LANES, (t + 1) * LANES)
    gates = [jnp.zeros((N_SUBKEYS, LANES), BF16) for _ in ns]
    for hd in range(PEER_HEADS):
        rank2 = rank2_ref[hd, t]
        e2 = e2_ref[hd, t]
        for i, n in enumerate(ns):
            count = _bf16_rows(count_ref[hd, t, pl.ds(first_n1 + n, 1), :], N_SUBKEYS)
            e1 = _bf16_rows(e1_ref[hd, t, pl.ds(first_n1 + n, 1), :], N_SUBKEYS)
            gates[i] = gates[i] + jnp.where(rank2 < count, e2, jnp.zeros((), BF16)) * e1
    for i, n in enumerate(ns):
        rs = slice(n * N_SUBKEYS, (n + 1) * N_SUBKEYS)
        coef_ref[rs, ls] = gates[i] * _gelu(act_ref[rs, ls])


def _peer_kernel(h_ref, wq_ref, sk_ref, u_ref, vt_ref, ln_g_ref, ln_b_ref, y_ref,
                 ht_ref, qt_ref, rank2_ref, e2_ref, count_ref, e1_ref, acc_ref,
                 act0_ref, act1_ref, coef0_ref, coef1_ref, *, alpha, tb, se):
    j = pl.program_id(1)
    ntb = tb // LANES
    nblocks = N_EXPERTS // se
    sel_refs = (rank2_ref, e2_ref, count_ref, e1_ref)
    act_refs = (act0_ref, act1_ref)
    coef_refs = (coef0_ref, coef1_ref)

    @pl.when(j == 0)
    def _():
        ht = h_ref[...].T.astype(BF16)
        ht_ref[...] = ht
        qt_ref[...] = _dot(wq_ref[...], ht).astype(BF16)

        def per_head(hd, carry):
            for t in range(ntb):
                ls = slice(t * LANES, (t + 1) * LANES)
                s = []
                for c in range(2):
                    r0 = pl.multiple_of((hd * 2 + c) * PEER_HALF, PEER_HALF)
                    s.append(_dot(sk_ref[hd * 2 + c], qt_ref[pl.ds(r0, PEER_HALF), ls]))
                rank2, e2, count, e1 = _peer_select(s[0], s[1])
                rank2_ref[hd, t] = rank2.astype(BF16)
                e2_ref[hd, t] = e2.astype(BF16)
                count_ref[hd, t] = count
                e1_ref[hd, t] = e1
            return carry

        lax.fori_loop(0, PEER_HEADS, per_head, 0)
        acc_ref[...] = jnp.zeros_like(acc_ref)

    def step_body(q, with_act, with_gate, with_combine):
        piece_t, piece_e, piece_d = tb // PEER_TOKEN_PIECES, se // PEER_ROW_PIECES, acc_ref.shape[0] // PEER_ROW_PIECES
        pieces = []
        for tp in range(PEER_TOKEN_PIECES):
            hs = slice(tp * piece_t, (tp + 1) * piece_t)
            for lo in range(PEER_ROW_PIECES):
                if with_act:
                    pieces.append(("act", hs, slice(lo * piece_e, (lo + 1) * piece_e)))
                if with_combine:
                    pieces.append(("combine", hs, slice(lo * piece_d, (lo + 1) * piece_d)))
        n_sub = se // N_SUBKEYS
        groups = [tuple(range(g, min(g + PEER_GATE_GROUP, n_sub))) for g in range(0, n_sub, PEER_GATE_GROUP)]
        tiles = [(ns, t) for t in range(ntb) for ns in groups] if with_gate else []
        per_piece = -(-len(tiles) // max(len(pieces), 1))
        for k, (kind, hs, es) in enumerate(pieces):
            for n, t in tiles[k * per_piece:(k + 1) * per_piece]:
                _peer_gate_tile(act_refs[1 - q], coef_refs[1 - q], (j - 1) * (se // N_SUBKEYS), n, t, sel_refs)
            if kind == "act":
                act_refs[q][es, hs] = _dot(u_ref[es, :], ht_ref[:, hs]).astype(BF16)
            else:
                acc_ref[es, hs] += _dot(vt_ref[es, :], coef_refs[q][:, hs])
        for n, t in tiles[len(pieces) * per_piece:]:
            _peer_gate_tile(act_refs[1 - q], coef_refs[1 - q], (j - 1) * (se // N_SUBKEYS), n, t, sel_refs)

    assert nblocks % 2 == 0 and nblocks >= 4
    pl.when(j == 0)(functools.partial(step_body, 0, True, False, False))
    pl.when(j == 1)(functools.partial(step_body, 1, True, True, False))
    for q in range(2):
        pl.when((j >= 2) & (j < nblocks) & (j % 2 == q))(functools.partial(step_body, q, True, True, True))
    pl.when(j == nblocks)(functools.partial(step_body, 0, False, True, True))
    pl.when(j == nblocks + 1)(functools.partial(step_body, 1, False, False, True))

    @pl.when(j == pl.num_programs(1) - 1)
    def _():
        y_ref[...] = _layer_norm(alpha * h_ref[...] + acc_ref[...].T, ln_g_ref[...], ln_b_ref[...])


def _peer(h2d, p, alpha, tb=PEER_TOKEN_BLOCK):
    n, D = h2d.shape
    tb = min(tb, n)
    se = p["peer_vt"].shape[2]
    assert n % tb == 0 and tb % LANES == 0 and N_EXPERTS % se == 0 and se % N_SUBKEYS == 0
    ntb = tb // LANES
    qdim = PEER_HEADS * PEER_KEY_DIM
    nblocks = N_EXPERTS // se
    in_specs = [
        pl.BlockSpec((tb, D), lambda i, j: (i, 0)),
        _full((qdim, D)),
        _full((PEER_HEADS * 2, N_SUBKEYS, PEER_HALF)),
        pl.BlockSpec((se, D), lambda i, j: (jnp.minimum(j, nblocks - 1), 0)),
        pl.BlockSpec((None, D, se), lambda i, j: (jnp.clip(j - 2, 0, nblocks - 1), 0, 0)),
        _full((1, D)), _full((1, D)),
    ]
    per_head = (PEER_HEADS, ntb, N_SUBKEYS, LANES)
    scratch = [
        pltpu.VMEM((D, tb), BF16),
        pltpu.VMEM((qdim, tb), BF16),
        pltpu.VMEM(per_head, BF16), pltpu.VMEM(per_head, BF16),
        pltpu.VMEM(per_head, F32), pltpu.VMEM(per_head, F32),
        pltpu.VMEM((D, tb), F32),
    ] + [pltpu.VMEM((se, tb), BF16)] * 4
    return pl.pallas_call(
        functools.partial(_peer_kernel, alpha=alpha, tb=tb, se=se),
        grid=(n // tb, nblocks + 2), in_specs=in_specs,
        out_specs=pl.BlockSpec((tb, D), lambda i, j: (i, 0), pipeline_mode=pl.Buffered(1)),
        out_shape=jax.ShapeDtypeStruct((n, D), F32), scratch_shapes=scratch, name="peer",
        compiler_params=pltpu.CompilerParams(
            dimension_semantics=("arbitrary", "arbitrary"), vmem_limit_bytes=VMEM_LIMIT_BYTES),
    )(h2d, p["peer_wq_t"], p["peer_sk"], p["peer_u"], p["peer_vt"], p["ln2_g"], p["ln2_b"])


def _layer_params(l, w_in, sinks, sgu_ln_g, sgu_ln_b, sgu_w, sgu_b, w_branch_sgu, w_branch_attn, w_out,
                  ln1_g, ln1_b, peer_w_q, peer_sub_keys, peer_u, peer_v, ln2_g, ln2_b):
    return {
        "w_in": w_in[l].astype(BF16),
        "sinks": sinks[l],
        "sgu_ln_g": sgu_ln_g[l][None, :], "sgu_ln_b": sgu_ln_b[l][None, :],
        "sgu_w": sgu_w[l],
        "sgu_b": sgu_b[l], "sgu_bcol": sgu_b[l].T,
        "w_bs": w_branch_sgu[l].astype(BF16), "w_ba": w_branch_attn[l].astype(BF16),
        "w_out": w_out[l].astype(BF16),
        "ln1_g": ln1_g[l][None, :], "ln1_b": ln1_b[l][None, :],
        "peer_wq_t": peer_w_q[l].T.astype(BF16),
        "peer_sk": peer_sub_keys[l].reshape(PEER_HEADS * 2, N_SUBKEYS, PEER_HALF).astype(BF16),
        "peer_u": peer_u[l].astype(BF16),
        "peer_vt": peer_v[l].reshape(N_EXPERTS // PEER_EXPERT_BLOCK, PEER_EXPERT_BLOCK, D_MODEL)
                            .transpose(0, 2, 1).astype(BF16),
        "ln2_g": ln2_g[l][None, :], "ln2_b": ln2_b[l][None, :],
    }


def _prompt_layer(x, p, alpha):
    B, S, D = x.shape
    h, k_last, v_last, sv_last = _prompt_mixer(x, p, alpha)
    y = _peer(h.reshape(B * S, D), p, alpha).reshape(B, S, D)
    return (y, k_last.reshape(B, WINDOW, N_KV_HEADS, HEAD_DIM), v_last.reshape(B, WINDOW, N_KV_HEADS, HEAD_DIM),
            sv_last.reshape(B, CHUNK, N_SGU_GROUPS, SGU_GROUP_DIM))


def _sample_layer(x, cache_k, cache_v, p, alpha):
    B, T, D = x.shape
    x2d = x.reshape(B * T, D)
    q, k_new, v_new, svn, part, gate_b = _sample_pre(x2d, p, T)
    qg = q.reshape(B, T, N_KV_HEADS, Q_PER_KV, HEAD_DIM).transpose(0, 2, 3, 1, 4)
    qg = qg.reshape(B, N_KV_HEADS, Q_PER_KV * T, HEAD_DIM)
    og, new_k, new_v = _sample_attn(qg, k_new.reshape(B, T, D_KV), v_new.reshape(B, T, D_KV),
                                    cache_k.reshape(B, WINDOW, D_KV), cache_v.reshape(B, WINDOW, D_KV),
                                    p["sinks"], T)
    o2d = og.reshape(B, N_KV_HEADS, Q_PER_KV, T, HEAD_DIM).transpose(0, 3, 1, 2, 4).reshape(B * T, D_ATTN)
    h = _sample_post(x2d, part, gate_b, o2d, p, alpha)
    y = _peer(h, p, alpha).reshape(B, T, D)
    return (y, new_k.reshape(B, WINDOW, N_KV_HEADS, HEAD_DIM), new_v.reshape(B, WINDOW, N_KV_HEADS, HEAD_DIM),
            svn.reshape(B, T, N_SGU_GROUPS, SGU_GROUP_DIM))


def kernel(x_prompt, x_sample, cache_k, cache_v, w_in, sinks, sgu_ln_g, sgu_ln_b, sgu_w, sgu_b, w_branch_sgu, w_branch_attn, w_out, ln1_g, ln1_b, peer_w_q, peer_sub_keys, peer_u, peer_v, ln2_g, ln2_b):
    depth = w_in.shape[0]
    alpha = (2.0 * depth) ** 0.25
    yp, ys = x_prompt, x_sample
    outs = [[] for _ in range(6)]
    for l in range(depth):
        p = _layer_params(l, w_in, sinks, sgu_ln_g, sgu_ln_b, sgu_w, sgu_b, w_branch_sgu, w_branch_attn,
                          w_out, ln1_g, ln1_b, peer_w_q, peer_sub_keys, peer_u, peer_v, ln2_g, ln2_b)
        yp, kp, vp, svp = _prompt_layer(yp, p, alpha)
        ys, ksm, vsm, svs = _sample_layer(ys, cache_k[l], cache_v[l], p, alpha)
        for acc, val in zip(outs, (kp, vp, ksm, vsm, svp, svs)):
            acc.append(val)
    return (yp, ys) + tuple(jnp.stack(o) for o in outs)
```

```python
import functools

import jax
import jax.numpy as jnp
from jax import lax
from jax.experimental import pallas as pl
from jax.experimental.pallas import tpu as pltpu

F32 = jnp.float32
BF16 = jnp.bfloat16

D_MODEL = 1024
N_Q_HEADS = 8
N_KV_HEADS = 2
HEAD_DIM = 64
Q_PER_KV = N_Q_HEADS // N_KV_HEADS
D_ATTN = N_Q_HEADS * HEAD_DIM
D_KV = N_KV_HEADS * HEAD_DIM
WINDOW = 128
CHUNK = 128
D_SGU = D_MODEL // 2
N_SGU_GROUPS = 4
SGU_GROUP_DIM = D_SGU // N_SGU_GROUPS
PEER_HEADS = 8
N_SUBKEYS = 128
N_EXPERTS = N_SUBKEYS * N_SUBKEYS
PEER_TOPK = 16
PEER_KEY_DIM = 256
PEER_HALF = PEER_KEY_DIM // 2
LN_EPS = 1e-5
NEG_INF = -1e30
D_IN = D_ATTN + 2 * D_KV + 2 * D_SGU + 2 * D_MODEL
_OFF_Q = 0
_OFF_K = D_ATTN
_OFF_V = _OFF_K + D_KV
_OFF_U = _OFF_V + D_KV
_OFF_SV = _OFF_U + D_SGU
_OFF_GA = _OFF_SV + D_SGU
_OFF_GB = _OFF_GA + D_MODEL

LANES = 128
VMEM_LIMIT_BYTES = 56 * 1024 * 1024

PROMPT_BLOCK = 512
SAMPLE_ATTN_BATCH = 8
PEER_TOKEN_BLOCK = 1024
PEER_EXPERT_BLOCK = 512
PEER_GATE_GROUP = 1
PEER_TOKEN_PIECES = 4
PEER_ROW_PIECES = 1


def _gelu(x):
    return 0.5 * x * (1.0 + jnp.tanh(0.7978845608028654 * (x + 0.044715 * (x * x * x))))


def _sigmoid(x):
    return 1.0 / (1.0 + jnp.exp(-x))


def _layer_norm(x, g, b):
    mu = jnp.mean(x, -1, keepdims=True)
    xc = x - mu
    var = jnp.mean(xc * xc, -1, keepdims=True)
    return xc * lax.rsqrt(var + LN_EPS) * g + b


def _dot(a, b):
    return jnp.dot(a, b, preferred_element_type=F32)


def _dot_nt(a, b):
    return lax.dot_general(a, b, (((1,), (1,)), ((), ())), preferred_element_type=F32)


def _sink_softmax_pv(logits, sink_col, vals):
    m = jnp.maximum(jnp.max(logits, -1, keepdims=True), sink_col)
    p = jnp.exp(logits - m)
    denom = jnp.sum(p, -1, keepdims=True) + jnp.exp(sink_col - m)
    return _dot(p.astype(BF16), vals) / denom


def _prompt_mixer_kernel(sinks_ref, x_ref, w_in_ref, sgu_g_ref, sgu_b_ref, sgu_w_ref, sgu_bcol_ref,
                         w_bs_ref, w_ba_ref, w_out_ref, ln1_g_ref, ln1_b_ref,
                         h_ref, k_ref, v_ref, sv_ref,
                         kext_ref, vext_ref, sgu_o_ref, attn_o_ref, *, alpha, ts):
    s = pl.program_id(1)
    last = pl.num_programs(1) - 1
    nblk = ts // WINDOW

    x = x_ref[...]
    z = _dot(x.astype(BF16), w_in_ref[...])
    k = z[:, _OFF_K:_OFF_V]
    v = z[:, _OFF_V:_OFF_U]
    u = _gelu(z[:, _OFF_U:_OFF_SV])
    svn = _layer_norm(_gelu(z[:, _OFF_SV:_OFF_GA]), sgu_g_ref[...], sgu_b_ref[...])

    @pl.when(s == 0)
    def _():
        kext_ref[0:WINDOW, :] = jnp.zeros((WINDOW, D_KV), BF16)
        vext_ref[0:WINDOW, :] = jnp.zeros((WINDOW, D_KV), BF16)

    kext_ref[WINDOW:, :] = k.astype(BF16)
    vext_ref[WINDOW:, :] = v.astype(BF16)

    @pl.when(s == last)
    def _():
        k_ref[...] = k[ts - WINDOW:, :]
        v_ref[...] = v[ts - WINDOW:, :]
        sv_ref[...] = svn[ts - CHUNK:, :]

    row = lax.broadcasted_iota(jnp.int32, (CHUNK, CHUNK), 0)
    col = lax.broadcasted_iota(jnp.int32, (CHUNK, CHUNK), 1)
    for g in range(N_SGU_GROUPS):
        wt = jnp.where(col <= row, sgu_w_ref[g], 0.0).astype(BF16)
        bcol = sgu_bcol_ref[:, g:g + 1]
        gs = slice(g * SGU_GROUP_DIM, (g + 1) * SGU_GROUP_DIM)
        for c in range(ts // CHUNK):
            rs = slice(c * CHUNK, (c + 1) * CHUNK)
            mix = _dot(wt, svn[rs, gs].astype(BF16)) + bcol
            sgu_o_ref[rs, gs] = (u[rs, gs] * mix).astype(BF16)

    nq = Q_PER_KV * WINDOW
    kj = lax.broadcasted_iota(jnp.int32, (2 * WINDOW, nq), 0)
    qi = lax.broadcasted_iota(jnp.int32, (2 * WINDOW, nq), 1) % WINDOW
    rel = qi + WINDOW - kj
    bias = jnp.where((rel >= 0) & (rel < WINDOW), 0.0, NEG_INF).astype(F32)
    bias_first = jnp.where(s > 0, bias, jnp.where(kj < WINDOW, NEG_INF, bias))
    scale = HEAD_DIM ** -0.5
    for i in range(nblk):
        qs = slice(i * WINDOW, (i + 1) * WINDOW)
        outs = []
        for g in range(N_KV_HEADS):
            ds = slice(g * HEAD_DIM, (g + 1) * HEAD_DIM)
            kb = kext_ref[i * WINDOW:(i + 2) * WINDOW, ds]
            vb = vext_ref[i * WINDOW:(i + 2) * WINDOW, ds]
            q4 = jnp.concatenate(
                [z[qs, (g * Q_PER_KV + r) * HEAD_DIM:(g * Q_PER_KV + r + 1) * HEAD_DIM]
                 for r in range(Q_PER_KV)], axis=0).astype(BF16)
            logits = _dot_nt(kb, q4) * scale + (bias_first if i == 0 else bias)
            sink = jnp.concatenate(
                [jnp.full((1, WINDOW), sinks_ref[g * Q_PER_KV + r], F32) for r in range(Q_PER_KV)], axis=1)
            m = jnp.maximum(jnp.max(logits, 0, keepdims=True), sink)
            p = jnp.exp(logits - m)
            denom = jnp.sum(p, 0, keepdims=True) + jnp.exp(sink - m)
            pv = lax.dot_general(vb, p.astype(BF16), (((0,), (0,)), ((), ())), preferred_element_type=F32)
            outs.append(pv / denom)
        o_all = jnp.concatenate(outs, axis=0).T
        heads = [o_all[r * WINDOW:(r + 1) * WINDOW, g * HEAD_DIM:(g + 1) * HEAD_DIM]
                 for g in range(N_KV_HEADS) for r in range(Q_PER_KV)]
        attn_o_ref[qs, :] = jnp.concatenate(heads, axis=1).astype(BF16)

    kext_ref[0:WINDOW, :] = kext_ref[ts:ts + WINDOW, :]
    vext_ref[0:WINDOW, :] = vext_ref[ts:ts + WINDOW, :]

    merged = (_sigmoid(z[:, _OFF_GA:_OFF_GB]) * _dot(sgu_o_ref[...], w_bs_ref[...])
              + _sigmoid(z[:, _OFF_GB:]) * _dot(attn_o_ref[...], w_ba_ref[...]))
    y = _dot(merged.astype(BF16), w_out_ref[...])
    h_ref[...] = _layer_norm(alpha * x + y, ln1_g_ref[...], ln1_b_ref[...])


def _full(shape):
    return pl.BlockSpec(shape, lambda *_: (0,) * len(shape), pipeline_mode=pl.Buffered(1))


def _prompt_mixer(x, p, alpha, ts=PROMPT_BLOCK):
    B, S, D = x.shape
    ts = min(ts, S)
    assert S % ts == 0 and ts % WINDOW == 0 and ts % CHUNK == 0
    grid = (B, S // ts)
    smem = pl.BlockSpec(memory_space=pltpu.SMEM)
    in_specs = [
        smem,
        pl.BlockSpec((None, ts, D), lambda b, s: (b, s, 0)),
        _full((D, D_IN)), _full((1, D_SGU)), _full((1, D_SGU)),
        _full((N_SGU_GROUPS, CHUNK, CHUNK)), _full((CHUNK, N_SGU_GROUPS)),
        _full((D_SGU, D)), _full((D_ATTN, D)), _full((D, D)), _full((1, D)), _full((1, D)),
    ]
    out_specs = [
        pl.BlockSpec((None, ts, D), lambda b, s: (b, s, 0)),
        pl.BlockSpec((None, WINDOW, D_KV), lambda b, s: (b, 0, 0)),
        pl.BlockSpec((None, WINDOW, D_KV), lambda b, s: (b, 0, 0)),
        pl.BlockSpec((None, CHUNK, D_SGU), lambda b, s: (b, 0, 0)),
    ]
    out_shape = [
        jax.ShapeDtypeStruct((B, S, D), F32),
        jax.ShapeDtypeStruct((B, WINDOW, D_KV), F32),
        jax.ShapeDtypeStruct((B, WINDOW, D_KV), F32),
        jax.ShapeDtypeStruct((B, CHUNK, D_SGU), F32),
    ]
    scratch = [
        pltpu.VMEM((ts + WINDOW, D_KV), BF16),
        pltpu.VMEM((ts + WINDOW, D_KV), BF16),
        pltpu.VMEM((ts, D_SGU), BF16),
        pltpu.VMEM((ts, D_ATTN), BF16),
    ]
    return pl.pallas_call(
        functools.partial(_prompt_mixer_kernel, alpha=alpha, ts=ts),
        grid=grid, in_specs=in_specs, out_specs=out_specs, out_shape=out_shape,
        scratch_shapes=scratch, name="prompt_mixer",
        compiler_params=pltpu.CompilerParams(
            dimension_semantics=("arbitrary", "arbitrary"), vmem_limit_bytes=VMEM_LIMIT_BYTES),
    )(p["sinks"], x, p["w_in"], p["sgu_ln_g"], p["sgu_ln_b"], p["sgu_w"], p["sgu_bcol"],
      p["w_bs"], p["w_ba"], p["w_out"], p["ln1_g"], p["ln1_b"])


def _sample_pre_kernel(sgu_w_ref, sgu_b_ref, x_ref, w_in_ref, sgu_g_ref, sgu_bb_ref, w_bs_ref,
                       q_ref, k_ref, v_ref, sv_ref, part_ref, gate_b_ref, *, t_new):
    x = x_ref[...]
    n = x.shape[0]
    z = _dot(x.astype(BF16), w_in_ref[...])
    q_ref[...] = z[:, _OFF_Q:_OFF_K]
    k_ref[...] = z[:, _OFF_K:_OFF_V]
    v_ref[...] = z[:, _OFF_V:_OFF_U]
    u = _gelu(z[:, _OFF_U:_OFF_SV])
    svn = _layer_norm(_gelu(z[:, _OFF_SV:_OFF_GA]), sgu_g_ref[...], sgu_bb_ref[...])
    sv_ref[...] = svn
    t_of_row = lax.broadcasted_iota(jnp.int32, (n, SGU_GROUP_DIM), 0) % t_new
    pieces = []
    for g in range(N_SGU_GROUPS):
        vg = svn[:, g * SGU_GROUP_DIM:(g + 1) * SGU_GROUP_DIM]
        mix = jnp.zeros((n, SGU_GROUP_DIM), F32)
        for t in range(t_new):
            mix = jnp.where(t_of_row == t, sgu_b_ref[g, t], mix)
        for d in range(t_new):
            coef = jnp.zeros((n, SGU_GROUP_DIM), F32)
            for t in range(d, t_new):
                coef = jnp.where(t_of_row == t, sgu_w_ref[g, t * t_new + t - d], coef)
            shifted = vg if d == 0 else pltpu.roll(vg, d, 0)
            mix = mix + coef * shifted
        pieces.append(u[:, g * SGU_GROUP_DIM:(g + 1) * SGU_GROUP_DIM] * mix)
    sgu_o = jnp.concatenate(pieces, axis=1).astype(BF16)
    part_ref[...] = _sigmoid(z[:, _OFF_GA:_OFF_GB]) * _dot(sgu_o, w_bs_ref[...])
    gate_b_ref[...] = _sigmoid(z[:, _OFF_GB:])


def _sample_pre(x2d, p, t_new):
    n, D = x2d.shape
    smem = pl.BlockSpec(memory_space=pltpu.SMEM)
    out_shape = [
        jax.ShapeDtypeStruct((n, D_ATTN), F32),
        jax.ShapeDtypeStruct((n, D_KV), F32),
        jax.ShapeDtypeStruct((n, D_KV), F32),
        jax.ShapeDtypeStruct((n, D_SGU), F32),
        jax.ShapeDtypeStruct((n, D), F32),
        jax.ShapeDtypeStruct((n, D), F32),
    ]
    return pl.pallas_call(
        functools.partial(_sample_pre_kernel, t_new=t_new),
        in_specs=[smem, smem] + [pl.BlockSpec(memory_space=pltpu.VMEM)] * 5,
        out_specs=[pl.BlockSpec(memory_space=pltpu.VMEM)] * 6,
        out_shape=out_shape, name="sample_pre",
        compiler_params=pltpu.CompilerParams(vmem_limit_bytes=VMEM_LIMIT_BYTES),
    )(p["sgu_w"][:, :t_new, :t_new].reshape(N_SGU_GROUPS, t_new * t_new), p["sgu_b"],
      x2d, p["w_in"], p["sgu_ln_g"], p["sgu_ln_b"], p["w_bs"])


def _sample_attn_kernel(sinks_ref, q_ref, kn_ref, vn_ref, ck_ref, cv_ref,
                        o_ref, nk_ref, nv_ref, kbuf_ref, vbuf_ref, *, t_new, bb):
    nq = Q_PER_KV * t_new
    nkeys = WINDOW + 8
    row = lax.broadcasted_iota(jnp.int32, (nq, nkeys), 0)
    kj = lax.broadcasted_iota(jnp.int32, (nq, nkeys), 1)
    t_q = row % t_new
    bias = jnp.where((kj > t_q) & (kj <= t_q + WINDOW), 0.0, NEG_INF).astype(F32)
    r_of_row = lax.broadcasted_iota(jnp.int32, (nq, 1), 0) // t_new
    scale = HEAD_DIM ** -0.5
    pad = jnp.zeros((8 - t_new, D_KV), F32)
    for b in range(bb):
        kbuf_ref[0:WINDOW, :] = ck_ref[b]
        kbuf_ref[WINDOW:WINDOW + t_new, :] = kn_ref[b]
        kbuf_ref[WINDOW + t_new:, :] = pad
        vbuf_ref[0:WINDOW, :] = cv_ref[b]
        vbuf_ref[WINDOW:WINDOW + t_new, :] = vn_ref[b]
        vbuf_ref[WINDOW + t_new:, :] = pad
        nk_ref[b] = kbuf_ref[t_new:t_new + WINDOW, :]
        nv_ref[b] = vbuf_ref[t_new:t_new + WINDOW, :]
        for g in range(N_KV_HEADS):
            ds = slice(g * HEAD_DIM, (g + 1) * HEAD_DIM)
            kg = kbuf_ref[:, ds].astype(BF16)
            vg = vbuf_ref[:, ds].astype(BF16)
            logits = _dot_nt(q_ref[b, g].astype(BF16), kg) * scale + bias
            sink_col = jnp.zeros((nq, 1), F32)
            for r in range(Q_PER_KV):
                sink_col = jnp.where(r_of_row == r, sinks_ref[g * Q_PER_KV + r], sink_col)
            o_ref[b, g] = _sink_softmax_pv(logits, sink_col, vg)


def _sample_attn(q_grouped, k_new, v_new, cache_k, cache_v, sinks, t_new, bb=SAMPLE_ATTN_BATCH):
    nb = q_grouped.shape[0]
    bb = min(bb, nb)
    assert nb % bb == 0 and t_new <= 8
    nq = Q_PER_KV * t_new
    smem = pl.BlockSpec(memory_space=pltpu.SMEM)
    in_specs = [
        smem,
        pl.BlockSpec((bb, N_KV_HEADS, nq, HEAD_DIM), lambda i: (i, 0, 0, 0)),
        pl.BlockSpec((bb, t_new, D_KV), lambda i: (i, 0, 0)),
        pl.BlockSpec((bb, t_new, D_KV), lambda i: (i, 0, 0)),
        pl.BlockSpec((bb, WINDOW, D_KV), lambda i: (i, 0, 0)),
        pl.BlockSpec((bb, WINDOW, D_KV), lambda i: (i, 0, 0)),
    ]
    out_specs = [
        pl.BlockSpec((bb, N_KV_HEADS, nq, HEAD_DIM), lambda i: (i, 0, 0, 0)),
        pl.BlockSpec((bb, WINDOW, D_KV), lambda i: (i, 0, 0)),
        pl.BlockSpec((bb, WINDOW, D_KV), lambda i: (i, 0, 0)),
    ]
    out_shape = [
        jax.ShapeDtypeStruct((nb, N_KV_HEADS, nq, HEAD_DIM), F32),
        jax.ShapeDtypeStruct((nb, WINDOW, D_KV), F32),
        jax.ShapeDtypeStruct((nb, WINDOW, D_KV), F32),
    ]
    return pl.pallas_call(
        functools.partial(_sample_attn_kernel, t_new=t_new, bb=bb),
        grid=(nb // bb,), in_specs=in_specs, out_specs=out_specs, out_shape=out_shape,
        scratch_shapes=[pltpu.VMEM((WINDOW + 8, D_KV), F32), pltpu.VMEM((WINDOW + 8, D_KV), F32)],
        name="sample_attn",
        compiler_params=pltpu.CompilerParams(dimension_semantics=("arbitrary",)),
    )(sinks, q_grouped, k_new, v_new, cache_k, cache_v)


def _sample_post_kernel(x_ref, part_ref, gate_b_ref, o_ref, w_ba_ref, w_out_ref, ln1_g_ref, ln1_b_ref,
                        h_ref, *, alpha):
    merged = part_ref[...] + gate_b_ref[...] * _dot(o_ref[...].astype(BF16), w_ba_ref[...])
    y = _dot(merged.astype(BF16), w_out_ref[...])
    h_ref[...] = _layer_norm(alpha * x_ref[...] + y, ln1_g_ref[...], ln1_b_ref[...])


def _sample_post(x2d, part, gate_b, o2d, p, alpha):
    return pl.pallas_call(
        functools.partial(_sample_post_kernel, alpha=alpha),
        in_specs=[pl.BlockSpec(memory_space=pltpu.VMEM)] * 8,
        out_specs=pl.BlockSpec(memory_space=pltpu.VMEM),
        out_shape=jax.ShapeDtypeStruct(x2d.shape, F32), name="sample_post",
        compiler_params=pltpu.CompilerParams(vmem_limit_bytes=VMEM_LIMIT_BYTES),
    )(x2d, part, gate_b, o2d, p["w_ba"], p["w_out"], p["ln1_g"], p["ln1_b"])


def _top_values(work, count, with_rank=False):
    vals = []
    rank = jnp.full(work.shape, float(count), F32) if with_rank else None
    for i in range(count):
        top = jnp.max(work, axis=0, keepdims=True)
        vals.append(top)
        hit = work == top
        if with_rank:
            rank = jnp.where(hit, float(i), rank)
        if i + 1 < count:
            work = jnp.where(hit, -jnp.inf, work)
    return (vals, rank) if with_rank else vals


def _candidate_rows(a_rows, b_rows):
    assert PEER_TOPK == 16
    a_all = jnp.concatenate(a_rows, axis=0)
    a_low = a_all[0:8]
    row = lax.broadcasted_iota(jnp.int32, a_low.shape, 0)

    def pair(j):
        return a_low + b_rows[j]

    def corner(j):
        return a_rows[0] + b_rows[j]

    t3 = jnp.where(row >= 5, pltpu.roll(pair(4), 5, 0), pair(2))
    t4 = jnp.where(row >= 6, pltpu.roll(pair(6), 6, 0),
                   jnp.where(row >= 4, pltpu.roll(pair(5), 4, 0), pair(3)))
    t5 = pair(7)
    for k in range(6):
        t5 = jnp.where(row >= 2 + k, corner(8 + k), t5)
    t6 = jnp.where(row >= 2, -jnp.inf, jnp.where(row >= 1, corner(15), corner(14)))
    return jnp.concatenate([a_all + b_rows[0], pair(1), t3, t4, t5, t6], axis=0)


def _bf16_rows(row, rows):
    return jnp.broadcast_to(row.astype(BF16), (rows, row.shape[1]))


def _peer_select(s1, s2):
    a_rows = _top_values(s1, PEER_TOPK)
    b_rows, rank2 = _top_values(s2, PEER_TOPK, with_rank=True)
    best = _top_values(_candidate_rows(a_rows, b_rows), PEER_TOPK)
    norm = jnp.zeros_like(best[0])
    for i in range(PEER_TOPK):
        norm = norm + jnp.exp(best[i] - best[0])
    count = jnp.zeros_like(s1)
    for jj in range(PEER_TOPK):
        count = jnp.where(s1 + b_rows[jj] >= best[PEER_TOPK - 1], float(jj + 1), count)
    e1 = jnp.exp(s1 - a_rows[0]) / norm
    e2 = jnp.exp(s2 - b_rows[0])
    return rank2, e2, count, e1


def _peer_gate_tile(act_ref, coef_ref, first_n1, ns, t, sel_refs):
    rank2_ref, e2_ref, count_ref, e1_ref = sel_refs
    ls = slice(t * LANES, (t + 1) * LANES)
    gates = [jnp.zeros((N_SUBKEYS, LANES), BF16) for _ in ns]
    for hd in range(PEER_HEADS):
        rank2 = rank2_ref[hd, t]
        e2 = e2_ref[hd, t]
        for i, n in enumerate(ns):
            count = _bf16_rows(count_ref[hd, t, pl.ds(first_n1 + n, 1), :], N_SUBKEYS)
            e1 = _bf16_rows(e1_ref[hd, t, pl.ds(first_n1 + n, 1), :], N_SUBKEYS)
            gates[i] = gates[i] + jnp.where(rank2 < count, e2, jnp.zeros((), BF16)) * e1
    for i, n in enumerate(ns):
        rs = slice(n * N_SUBKEYS, (n + 1) * N_SUBKEYS)
        coef_ref[rs, ls] = gates[i] * _gelu(act_ref[rs, ls])


def _peer_kernel(h_ref, wq_ref, sk_ref, u_ref, vt_ref, ln_g_ref, ln_b_ref, y_ref,
                 ht_ref, qt_ref, rank2_ref, e2_ref, count_ref, e1_ref, acc_ref,
                 act0_ref, act1_ref, coef0_ref, coef1_ref, *, alpha, tb, se):
    j = pl.program_id(1)
    ntb = tb // LANES
    nblocks = N_EXPERTS // se
    sel_refs = (rank2_ref, e2_ref, count_ref, e1_ref)
    act_refs = (act0_ref, act1_ref)
    coef_refs = (coef0_ref, coef1_ref)

    @pl.when(j == 0)
    def _():
        ht = h_ref[...].T.astype(BF16)
        ht_ref[...] = ht
        qt_ref[...] = _dot(wq_ref[...], ht).astype(BF16)

        def per_head(hd, carry):
            for t in range(ntb):
                ls = slice(t * LANES, (t + 1) * LANES)
                s = []
                for c in range(2):
                    r0 = pl.multiple_of((hd * 2 + c) * PEER_HALF, PEER_HALF)
                    s.append(_dot(sk_ref[hd * 2 + c], qt_ref[pl.ds(r0, PEER_HALF), ls]))
                rank2, e2, count, e1 = _peer_select(s[0], s[1])
                rank2_ref[hd, t] = rank2.astype(BF16)
                e2_ref[hd, t] = e2.astype(BF16)
                count_ref[hd, t] = count
                e1_ref[hd, t] = e1
            return carry

        lax.fori_loop(0, PEER_HEADS, per_head, 0)
        acc_ref[...] = jnp.zeros_like(acc_ref)

    def step_body(q, with_act, with_gate, with_combine):
        piece_t, piece_e, piece_d = tb // PEER_TOKEN_PIECES, se // PEER_ROW_PIECES, acc_ref.shape[0] // PEER_ROW_PIECES
        pieces = []
        for tp in range(PEER_TOKEN_PIECES):
            hs = slice(tp * piece_t, (tp + 1) * piece_t)
            for lo in range(PEER_ROW_PIECES):
                if with_act:
                    pieces.append(("act", hs, slice(lo * piece_e, (lo + 1) * piece_e)))
                if with_combine:
                    pieces.append(("combine", hs, slice(lo * piece_d, (lo + 1) * piece_d)))
        n_sub = se // N_SUBKEYS
        groups = [tuple(range(g, min(g + PEER_GATE_GROUP, n_sub))) for g in range(0, n_sub, PEER_GATE_GROUP)]
        tiles = [(ns, t) for t in range(ntb) for ns in groups] if with_gate else []
        per_piece = -(-len(tiles) // max(len(pieces), 1))
        for k, (kind, hs, es) in enumerate(pieces):
            for n, t in tiles[k * per_piece:(k + 1) * per_piece]:
                _peer_gate_tile(act_refs[1 - q], coef_refs[1 - q], (j - 1) * (se // N_SUBKEYS), n, t, sel_refs)
            if kind == "act":
                act_refs[q][es, hs] = _dot(u_ref[es, :], ht_ref[:, hs]).astype(BF16)
            else:
                acc_ref[es, hs] += _dot(vt_ref[es, :], coef_refs[q][:, hs])
        for n, t in tiles[len(pieces) * per_piece:]:
            _peer_gate_tile(act_refs[1 - q], coef_refs[1 - q], (j - 1) * (se // N_SUBKEYS), n, t, sel_refs)

    assert nblocks % 2 == 0 and nblocks >= 4
    pl.when(j == 0)(functools.partial(step_body, 0, True, False, False))
    pl.when(j == 1)(functools.partial(step_body, 1, True, True, False))
    for q in range(2):
        pl.when((j >= 2) & (j < nblocks) & (j % 2 == q))(functools.partial(step_body, q, True, True, True))
    pl.when(j == nblocks)(functools.partial(step_body, 0, False, True, True))
    pl.when(j == nblocks + 1)(functools.partial(step_body, 1, False, False, True))

    @pl.when(j == pl.num_programs(1) - 1)
    def _():
        y_ref[...] = _layer_norm(alpha * h_ref[...] + acc_ref[...].T, ln_g_ref[...], ln_b_ref[...])


def _peer(h2d, p, alpha, tb=PEER_TOKEN_BLOCK):
    n, D = h2d.shape
    tb = min(tb, n)
    se = p["peer_vt"].shape[2]
    assert n % tb == 0 and tb % LANES == 0 and N_EXPERTS % se == 0 and se % N_SUBKEYS == 0
    ntb = tb // LANES
    qdim = PEER_HEADS * PEER_KEY_DIM
    nblocks = N_EXPERTS // se
    in_specs = [
        pl.BlockSpec((tb, D), lambda i, j: (i, 0)),
        _full((qdim, D)),
        _full((PEER_HEADS * 2, N_SUBKEYS, PEER_HALF)),
        pl.BlockSpec((se, D), lambda i, j: (jnp.minimum(j, nblocks - 1), 0)),
        pl.BlockSpec((None, D, se), lambda i, j: (jnp.clip(j - 2, 0, nblocks - 1), 0, 0)),
        _full((1, D)), _full((1, D)),
    ]
    per_head = (PEER_HEADS, ntb, N_SUBKEYS, LANES)
    scratch = [
        pltpu.VMEM((D, tb), BF16),
        pltpu.VMEM((qdim, tb), BF16),
        pltpu.VMEM(per_head, BF16), pltpu.VMEM(per_head, BF16),
        pltpu.VMEM(per_head, F32), pltpu.VMEM(per_head, F32),
        pltpu.VMEM((D, tb), F32),
    ] + [pltpu.VMEM((se, tb), BF16)] * 4
    return pl.pallas_call(
        functools.partial(_peer_kernel, alpha=alpha, tb=tb, se=se),
        grid=(n // tb, nblocks + 2), in_specs=in_specs,
        out_specs=pl.BlockSpec((tb, D), lambda i, j: (i, 0), pipeline_mode=pl.Buffered(1)),
        out_shape=jax.ShapeDtypeStruct((n, D), F32), scratch_shapes=scratch, name="peer",
        compiler_params=pltpu.CompilerParams(
            dimension_semantics=("arbitrary", "arbitrary"), vmem_limit_bytes=VMEM_LIMIT_BYTES),
    )(h2d, p["peer_wq_t"], p["peer_sk"], p["peer_u"], p["peer_vt"], p["ln2_g"], p["ln2_b"])


def _layer_params(l, w_in, sinks, sgu_ln_g, sgu_ln_b, sgu_w, sgu_b, w_branch_sgu, w_branch_attn, w_out,
                  ln1_g, ln1_b, peer_w_q, peer_sub_keys, peer_u, peer_v, ln2_g, ln2_b):
    return {
        "w_in": w_in[l].astype(BF16),
        "sinks": sinks[l],
        "sgu_ln_g": sgu_ln_g[l][None, :], "sgu_ln_b": sgu_ln_b[l][None, :],
        "sgu_w": sgu_w[l],
        "sgu_b": sgu_b[l], "sgu_bcol": sgu_b[l].T,
        "w_bs": w_branch_sgu[l].astype(BF16), "w_ba": w_branch_attn[l].astype(BF16),
        "w_out": w_out[l].astype(BF16),
        "ln1_g": ln1_g[l][None, :], "ln1_b": ln1_b[l][None, :],
        "peer_wq_t": peer_w_q[l].T.astype(BF16),
        "peer_sk": peer_sub_keys[l].reshape(PEER_HEADS * 2, N_SUBKEYS, PEER_HALF).astype(BF16),
        "peer_u": peer_u[l].astype(BF16),
        "peer_vt": peer_v[l].reshape(N_EXPERTS // PEER_EXPERT_BLOCK, PEER_EXPERT_BLOCK, D_MODEL)
                            .transpose(0, 2, 1).astype(BF16),
        "ln2_g": ln2_g[l][None, :], "ln2_b": ln2_b[l][None, :],
    }


def _prompt_layer(x, p, alpha):
    B, S, D = x.shape
    h, k_last, v_last, sv_last = _prompt_mixer(x, p, alpha)
    y = _peer(h.reshape(B * S, D), p, alpha).reshape(B, S, D)
    return (y, k_last.reshape(B, WINDOW, N_KV_HEADS, HEAD_DIM), v_last.reshape(B, WINDOW, N_KV_HEADS, HEAD_DIM),
            sv_last.reshape(B, CHUNK, N_SGU_GROUPS, SGU_GROUP_DIM))


def _sample_layer(x, cache_k, cache_v, p, alpha):
    B, T, D = x.shape
    x2d = x.reshape(B * T, D)
    q, k_new, v_new, svn, part, gate_b = _sample_pre(x2d, p, T)
    qg = q.reshape(B, T, N_KV_HEADS, Q_PER_KV, HEAD_DIM).transpose(0, 2, 3, 1, 4)
    qg = qg.reshape(B, N_KV_HEADS, Q_PER_KV * T, HEAD_DIM)
    og, new_k, new_v = _sample_attn(qg, k_new.reshape(B, T, D_KV), v_new.reshape(B, T, D_KV),
                                    cache_k.reshape(B, WINDOW, D_KV), cache_v.reshape(B, WINDOW, D_KV),
                                    p["sinks"], T)
    o2d = og.reshape(B, N_KV_HEADS, Q_PER_KV, T, HEAD_DIM).transpose(0, 3, 1, 2, 4).reshape(B * T, D_ATTN)
    h = _sample_post(x2d, part, gate_b, o2d, p, alpha)
    y = _peer(h, p, alpha).reshape(B, T, D)
    return (y, new_k.reshape(B, WINDOW, N_KV_HEADS, HEAD_DIM), new_v.reshape(B, WINDOW, N_KV_HEADS, HEAD_DIM),
            svn.reshape(B, T, N_SGU_GROUPS, SGU_GROUP_DIM))


def kernel(x_prompt, x_sample, cache_k, cache_v, w_in, sinks, sgu_ln_g, sgu_ln_b, sgu_w, sgu_b, w_branch_sgu, w_branch_attn, w_out, ln1_g, ln1_b, peer_w_q, peer_sub_keys, peer_u, peer_v, ln2_g, ln2_b):
    depth = w_in.shape[0]
    alpha = (2.0 * depth) ** 0.25
    yp, ys = x_prompt, x_sample
    outs = [[] for _ in range(6)]
    for l in range(depth):
        p = _layer_params(l, w_in, sinks, sgu_ln_g, sgu_ln_b, sgu_w, sgu_b, w_branch_sgu, w_branch_attn,
                          w_out, ln1_g, ln1_b, peer_w_q, peer_sub_keys, peer_u, peer_v, ln2_g, ln2_b)
        yp, kp, vp, svp = _prompt_layer(yp, p, alpha)
        ys, ksm, vsm, svs = _sample_layer(ys, cache_k[l], cache_v[l], p, alpha)
        for acc, val in zip(outs, (kp, vp, ksm, vsm, svp, svs)):
            acc.append(val)
    return (yp, ys) + tuple(jnp.stack(o) for o in outs)
```

```python
import functools

import jax
import jax.numpy as jnp
from jax import lax
from jax.experimental import pallas as pl
from jax.experimental.pallas import tpu as pltpu

F32 = jnp.float32
BF16 = jnp.bfloat16

D_MODEL = 1024
N_Q_HEADS = 8
N_KV_HEADS = 2
HEAD_DIM = 64
Q_PER_KV = N_Q_HEADS // N_KV_HEADS
D_ATTN = N_Q_HEADS * HEAD_DIM
D_KV = N_KV_HEADS * HEAD_DIM
WINDOW = 128
CHUNK = 128
D_SGU = D_MODEL // 2
N_SGU_GROUPS = 4
SGU_GROUP_DIM = D_SGU // N_SGU_GROUPS
PEER_HEADS = 8
N_SUBKEYS = 128
N_EXPERTS = N_SUBKEYS * N_SUBKEYS
PEER_TOPK = 16
PEER_KEY_DIM = 256
PEER_HALF = PEER_KEY_DIM // 2
LN_EPS = 1e-5
NEG_INF = -1e30
D_IN = D_ATTN + 2 * D_KV + 2 * D_SGU + 2 * D_MODEL
_OFF_Q = 0
_OFF_K = D_ATTN
_OFF_V = _OFF_K + D_KV
_OFF_U = _OFF_V + D_KV
_OFF_SV = _OFF_U + D_SGU
_OFF_GA = _OFF_SV + D_SGU
_OFF_GB = _OFF_GA + D_MODEL

LANES = 128
VMEM_LIMIT_BYTES = 56 * 1024 * 1024

PROMPT_BLOCK = 512
SAMPLE_ATTN_BATCH = 8
PEER_TOKEN_BLOCK = 512
PEER_EXPERT_BLOCK = 1024
PEER_GATE_GROUP = 1
PEER_TOKEN_PIECES = 2
PEER_ROW_PIECES = 1


def _gelu(x):
    return 0.5 * x * (1.0 + jnp.tanh(0.7978845608028654 * (x + 0.044715 * (x * x * x))))


def _sigmoid(x):
    return 1.0 / (1.0 + jnp.exp(-x))


def _layer_norm(x, g, b):
    mu = jnp.mean(x, -1, keepdims=True)
    xc = x - mu
    var = jnp.mean(xc * xc, -1, keepdims=True)
    return xc * lax.rsqrt(var + LN_EPS) * g + b


def _dot(a, b):
    return jnp.dot(a, b, preferred_element_type=F32)


def _dot_nt(a, b):
    return lax.dot_general(a, b, (((1,), (1,)), ((), ())), preferred_element_type=F32)


def _sink_softmax_pv(logits, sink_col, vals):
    m = jnp.maximum(jnp.max(logits, -1, keepdims=True), sink_col)
    p = jnp.exp(logits - m)
    denom = jnp.sum(p, -1, keepdims=True) + jnp.exp(sink_col - m)
    return _dot(p.astype(BF16), vals) / denom


def _prompt_mixer_kernel(sinks_ref, x_ref, w_in_ref, sgu_g_ref, sgu_b_ref, sgu_w_ref, sgu_bcol_ref,
                         w_bs_ref, w_ba_ref, w_out_ref, ln1_g_ref, ln1_b_ref,
                         h_ref, k_ref, v_ref, sv_ref,
                         kext_ref, vext_ref, sgu_o_ref, attn_o_ref, *, alpha, ts):
    s = pl.program_id(1)
    last = pl.num_programs(1) - 1
    nblk = ts // WINDOW

    x = x_ref[...]
    z = _dot(x.astype(BF16), w_in_ref[...])
    k = z[:, _OFF_K:_OFF_V]
    v = z[:, _OFF_V:_OFF_U]
    u = _gelu(z[:, _OFF_U:_OFF_SV])
    svn = _layer_norm(_gelu(z[:, _OFF_SV:_OFF_GA]), sgu_g_ref[...], sgu_b_ref[...])

    @pl.when(s == 0)
    def _():
        kext_ref[0:WINDOW, :] = jnp.zeros((WINDOW, D_KV), BF16)
        vext_ref[0:WINDOW, :] = jnp.zeros((WINDOW, D_KV), BF16)

    kext_ref[WINDOW:, :] = k.astype(BF16)
    vext_ref[WINDOW:, :] = v.astype(BF16)

    @pl.when(s == last)
    def _():
        k_ref[...] = k[ts - WINDOW:, :]
        v_ref[...] = v[ts - WINDOW:, :]
        sv_ref[...] = svn[ts - CHUNK:, :]

    row = lax.broadcasted_iota(jnp.int32, (CHUNK, CHUNK), 0)
    col = lax.broadcasted_iota(jnp.int32, (CHUNK, CHUNK), 1)
    for g in range(N_SGU_GROUPS):
        wt = jnp.where(col <= row, sgu_w_ref[g], 0.0).astype(BF16)
        bcol = sgu_bcol_ref[:, g:g + 1]
        gs = slice(g * SGU_GROUP_DIM, (g + 1) * SGU_GROUP_DIM)
        for c in range(ts // CHUNK):
            rs = slice(c * CHUNK, (c + 1) * CHUNK)
            mix = _dot(wt, svn[rs, gs].astype(BF16)) + bcol
            sgu_o_ref[rs, gs] = (u[rs, gs] * mix).astype(BF16)

    nq = Q_PER_KV * WINDOW
    kj = lax.broadcasted_iota(jnp.int32, (2 * WINDOW, nq), 0)
    qi = lax.broadcasted_iota(jnp.int32, (2 * WINDOW, nq), 1) % WINDOW
    rel = qi + WINDOW - kj
    bias = jnp.where((rel >= 0) & (rel < WINDOW), 0.0, NEG_INF).astype(F32)
    bias_first = jnp.where(s > 0, bias, jnp.where(kj < WINDOW, NEG_INF, bias))
    scale = HEAD_DIM ** -0.5
    for i in range(nblk):
        qs = slice(i * WINDOW, (i + 1) * WINDOW)
        outs = []
        for g in range(N_KV_HEADS):
            ds = slice(g * HEAD_DIM, (g + 1) * HEAD_DIM)
            kb = kext_ref[i * WINDOW:(i + 2) * WINDOW, ds]
            vb = vext_ref[i * WINDOW:(i + 2) * WINDOW, ds]
            q4 = jnp.concatenate(
                [z[qs, (g * Q_PER_KV + r) * HEAD_DIM:(g * Q_PER_KV + r + 1) * HEAD_DIM]
                 for r in range(Q_PER_KV)], axis=0).astype(BF16)
            logits = _dot_nt(kb, q4) * scale + (bias_first if i == 0 else bias)
            sink = jnp.concatenate(
                [jnp.full((1, WINDOW), sinks_ref[g * Q_PER_KV + r], F32) for r in range(Q_PER_KV)], axis=1)
            m = jnp.maximum(jnp.max(logits, 0, keepdims=True), sink)
            p = jnp.exp(logits - m)
            denom = jnp.sum(p, 0, keepdims=True) + jnp.exp(sink - m)
            pv = lax.dot_general(vb, p.astype(BF16), (((0,), (0,)), ((), ())), preferred_element_type=F32)
            outs.append(pv / denom)
        o_all = jnp.concatenate(outs, axis=0).T
        heads = [o_all[r * WINDOW:(r + 1) * WINDOW, g * HEAD_DIM:(g + 1) * HEAD_DIM]
                 for g in range(N_KV_HEADS) for r in range(Q_PER_KV)]
        attn_o_ref[qs, :] = jnp.concatenate(heads, axis=1).astype(BF16)

    kext_ref[0:WINDOW, :] = kext_ref[ts:ts + WINDOW, :]
    vext_ref[0:WINDOW, :] = vext_ref[ts:ts + WINDOW, :]

    merged = (_sigmoid(z[:, _OFF_GA:_OFF_GB]) * _dot(sgu_o_ref[...], w_bs_ref[...])
              + _sigmoid(z[:, _OFF_GB:]) * _dot(attn_o_ref[...], w_ba_ref[...]))
    y = _dot(merged.astype(BF16), w_out_ref[...])
    h_ref[...] = _layer_norm(alpha * x + y, ln1_g_ref[...], ln1_b_ref[...])


def _full(shape):
    return pl.BlockSpec(shape, lambda *_: (0,) * len(shape), pipeline_mode=pl.Buffered(1))


def _prompt_mixer(x, p, alpha, ts=PROMPT_BLOCK):
    B, S, D = x.shape
    ts = min(ts, S)
    assert S % ts == 0 and ts % WINDOW == 0 and ts % CHUNK == 0
    grid = (B, S // ts)
    smem = pl.BlockSpec(memory_space=pltpu.SMEM)
    in_specs = [
        smem,
        pl.BlockSpec((None, ts, D), lambda b, s: (b, s, 0)),
        _full((D, D_IN)), _full((1, D_SGU)), _full((1, D_SGU)),
        _full((N_SGU_GROUPS, CHUNK, CHUNK)), _full((CHUNK, N_SGU_GROUPS)),
        _full((D_SGU, D)), _full((D_ATTN, D)), _full((D, D)), _full((1, D)), _full((1, D)),
    ]
    out_specs = [
        pl.BlockSpec((None, ts, D), lambda b, s: (b, s, 0)),
        pl.BlockSpec((None, WINDOW, D_KV), lambda b, s: (b, 0, 0)),
        pl.BlockSpec((None, WINDOW, D_KV), lambda b, s: (b, 0, 0)),
        pl.BlockSpec((None, CHUNK, D_SGU), lambda b, s: (b, 0, 0)),
    ]
    out_shape = [
        jax.ShapeDtypeStruct((B, S, D), F32),
        jax.ShapeDtypeStruct((B, WINDOW, D_KV), F32),
        jax.ShapeDtypeStruct((B, WINDOW, D_KV), F32),
        jax.ShapeDtypeStruct((B, CHUNK, D_SGU), F32),
    ]
    scratch = [
        pltpu.VMEM((ts + WINDOW, D_KV), BF16),
        pltpu.VMEM((ts + WINDOW, D_KV), BF16),
        pltpu.VMEM((ts, D_SGU), BF16),
        pltpu.VMEM((ts, D_ATTN), BF16),
    ]
    return pl.pallas_call(
        functools.partial(_prompt_mixer_kernel, alpha=alpha, ts=ts),
        grid=grid, in_specs=in_specs, out_specs=out_specs, out_shape=out_shape,
        scratch_shapes=scratch, name="prompt_mixer",
        compiler_params=pltpu.CompilerParams(
            dimension_semantics=("arbitrary", "arbitrary"), vmem_limit_bytes=VMEM_LIMIT_BYTES),
    )(p["sinks"], x, p["w_in"], p["sgu_ln_g"], p["sgu_ln_b"], p["sgu_w"], p["sgu_bcol"],
      p["w_bs"], p["w_ba"], p["w_out"], p["ln1_g"], p["ln1_b"])


def _sample_pre_kernel(sgu_w_ref, sgu_b_ref, x_ref, w_in_ref, sgu_g_ref, sgu_bb_ref, w_bs_ref,
                       q_ref, k_ref, v_ref, sv_ref, part_ref, gate_b_ref, *, t_new):
    x = x_ref[...]
    n = x.shape[0]
    z = _dot(x.astype(BF16), w_in_ref[...])
    q_ref[...] = z[:, _OFF_Q:_OFF_K]
    k_ref[...] = z[:, _OFF_K:_OFF_V]
    v_ref[...] = z[:, _OFF_V:_OFF_U]
    u = _gelu(z[:, _OFF_U:_OFF_SV])
    svn = _layer_norm(_gelu(z[:, _OFF_SV:_OFF_GA]), sgu_g_ref[...], sgu_bb_ref[...])
    sv_ref[...] = svn
    t_of_row = lax.broadcasted_iota(jnp.int32, (n, SGU_GROUP_DIM), 0) % t_new
    pieces = []
    for g in range(N_SGU_GROUPS):
        vg = svn[:, g * SGU_GROUP_DIM:(g + 1) * SGU_GROUP_DIM]
        mix = jnp.zeros((n, SGU_GROUP_DIM), F32)
        for t in range(t_new):
            mix = jnp.where(t_of_row == t, sgu_b_ref[g, t], mix)
        for d in range(t_new):
            coef = jnp.zeros((n, SGU_GROUP_DIM), F32)
            for t in range(d, t_new):
                coef = jnp.where(t_of_row == t, sgu_w_ref[g, t * t_new + t - d], coef)
            shifted = vg if d == 0 else pltpu.roll(vg, d, 0)
            mix = mix + coef * shifted
        pieces.append(u[:, g * SGU_GROUP_DIM:(g + 1) * SGU_GROUP_DIM] * mix)
    sgu_o = jnp.concatenate(pieces, axis=1).astype(BF16)
    part_ref[...] = _sigmoid(z[:, _OFF_GA:_OFF_GB]) * _dot(sgu_o, w_bs_ref[...])
    gate_b_ref[...] = _sigmoid(z[:, _OFF_GB:])


def _sample_pre(x2d, p, t_new):
    n, D = x2d.shape
    smem = pl.BlockSpec(memory_space=pltpu.SMEM)
    out_shape = [
        jax.ShapeDtypeStruct((n, D_ATTN), F32),
        jax.ShapeDtypeStruct((n, D_KV), F32),
        jax.ShapeDtypeStruct((n, D_KV), F32),
        jax.ShapeDtypeStruct((n, D_SGU), F32),
        jax.ShapeDtypeStruct((n, D), F32),
        jax.ShapeDtypeStruct((n, D), F32),
    ]
    return pl.pallas_call(
        functools.partial(_sample_pre_kernel, t_new=t_new),
        in_specs=[smem, smem] + [pl.BlockSpec(memory_space=pltpu.VMEM)] * 5,
        out_specs=[pl.BlockSpec(memory_space=pltpu.VMEM)] * 6,
        out_shape=out_shape, name="sample_pre",
        compiler_params=pltpu.CompilerParams(vmem_limit_bytes=VMEM_LIMIT_BYTES),
    )(p["sgu_w"][:, :t_new, :t_new].reshape(N_SGU_GROUPS, t_new * t_new), p["sgu_b"],
      x2d, p["w_in"], p["sgu_ln_g"], p["sgu_ln_b"], p["w_bs"])


def _sample_attn_kernel(sinks_ref, q_ref, kn_ref, vn_ref, ck_ref, cv_ref,
                        o_ref, nk_ref, nv_ref, kbuf_ref, vbuf_ref, *, t_new, bb):
    nq = Q_PER_KV * t_new
    nkeys = WINDOW + 8
    row = lax.broadcasted_iota(jnp.int32, (nq, nkeys), 0)
    kj = lax.broadcasted_iota(jnp.int32, (nq, nkeys), 1)
    t_q = row % t_new
    bias = jnp.where((kj > t_q) & (kj <= t_q + WINDOW), 0.0, NEG_INF).astype(F32)
    r_of_row = lax.broadcasted_iota(jnp.int32, (nq, 1), 0) // t_new
    scale = HEAD_DIM ** -0.5
    pad = jnp.zeros((8 - t_new, D_KV), F32)
    for b in range(bb):
        kbuf_ref[0:WINDOW, :] = ck_ref[b]
        kbuf_ref[WINDOW:WINDOW + t_new, :] = kn_ref[b]
        kbuf_ref[WINDOW + t_new:, :] = pad
        vbuf_ref[0:WINDOW, :] = cv_ref[b]
        vbuf_ref[WINDOW:WINDOW + t_new, :] = vn_ref[b]
        vbuf_ref[WINDOW + t_new:, :] = pad
        nk_ref[b] = kbuf_ref[t_new:t_new + WINDOW, :]
        nv_ref[b] = vbuf_ref[t_new:t_new + WINDOW, :]
        for g in range(N_KV_HEADS):
            ds = slice(g * HEAD_DIM, (g + 1) * HEAD_DIM)
            kg = kbuf_ref[:, ds].astype(BF16)
            vg = vbuf_ref[:, ds].astype(BF16)
            logits = _dot_nt(q_ref[b, g].astype(BF16), kg) * scale + bias
            sink_col = jnp.zeros((nq, 1), F32)
            for r in range(Q_PER_KV):
                sink_col = jnp.where(r_of_row == r, sinks_ref[g * Q_PER_KV + r], sink_col)
            o_ref[b, g] = _sink_softmax_pv(logits, sink_col, vg)


def _sample_attn(q_grouped, k_new, v_new, cache_k, cache_v, sinks, t_new, bb=SAMPLE_ATTN_BATCH):
    nb = q_grouped.shape[0]
    bb = min(bb, nb)
    assert nb % bb == 0 and t_new <= 8
    nq = Q_PER_KV * t_new
    smem = pl.BlockSpec(memory_space=pltpu.SMEM)
    in_specs = [
        smem,
        pl.BlockSpec((bb, N_KV_HEADS, nq, HEAD_DIM), lambda i: (i, 0, 0, 0)),
        pl.BlockSpec((bb, t_new, D_KV), lambda i: (i, 0, 0)),
        pl.BlockSpec((bb, t_new, D_KV), lambda i: (i, 0, 0)),
        pl.BlockSpec((bb, WINDOW, D_KV), lambda i: (i, 0, 0)),
        pl.BlockSpec((bb, WINDOW, D_KV), lambda i: (i, 0, 0)),
    ]
    out_specs = [
        pl.BlockSpec((bb, N_KV_HEADS, nq, HEAD_DIM), lambda i: (i, 0, 0, 0)),
        pl.BlockSpec((bb, WINDOW, D_KV), lambda i: (i, 0, 0)),
        pl.BlockSpec((bb, WINDOW, D_KV), lambda i: (i, 0, 0)),
    ]
    out_shape = [
        jax.ShapeDtypeStruct((nb, N_KV_HEADS, nq, HEAD_DIM), F32),
        jax.ShapeDtypeStruct((nb, WINDOW, D_KV), F32),
        jax.ShapeDtypeStruct((nb, WINDOW, D_KV), F32),
    ]
    return pl.pallas_call(
        functools.partial(_sample_attn_kernel, t_new=t_new, bb=bb),
        grid=(nb // bb,), in_specs=in_specs, out_specs=out_specs, out_shape=out_shape,
        scratch_shapes=[pltpu.VMEM((WINDOW + 8, D_KV), F32), pltpu.VMEM((WINDOW + 8, D_KV), F32)],
        name="sample_attn",
        compiler_params=pltpu.CompilerParams(dimension_semantics=("arbitrary",)),
    )(sinks, q_grouped, k_new, v_new, cache_k, cache_v)


def _sample_post_kernel(x_ref, part_ref, gate_b_ref, o_ref, w_ba_ref, w_out_ref, ln1_g_ref, ln1_b_ref,
                        h_ref, *, alpha):
    merged = part_ref[...] + gate_b_ref[...] * _dot(o_ref[...].astype(BF16), w_ba_ref[...])
    y = _dot(merged.astype(BF16), w_out_ref[...])
    h_ref[...] = _layer_norm(alpha * x_ref[...] + y, ln1_g_ref[...], ln1_b_ref[...])


def _sample_post(x2d, part, gate_b, o2d, p, alpha):
    return pl.pallas_call(
        functools.partial(_sample_post_kernel, alpha=alpha),
        in_specs=[pl.BlockSpec(memory_space=pltpu.VMEM)] * 8,
        out_specs=pl.BlockSpec(memory_space=pltpu.VMEM),
        out_shape=jax.ShapeDtypeStruct(x2d.shape, F32), name="sample_post",
        compiler_params=pltpu.CompilerParams(vmem_limit_bytes=VMEM_LIMIT_BYTES),
    )(x2d, part, gate_b, o2d, p["w_ba"], p["w_out"], p["ln1_g"], p["ln1_b"])


def _top_values(work, count, with_rank=False):
    vals = []
    rank = jnp.full(work.shape, float(count), F32) if with_rank else None
    for i in range(count):
        top = jnp.max(work, axis=0, keepdims=True)
        vals.append(top)
        hit = work == top
        if with_rank:
            rank = jnp.where(hit, float(i), rank)
        if i + 1 < count:
            work = jnp.where(hit, -jnp.inf, work)
    return (vals, rank) if with_rank else vals


def _candidate_rows(a_rows, b_rows):
    assert PEER_TOPK == 16
    a_all = jnp.concatenate(a_rows, axis=0)
    a_low = a_all[0:8]
    row = lax.broadcasted_iota(jnp.int32, a_low.shape, 0)

    def pair(j):
        return a_low + b_rows[j]

    def corner(j):
        return a_rows[0] + b_rows[j]

    t3 = jnp.where(row >= 5, pltpu.roll(pair(4), 5, 0), pair(2))
    t4 = jnp.where(row >= 6, pltpu.roll(pair(6), 6, 0),
                   jnp.where(row >= 4, pltpu.roll(pair(5), 4, 0), pair(3)))
    t5 = pair(7)
    for k in range(6):
        t5 = jnp.where(row >= 2 + k, corner(8 + k), t5)
    t6 = jnp.where(row >= 2, -jnp.inf, jnp.where(row >= 1, corner(15), corner(14)))
    return jnp.concatenate([a_all + b_rows[0], pair(1), t3, t4, t5, t6], axis=0)


def _peer_select(s1, s2):
    a_rows = _top_values(s1, PEER_TOPK)
    b_rows, rank2 = _top_values(s2, PEER_TOPK, with_rank=True)
    best = _top_values(_candidate_rows(a_rows, b_rows), PEER_TOPK)
    norm = jnp.zeros_like(best[0])
    for i in range(PEER_TOPK):
        norm = norm + jnp.exp(best[i] - best[0])
    count = jnp.zeros_like(s1)
    for jj in range(PEER_TOPK):
        count = jnp.where(s1 + b_rows[jj] >= best[PEER_TOPK - 1], float(jj + 1), count)
    e1 = jnp.exp(s1 - a_rows[0]) / norm
    e2 = jnp.exp(s2 - b_rows[0])
    return rank2, e2, count, e1


def _peer_gate_tile(act_ref, coef_ref, first_n1, ns, t, sel_refs):
    rank2_ref, e2_ref, count_ref, e1_ref = sel_refs
    ls = slice(t * LANES, (t + 1) * LANES)
    gates = [jnp.zeros((N_SUBKEYS, LANES), F32) for _ in ns]
    for hd in range(PEER_HEADS):
        rank2 = rank2_ref[hd, t]
        e2 = e2_ref[hd, t]
        for i, n in enumerate(ns):
            count = count_ref[hd, t, pl.ds(first_n1 + n, 1), :]
            e1 = e1_ref[hd, t, pl.ds(first_n1 + n, 1), :]
            gates[i] = gates[i] + jnp.where(rank2 < count, e2, 0.0) * e1
    for i, n in enumerate(ns):
        rs = slice(n * N_SUBKEYS, (n + 1) * N_SUBKEYS)
        coef_ref[rs, ls] = (gates[i] * _gelu(act_ref[rs, ls])).astype(BF16)


def _peer_kernel(h_ref, wq_ref, sk_ref, u_ref, vt_ref, ln_g_ref, ln_b_ref, y_ref,
                 ht_ref, qt_ref, rank2_ref, e2_ref, count_ref, e1_ref, acc_ref,
                 act0_ref, act1_ref, coef0_ref, coef1_ref, *, alpha, tb, se):
    j = pl.program_id(1)
    ntb = tb // LANES
    nblocks = N_EXPERTS // se
    sel_refs = (rank2_ref, e2_ref, count_ref, e1_ref)
    act_refs = (act0_ref, act1_ref)
    coef_refs = (coef0_ref, coef1_ref)

    @pl.when(j == 0)
    def _():
        ht = h_ref[...].T.astype(BF16)
        ht_ref[...] = ht
        qt_ref[...] = _dot(wq_ref[...], ht).astype(BF16)

        def per_head(hd, carry):
            for t in range(ntb):
                ls = slice(t * LANES, (t + 1) * LANES)
                s = []
                for c in range(2):
                    r0 = pl.multiple_of((hd * 2 + c) * PEER_HALF, PEER_HALF)
                    s.append(_dot(sk_ref[hd * 2 + c], qt_ref[pl.ds(r0, PEER_HALF), ls]))
                rank2, e2, count, e1 = _peer_select(s[0], s[1])
                rank2_ref[hd, t] = rank2
                e2_ref[hd, t] = e2
                count_ref[hd, t] = count
                e1_ref[hd, t] = e1
            return carry

        lax.fori_loop(0, PEER_HEADS, per_head, 0)
        acc_ref[...] = jnp.zeros_like(acc_ref)

    def step_body(q, with_act, with_gate, with_combine):
        piece_t, piece_e, piece_d = tb // PEER_TOKEN_PIECES, se // PEER_ROW_PIECES, acc_ref.shape[0] // PEER_ROW_PIECES
        pieces = []
        for tp in range(PEER_TOKEN_PIECES):
            hs = slice(tp * piece_t, (tp + 1) * piece_t)
            for lo in range(PEER_ROW_PIECES):
                if with_act:
                    pieces.append(("act", hs, slice(lo * piece_e, (lo + 1) * piece_e)))
                if with_combine:
                    pieces.append(("combine", hs, slice(lo * piece_d, (lo + 1) * piece_d)))
        n_sub = se // N_SUBKEYS
        groups = [tuple(range(g, min(g + PEER_GATE_GROUP, n_sub))) for g in range(0, n_sub, PEER_GATE_GROUP)]
        tiles = [(ns, t) for t in range(ntb) for ns in groups] if with_gate else []
        per_piece = -(-len(tiles) // max(len(pieces), 1))
        for k, (kind, hs, es) in enumerate(pieces):
            for n, t in tiles[k * per_piece:(k + 1) * per_piece]:
                _peer_gate_tile(act_refs[1 - q], coef_refs[1 - q], (j - 1) * (se // N_SUBKEYS), n, t, sel_refs)
            if kind == "act":
                act_refs[q][es, hs] = _dot(u_ref[es, :], ht_ref[:, hs])
            else:
                acc_ref[es, hs] += _dot(vt_ref[es, :], coef_refs[q][:, hs])
        for n, t in tiles[len(pieces) * per_piece:]:
            _peer_gate_tile(act_refs[1 - q], coef_refs[1 - q], (j - 1) * (se // N_SUBKEYS), n, t, sel_refs)

    assert nblocks % 2 == 0 and nblocks >= 4
    pl.when(j == 0)(functools.partial(step_body, 0, True, False, False))
    pl.when(j == 1)(functools.partial(step_body, 1, True, True, False))
    for q in range(2):
        pl.when((j >= 2) & (j < nblocks) & (j % 2 == q))(functools.partial(step_body, q, True, True, True))
    pl.when(j == nblocks)(functools.partial(step_body, 0, False, True, True))
    pl.when(j == nblocks + 1)(functools.partial(step_body, 1, False, False, True))

    @pl.when(j == pl.num_programs(1) - 1)
    def _():
        y_ref[...] = _layer_norm(alpha * h_ref[...] + acc_ref[...].T, ln_g_ref[...], ln_b_ref[...])


def _peer(h2d, p, alpha, tb=PEER_TOKEN_BLOCK):
    n, D = h2d.shape
    tb = min(tb, n)
    se = p["peer_vt"].shape[2]
    assert n % tb == 0 and tb % LANES == 0 and N_EXPERTS % se == 0 and se % N_SUBKEYS == 0
    ntb = tb // LANES
    qdim = PEER_HEADS * PEER_KEY_DIM
    nblocks = N_EXPERTS // se
    in_specs = [
        pl.BlockSpec((tb, D), lambda i, j: (i, 0)),
        _full((qdim, D)),
        _full((PEER_HEADS * 2, N_SUBKEYS, PEER_HALF)),
        pl.BlockSpec((se, D), lambda i, j: (jnp.minimum(j, nblocks - 1), 0)),
        pl.BlockSpec((None, D, se), lambda i, j: (jnp.clip(j - 2, 0, nblocks - 1), 0, 0)),
        _full((1, D)), _full((1, D)),
    ]
    per_head = (PEER_HEADS, ntb, N_SUBKEYS, LANES)
    scratch = [
        pltpu.VMEM((D, tb), BF16),
        pltpu.VMEM((qdim, tb), BF16),
        pltpu.VMEM(per_head, F32), pltpu.VMEM(per_head, F32),
        pltpu.VMEM(per_head, F32), pltpu.VMEM(per_head, F32),
        pltpu.VMEM((D, tb), F32),
    ] + [pltpu.VMEM((se, tb), F32)] * 2 + [pltpu.VMEM((se, tb), BF16)] * 2
    return pl.pallas_call(
        functools.partial(_peer_kernel, alpha=alpha, tb=tb, se=se),
        grid=(n // tb, nblocks + 2), in_specs=in_specs,
        out_specs=pl.BlockSpec((tb, D), lambda i, j: (i, 0), pipeline_mode=pl.Buffered(1)),
        out_shape=jax.ShapeDtypeStruct((n, D), F32), scratch_shapes=scratch, name="peer",
        compiler_params=pltpu.CompilerParams(
            dimension_semantics=("arbitrary", "arbitrary"), vmem_limit_bytes=VMEM_LIMIT_BYTES),
    )(h2d, p["peer_wq_t"], p["peer_sk"], p["peer_u"], p["peer_vt"], p["ln2_g"], p["ln2_b"])


def _layer_params(l, w_in, sinks, sgu_ln_g, sgu_ln_b, sgu_w, sgu_b, w_branch_sgu, w_branch_attn, w_out,
                  ln1_g, ln1_b, peer_w_q, peer_sub_keys, peer_u, peer_v, ln2_g, ln2_b):
    return {
        "w_in": w_in[l].astype(BF16),
        "sinks": sinks[l],
        "sgu_ln_g": sgu_ln_g[l][None, :], "sgu_ln_b": sgu_ln_b[l][None, :],
        "sgu_w": sgu_w[l],
        "sgu_b": sgu_b[l], "sgu_bcol": sgu_b[l].T,
        "w_bs": w_branch_sgu[l].astype(BF16), "w_ba": w_branch_attn[l].astype(BF16),
        "w_out": w_out[l].astype(BF16),
        "ln1_g": ln1_g[l][None, :], "ln1_b": ln1_b[l][None, :],
        "peer_wq_t": peer_w_q[l].T.astype(BF16),
        "peer_sk": peer_sub_keys[l].reshape(PEER_HEADS * 2, N_SUBKEYS, PEER_HALF).astype(BF16),
        "peer_u": peer_u[l].astype(BF16),
        "peer_vt": peer_v[l].reshape(N_EXPERTS // PEER_EXPERT_BLOCK, PEER_EXPERT_BLOCK, D_MODEL)
                            .transpose(0, 2, 1).astype(BF16),
        "ln2_g": ln2_g[l][None, :], "ln2_b": ln2_b[l][None, :],
    }


def _prompt_layer(x, p, alpha):
    B, S, D = x.shape
    h, k_last, v_last, sv_last = _prompt_mixer(x, p, alpha)
    y = _peer(h.reshape(B * S, D), p, alpha).reshape(B, S, D)
    return (y, k_last.reshape(B, WINDOW, N_KV_HEADS, HEAD_DIM), v_last.reshape(B, WINDOW, N_KV_HEADS, HEAD_DIM),
            sv_last.reshape(B, CHUNK, N_SGU_GROUPS, SGU_GROUP_DIM))


def _sample_layer(x, cache_k, cache_v, p, alpha):
    B, T, D = x.shape
    x2d = x.reshape(B * T, D)
    q, k_new, v_new, svn, part, gate_b = _sample_pre(x2d, p, T)
    qg = q.reshape(B, T, N_KV_HEADS, Q_PER_KV, HEAD_DIM).transpose(0, 2, 3, 1, 4)
    qg = qg.reshape(B, N_KV_HEADS, Q_PER_KV * T, HEAD_DIM)
    og, new_k, new_v = _sample_attn(qg, k_new.reshape(B, T, D_KV), v_new.reshape(B, T, D_KV),
                                    cache_k.reshape(B, WINDOW, D_KV), cache_v.reshape(B, WINDOW, D_KV),
                                    p["sinks"], T)
    o2d = og.reshape(B, N_KV_HEADS, Q_PER_KV, T, HEAD_DIM).transpose(0, 3, 1, 2, 4).reshape(B * T, D_ATTN)
    h = _sample_post(x2d, part, gate_b, o2d, p, alpha)
    y = _peer(h, p, alpha).reshape(B, T, D)
    return (y, new_k.reshape(B, WINDOW, N_KV_HEADS, HEAD_DIM), new_v.reshape(B, WINDOW, N_KV_HEADS, HEAD_DIM),
            svn.reshape(B, T, N_SGU_GROUPS, SGU_GROUP_DIM))


def kernel(x_prompt, x_sample, cache_k, cache_v, w_in, sinks, sgu_ln_g, sgu_ln_b, sgu_w, sgu_b, w_branch_sgu, w_branch_attn, w_out, ln1_g, ln1_b, peer_w_q, peer_sub_keys, peer_u, peer_v, ln2_g, ln2_b):
    depth = w_in.shape[0]
    alpha = (2.0 * depth) ** 0.25
    yp, ys = x_prompt, x_sample
    outs = [[] for _ in range(6)]
    for l in range(depth):
        p = _layer_params(l, w_in, sinks, sgu_ln_g, sgu_ln_b, sgu_w, sgu_b, w_branch_sgu, w_branch_attn,
                          w_out, ln1_g, ln1_b, peer_w_q, peer_sub_keys, peer_u, peer_v, ln2_g, ln2_b)
        yp, kp, vp, svp = _prompt_layer(yp, p, alpha)
        ys, ksm, vsm, svs = _sample_layer(ys, cache_k[l], cache_v[l], p, alpha)
        for acc, val in zip(outs, (kp, vp, ksm, vsm, svp, svs)):
            acc.append(val)
    return (yp, ys) + tuple(jnp.stack(o) for o in outs)
```

```python
import functools

import jax
import jax.numpy as jnp
from jax import lax
from jax.experimental import pallas as pl
from jax.experimental.pallas import tpu as pltpu

F32 = jnp.float32
BF16 = jnp.bfloat16

D_MODEL = 1024
N_Q_HEADS = 8
N_KV_HEADS = 2
HEAD_DIM = 64
Q_PER_KV = N_Q_HEADS // N_KV_HEADS
D_ATTN = N_Q_HEADS * HEAD_DIM
D_KV = N_KV_HEADS * HEAD_DIM
WINDOW = 128
CHUNK = 128
D_SGU = D_MODEL // 2
N_SGU_GROUPS = 4
SGU_GROUP_DIM = D_SGU // N_SGU_GROUPS
PEER_HEADS = 8
N_SUBKEYS = 128
N_EXPERTS = N_SUBKEYS * N_SUBKEYS
PEER_TOPK = 16
PEER_KEY_DIM = 256
PEER_HALF = PEER_KEY_DIM // 2
LN_EPS = 1e-5
NEG_INF = -1e30
D_IN = D_ATTN + 2 * D_KV + 2 * D_SGU + 2 * D_MODEL
_OFF_Q = 0
_OFF_K = D_ATTN
_OFF_V = _OFF_K + D_KV
_OFF_U = _OFF_V + D_KV
_OFF_SV = _OFF_U + D_SGU
_OFF_GA = _OFF_SV + D_SGU
_OFF_GB = _OFF_GA + D_MODEL

LANES = 128
VMEM_LIMIT_BYTES = 56 * 1024 * 1024

PROMPT_BLOCK = 512
SAMPLE_ATTN_BATCH = 8
PEER_TOKEN_BLOCK = 512
PEER_EXPERT_BLOCK = 1024
PEER_GATE_GROUP = 1
PEER_TOKEN_PIECES = 2
PEER_ROW_PIECES = 1


GELU_C0 = 0.7978845608028654
GELU_C1 = 0.044715


def _gelu(x):
    return 0.5 * x * (1.0 + jnp.tanh(GELU_C0 * (x + GELU_C1 * (x * x * x))))


def _sigmoid(x):
    return 1.0 / (1.0 + jnp.exp(-x))


def _layer_norm(x, g, b):
    mu = jnp.mean(x, -1, keepdims=True)
    xc = x - mu
    var = jnp.mean(xc * xc, -1, keepdims=True)
    return xc * lax.rsqrt(var + LN_EPS) * g + b


def _dot(a, b):
    return jnp.dot(a, b, preferred_element_type=F32)


def _dot_nt(a, b):
    return lax.dot_general(a, b, (((1,), (1,)), ((), ())), preferred_element_type=F32)


def _sink_softmax_pv(logits, sink_col, vals):
    m = jnp.maximum(jnp.max(logits, -1, keepdims=True), sink_col)
    p = jnp.exp(logits - m)
    denom = jnp.sum(p, -1, keepdims=True) + jnp.exp(sink_col - m)
    return _dot(p.astype(BF16), vals) / denom


def _prompt_mixer_kernel(sinks_ref, x_ref, w_in_ref, sgu_g_ref, sgu_b_ref, sgu_w_ref, sgu_bcol_ref,
                         w_bs_ref, w_ba_ref, w_out_ref, ln1_g_ref, ln1_b_ref,
                         h_ref, k_ref, v_ref, sv_ref,
                         kext_ref, vext_ref, sgu_o_ref, attn_o_ref, *, alpha, ts):
    s = pl.program_id(1)
    last = pl.num_programs(1) - 1
    nblk = ts // WINDOW

    x = x_ref[...]
    z = _dot(x.astype(BF16), w_in_ref[...])
    k = z[:, _OFF_K:_OFF_V]
    v = z[:, _OFF_V:_OFF_U]
    u = _gelu(z[:, _OFF_U:_OFF_SV])
    svn = _layer_norm(_gelu(z[:, _OFF_SV:_OFF_GA]), sgu_g_ref[...], sgu_b_ref[...])

    @pl.when(s == 0)
    def _():
        kext_ref[0:WINDOW, :] = jnp.zeros((WINDOW, D_KV), BF16)
        vext_ref[0:WINDOW, :] = jnp.zeros((WINDOW, D_KV), BF16)

    kext_ref[WINDOW:, :] = k.astype(BF16)
    vext_ref[WINDOW:, :] = v.astype(BF16)

    @pl.when(s == last)
    def _():
        k_ref[...] = k[ts - WINDOW:, :]
        v_ref[...] = v[ts - WINDOW:, :]
        sv_ref[...] = svn[ts - CHUNK:, :]

    row = lax.broadcasted_iota(jnp.int32, (CHUNK, CHUNK), 0)
    col = lax.broadcasted_iota(jnp.int32, (CHUNK, CHUNK), 1)
    for g in range(N_SGU_GROUPS):
        wt = jnp.where(col <= row, sgu_w_ref[g], 0.0).astype(BF16)
        bcol = sgu_bcol_ref[:, g:g + 1]
        gs = slice(g * SGU_GROUP_DIM, (g + 1) * SGU_GROUP_DIM)
        for c in range(ts // CHUNK):
            rs = slice(c * CHUNK, (c + 1) * CHUNK)
            mix = _dot(wt, svn[rs, gs].astype(BF16)) + bcol
            sgu_o_ref[rs, gs] = (u[rs, gs] * mix).astype(BF16)

    nq = Q_PER_KV * WINDOW
    kj = lax.broadcasted_iota(jnp.int32, (2 * WINDOW, nq), 0)
    qi = lax.broadcasted_iota(jnp.int32, (2 * WINDOW, nq), 1) % WINDOW
    rel = qi + WINDOW - kj
    bias = jnp.where((rel >= 0) & (rel < WINDOW), 0.0, NEG_INF).astype(F32)
    bias_first = jnp.where(s > 0, bias, jnp.where(kj < WINDOW, NEG_INF, bias))
    scale = HEAD_DIM ** -0.5
    for i in range(nblk):
        qs = slice(i * WINDOW, (i + 1) * WINDOW)
        outs = []
        for g in range(N_KV_HEADS):
            ds = slice(g * HEAD_DIM, (g + 1) * HEAD_DIM)
            kb = kext_ref[i * WINDOW:(i + 2) * WINDOW, ds]
            vb = vext_ref[i * WINDOW:(i + 2) * WINDOW, ds]
            q4 = jnp.concatenate(
                [z[qs, (g * Q_PER_KV + r) * HEAD_DIM:(g * Q_PER_KV + r + 1) * HEAD_DIM]
                 for r in range(Q_PER_KV)], axis=0).astype(BF16)
            logits = _dot_nt(kb, q4) * scale + (bias_first if i == 0 else bias)
            sink = jnp.concatenate(
                [jnp.full((1, WINDOW), sinks_ref[g * Q_PER_KV + r], F32) for r in range(Q_PER_KV)], axis=1)
            m = jnp.maximum(jnp.max(logits, 0, keepdims=True), sink)
            p = jnp.exp(logits - m)
            denom = jnp.sum(p, 0, keepdims=True) + jnp.exp(sink - m)
            pv = lax.dot_general(vb, p.astype(BF16), (((0,), (0,)), ((), ())), preferred_element_type=F32)
            outs.append(pv / denom)
        o_all = jnp.concatenate(outs, axis=0).T
        heads = [o_all[r * WINDOW:(r + 1) * WINDOW, g * HEAD_DIM:(g + 1) * HEAD_DIM]
                 for g in range(N_KV_HEADS) for r in range(Q_PER_KV)]
        attn_o_ref[qs, :] = jnp.concatenate(heads, axis=1).astype(BF16)

    kext_ref[0:WINDOW, :] = kext_ref[ts:ts + WINDOW, :]
    vext_ref[0:WINDOW, :] = vext_ref[ts:ts + WINDOW, :]

    merged = (_sigmoid(z[:, _OFF_GA:_OFF_GB]) * _dot(sgu_o_ref[...], w_bs_ref[...])
              + _sigmoid(z[:, _OFF_GB:]) * _dot(attn_o_ref[...], w_ba_ref[...]))
    y = _dot(merged.astype(BF16), w_out_ref[...])
    h_ref[...] = _layer_norm(alpha * x + y, ln1_g_ref[...], ln1_b_ref[...])


def _full(shape):
    return pl.BlockSpec(shape, lambda *_: (0,) * len(shape), pipeline_mode=pl.Buffered(1))


def _prompt_mixer(x, p, alpha, ts=PROMPT_BLOCK):
    B, S, D = x.shape
    ts = min(ts, S)
    assert S % ts == 0 and ts % WINDOW == 0 and ts % CHUNK == 0
    grid = (B, S // ts)
    smem = pl.BlockSpec(memory_space=pltpu.SMEM)
    in_specs = [
        smem,
        pl.BlockSpec((None, ts, D), lambda b, s: (b, s, 0)),
        _full((D, D_IN)), _full((1, D_SGU)), _full((1, D_SGU)),
        _full((N_SGU_GROUPS, CHUNK, CHUNK)), _full((CHUNK, N_SGU_GROUPS)),
        _full((D_SGU, D)), _full((D_ATTN, D)), _full((D, D)), _full((1, D)), _full((1, D)),
    ]
    out_specs = [
        pl.BlockSpec((None, ts, D), lambda b, s: (b, s, 0)),
        pl.BlockSpec((None, WINDOW, D_KV), lambda b, s: (b, 0, 0)),
        pl.BlockSpec((None, WINDOW, D_KV), lambda b, s: (b, 0, 0)),
        pl.BlockSpec((None, CHUNK, D_SGU), lambda b, s: (b, 0, 0)),
    ]
    out_shape = [
        jax.ShapeDtypeStruct((B, S, D), F32),
        jax.ShapeDtypeStruct((B, WINDOW, D_KV), F32),
        jax.ShapeDtypeStruct((B, WINDOW, D_KV), F32),
        jax.ShapeDtypeStruct((B, CHUNK, D_SGU), F32),
    ]
    scratch = [
        pltpu.VMEM((ts + WINDOW, D_KV), BF16),
        pltpu.VMEM((ts + WINDOW, D_KV), BF16),
        pltpu.VMEM((ts, D_SGU), BF16),
        pltpu.VMEM((ts, D_ATTN), BF16),
    ]
    return pl.pallas_call(
        functools.partial(_prompt_mixer_kernel, alpha=alpha, ts=ts),
        grid=grid, in_specs=in_specs, out_specs=out_specs, out_shape=out_shape,
        scratch_shapes=scratch, name="prompt_mixer",
        compiler_params=pltpu.CompilerParams(
            dimension_semantics=("arbitrary", "arbitrary"), vmem_limit_bytes=VMEM_LIMIT_BYTES),
    )(p["sinks"], x, p["w_in"], p["sgu_ln_g"], p["sgu_ln_b"], p["sgu_w"], p["sgu_bcol"],
      p["w_bs"], p["w_ba"], p["w_out"], p["ln1_g"], p["ln1_b"])


def _sample_pre_kernel(sgu_w_ref, sgu_b_ref, x_ref, w_in_ref, sgu_g_ref, sgu_bb_ref, w_bs_ref,
                       q_ref, k_ref, v_ref, sv_ref, part_ref, gate_b_ref, *, t_new):
    x = x_ref[...]
    n = x.shape[0]
    z = _dot(x.astype(BF16), w_in_ref[...])
    q_ref[...] = z[:, _OFF_Q:_OFF_K]
    k_ref[...] = z[:, _OFF_K:_OFF_V]
    v_ref[...] = z[:, _OFF_V:_OFF_U]
    u = _gelu(z[:, _OFF_U:_OFF_SV])
    svn = _layer_norm(_gelu(z[:, _OFF_SV:_OFF_GA]), sgu_g_ref[...], sgu_bb_ref[...])
    sv_ref[...] = svn
    t_of_row = lax.broadcasted_iota(jnp.int32, (n, SGU_GROUP_DIM), 0) % t_new
    pieces = []
    for g in range(N_SGU_GROUPS):
        vg = svn[:, g * SGU_GROUP_DIM:(g + 1) * SGU_GROUP_DIM]
        mix = jnp.zeros((n, SGU_GROUP_DIM), F32)
        for t in range(t_new):
            mix = jnp.where(t_of_row == t, sgu_b_ref[g, t], mix)
        for d in range(t_new):
            coef = jnp.zeros((n, SGU_GROUP_DIM), F32)
            for t in range(d, t_new):
                coef = jnp.where(t_of_row == t, sgu_w_ref[g, t * t_new + t - d], coef)
            shifted = vg if d == 0 else pltpu.roll(vg, d, 0)
            mix = mix + coef * shifted
        pieces.append(u[:, g * SGU_GROUP_DIM:(g + 1) * SGU_GROUP_DIM] * mix)
    sgu_o = jnp.concatenate(pieces, axis=1).astype(BF16)
    part_ref[...] = _sigmoid(z[:, _OFF_GA:_OFF_GB]) * _dot(sgu_o, w_bs_ref[...])
    gate_b_ref[...] = _sigmoid(z[:, _OFF_GB:])


def _sample_pre(x2d, p, t_new):
    n, D = x2d.shape
    smem = pl.BlockSpec(memory_space=pltpu.SMEM)
    out_shape = [
        jax.ShapeDtypeStruct((n, D_ATTN), F32),
        jax.ShapeDtypeStruct((n, D_KV), F32),
        jax.ShapeDtypeStruct((n, D_KV), F32),
        jax.ShapeDtypeStruct((n, D_SGU), F32),
        jax.ShapeDtypeStruct((n, D), F32),
        jax.ShapeDtypeStruct((n, D), F32),
    ]
    return pl.pallas_call(
        functools.partial(_sample_pre_kernel, t_new=t_new),
        in_specs=[smem, smem] + [pl.BlockSpec(memory_space=pltpu.VMEM)] * 5,
        out_specs=[pl.BlockSpec(memory_space=pltpu.VMEM)] * 6,
        out_shape=out_shape, name="sample_pre",
        compiler_params=pltpu.CompilerParams(vmem_limit_bytes=VMEM_LIMIT_BYTES),
    )(p["sgu_w"][:, :t_new, :t_new].reshape(N_SGU_GROUPS, t_new * t_new), p["sgu_b"],
      x2d, p["w_in"], p["sgu_ln_g"], p["sgu_ln_b"], p["w_bs"])


def _sample_attn_kernel(sinks_ref, q_ref, kn_ref, vn_ref, ck_ref, cv_ref,
                        o_ref, nk_ref, nv_ref, kbuf_ref, vbuf_ref, *, t_new, bb):
    nq = Q_PER_KV * t_new
    nkeys = WINDOW + 8
    row = lax.broadcasted_iota(jnp.int32, (nq, nkeys), 0)
    kj = lax.broadcasted_iota(jnp.int32, (nq, nkeys), 1)
    t_q = row % t_new
    bias = jnp.where((kj > t_q) & (kj <= t_q + WINDOW), 0.0, NEG_INF).astype(F32)
    r_of_row = lax.broadcasted_iota(jnp.int32, (nq, 1), 0) // t_new
    scale = HEAD_DIM ** -0.5
    pad = jnp.zeros((8 - t_new, D_KV), F32)
    for b in range(bb):
        kbuf_ref[0:WINDOW, :] = ck_ref[b]
        kbuf_ref[WINDOW:WINDOW + t_new, :] = kn_ref[b]
        kbuf_ref[WINDOW + t_new:, :] = pad
        vbuf_ref[0:WINDOW, :] = cv_ref[b]
        vbuf_ref[WINDOW:WINDOW + t_new, :] = vn_ref[b]
        vbuf_ref[WINDOW + t_new:, :] = pad
        nk_ref[b] = kbuf_ref[t_new:t_new + WINDOW, :]
        nv_ref[b] = vbuf_ref[t_new:t_new + WINDOW, :]
        for g in range(N_KV_HEADS):
            ds = slice(g * HEAD_DIM, (g + 1) * HEAD_DIM)
            kg = kbuf_ref[:, ds].astype(BF16)
            vg = vbuf_ref[:, ds].astype(BF16)
            logits = _dot_nt(q_ref[b, g].astype(BF16), kg) * scale + bias
            sink_col = jnp.zeros((nq, 1), F32)
            for r in range(Q_PER_KV):
                sink_col = jnp.where(r_of_row == r, sinks_ref[g * Q_PER_KV + r], sink_col)
            o_ref[b, g] = _sink_softmax_pv(logits, sink_col, vg)


def _sample_attn(q_grouped, k_new, v_new, cache_k, cache_v, sinks, t_new, bb=SAMPLE_ATTN_BATCH):
    nb = q_grouped.shape[0]
    bb = min(bb, nb)
    assert nb % bb == 0 and t_new <= 8
    nq = Q_PER_KV * t_new
    smem = pl.BlockSpec(memory_space=pltpu.SMEM)
    in_specs = [
        smem,
        pl.BlockSpec((bb, N_KV_HEADS, nq, HEAD_DIM), lambda i: (i, 0, 0, 0)),
        pl.BlockSpec((bb, t_new, D_KV), lambda i: (i, 0, 0)),
        pl.BlockSpec((bb, t_new, D_KV), lambda i: (i, 0, 0)),
        pl.BlockSpec((bb, WINDOW, D_KV), lambda i: (i, 0, 0)),
        pl.BlockSpec((bb, WINDOW, D_KV), lambda i: (i, 0, 0)),
    ]
    out_specs = [
        pl.BlockSpec((bb, N_KV_HEADS, nq, HEAD_DIM), lambda i: (i, 0, 0, 0)),
        pl.BlockSpec((bb, WINDOW, D_KV), lambda i: (i, 0, 0)),
        pl.BlockSpec((bb, WINDOW, D_KV), lambda i: (i, 0, 0)),
    ]
    out_shape = [
        jax.ShapeDtypeStruct((nb, N_KV_HEADS, nq, HEAD_DIM), F32),
        jax.ShapeDtypeStruct((nb, WINDOW, D_KV), F32),
        jax.ShapeDtypeStruct((nb, WINDOW, D_KV), F32),
    ]
    return pl.pallas_call(
        functools.partial(_sample_attn_kernel, t_new=t_new, bb=bb),
        grid=(nb // bb,), in_specs=in_specs, out_specs=out_specs, out_shape=out_shape,
        scratch_shapes=[pltpu.VMEM((WINDOW + 8, D_KV), F32), pltpu.VMEM((WINDOW + 8, D_KV), F32)],
        name="sample_attn",
        compiler_params=pltpu.CompilerParams(dimension_semantics=("arbitrary",)),
    )(sinks, q_grouped, k_new, v_new, cache_k, cache_v)


def _sample_post_kernel(x_ref, part_ref, gate_b_ref, o_ref, w_ba_ref, w_out_ref, ln1_g_ref, ln1_b_ref,
                        h_ref, *, alpha):
    merged = part_ref[...] + gate_b_ref[...] * _dot(o_ref[...].astype(BF16), w_ba_ref[...])
    y = _dot(merged.astype(BF16), w_out_ref[...])
    h_ref[...] = _layer_norm(alpha * x_ref[...] + y, ln1_g_ref[...], ln1_b_ref[...])


def _sample_post(x2d, part, gate_b, o2d, p, alpha):
    return pl.pallas_call(
        functools.partial(_sample_post_kernel, alpha=alpha),
        in_specs=[pl.BlockSpec(memory_space=pltpu.VMEM)] * 8,
        out_specs=pl.BlockSpec(memory_space=pltpu.VMEM),
        out_shape=jax.ShapeDtypeStruct(x2d.shape, F32), name="sample_post",
        compiler_params=pltpu.CompilerParams(vmem_limit_bytes=VMEM_LIMIT_BYTES),
    )(x2d, part, gate_b, o2d, p["w_ba"], p["w_out"], p["ln1_g"], p["ln1_b"])


def _top_values(work, count, with_rank=False):
    vals = []
    rank = jnp.full(work.shape, float(count), F32) if with_rank else None
    for i in range(count):
        top = jnp.max(work, axis=0, keepdims=True)
        vals.append(top)
        hit = work == top
        if with_rank:
            rank = jnp.where(hit, float(i), rank)
        if i + 1 < count:
            work = jnp.where(hit, -jnp.inf, work)
    return (vals, rank) if with_rank else vals


def _candidate_rows(a_rows, b_rows):
    assert PEER_TOPK == 16
    a_all = jnp.concatenate(a_rows, axis=0)
    a_low = a_all[0:8]
    row = lax.broadcasted_iota(jnp.int32, a_low.shape, 0)

    def pair(j):
        return a_low + b_rows[j]

    def corner(j):
        return a_rows[0] + b_rows[j]

    t3 = jnp.where(row >= 5, pltpu.roll(pair(4), 5, 0), pair(2))
    t4 = jnp.where(row >= 6, pltpu.roll(pair(6), 6, 0),
                   jnp.where(row >= 4, pltpu.roll(pair(5), 4, 0), pair(3)))
    t5 = pair(7)
    for k in range(6):
        t5 = jnp.where(row >= 2 + k, corner(8 + k), t5)
    t6 = jnp.where(row >= 2, -jnp.inf, jnp.where(row >= 1, corner(15), corner(14)))
    return jnp.concatenate([a_all + b_rows[0], pair(1), t3, t4, t5, t6], axis=0)


def _peer_select(s1, s2):
    a_rows = _top_values(s1, PEER_TOPK)
    b_rows, rank2 = _top_values(s2, PEER_TOPK, with_rank=True)
    best = _top_values(_candidate_rows(a_rows, b_rows), PEER_TOPK)
    norm = jnp.zeros_like(best[0])
    for i in range(PEER_TOPK):
        norm = norm + jnp.exp(best[i] - best[0])
    count = jnp.zeros_like(s1)
    for jj in range(PEER_TOPK):
        count = jnp.where(s1 + b_rows[jj] >= best[PEER_TOPK - 1], float(jj + 1), count)
    e1 = jnp.exp(s1 - a_rows[0]) / norm
    e2 = jnp.exp(s2 - b_rows[0])
    return rank2, e2, count, e1


def _peer_gate_tile(act_ref, coef_ref, first_n1, ns, t, sel_refs):
    rank2_ref, e2_ref, count_ref, e1_ref = sel_refs
    ls = slice(t * LANES, (t + 1) * LANES)
    gates = [None for _ in ns]
    for hd in range(PEER_HEADS):
        rank2 = rank2_ref[hd, t]
        e2 = e2_ref[hd, t]
        for i, n in enumerate(ns):
            count = count_ref[hd, t, pl.ds(first_n1 + n, 1), :]
            e1 = e1_ref[hd, t, pl.ds(first_n1 + n, 1), :]
            term = jnp.where(rank2 < count, e2, 0.0) * e1
            gates[i] = term if gates[i] is None else gates[i] + term
    for i, n in enumerate(ns):
        rs = slice(n * N_SUBKEYS, (n + 1) * N_SUBKEYS)
        x = act_ref[rs, ls]
        inner = x * (GELU_C0 + (GELU_C0 * GELU_C1) * (x * x))
        coef_ref[rs, ls] = ((gates[i] * x) * (1.0 + jnp.tanh(inner))).astype(BF16)


def _peer_kernel(h_ref, wq_ref, sk_ref, u_ref, vt_ref, ln_g_ref, ln_b_ref, y_ref,
                 ht_ref, qt_ref, rank2_ref, e2_ref, count_ref, e1_ref, acc_ref,
                 act0_ref, act1_ref, coef0_ref, coef1_ref, *, alpha, tb, se):
    j = pl.program_id(1)
    ntb = tb // LANES
    nblocks = N_EXPERTS // se
    sel_refs = (rank2_ref, e2_ref, count_ref, e1_ref)
    act_refs = (act0_ref, act1_ref)
    coef_refs = (coef0_ref, coef1_ref)

    @pl.when(j == 0)
    def _():
        ht = h_ref[...].T.astype(BF16)
        ht_ref[...] = ht
        qt_ref[...] = _dot(wq_ref[...], ht).astype(BF16)

        def per_head(hd, carry):
            for t in range(ntb):
                ls = slice(t * LANES, (t + 1) * LANES)
                s = []
                for c in range(2):
                    r0 = pl.multiple_of((hd * 2 + c) * PEER_HALF, PEER_HALF)
                    s.append(_dot(sk_ref[hd * 2 + c], qt_ref[pl.ds(r0, PEER_HALF), ls]))
                rank2, e2, count, e1 = _peer_select(s[0], s[1])
                rank2_ref[hd, t] = rank2
                e2_ref[hd, t] = e2
                count_ref[hd, t] = count
                e1_ref[hd, t] = 0.5 * e1
            return carry

        lax.fori_loop(0, PEER_HEADS, per_head, 0)
        acc_ref[...] = jnp.zeros_like(acc_ref)

    def step_body(q, with_act, with_gate, with_combine):
        piece_t, piece_e, piece_d = tb // PEER_TOKEN_PIECES, se // PEER_ROW_PIECES, acc_ref.shape[0] // PEER_ROW_PIECES
        pieces = []
        for tp in range(PEER_TOKEN_PIECES):
            hs = slice(tp * piece_t, (tp + 1) * piece_t)
            for lo in range(PEER_ROW_PIECES):
                if with_act:
                    pieces.append(("act", hs, slice(lo * piece_e, (lo + 1) * piece_e)))
                if with_combine:
                    pieces.append(("combine", hs, slice(lo * piece_d, (lo + 1) * piece_d)))
        n_sub = se // N_SUBKEYS
        groups = [tuple(range(g, min(g + PEER_GATE_GROUP, n_sub))) for g in range(0, n_sub, PEER_GATE_GROUP)]
        tiles = [(ns, t) for t in range(ntb) for ns in groups] if with_gate else []
        per_piece = -(-len(tiles) // max(len(pieces), 1))
        for k, (kind, hs, es) in enumerate(pieces):
            for n, t in tiles[k * per_piece:(k + 1) * per_piece]:
                _peer_gate_tile(act_refs[1 - q], coef_refs[1 - q], (j - 1) * (se // N_SUBKEYS), n, t, sel_refs)
            if kind == "act":
                act_refs[q][es, hs] = _dot(u_ref[es, :], ht_ref[:, hs])
            else:
                acc_ref[es, hs] += _dot(vt_ref[es, :], coef_refs[q][:, hs])
        for n, t in tiles[len(pieces) * per_piece:]:
            _peer_gate_tile(act_refs[1 - q], coef_refs[1 - q], (j - 1) * (se // N_SUBKEYS), n, t, sel_refs)

    assert nblocks % 2 == 0 and nblocks >= 4
    pl.when(j == 0)(functools.partial(step_body, 0, True, False, False))
    pl.when(j == 1)(functools.partial(step_body, 1, True, True, False))
    for q in range(2):
        pl.when((j >= 2) & (j < nblocks) & (j % 2 == q))(functools.partial(step_body, q, True, True, True))
    pl.when(j == nblocks)(functools.partial(step_body, 0, False, True, True))
    pl.when(j == nblocks + 1)(functools.partial(step_body, 1, False, False, True))

    @pl.when(j == pl.num_programs(1) - 1)
    def _():
        y_ref[...] = _layer_norm(alpha * h_ref[...] + acc_ref[...].T, ln_g_ref[...], ln_b_ref[...])


def _peer(h2d, p, alpha, tb=PEER_TOKEN_BLOCK):
    n, D = h2d.shape
    tb = min(tb, n)
    se = p["peer_vt"].shape[2]
    assert n % tb == 0 and tb % LANES == 0 and N_EXPERTS % se == 0 and se % N_SUBKEYS == 0
    ntb = tb // LANES
    qdim = PEER_HEADS * PEER_KEY_DIM
    nblocks = N_EXPERTS // se
    in_specs = [
        pl.BlockSpec((tb, D), lambda i, j: (i, 0)),
        _full((qdim, D)),
        _full((PEER_HEADS * 2, N_SUBKEYS, PEER_HALF)),
        pl.BlockSpec((se, D), lambda i, j: (jnp.minimum(j, nblocks - 1), 0)),
        pl.BlockSpec((None, D, se), lambda i, j: (jnp.clip(j - 2, 0, nblocks - 1), 0, 0)),
        _full((1, D)), _full((1, D)),
    ]
    per_head = (PEER_HEADS, ntb, N_SUBKEYS, LANES)
    scratch = [
        pltpu.VMEM((D, tb), BF16),
        pltpu.VMEM((qdim, tb), BF16),
        pltpu.VMEM(per_head, F32), pltpu.VMEM(per_head, F32),
        pltpu.VMEM(per_head, F32), pltpu.VMEM(per_head, F32),
        pltpu.VMEM((D, tb), F32),
    ] + [pltpu.VMEM((se, tb), F32)] * 2 + [pltpu.VMEM((se, tb), BF16)] * 2
    return pl.pallas_call(
        functools.partial(_peer_kernel, alpha=alpha, tb=tb, se=se),
        grid=(n // tb, nblocks + 2), in_specs=in_specs,
        out_specs=pl.BlockSpec((tb, D), lambda i, j: (i, 0), pipeline_mode=pl.Buffered(1)),
        out_shape=jax.ShapeDtypeStruct((n, D), F32), scratch_shapes=scratch, name="peer",
        compiler_params=pltpu.CompilerParams(
            dimension_semantics=("arbitrary", "arbitrary"), vmem_limit_bytes=VMEM_LIMIT_BYTES),
    )(h2d, p["peer_wq_t"], p["peer_sk"], p["peer_u"], p["peer_vt"], p["ln2_g"], p["ln2_b"])


def _layer_params(l, w_in, sinks, sgu_ln_g, sgu_ln_b, sgu_w, sgu_b, w_branch_sgu, w_branch_attn, w_out,
                  ln1_g, ln1_b, peer_w_q, peer_sub_keys, peer_u, peer_v, ln2_g, ln2_b):
    return {
        "w_in": w_in[l].astype(BF16),
        "sinks": sinks[l],
        "sgu_ln_g": sgu_ln_g[l][None, :], "sgu_ln_b": sgu_ln_b[l][None, :],
        "sgu_w": sgu_w[l],
        "sgu_b": sgu_b[l], "sgu_bcol": sgu_b[l].T,
        "w_bs": w_branch_sgu[l].astype(BF16), "w_ba": w_branch_attn[l].astype(BF16),
        "w_out": w_out[l].astype(BF16),
        "ln1_g": ln1_g[l][None, :], "ln1_b": ln1_b[l][None, :],
        "peer_wq_t": peer_w_q[l].T.astype(BF16),
        "peer_sk": peer_sub_keys[l].reshape(PEER_HEADS * 2, N_SUBKEYS, PEER_HALF).astype(BF16),
        "peer_u": peer_u[l].astype(BF16),
        "peer_vt": peer_v[l].reshape(N_EXPERTS // PEER_EXPERT_BLOCK, PEER_EXPERT_BLOCK, D_MODEL)
                            .transpose(0, 2, 1).astype(BF16),
        "ln2_g": ln2_g[l][None, :], "ln2_b": ln2_b[l][None, :],
    }


def _prompt_layer(x, p, alpha):
    B, S, D = x.shape
    h, k_last, v_last, sv_last = _prompt_mixer(x, p, alpha)
    y = _peer(h.reshape(B * S, D), p, alpha).reshape(B, S, D)
    return (y, k_last.reshape(B, WINDOW, N_KV_HEADS, HEAD_DIM), v_last.reshape(B, WINDOW, N_KV_HEADS, HEAD_DIM),
            sv_last.reshape(B, CHUNK, N_SGU_GROUPS, SGU_GROUP_DIM))


def _sample_layer(x, cache_k, cache_v, p, alpha):
    B, T, D = x.shape
    x2d = x.reshape(B * T, D)
    q, k_new, v_new, svn, part, gate_b = _sample_pre(x2d, p, T)
    qg = q.reshape(B, T, N_KV_HEADS, Q_PER_KV, HEAD_DIM).transpose(0, 2, 3, 1, 4)
    qg = qg.reshape(B, N_KV_HEADS, Q_PER_KV * T, HEAD_DIM)
    og, new_k, new_v = _sample_attn(qg, k_new.reshape(B, T, D_KV), v_new.reshape(B, T, D_KV),
                                    cache_k.reshape(B, WINDOW, D_KV), cache_v.reshape(B, WINDOW, D_KV),
                                    p["sinks"], T)
    o2d = og.reshape(B, N_KV_HEADS, Q_PER_KV, T, HEAD_DIM).transpose(0, 3, 1, 2, 4).reshape(B * T, D_ATTN)
    h = _sample_post(x2d, part, gate_b, o2d, p, alpha)
    y = _peer(h, p, alpha).reshape(B, T, D)
    return (y, new_k.reshape(B, WINDOW, N_KV_HEADS, HEAD_DIM), new_v.reshape(B, WINDOW, N_KV_HEADS, HEAD_DIM),
            svn.reshape(B, T, N_SGU_GROUPS, SGU_GROUP_DIM))


def kernel(x_prompt, x_sample, cache_k, cache_v, w_in, sinks, sgu_ln_g, sgu_ln_b, sgu_w, sgu_b, w_branch_sgu, w_branch_attn, w_out, ln1_g, ln1_b, peer_w_q, peer_sub_keys, peer_u, peer_v, ln2_g, ln2_b):
    depth = w_in.shape[0]
    alpha = (2.0 * depth) ** 0.25
    yp, ys = x_prompt, x_sample
    outs = [[] for _ in range(6)]
    for l in range(depth):
        p = _layer_params(l, w_in, sinks, sgu_ln_g, sgu_ln_b, sgu_w, sgu_b, w_branch_sgu, w_branch_attn,
                          w_out, ln1_g, ln1_b, peer_w_q, peer_sub_keys, peer_u, peer_v, ln2_g, ln2_b)
        yp, kp, vp, svp = _prompt_layer(yp, p, alpha)
        ys, ksm, vsm, svs = _sample_layer(ys, cache_k[l], cache_v[l], p, alpha)
        for acc, val in zip(outs, (kp, vp, ksm, vsm, svp, svs)):
            acc.append(val)
    return (yp, ys) + tuple(jnp.stack(o) for o in outs)
```

```python
import functools

import jax
import jax.numpy as jnp
from jax import lax
from jax.experimental import pallas as pl
from jax.experimental.pallas import tpu as pltpu

F32 = jnp.float32
BF16 = jnp.bfloat16

D_MODEL = 1024
N_Q_HEADS = 8
N_KV_HEADS = 2
HEAD_DIM = 64
Q_PER_KV = N_Q_HEADS // N_KV_HEADS
D_ATTN = N_Q_HEADS * HEAD_DIM
D_KV = N_KV_HEADS * HEAD_DIM
WINDOW = 128
CHUNK = 128
D_SGU = D_MODEL // 2
N_SGU_GROUPS = 4
SGU_GROUP_DIM = D_SGU // N_SGU_GROUPS
PEER_HEADS = 8
N_SUBKEYS = 128
N_EXPERTS = N_SUBKEYS * N_SUBKEYS
PEER_TOPK = 16
PEER_KEY_DIM = 256
PEER_HALF = PEER_KEY_DIM // 2
LN_EPS = 1e-5
NEG_INF = -1e30
D_IN = D_ATTN + 2 * D_KV + 2 * D_SGU + 2 * D_MODEL
_OFF_Q = 0
_OFF_K = D_ATTN
_OFF_V = _OFF_K + D_KV
_OFF_U = _OFF_V + D_KV
_OFF_SV = _OFF_U + D_SGU
_OFF_GA = _OFF_SV + D_SGU
_OFF_GB = _OFF_GA + D_MODEL

LANES = 128
VMEM_LIMIT_BYTES = 58 * 1024 * 1024

PROMPT_BLOCK = 512
SAMPLE_ATTN_BATCH = 8
PEER_TOKEN_BLOCK = 512
PEER_EXPERT_BLOCK = 2048
PEER_GATE_GROUP = 1
PEER_TOKEN_PIECES = 2
PEER_ROW_PIECES = 1


GELU_C0 = 0.7978845608028654
GELU_C1 = 0.044715


def _gelu(x):
    return 0.5 * x * (1.0 + jnp.tanh(GELU_C0 * (x + GELU_C1 * (x * x * x))))


def _sigmoid(x):
    return 1.0 / (1.0 + jnp.exp(-x))


def _layer_norm(x, g, b):
    mu = jnp.mean(x, -1, keepdims=True)
    xc = x - mu
    var = jnp.mean(xc * xc, -1, keepdims=True)
    return xc * lax.rsqrt(var + LN_EPS) * g + b


def _dot(a, b):
    return jnp.dot(a, b, preferred_element_type=F32)


def _dot_nt(a, b):
    return lax.dot_general(a, b, (((1,), (1,)), ((), ())), preferred_element_type=F32)


def _sink_softmax_pv(logits, sink_col, vals):
    m = jnp.maximum(jnp.max(logits, -1, keepdims=True), sink_col)
    p = jnp.exp(logits - m)
    denom = jnp.sum(p, -1, keepdims=True) + jnp.exp(sink_col - m)
    return _dot(p.astype(BF16), vals) / denom


def _prompt_mixer_kernel(sinks_ref, x_ref, w_in_ref, sgu_g_ref, sgu_b_ref, sgu_w_ref, sgu_bcol_ref,
                         w_bs_ref, w_ba_ref, w_out_ref, ln1_g_ref, ln1_b_ref,
                         h_ref, k_ref, v_ref, sv_ref,
                         kext_ref, vext_ref, sgu_o_ref, attn_o_ref, *, alpha, ts):
    s = pl.program_id(1)
    last = pl.num_programs(1) - 1
    nblk = ts // WINDOW

    x = x_ref[...]
    z = _dot(x.astype(BF16), w_in_ref[...])
    k = z[:, _OFF_K:_OFF_V]
    v = z[:, _OFF_V:_OFF_U]
    u = _gelu(z[:, _OFF_U:_OFF_SV])
    svn = _layer_norm(_gelu(z[:, _OFF_SV:_OFF_GA]), sgu_g_ref[...], sgu_b_ref[...])

    @pl.when(s == 0)
    def _():
        kext_ref[0:WINDOW, :] = jnp.zeros((WINDOW, D_KV), BF16)
        vext_ref[0:WINDOW, :] = jnp.zeros((WINDOW, D_KV), BF16)

    kext_ref[WINDOW:, :] = k.astype(BF16)
    vext_ref[WINDOW:, :] = v.astype(BF16)

    @pl.when(s == last)
    def _():
        k_ref[...] = k[ts - WINDOW:, :]
        v_ref[...] = v[ts - WINDOW:, :]
        sv_ref[...] = svn[ts - CHUNK:, :]

    row = lax.broadcasted_iota(jnp.int32, (CHUNK, CHUNK), 0)
    col = lax.broadcasted_iota(jnp.int32, (CHUNK, CHUNK), 1)
    for g in range(N_SGU_GROUPS):
        wt = jnp.where(col <= row, sgu_w_ref[g], 0.0).astype(BF16)
        bcol = sgu_bcol_ref[:, g:g + 1]
        gs = slice(g * SGU_GROUP_DIM, (g + 1) * SGU_GROUP_DIM)
        for c in range(ts // CHUNK):
            rs = slice(c * CHUNK, (c + 1) * CHUNK)
            mix = _dot(wt, svn[rs, gs].astype(BF16)) + bcol
            sgu_o_ref[rs, gs] = (u[rs, gs] * mix).astype(BF16)

    nq = Q_PER_KV * WINDOW
    kj = lax.broadcasted_iota(jnp.int32, (2 * WINDOW, nq), 0)
    qi = lax.broadcasted_iota(jnp.int32, (2 * WINDOW, nq), 1) % WINDOW
    rel = qi + WINDOW - kj
    bias = jnp.where((rel >= 0) & (rel < WINDOW), 0.0, NEG_INF).astype(F32)
    bias_first = jnp.where(s > 0, bias, jnp.where(kj < WINDOW, NEG_INF, bias))
    scale = HEAD_DIM ** -0.5
    for i in range(nblk):
        qs = slice(i * WINDOW, (i + 1) * WINDOW)
        outs = []
        for g in range(N_KV_HEADS):
            ds = slice(g * HEAD_DIM, (g + 1) * HEAD_DIM)
            kb = kext_ref[i * WINDOW:(i + 2) * WINDOW, ds]
            vb = vext_ref[i * WINDOW:(i + 2) * WINDOW, ds]
            q4 = jnp.concatenate(
                [z[qs, (g * Q_PER_KV + r) * HEAD_DIM:(g * Q_PER_KV + r + 1) * HEAD_DIM]
                 for r in range(Q_PER_KV)], axis=0).astype(BF16)
            logits = _dot_nt(kb, q4) * scale + (bias_first if i == 0 else bias)
            sink = jnp.concatenate(
                [jnp.full((1, WINDOW), sinks_ref[g * Q_PER_KV + r], F32) for r in range(Q_PER_KV)], axis=1)
            m = jnp.maximum(jnp.max(logits, 0, keepdims=True), sink)
            p = jnp.exp(logits - m)
            denom = jnp.sum(p, 0, keepdims=True) + jnp.exp(sink - m)
            pv = lax.dot_general(vb, p.astype(BF16), (((0,), (0,)), ((), ())), preferred_element_type=F32)
            outs.append(pv / denom)
        o_all = jnp.concatenate(outs, axis=0).T
        heads = [o_all[r * WINDOW:(r + 1) * WINDOW, g * HEAD_DIM:(g + 1) * HEAD_DIM]
                 for g in range(N_KV_HEADS) for r in range(Q_PER_KV)]
        attn_o_ref[qs, :] = jnp.concatenate(heads, axis=1).astype(BF16)

    kext_ref[0:WINDOW, :] = kext_ref[ts:ts + WINDOW, :]
    vext_ref[0:WINDOW, :] = vext_ref[ts:ts + WINDOW, :]

    merged = (_sigmoid(z[:, _OFF_GA:_OFF_GB]) * _dot(sgu_o_ref[...], w_bs_ref[...])
              + _sigmoid(z[:, _OFF_GB:]) * _dot(attn_o_ref[...], w_ba_ref[...]))
    y = _dot(merged.astype(BF16), w_out_ref[...])
    h_ref[...] = _layer_norm(alpha * x + y, ln1_g_ref[...], ln1_b_ref[...])


def _full(shape):
    return pl.BlockSpec(shape, lambda *_: (0,) * len(shape), pipeline_mode=pl.Buffered(1))


def _prompt_mixer(x, p, alpha, ts=PROMPT_BLOCK):
    B, S, D = x.shape
    ts = min(ts, S)
    assert S % ts == 0 and ts % WINDOW == 0 and ts % CHUNK == 0
    grid = (B, S // ts)
    smem = pl.BlockSpec(memory_space=pltpu.SMEM)
    in_specs = [
        smem,
        pl.BlockSpec((None, ts, D), lambda b, s: (b, s, 0)),
        _full((D, D_IN)), _full((1, D_SGU)), _full((1, D_SGU)),
        _full((N_SGU_GROUPS, CHUNK, CHUNK)), _full((CHUNK, N_SGU_GROUPS)),
        _full((D_SGU, D)), _full((D_ATTN, D)), _full((D, D)), _full((1, D)), _full((1, D)),
    ]
    out_specs = [
        pl.BlockSpec((None, ts, D), lambda b, s: (b, s, 0)),
        pl.BlockSpec((None, WINDOW, D_KV), lambda b, s: (b, 0, 0)),
        pl.BlockSpec((None, WINDOW, D_KV), lambda b, s: (b, 0, 0)),
        pl.BlockSpec((None, CHUNK, D_SGU), lambda b, s: (b, 0, 0)),
    ]
    out_shape = [
        jax.ShapeDtypeStruct((B, S, D), F32),
        jax.ShapeDtypeStruct((B, WINDOW, D_KV), F32),
        jax.ShapeDtypeStruct((B, WINDOW, D_KV), F32),
        jax.ShapeDtypeStruct((B, CHUNK, D_SGU), F32),
    ]
    scratch = [
        pltpu.VMEM((ts + WINDOW, D_KV), BF16),
        pltpu.VMEM((ts + WINDOW, D_KV), BF16),
        pltpu.VMEM((ts, D_SGU), BF16),
        pltpu.VMEM((ts, D_ATTN), BF16),
    ]
    return pl.pallas_call(
        functools.partial(_prompt_mixer_kernel, alpha=alpha, ts=ts),
        grid=grid, in_specs=in_specs, out_specs=out_specs, out_shape=out_shape,
        scratch_shapes=scratch, name="prompt_mixer",
        compiler_params=pltpu.CompilerParams(
            dimension_semantics=("arbitrary", "arbitrary"), vmem_limit_bytes=VMEM_LIMIT_BYTES),
    )(p["sinks"], x, p["w_in"], p["sgu_ln_g"], p["sgu_ln_b"], p["sgu_w"], p["sgu_bcol"],
      p["w_bs"], p["w_ba"], p["w_out"], p["ln1_g"], p["ln1_b"])


def _sample_pre_kernel(sgu_w_ref, sgu_b_ref, x_ref, w_in_ref, sgu_g_ref, sgu_bb_ref, w_bs_ref,
                       q_ref, k_ref, v_ref, sv_ref, part_ref, gate_b_ref, *, t_new):
    x = x_ref[...]
    n = x.shape[0]
    z = _dot(x.astype(BF16), w_in_ref[...])
    q_ref[...] = z[:, _OFF_Q:_OFF_K]
    k_ref[...] = z[:, _OFF_K:_OFF_V]
    v_ref[...] = z[:, _OFF_V:_OFF_U]
    u = _gelu(z[:, _OFF_U:_OFF_SV])
    svn = _layer_norm(_gelu(z[:, _OFF_SV:_OFF_GA]), sgu_g_ref[...], sgu_bb_ref[...])
    sv_ref[...] = svn
    t_of_row = lax.broadcasted_iota(jnp.int32, (n, SGU_GROUP_DIM), 0) % t_new
    pieces = []
    for g in range(N_SGU_GROUPS):
        vg = svn[:, g * SGU_GROUP_DIM:(g + 1) * SGU_GROUP_DIM]
        mix = jnp.zeros((n, SGU_GROUP_DIM), F32)
        for t in range(t_new):
            mix = jnp.where(t_of_row == t, sgu_b_ref[g, t], mix)
        for d in range(t_new):
            coef = jnp.zeros((n, SGU_GROUP_DIM), F32)
            for t in range(d, t_new):
                coef = jnp.where(t_of_row == t, sgu_w_ref[g, t * t_new + t - d], coef)
            shifted = vg if d == 0 else pltpu.roll(vg, d, 0)
            mix = mix + coef * shifted
        pieces.append(u[:, g * SGU_GROUP_DIM:(g + 1) * SGU_GROUP_DIM] * mix)
    sgu_o = jnp.concatenate(pieces, axis=1).astype(BF16)
    part_ref[...] = _sigmoid(z[:, _OFF_GA:_OFF_GB]) * _dot(sgu_o, w_bs_ref[...])
    gate_b_ref[...] = _sigmoid(z[:, _OFF_GB:])


def _sample_pre(x2d, p, t_new):
    n, D = x2d.shape
    smem = pl.BlockSpec(memory_space=pltpu.SMEM)
    out_shape = [
        jax.ShapeDtypeStruct((n, D_ATTN), F32),
        jax.ShapeDtypeStruct((n, D_KV), F32),
        jax.ShapeDtypeStruct((n, D_KV), F32),
        jax.ShapeDtypeStruct((n, D_SGU), F32),
        jax.ShapeDtypeStruct((n, D), F32),
        jax.ShapeDtypeStruct((n, D), F32),
    ]
    return pl.pallas_call(
        functools.partial(_sample_pre_kernel, t_new=t_new),
        in_specs=[smem, smem] + [pl.BlockSpec(memory_space=pltpu.VMEM)] * 5,
        out_specs=[pl.BlockSpec(memory_space=pltpu.VMEM)] * 6,
        out_shape=out_shape, name="sample_pre",
        compiler_params=pltpu.CompilerParams(vmem_limit_bytes=VMEM_LIMIT_BYTES),
    )(p["sgu_w"][:, :t_new, :t_new].reshape(N_SGU_GROUPS, t_new * t_new), p["sgu_b"],
      x2d, p["w_in"], p["sgu_ln_g"], p["sgu_ln_b"], p["w_bs"])


def _sample_attn_kernel(sinks_ref, q_ref, kn_ref, vn_ref, ck_ref, cv_ref,
                        o_ref, nk_ref, nv_ref, kbuf_ref, vbuf_ref, *, t_new, bb):
    nq = Q_PER_KV * t_new
    nkeys = WINDOW + 8
    row = lax.broadcasted_iota(jnp.int32, (nq, nkeys), 0)
    kj = lax.broadcasted_iota(jnp.int32, (nq, nkeys), 1)
    t_q = row % t_new
    bias = jnp.where((kj > t_q) & (kj <= t_q + WINDOW), 0.0, NEG_INF).astype(F32)
    r_of_row = lax.broadcasted_iota(jnp.int32, (nq, 1), 0) // t_new
    scale = HEAD_DIM ** -0.5
    pad = jnp.zeros((8 - t_new, D_KV), F32)
    for b in range(bb):
        kbuf_ref[0:WINDOW, :] = ck_ref[b]
        kbuf_ref[WINDOW:WINDOW + t_new, :] = kn_ref[b]
        kbuf_ref[WINDOW + t_new:, :] = pad
        vbuf_ref[0:WINDOW, :] = cv_ref[b]
        vbuf_ref[WINDOW:WINDOW + t_new, :] = vn_ref[b]
        vbuf_ref[WINDOW + t_new:, :] = pad
        nk_ref[b] = kbuf_ref[t_new:t_new + WINDOW, :]
        nv_ref[b] = vbuf_ref[t_new:t_new + WINDOW, :]
        for g in range(N_KV_HEADS):
            ds = slice(g * HEAD_DIM, (g + 1) * HEAD_DIM)
            kg = kbuf_ref[:, ds].astype(BF16)
            vg = vbuf_ref[:, ds].astype(BF16)
            logits = _dot_nt(q_ref[b, g].astype(BF16), kg) * scale + bias
            sink_col = jnp.zeros((nq, 1), F32)
            for r in range(Q_PER_KV):
                sink_col = jnp.where(r_of_row == r, sinks_ref[g * Q_PER_KV + r], sink_col)
            o_ref[b, g] = _sink_softmax_pv(logits, sink_col, vg)


def _sample_attn(q_grouped, k_new, v_new, cache_k, cache_v, sinks, t_new, bb=SAMPLE_ATTN_BATCH):
    nb = q_grouped.shape[0]
    bb = min(bb, nb)
    assert nb % bb == 0 and t_new <= 8
    nq = Q_PER_KV * t_new
    smem = pl.BlockSpec(memory_space=pltpu.SMEM)
    in_specs = [
        smem,
        pl.BlockSpec((bb, N_KV_HEADS, nq, HEAD_DIM), lambda i: (i, 0, 0, 0)),
        pl.BlockSpec((bb, t_new, D_KV), lambda i: (i, 0, 0)),
        pl.BlockSpec((bb, t_new, D_KV), lambda i: (i, 0, 0)),
        pl.BlockSpec((bb, WINDOW, D_KV), lambda i: (i, 0, 0)),
        pl.BlockSpec((bb, WINDOW, D_KV), lambda i: (i, 0, 0)),
    ]
    out_specs = [
        pl.BlockSpec((bb, N_KV_HEADS, nq, HEAD_DIM), lambda i: (i, 0, 0, 0)),
        pl.BlockSpec((bb, WINDOW, D_KV), lambda i: (i, 0, 0)),
        pl.BlockSpec((bb, WINDOW, D_KV), lambda i: (i, 0, 0)),
    ]
    out_shape = [
        jax.ShapeDtypeStruct((nb, N_KV_HEADS, nq, HEAD_DIM), F32),
        jax.ShapeDtypeStruct((nb, WINDOW, D_KV), F32),
        jax.ShapeDtypeStruct((nb, WINDOW, D_KV), F32),
    ]
    return pl.pallas_call(
        functools.partial(_sample_attn_kernel, t_new=t_new, bb=bb),
        grid=(nb // bb,), in_specs=in_specs, out_specs=out_specs, out_shape=out_shape,
        scratch_shapes=[pltpu.VMEM((WINDOW + 8, D_KV), F32), pltpu.VMEM((WINDOW + 8, D_KV), F32)],
        name="sample_attn",
        compiler_params=pltpu.CompilerParams(dimension_semantics=("arbitrary",)),
    )(sinks, q_grouped, k_new, v_new, cache_k, cache_v)


def _sample_post_kernel(x_ref, part_ref, gate_b_ref, o_ref, w_ba_ref, w_out_ref, ln1_g_ref, ln1_b_ref,
                        h_ref, *, alpha):
    merged = part_ref[...] + gate_b_ref[...] * _dot(o_ref[...].astype(BF16), w_ba_ref[...])
    y = _dot(merged.astype(BF16), w_out_ref[...])
    h_ref[...] = _layer_norm(alpha * x_ref[...] + y, ln1_g_ref[...], ln1_b_ref[...])


def _sample_post(x2d, part, gate_b, o2d, p, alpha):
    return pl.pallas_call(
        functools.partial(_sample_post_kernel, alpha=alpha),
        in_specs=[pl.BlockSpec(memory_space=pltpu.VMEM)] * 8,
        out_specs=pl.BlockSpec(memory_space=pltpu.VMEM),
        out_shape=jax.ShapeDtypeStruct(x2d.shape, F32), name="sample_post",
        compiler_params=pltpu.CompilerParams(vmem_limit_bytes=VMEM_LIMIT_BYTES),
    )(x2d, part, gate_b, o2d, p["w_ba"], p["w_out"], p["ln1_g"], p["ln1_b"])


def _top_values(work, count, with_rank=False):
    vals = []
    rank = jnp.full(work.shape, float(count), F32) if with_rank else None
    for i in range(count):
        top = jnp.max(work, axis=0, keepdims=True)
        vals.append(top)
        hit = work == top
        if with_rank:
            rank = jnp.where(hit, float(i), rank)
        if i + 1 < count:
            work = jnp.where(hit, -jnp.inf, work)
    return (vals, rank) if with_rank else vals


def _candidate_rows(a_rows, b_rows):
    assert PEER_TOPK == 16
    a_all = jnp.concatenate(a_rows, axis=0)
    a_low = a_all[0:8]
    row = lax.broadcasted_iota(jnp.int32, a_low.shape, 0)

    def pair(j):
        return a_low + b_rows[j]

    def corner(j):
        return a_rows[0] + b_rows[j]

    t3 = jnp.where(row >= 5, pltpu.roll(pair(4), 5, 0), pair(2))
    t4 = jnp.where(row >= 6, pltpu.roll(pair(6), 6, 0),
                   jnp.where(row >= 4, pltpu.roll(pair(5), 4, 0), pair(3)))
    t5 = pair(7)
    for k in range(6):
        t5 = jnp.where(row >= 2 + k, corner(8 + k), t5)
    t6 = jnp.where(row >= 2, -jnp.inf, jnp.where(row >= 1, corner(15), corner(14)))
    return jnp.concatenate([a_all + b_rows[0], pair(1), t3, t4, t5, t6], axis=0)


def _peer_select(s1, s2):
    a_rows = _top_values(s1, PEER_TOPK)
    b_rows, rank2 = _top_values(s2, PEER_TOPK, with_rank=True)
    best = _top_values(_candidate_rows(a_rows, b_rows), PEER_TOPK)
    norm = jnp.zeros_like(best[0])
    for i in range(PEER_TOPK):
        norm = norm + jnp.exp(best[i] - best[0])
    count = jnp.zeros_like(s1)
    for jj in range(PEER_TOPK):
        count = jnp.where(s1 + b_rows[jj] >= best[PEER_TOPK - 1], float(jj + 1), count)
    e1 = jnp.exp(s1 - a_rows[0]) / norm
    e2 = jnp.exp(s2 - b_rows[0])
    return rank2, e2, count, e1


def _peer_gate_tile(act_ref, coef_ref, first_n1, ns, t, sel_refs):
    rank2_ref, e2_ref, count_ref, e1_ref = sel_refs
    ls = slice(t * LANES, (t + 1) * LANES)
    gates = [None for _ in ns]
    for hd in range(PEER_HEADS):
        rank2 = rank2_ref[hd, t]
        e2 = e2_ref[hd, t]
        for i, n in enumerate(ns):
            count = count_ref[hd, t, pl.ds(first_n1 + n, 1), :]
            e1 = e1_ref[hd, t, pl.ds(first_n1 + n, 1), :]
            term = jnp.where(rank2 < count, e2, 0.0) * e1
            gates[i] = term if gates[i] is None else gates[i] + term
    for i, n in enumerate(ns):
        rs = slice(n * N_SUBKEYS, (n + 1) * N_SUBKEYS)
        x = act_ref[rs, ls]
        inner = x * (GELU_C0 + (GELU_C0 * GELU_C1) * (x * x))
        coef_ref[rs, ls] = ((gates[i] * x) * (1.0 + jnp.tanh(inner))).astype(BF16)


def _peer_kernel(h_ref, wq_ref, sk_ref, u_ref, vt_ref, ln_g_ref, ln_b_ref, y_ref,
                 ht_ref, qt_ref, rank2_ref, e2_ref, count_ref, e1_ref, acc_ref,
                 act0_ref, act1_ref, coef0_ref, coef1_ref, *, alpha, tb, se):
    j = pl.program_id(1)
    ntb = tb // LANES
    nblocks = N_EXPERTS // se
    sel_refs = (rank2_ref, e2_ref, count_ref, e1_ref)
    act_refs = (act0_ref, act1_ref)
    coef_refs = (coef0_ref, coef1_ref)

    @pl.when(j == 0)
    def _():
        ht = h_ref[...].T.astype(BF16)
        ht_ref[...] = ht
        qt_ref[...] = _dot(wq_ref[...], ht).astype(BF16)

        def per_head(hd, carry):
            for t in range(ntb):
                ls = slice(t * LANES, (t + 1) * LANES)
                s = []
                for c in range(2):
                    r0 = pl.multiple_of((hd * 2 + c) * PEER_HALF, PEER_HALF)
                    s.append(_dot(sk_ref[hd * 2 + c], qt_ref[pl.ds(r0, PEER_HALF), ls]))
                rank2, e2, count, e1 = _peer_select(s[0], s[1])
                rank2_ref[hd, t] = rank2
                e2_ref[hd, t] = e2
                count_ref[hd, t] = count
                e1_ref[hd, t] = 0.5 * e1
            return carry

        lax.fori_loop(0, PEER_HEADS, per_head, 0)
        acc_ref[...] = jnp.zeros_like(acc_ref)

    def step_body(q, with_act, with_gate, with_combine):
        piece_t, piece_e, piece_d = tb // PEER_TOKEN_PIECES, se // PEER_ROW_PIECES, acc_ref.shape[0] // PEER_ROW_PIECES
        pieces = []
        for tp in range(PEER_TOKEN_PIECES):
            hs = slice(tp * piece_t, (tp + 1) * piece_t)
            for lo in range(PEER_ROW_PIECES):
                if with_act:
                    pieces.append(("act", hs, slice(lo * piece_e, (lo + 1) * piece_e)))
                if with_combine:
                    pieces.append(("combine", hs, slice(lo * piece_d, (lo + 1) * piece_d)))
        n_sub = se // N_SUBKEYS
        groups = [tuple(range(g, min(g + PEER_GATE_GROUP, n_sub))) for g in range(0, n_sub, PEER_GATE_GROUP)]
        tiles = [(ns, t) for t in range(ntb) for ns in groups] if with_gate else []
        per_piece = -(-len(tiles) // max(len(pieces), 1))
        for k, (kind, hs, es) in enumerate(pieces):
            for n, t in tiles[k * per_piece:(k + 1) * per_piece]:
                _peer_gate_tile(act_refs[1 - q], coef_refs[1 - q], (j - 1) * (se // N_SUBKEYS), n, t, sel_refs)
            if kind == "act":
                act_refs[q][es, hs] = _dot(u_ref[es, :], ht_ref[:, hs])
            else:
                acc_ref[es, hs] += _dot(vt_ref[es, :], coef_refs[q][:, hs])
        for n, t in tiles[len(pieces) * per_piece:]:
            _peer_gate_tile(act_refs[1 - q], coef_refs[1 - q], (j - 1) * (se // N_SUBKEYS), n, t, sel_refs)

    assert nblocks % 2 == 0 and nblocks >= 4
    pl.when(j == 0)(functools.partial(step_body, 0, True, False, False))
    pl.when(j == 1)(functools.partial(step_body, 1, True, True, False))
    for q in range(2):
        pl.when((j >= 2) & (j < nblocks) & (j % 2 == q))(functools.partial(step_body, q, True, True, True))
    pl.when(j == nblocks)(functools.partial(step_body, 0, False, True, True))
    pl.when(j == nblocks + 1)(functools.partial(step_body, 1, False, False, True))

    @pl.when(j == pl.num_programs(1) - 1)
    def _():
        y_ref[...] = _layer_norm(alpha * h_ref[...] + acc_ref[...].T, ln_g_ref[...], ln_b_ref[...])


def _peer(h2d, p, alpha, tb=PEER_TOKEN_BLOCK):
    n, D = h2d.shape
    tb = min(tb, n)
    se = p["peer_vt"].shape[2]
    assert n % tb == 0 and tb % LANES == 0 and N_EXPERTS % se == 0 and se % N_SUBKEYS == 0
    ntb = tb // LANES
    qdim = PEER_HEADS * PEER_KEY_DIM
    nblocks = N_EXPERTS // se
    in_specs = [
        pl.BlockSpec((tb, D), lambda i, j: (i, 0)),
        _full((qdim, D)),
        _full((PEER_HEADS * 2, N_SUBKEYS, PEER_HALF)),
        pl.BlockSpec((se, D), lambda i, j: (jnp.minimum(j, nblocks - 1), 0)),
        pl.BlockSpec((None, D, se), lambda i, j: (jnp.clip(j - 2, 0, nblocks - 1), 0, 0)),
        _full((1, D)), _full((1, D)),
    ]
    per_head = (PEER_HEADS, ntb, N_SUBKEYS, LANES)
    scratch = [
        pltpu.VMEM((D, tb), BF16),
        pltpu.VMEM((qdim, tb), BF16),
        pltpu.VMEM(per_head, F32), pltpu.VMEM(per_head, F32),
        pltpu.VMEM(per_head, F32), pltpu.VMEM(per_head, F32),
        pltpu.VMEM((D, tb), F32),
    ] + [pltpu.VMEM((se, tb), F32)] * 2 + [pltpu.VMEM((se, tb), BF16)] * 2
    return pl.pallas_call(
        functools.partial(_peer_kernel, alpha=alpha, tb=tb, se=se),
        grid=(n // tb, nblocks + 2), in_specs=in_specs,
        out_specs=pl.BlockSpec((tb, D), lambda i, j: (i, 0), pipeline_mode=pl.Buffered(1)),
        out_shape=jax.ShapeDtypeStruct((n, D), F32), scratch_shapes=scratch, name="peer",
        compiler_params=pltpu.CompilerParams(
            dimension_semantics=("arbitrary", "arbitrary"), vmem_limit_bytes=VMEM_LIMIT_BYTES),
    )(h2d, p["peer_wq_t"], p["peer_sk"], p["peer_u"], p["peer_vt"], p["ln2_g"], p["ln2_b"])


def _layer_params(l, w_in, sinks, sgu_ln_g, sgu_ln_b, sgu_w, sgu_b, w_branch_sgu, w_branch_attn, w_out,
                  ln1_g, ln1_b, peer_w_q, peer_sub_keys, peer_u, peer_v, ln2_g, ln2_b):
    return {
        "w_in": w_in[l].astype(BF16),
        "sinks": sinks[l],
        "sgu_ln_g": sgu_ln_g[l][None, :], "sgu_ln_b": sgu_ln_b[l][None, :],
        "sgu_w": sgu_w[l],
        "sgu_b": sgu_b[l], "sgu_bcol": sgu_b[l].T,
        "w_bs": w_branch_sgu[l].astype(BF16), "w_ba": w_branch_attn[l].astype(BF16),
        "w_out": w_out[l].astype(BF16),
        "ln1_g": ln1_g[l][None, :], "ln1_b": ln1_b[l][None, :],
        "peer_wq_t": peer_w_q[l].T.astype(BF16),
        "peer_sk": peer_sub_keys[l].reshape(PEER_HEADS * 2, N_SUBKEYS, PEER_HALF).astype(BF16),
        "peer_u": peer_u[l].astype(BF16),
        "peer_vt": peer_v[l].reshape(N_EXPERTS // PEER_EXPERT_BLOCK, PEER_EXPERT_BLOCK, D_MODEL)
                            .transpose(0, 2, 1).astype(BF16),
        "ln2_g": ln2_g[l][None, :], "ln2_b": ln2_b[l][None, :],
    }


def _prompt_layer(x, p, alpha):
    B, S, D = x.shape
    h, k_last, v_last, sv_last = _prompt_mixer(x, p, alpha)
    y = _peer(h.reshape(B * S, D), p, alpha).reshape(B, S, D)
    return (y, k_last.reshape(B, WINDOW, N_KV_HEADS, HEAD_DIM), v_last.reshape(B, WINDOW, N_KV_HEADS, HEAD_DIM),
            sv_last.reshape(B, CHUNK, N_SGU_GROUPS, SGU_GROUP_DIM))


def _sample_layer(x, cache_k, cache_v, p, alpha):
    B, T, D = x.shape
    x2d = x.reshape(B * T, D)
    q, k_new, v_new, svn, part, gate_b = _sample_pre(x2d, p, T)
    qg = q.reshape(B, T, N_KV_HEADS, Q_PER_KV, HEAD_DIM).transpose(0, 2, 3, 1, 4)
    qg = qg.reshape(B, N_KV_HEADS, Q_PER_KV * T, HEAD_DIM)
    og, new_k, new_v = _sample_attn(qg, k_new.reshape(B, T, D_KV), v_new.reshape(B, T, D_KV),
                                    cache_k.reshape(B, WINDOW, D_KV), cache_v.reshape(B, WINDOW, D_KV),
                                    p["sinks"], T)
    o2d = og.reshape(B, N_KV_HEADS, Q_PER_KV, T, HEAD_DIM).transpose(0, 3, 1, 2, 4).reshape(B * T, D_ATTN)
    h = _sample_post(x2d, part, gate_b, o2d, p, alpha)
    y = _peer(h, p, alpha).reshape(B, T, D)
    return (y, new_k.reshape(B, WINDOW, N_KV_HEADS, HEAD_DIM), new_v.reshape(B, WINDOW, N_KV_HEADS, HEAD_DIM),
            svn.reshape(B, T, N_SGU_GROUPS, SGU_GROUP_DIM))


def kernel(x_prompt, x_sample, cache_k, cache_v, w_in, sinks, sgu_ln_g, sgu_ln_b, sgu_w, sgu_b, w_branch_sgu, w_branch_attn, w_out, ln1_g, ln1_b, peer_w_q, peer_sub_keys, peer_u, peer_v, ln2_g, ln2_b):
    depth = w_in.shape[0]
    alpha = (2.0 * depth) ** 0.25
    yp, ys = x_prompt, x_sample
    outs = [[] for _ in range(6)]
    for l in range(depth):
        p = _layer_params(l, w_in, sinks, sgu_ln_g, sgu_ln_b, sgu_w, sgu_b, w_branch_sgu, w_branch_attn,
                          w_out, ln1_g, ln1_b, peer_w_q, peer_sub_keys, peer_u, peer_v, ln2_g, ln2_b)
        yp, kp, vp, svp = _prompt_layer(yp, p, alpha)
        ys, ksm, vsm, svs = _sample_layer(ys, cache_k[l], cache_v[l], p, alpha)
        for acc, val in zip(outs, (kp, vp, ksm, vsm, svp, svs)):
            acc.append(val)
    return (yp, ys) + tuple(jnp.stack(o) for o in outs)
```

```python
import functools

import jax
import jax.numpy as jnp
from jax import lax
from jax.experimental import pallas as pl
from jax.experimental.pallas import tpu as pltpu

F32 = jnp.float32
BF16 = jnp.bfloat16

D_MODEL = 1024
N_Q_HEADS = 8
N_KV_HEADS = 2
HEAD_DIM = 64
Q_PER_KV = N_Q_HEADS // N_KV_HEADS
D_ATTN = N_Q_HEADS * HEAD_DIM
D_KV = N_KV_HEADS * HEAD_DIM
WINDOW = 128
CHUNK = 128
D_SGU = D_MODEL // 2
N_SGU_GROUPS = 4
SGU_GROUP_DIM = D_SGU // N_SGU_GROUPS
PEER_HEADS = 8
N_SUBKEYS = 128
N_EXPERTS = N_SUBKEYS * N_SUBKEYS
PEER_TOPK = 16
PEER_KEY_DIM = 256
PEER_HALF = PEER_KEY_DIM // 2
LN_EPS = 1e-5
NEG_INF = -1e30
D_IN = D_ATTN + 2 * D_KV + 2 * D_SGU + 2 * D_MODEL
_OFF_Q = 0
_OFF_K = D_ATTN
_OFF_V = _OFF_K + D_KV
_OFF_U = _OFF_V + D_KV
_OFF_SV = _OFF_U + D_SGU
_OFF_GA = _OFF_SV + D_SGU
_OFF_GB = _OFF_GA + D_MODEL

LANES = 128
VMEM_LIMIT_BYTES = 56 * 1024 * 1024

PROMPT_BLOCK = 512
SAMPLE_ATTN_BATCH = 8
PEER_TOKEN_BLOCK = 512
PEER_EXPERT_BLOCK = 512
PEER_GATE_GROUP = 1
PEER_TOKEN_PIECES = 2
PEER_ROW_PIECES = 1


GELU_C0 = 0.7978845608028654
GELU_C1 = 0.044715


def _gelu(x):
    return 0.5 * x * (1.0 + jnp.tanh(GELU_C0 * (x + GELU_C1 * (x * x * x))))


def _sigmoid(x):
    return 1.0 / (1.0 + jnp.exp(-x))


def _layer_norm(x, g, b):
    mu = jnp.mean(x, -1, keepdims=True)
    xc = x - mu
    var = jnp.mean(xc * xc, -1, keepdims=True)
    return xc * lax.rsqrt(var + LN_EPS) * g + b


def _dot(a, b):
    return jnp.dot(a, b, preferred_element_type=F32)


def _dot_nt(a, b):
    return lax.dot_general(a, b, (((1,), (1,)), ((), ())), preferred_element_type=F32)


def _sink_softmax_pv(logits, sink_col, vals):
    m = jnp.maximum(jnp.max(logits, -1, keepdims=True), sink_col)
    p = jnp.exp(logits - m)
    denom = jnp.sum(p, -1, keepdims=True) + jnp.exp(sink_col - m)
    return _dot(p.astype(BF16), vals) / denom


def _prompt_mixer_kernel(sinks_ref, x_ref, w_in_ref, sgu_g_ref, sgu_b_ref, sgu_w_ref, sgu_bcol_ref,
                         w_bs_ref, w_ba_ref, w_out_ref, ln1_g_ref, ln1_b_ref,
                         h_ref, k_ref, v_ref, sv_ref,
                         kext_ref, vext_ref, sgu_o_ref, attn_o_ref, *, alpha, ts):
    s = pl.program_id(1)
    last = pl.num_programs(1) - 1
    nblk = ts // WINDOW

    x = x_ref[...]
    z = _dot(x.astype(BF16), w_in_ref[...])
    k = z[:, _OFF_K:_OFF_V]
    v = z[:, _OFF_V:_OFF_U]
    u = _gelu(z[:, _OFF_U:_OFF_SV])
    svn = _layer_norm(_gelu(z[:, _OFF_SV:_OFF_GA]), sgu_g_ref[...], sgu_b_ref[...])

    @pl.when(s == 0)
    def _():
        kext_ref[0:WINDOW, :] = jnp.zeros((WINDOW, D_KV), BF16)
        vext_ref[0:WINDOW, :] = jnp.zeros((WINDOW, D_KV), BF16)

    kext_ref[WINDOW:, :] = k.astype(BF16)
    vext_ref[WINDOW:, :] = v.astype(BF16)

    @pl.when(s == last)
    def _():
        k_ref[...] = k[ts - WINDOW:, :]
        v_ref[...] = v[ts - WINDOW:, :]
        sv_ref[...] = svn[ts - CHUNK:, :]

    row = lax.broadcasted_iota(jnp.int32, (CHUNK, CHUNK), 0)
    col = lax.broadcasted_iota(jnp.int32, (CHUNK, CHUNK), 1)
    for g in range(N_SGU_GROUPS):
        wt = jnp.where(col <= row, sgu_w_ref[g], 0.0).astype(BF16)
        bcol = sgu_bcol_ref[:, g:g + 1]
        gs = slice(g * SGU_GROUP_DIM, (g + 1) * SGU_GROUP_DIM)
        for c in range(ts // CHUNK):
            rs = slice(c * CHUNK, (c + 1) * CHUNK)
            mix = _dot(wt, svn[rs, gs].astype(BF16)) + bcol
            sgu_o_ref[rs, gs] = (u[rs, gs] * mix).astype(BF16)

    nq = Q_PER_KV * WINDOW
    kj = lax.broadcasted_iota(jnp.int32, (2 * WINDOW, nq), 0)
    qi = lax.broadcasted_iota(jnp.int32, (2 * WINDOW, nq), 1) % WINDOW
    rel = qi + WINDOW - kj
    bias = jnp.where((rel >= 0) & (rel < WINDOW), 0.0, NEG_INF).astype(F32)
    bias_first = jnp.where(s > 0, bias, jnp.where(kj < WINDOW, NEG_INF, bias))
    scale = HEAD_DIM ** -0.5
    for i in range(nblk):
        qs = slice(i * WINDOW, (i + 1) * WINDOW)
        outs = []
        for g in range(N_KV_HEADS):
            ds = slice(g * HEAD_DIM, (g + 1) * HEAD_DIM)
            kb = kext_ref[i * WINDOW:(i + 2) * WINDOW, ds]
            vb = vext_ref[i * WINDOW:(i + 2) * WINDOW, ds]
            q4 = jnp.concatenate(
                [z[qs, (g * Q_PER_KV + r) * HEAD_DIM:(g * Q_PER_KV + r + 1) * HEAD_DIM]
                 for r in range(Q_PER_KV)], axis=0).astype(BF16)
            logits = _dot_nt(kb, q4) * scale + (bias_first if i == 0 else bias)
            sink = jnp.concatenate(
                [jnp.full((1, WINDOW), sinks_ref[g * Q_PER_KV + r], F32) for r in range(Q_PER_KV)], axis=1)
            m = jnp.maximum(jnp.max(logits, 0, keepdims=True), sink)
            p = jnp.exp(logits - m)
            denom = jnp.sum(p, 0, keepdims=True) + jnp.exp(sink - m)
            pv = lax.dot_general(vb, p.astype(BF16), (((0,), (0,)), ((), ())), preferred_element_type=F32)
            outs.append(pv / denom)
        o_all = jnp.concatenate(outs, axis=0).T
        heads = [o_all[r * WINDOW:(r + 1) * WINDOW, g * HEAD_DIM:(g + 1) * HEAD_DIM]
                 for g in range(N_KV_HEADS) for r in range(Q_PER_KV)]
        attn_o_ref[qs, :] = jnp.concatenate(heads, axis=1).astype(BF16)

    kext_ref[0:WINDOW, :] = kext_ref[ts:ts + WINDOW, :]
    vext_ref[0:WINDOW, :] = vext_ref[ts:ts + WINDOW, :]

    merged = (_sigmoid(z[:, _OFF_GA:_OFF_GB]) * _dot(sgu_o_ref[...], w_bs_ref[...])
              + _sigmoid(z[:, _OFF_GB:]) * _dot(attn_o_ref[...], w_ba_ref[...]))
    y = _dot(merged.astype(BF16), w_out_ref[...])
    h_ref[...] = _layer_norm(alpha * x + y, ln1_g_ref[...], ln1_b_ref[...])


def _full(shape):
    return pl.BlockSpec(shape, lambda *_: (0,) * len(shape), pipeline_mode=pl.Buffered(1))


def _prompt_mixer(x, p, alpha, ts=PROMPT_BLOCK):
    B, S, D = x.shape
    ts = min(ts, S)
    assert S % ts == 0 and ts % WINDOW == 0 and ts % CHUNK == 0
    grid = (B, S // ts)
    smem = pl.BlockSpec(memory_space=pltpu.SMEM)
    in_specs = [
        smem,
        pl.BlockSpec((None, ts, D), lambda b, s: (b, s, 0)),
        _full((D, D_IN)), _full((1, D_SGU)), _full((1, D_SGU)),
        _full((N_SGU_GROUPS, CHUNK, CHUNK)), _full((CHUNK, N_SGU_GROUPS)),
        _full((D_SGU, D)), _full((D_ATTN, D)), _full((D, D)), _full((1, D)), _full((1, D)),
    ]
    out_specs = [
        pl.BlockSpec((None, ts, D), lambda b, s: (b, s, 0)),
        pl.BlockSpec((None, WINDOW, D_KV), lambda b, s: (b, 0, 0)),
        pl.BlockSpec((None, WINDOW, D_KV), lambda b, s: (b, 0, 0)),
        pl.BlockSpec((None, CHUNK, D_SGU), lambda b, s: (b, 0, 0)),
    ]
    out_shape = [
        jax.ShapeDtypeStruct((B, S, D), F32),
        jax.ShapeDtypeStruct((B, WINDOW, D_KV), F32),
        jax.ShapeDtypeStruct((B, WINDOW, D_KV), F32),
        jax.ShapeDtypeStruct((B, CHUNK, D_SGU), F32),
    ]
    scratch = [
        pltpu.VMEM((ts + WINDOW, D_KV), BF16),
        pltpu.VMEM((ts + WINDOW, D_KV), BF16),
        pltpu.VMEM((ts, D_SGU), BF16),
        pltpu.VMEM((ts, D_ATTN), BF16),
    ]
    return pl.pallas_call(
        functools.partial(_prompt_mixer_kernel, alpha=alpha, ts=ts),
        grid=grid, in_specs=in_specs, out_specs=out_specs, out_shape=out_shape,
        scratch_shapes=scratch, name="prompt_mixer",
        compiler_params=pltpu.CompilerParams(
            dimension_semantics=("arbitrary", "arbitrary"), vmem_limit_bytes=VMEM_LIMIT_BYTES),
    )(p["sinks"], x, p["w_in"], p["sgu_ln_g"], p["sgu_ln_b"], p["sgu_w"], p["sgu_bcol"],
      p["w_bs"], p["w_ba"], p["w_out"], p["ln1_g"], p["ln1_b"])


def _sample_pre_kernel(sgu_w_ref, sgu_b_ref, x_ref, w_in_ref, sgu_g_ref, sgu_bb_ref, w_bs_ref,
                       q_ref, k_ref, v_ref, sv_ref, part_ref, gate_b_ref, *, t_new):
    x = x_ref[...]
    n = x.shape[0]
    z = _dot(x.astype(BF16), w_in_ref[...])
    q_ref[...] = z[:, _OFF_Q:_OFF_K]
    k_ref[...] = z[:, _OFF_K:_OFF_V]
    v_ref[...] = z[:, _OFF_V:_OFF_U]
    u = _gelu(z[:, _OFF_U:_OFF_SV])
    svn = _layer_norm(_gelu(z[:, _OFF_SV:_OFF_GA]), sgu_g_ref[...], sgu_bb_ref[...])
    sv_ref[...] = svn
    t_of_row = lax.broadcasted_iota(jnp.int32, (n, SGU_GROUP_DIM), 0) % t_new
    pieces = []
    for g in range(N_SGU_GROUPS):
        vg = svn[:, g * SGU_GROUP_DIM:(g + 1) * SGU_GROUP_DIM]
        mix = jnp.zeros((n, SGU_GROUP_DIM), F32)
        for t in range(t_new):
            mix = jnp.where(t_of_row == t, sgu_b_ref[g, t], mix)
        for d in range(t_new):
            coef = jnp.zeros((n, SGU_GROUP_DIM), F32)
            for t in range(d, t_new):
                coef = jnp.where(t_of_row == t, sgu_w_ref[g, t * t_new + t - d], coef)
            shifted = vg if d == 0 else pltpu.roll(vg, d, 0)
            mix = mix + coef * shifted
        pieces.append(u[:, g * SGU_GROUP_DIM:(g + 1) * SGU_GROUP_DIM] * mix)
    sgu_o = jnp.concatenate(pieces, axis=1).astype(BF16)
    part_ref[...] = _sigmoid(z[:, _OFF_GA:_OFF_GB]) * _dot(sgu_o, w_bs_ref[...])
    gate_b_ref[...] = _sigmoid(z[:, _OFF_GB:])


def _sample_pre(x2d, p, t_new):
    n, D = x2d.shape
    smem = pl.BlockSpec(memory_space=pltpu.SMEM)
    out_shape = [
        jax.ShapeDtypeStruct((n, D_ATTN), F32),
        jax.ShapeDtypeStruct((n, D_KV), F32),
        jax.ShapeDtypeStruct((n, D_KV), F32),
        jax.ShapeDtypeStruct((n, D_SGU), F32),
        jax.ShapeDtypeStruct((n, D), F32),
        jax.ShapeDtypeStruct((n, D), F32),
    ]
    return pl.pallas_call(
        functools.partial(_sample_pre_kernel, t_new=t_new),
        in_specs=[smem, smem] + [pl.BlockSpec(memory_space=pltpu.VMEM)] * 5,
        out_specs=[pl.BlockSpec(memory_space=pltpu.VMEM)] * 6,
        out_shape=out_shape, name="sample_pre",
        compiler_params=pltpu.CompilerParams(vmem_limit_bytes=VMEM_LIMIT_BYTES),
    )(p["sgu_w"][:, :t_new, :t_new].reshape(N_SGU_GROUPS, t_new * t_new), p["sgu_b"],
      x2d, p["w_in"], p["sgu_ln_g"], p["sgu_ln_b"], p["w_bs"])


def _sample_attn_kernel(sinks_ref, q_ref, kn_ref, vn_ref, ck_ref, cv_ref,
                        o_ref, nk_ref, nv_ref, kbuf_ref, vbuf_ref, *, t_new, bb):
    nq = Q_PER_KV * t_new
    nkeys = WINDOW + 8
    row = lax.broadcasted_iota(jnp.int32, (nq, nkeys), 0)
    kj = lax.broadcasted_iota(jnp.int32, (nq, nkeys), 1)
    t_q = row % t_new
    bias = jnp.where((kj > t_q) & (kj <= t_q + WINDOW), 0.0, NEG_INF).astype(F32)
    r_of_row = lax.broadcasted_iota(jnp.int32, (nq, 1), 0) // t_new
    scale = HEAD_DIM ** -0.5
    pad = jnp.zeros((8 - t_new, D_KV), F32)
    for b in range(bb):
        kbuf_ref[0:WINDOW, :] = ck_ref[b]
        kbuf_ref[WINDOW:WINDOW + t_new, :] = kn_ref[b]
        kbuf_ref[WINDOW + t_new:, :] = pad
        vbuf_ref[0:WINDOW, :] = cv_ref[b]
        vbuf_ref[WINDOW:WINDOW + t_new, :] = vn_ref[b]
        vbuf_ref[WINDOW + t_new:, :] = pad
        nk_ref[b] = kbuf_ref[t_new:t_new + WINDOW, :]
        nv_ref[b] = vbuf_ref[t_new:t_new + WINDOW, :]
        for g in range(N_KV_HEADS):
            ds = slice(g * HEAD_DIM, (g + 1) * HEAD_DIM)
            kg = kbuf_ref[:, ds].astype(BF16)
            vg = vbuf_ref[:, ds].astype(BF16)
            logits = _dot_nt(q_ref[b, g].astype(BF16), kg) * scale + bias
            sink_col = jnp.zeros((nq, 1), F32)
            for r in range(Q_PER_KV):
                sink_col = jnp.where(r_of_row == r, sinks_ref[g * Q_PER_KV + r], sink_col)
            o_ref[b, g] = _sink_softmax_pv(logits, sink_col, vg)


def _sample_attn(q_grouped, k_new, v_new, cache_k, cache_v, sinks, t_new, bb=SAMPLE_ATTN_BATCH):
    nb = q_grouped.shape[0]
    bb = min(bb, nb)
    assert nb % bb == 0 and t_new <= 8
    nq = Q_PER_KV * t_new
    smem = pl.BlockSpec(memory_space=pltpu.SMEM)
    in_specs = [
        smem,
        pl.BlockSpec((bb, N_KV_HEADS, nq, HEAD_DIM), lambda i: (i, 0, 0, 0)),
        pl.BlockSpec((bb, t_new, D_KV), lambda i: (i, 0, 0)),
        pl.BlockSpec((bb, t_new, D_KV), lambda i: (i, 0, 0)),
        pl.BlockSpec((bb, WINDOW, D_KV), lambda i: (i, 0, 0)),
        pl.BlockSpec((bb, WINDOW, D_KV), lambda i: (i, 0, 0)),
    ]
    out_specs = [
        pl.BlockSpec((bb, N_KV_HEADS, nq, HEAD_DIM), lambda i: (i, 0, 0, 0)),
        pl.BlockSpec((bb, WINDOW, D_KV), lambda i: (i, 0, 0)),
        pl.BlockSpec((bb, WINDOW, D_KV), lambda i: (i, 0, 0)),
    ]
    out_shape = [
        jax.ShapeDtypeStruct((nb, N_KV_HEADS, nq, HEAD_DIM), F32),
        jax.ShapeDtypeStruct((nb, WINDOW, D_KV), F32),
        jax.ShapeDtypeStruct((nb, WINDOW, D_KV), F32),
    ]
    return pl.pallas_call(
        functools.partial(_sample_attn_kernel, t_new=t_new, bb=bb),
        grid=(nb // bb,), in_specs=in_specs, out_specs=out_specs, out_shape=out_shape,
        scratch_shapes=[pltpu.VMEM((WINDOW + 8, D_KV), F32), pltpu.VMEM((WINDOW + 8, D_KV), F32)],
        name="sample_attn",
        compiler_params=pltpu.CompilerParams(dimension_semantics=("arbitrary",)),
    )(sinks, q_grouped, k_new, v_new, cache_k, cache_v)


def _sample_post_kernel(x_ref, part_ref, gate_b_ref, o_ref, w_ba_ref, w_out_ref, ln1_g_ref, ln1_b_ref,
                        h_ref, *, alpha):
    merged = part_ref[...] + gate_b_ref[...] * _dot(o_ref[...].astype(BF16), w_ba_ref[...])
    y = _dot(merged.astype(BF16), w_out_ref[...])
    h_ref[...] = _layer_norm(alpha * x_ref[...] + y, ln1_g_ref[...], ln1_b_ref[...])


def _sample_post(x2d, part, gate_b, o2d, p, alpha):
    return pl.pallas_call(
        functools.partial(_sample_post_kernel, alpha=alpha),
        in_specs=[pl.BlockSpec(memory_space=pltpu.VMEM)] * 8,
        out_specs=pl.BlockSpec(memory_space=pltpu.VMEM),
        out_shape=jax.ShapeDtypeStruct(x2d.shape, F32), name="sample_post",
        compiler_params=pltpu.CompilerParams(vmem_limit_bytes=VMEM_LIMIT_BYTES),
    )(x2d, part, gate_b, o2d, p["w_ba"], p["w_out"], p["ln1_g"], p["ln1_b"])


def _top_values(work, count, with_rank=False):
    vals = []
    rank = jnp.full(work.shape, float(count), F32) if with_rank else None
    for i in range(count):
        top = jnp.max(work, axis=0, keepdims=True)
        vals.append(top)
        hit = work == top
        if with_rank:
            rank = jnp.where(hit, float(i), rank)
        if i + 1 < count:
            work = jnp.where(hit, -jnp.inf, work)
    return (vals, rank) if with_rank else vals


def _candidate_rows(a_rows, b_rows):
    assert PEER_TOPK == 16
    a_all = jnp.concatenate(a_rows, axis=0)
    a_low = a_all[0:8]
    row = lax.broadcasted_iota(jnp.int32, a_low.shape, 0)

    def pair(j):
        return a_low + b_rows[j]

    def corner(j):
        return a_rows[0] + b_rows[j]

    t3 = jnp.where(row >= 5, pltpu.roll(pair(4), 5, 0), pair(2))
    t4 = jnp.where(row >= 6, pltpu.roll(pair(6), 6, 0),
                   jnp.where(row >= 4, pltpu.roll(pair(5), 4, 0), pair(3)))
    t5 = pair(7)
    for k in range(6):
        t5 = jnp.where(row >= 2 + k, corner(8 + k), t5)
    t6 = jnp.where(row >= 2, -jnp.inf, jnp.where(row >= 1, corner(15), corner(14)))
    return jnp.concatenate([a_all + b_rows[0], pair(1), t3, t4, t5, t6], axis=0)


def _peer_select(s1, s2):
    a_rows = _top_values(s1, PEER_TOPK)
    b_rows, rank2 = _top_values(s2, PEER_TOPK, with_rank=True)
    best = _top_values(_candidate_rows(a_rows, b_rows), PEER_TOPK)
    norm = jnp.zeros_like(best[0])
    for i in range(PEER_TOPK):
        norm = norm + jnp.exp(best[i] - best[0])
    count = jnp.zeros_like(s1)
    for jj in range(PEER_TOPK):
        count = jnp.where(s1 + b_rows[jj] >= best[PEER_TOPK - 1], float(jj + 1), count)
    e1 = jnp.exp(s1 - a_rows[0]) / norm
    e2 = jnp.exp(s2 - b_rows[0])
    return rank2, e2, count, e1


def _peer_gate_tile(act_ref, coef_ref, first_n1, ns, t, sel_refs):
    rank2_ref, e2_ref, count_ref, e1_ref = sel_refs
    ls = slice(t * LANES, (t + 1) * LANES)
    gates = [None for _ in ns]
    for hd in range(PEER_HEADS):
        rank2 = rank2_ref[hd, t]
        e2 = e2_ref[hd, t]
        for i, n in enumerate(ns):
            count = count_ref[hd, t, pl.ds(first_n1 + n, 1), :]
            e1 = e1_ref[hd, t, pl.ds(first_n1 + n, 1), :]
            term = jnp.where(rank2 < count, e2, 0.0) * e1
            gates[i] = term if gates[i] is None else gates[i] + term
    for i, n in enumerate(ns):
        rs = slice(n * N_SUBKEYS, (n + 1) * N_SUBKEYS)
        x = act_ref[rs, ls]
        inner = x * (GELU_C0 + (GELU_C0 * GELU_C1) * (x * x))
        coef_ref[rs, ls] = ((gates[i] * x) * (1.0 + jnp.tanh(inner))).astype(BF16)


def _peer_kernel(h_ref, wq_ref, sk_ref, u_ref, vt_ref, ln_g_ref, ln_b_ref, y_ref,
                 ht_ref, qt_ref, rank2_ref, e2_ref, count_ref, e1_ref, acc_ref,
                 act0_ref, act1_ref, coef0_ref, coef1_ref, *, alpha, tb, se):
    j = pl.program_id(1)
    ntb = tb // LANES
    nblocks = N_EXPERTS // se
    sel_refs = (rank2_ref, e2_ref, count_ref, e1_ref)
    act_refs = (act0_ref, act1_ref)
    coef_refs = (coef0_ref, coef1_ref)

    @pl.when(j == 0)
    def _():
        ht = h_ref[...].T.astype(BF16)
        ht_ref[...] = ht
        qt_ref[...] = _dot(wq_ref[...], ht).astype(BF16)

        def per_head(hd, carry):
            for t in range(ntb):
                ls = slice(t * LANES, (t + 1) * LANES)
                s = []
                for c in range(2):
                    r0 = pl.multiple_of((hd * 2 + c) * PEER_HALF, PEER_HALF)
                    s.append(_dot(sk_ref[hd * 2 + c], qt_ref[pl.ds(r0, PEER_HALF), ls]))
                rank2, e2, count, e1 = _peer_select(s[0], s[1])
                rank2_ref[hd, t] = rank2
                e2_ref[hd, t] = e2
                count_ref[hd, t] = count
                e1_ref[hd, t] = 0.5 * e1
            return carry

        lax.fori_loop(0, PEER_HEADS, per_head, 0)
        acc_ref[...] = jnp.zeros_like(acc_ref)

    def step_body(q, with_act, with_gate, with_combine):
        piece_t, piece_e, piece_d = tb // PEER_TOKEN_PIECES, se // PEER_ROW_PIECES, acc_ref.shape[0] // PEER_ROW_PIECES
        pieces = []
        for tp in range(PEER_TOKEN_PIECES):
            hs = slice(tp * piece_t, (tp + 1) * piece_t)
            for lo in range(PEER_ROW_PIECES):
                if with_act:
                    pieces.append(("act", hs, slice(lo * piece_e, (lo + 1) * piece_e)))
                if with_combine:
                    pieces.append(("combine", hs, slice(lo * piece_d, (lo + 1) * piece_d)))
        n_sub = se // N_SUBKEYS
        groups = [tuple(range(g, min(g + PEER_GATE_GROUP, n_sub))) for g in range(0, n_sub, PEER_GATE_GROUP)]
        tiles = [(ns, t) for t in range(ntb) for ns in groups] if with_gate else []
        per_piece = -(-len(tiles) // max(len(pieces), 1))
        for k, (kind, hs, es) in enumerate(pieces):
            for n, t in tiles[k * per_piece:(k + 1) * per_piece]:
                _peer_gate_tile(act_refs[1 - q], coef_refs[1 - q], (j - 1) * (se // N_SUBKEYS), n, t, sel_refs)
            if kind == "act":
                act_refs[q][es, hs] = _dot(u_ref[es, :], ht_ref[:, hs])
            else:
                acc_ref[es, hs] += _dot(vt_ref[es, :], coef_refs[q][:, hs])
        for n, t in tiles[len(pieces) * per_piece:]:
            _peer_gate_tile(act_refs[1 - q], coef_refs[1 - q], (j - 1) * (se // N_SUBKEYS), n, t, sel_refs)

    assert nblocks % 2 == 0 and nblocks >= 4
    pl.when(j == 0)(functools.partial(step_body, 0, True, False, False))
    pl.when(j == 1)(functools.partial(step_body, 1, True, True, False))
    for q in range(2):
        pl.when((j >= 2) & (j < nblocks) & (j % 2 == q))(functools.partial(step_body, q, True, True, True))
    pl.when(j == nblocks)(functools.partial(step_body, 0, False, True, True))
    pl.when(j == nblocks + 1)(functools.partial(step_body, 1, False, False, True))

    @pl.when(j == pl.num_programs(1) - 1)
    def _():
        y_ref[...] = _layer_norm(alpha * h_ref[...] + acc_ref[...].T, ln_g_ref[...], ln_b_ref[...])


def _peer(h2d, p, alpha, tb=PEER_TOKEN_BLOCK):
    n, D = h2d.shape
    tb = min(tb, n)
    se = p["peer_vt"].shape[2]
    assert n % tb == 0 and tb % LANES == 0 and N_EXPERTS % se == 0 and se % N_SUBKEYS == 0
    ntb = tb // LANES
    qdim = PEER_HEADS * PEER_KEY_DIM
    nblocks = N_EXPERTS // se
    in_specs = [
        pl.BlockSpec((tb, D), lambda i, j: (i, 0)),
        _full((qdim, D)),
        _full((PEER_HEADS * 2, N_SUBKEYS, PEER_HALF)),
        pl.BlockSpec((se, D), lambda i, j: (jnp.minimum(j, nblocks - 1), 0)),
        pl.BlockSpec((None, D, se), lambda i, j: (jnp.clip(j - 2, 0, nblocks - 1), 0, 0)),
        _full((1, D)), _full((1, D)),
    ]
    per_head = (PEER_HEADS, ntb, N_SUBKEYS, LANES)
    scratch = [
        pltpu.VMEM((D, tb), BF16),
        pltpu.VMEM((qdim, tb), BF16),
        pltpu.VMEM(per_head, F32), pltpu.VMEM(per_head, F32),
        pltpu.VMEM(per_head, F32), pltpu.VMEM(per_head, F32),
        pltpu.VMEM((D, tb), F32),
    ] + [pltpu.VMEM((se, tb), F32)] * 2 + [pltpu.VMEM((se, tb), BF16)] * 2
    return pl.pallas_call(
        functools.partial(_peer_kernel, alpha=alpha, tb=tb, se=se),
        grid=(n // tb, nblocks + 2), in_specs=in_specs,
        out_specs=pl.BlockSpec((tb, D), lambda i, j: (i, 0), pipeline_mode=pl.Buffered(1)),
        out_shape=jax.ShapeDtypeStruct((n, D), F32), scratch_shapes=scratch, name="peer",
        compiler_params=pltpu.CompilerParams(
            dimension_semantics=("arbitrary", "arbitrary"), vmem_limit_bytes=VMEM_LIMIT_BYTES),
    )(h2d, p["peer_wq_t"], p["peer_sk"], p["peer_u"], p["peer_vt"], p["ln2_g"], p["ln2_b"])


def _layer_params(l, w_in, sinks, sgu_ln_g, sgu_ln_b, sgu_w, sgu_b, w_branch_sgu, w_branch_attn, w_out,
                  ln1_g, ln1_b, peer_w_q, peer_sub_keys, peer_u, peer_v, ln2_g, ln2_b):
    return {
        "w_in": w_in[l].astype(BF16),
        "sinks": sinks[l],
        "sgu_ln_g": sgu_ln_g[l][None, :], "sgu_ln_b": sgu_ln_b[l][None, :],
        "sgu_w": sgu_w[l],
        "sgu_b": sgu_b[l], "sgu_bcol": sgu_b[l].T,
        "w_bs": w_branch_sgu[l].astype(BF16), "w_ba": w_branch_attn[l].astype(BF16),
        "w_out": w_out[l].astype(BF16),
        "ln1_g": ln1_g[l][None, :], "ln1_b": ln1_b[l][None, :],
        "peer_wq_t": peer_w_q[l].T.astype(BF16),
        "peer_sk": peer_sub_keys[l].reshape(PEER_HEADS * 2, N_SUBKEYS, PEER_HALF).astype(BF16),
        "peer_u": peer_u[l].astype(BF16),
        "peer_vt": peer_v[l].reshape(N_EXPERTS // PEER_EXPERT_BLOCK, PEER_EXPERT_BLOCK, D_MODEL)
                            .transpose(0, 2, 1).astype(BF16),
        "ln2_g": ln2_g[l][None, :], "ln2_b": ln2_b[l][None, :],
    }


def _prompt_layer(x, p, alpha):
    B, S, D = x.shape
    h, k_last, v_last, sv_last = _prompt_mixer(x, p, alpha)
    y = _peer(h.reshape(B * S, D), p, alpha).reshape(B, S, D)
    return (y, k_last.reshape(B, WINDOW, N_KV_HEADS, HEAD_DIM), v_last.reshape(B, WINDOW, N_KV_HEADS, HEAD_DIM),
            sv_last.reshape(B, CHUNK, N_SGU_GROUPS, SGU_GROUP_DIM))


def _sample_layer(x, cache_k, cache_v, p, alpha):
    B, T, D = x.shape
    x2d = x.reshape(B * T, D)
    q, k_new, v_new, svn, part, gate_b = _sample_pre(x2d, p, T)
    qg = q.reshape(B, T, N_KV_HEADS, Q_PER_KV, HEAD_DIM).transpose(0, 2, 3, 1, 4)
    qg = qg.reshape(B, N_KV_HEADS, Q_PER_KV * T, HEAD_DIM)
    og, new_k, new_v = _sample_attn(qg, k_new.reshape(B, T, D_KV), v_new.reshape(B, T, D_KV),
                                    cache_k.reshape(B, WINDOW, D_KV), cache_v.reshape(B, WINDOW, D_KV),
                                    p["sinks"], T)
    o2d = og.reshape(B, N_KV_HEADS, Q_PER_KV, T, HEAD_DIM).transpose(0, 3, 1, 2, 4).reshape(B * T, D_ATTN)
    h = _sample_post(x2d, part, gate_b, o2d, p, alpha)
    y = _peer(h, p, alpha).reshape(B, T, D)
    return (y, new_k.reshape(B, WINDOW, N_KV_HEADS, HEAD_DIM), new_v.reshape(B, WINDOW, N_KV_HEADS, HEAD_DIM),
            svn.reshape(B, T, N_SGU_GROUPS, SGU_GROUP_DIM))


def kernel(x_prompt, x_sample, cache_k, cache_v, w_in, sinks, sgu_ln_g, sgu_ln_b, sgu_w, sgu_b, w_branch_sgu, w_branch_attn, w_out, ln1_g, ln1_b, peer_w_q, peer_sub_keys, peer_u, peer_v, ln2_g, ln2_b):
    depth = w_in.shape[0]
    alpha = (2.0 * depth) ** 0.25
    yp, ys = x_prompt, x_sample
    outs = [[] for _ in range(6)]
    for l in range(depth):
        p = _layer_params(l, w_in, sinks, sgu_ln_g, sgu_ln_b, sgu_w, sgu_b, w_branch_sgu, w_branch_attn,
                          w_out, ln1_g, ln1_b, peer_w_q, peer_sub_keys, peer_u, peer_v, ln2_g, ln2_b)
        yp, kp, vp, svp = _prompt_layer(yp, p, alpha)
        ys, ksm, vsm, svs = _sample_layer(ys, cache_k[l], cache_v[l], p, alpha)
        for acc, val in zip(outs, (kp, vp, ksm, vsm, svp, svs)):
            acc.append(val)
    return (yp, ys) + tuple(jnp.stack(o) for o in outs)
```

```python
import functools

import jax
import jax.numpy as jnp
from jax import lax
from jax.experimental import pallas as pl
from jax.experimental.pallas import tpu as pltpu

F32 = jnp.float32
BF16 = jnp.bfloat16

D_MODEL = 1024
N_Q_HEADS = 8
N_KV_HEADS = 2
HEAD_DIM = 64
Q_PER_KV = N_Q_HEADS // N_KV_HEADS
D_ATTN = N_Q_HEADS * HEAD_DIM
D_KV = N_KV_HEADS * HEAD_DIM
WINDOW = 128
CHUNK = 128
D_SGU = D_MODEL // 2
N_SGU_GROUPS = 4
SGU_GROUP_DIM = D_SGU // N_SGU_GROUPS
PEER_HEADS = 8
N_SUBKEYS = 128
N_EXPERTS = N_SUBKEYS * N_SUBKEYS
PEER_TOPK = 16
PEER_KEY_DIM = 256
PEER_HALF = PEER_KEY_DIM // 2
LN_EPS = 1e-5
NEG_INF = -1e30
D_IN = D_ATTN + 2 * D_KV + 2 * D_SGU + 2 * D_MODEL
_OFF_Q = 0
_OFF_K = D_ATTN
_OFF_V = _OFF_K + D_KV
_OFF_U = _OFF_V + D_KV
_OFF_SV = _OFF_U + D_SGU
_OFF_GA = _OFF_SV + D_SGU
_OFF_GB = _OFF_GA + D_MODEL

LANES = 128
VMEM_LIMIT_BYTES = 56 * 1024 * 1024

PROMPT_BLOCK = 512
SAMPLE_ATTN_BATCH = 8
PEER_TOKEN_BLOCK = 512
PEER_EXPERT_BLOCK = 1024
PEER_GATE_GROUP = 1
PEER_TOKEN_PIECES = 2
PEER_ROW_PIECES = 1


GELU_C0 = 0.7978845608028654
GELU_C1 = 0.044715


def _gelu(x):
    return 0.5 * x * (1.0 + jnp.tanh(GELU_C0 * (x + GELU_C1 * (x * x * x))))


def _sigmoid(x):
    return 1.0 / (1.0 + jnp.exp(-x))


def _layer_norm(x, g, b):
    mu = jnp.mean(x, -1, keepdims=True)
    xc = x - mu
    var = jnp.mean(xc * xc, -1, keepdims=True)
    return xc * lax.rsqrt(var + LN_EPS) * g + b


def _dot(a, b):
    return jnp.dot(a, b, preferred_element_type=F32)


def _dot_nt(a, b):
    return lax.dot_general(a, b, (((1,), (1,)), ((), ())), preferred_element_type=F32)


def _sink_softmax_pv(logits, sink_col, vals):
    m = jnp.maximum(jnp.max(logits, -1, keepdims=True), sink_col)
    p = jnp.exp(logits - m)
    denom = jnp.sum(p, -1, keepdims=True) + jnp.exp(sink_col - m)
    return _dot(p.astype(BF16), vals) / denom


def _prompt_mixer_kernel(sinks_ref, x_ref, w_in_ref, sgu_g_ref, sgu_b_ref, sgu_w_ref, sgu_bcol_ref,
                         w_bs_ref, w_ba_ref, w_out_ref, ln1_g_ref, ln1_b_ref,
                         h_ref, k_ref, v_ref, sv_ref,
                         kext_ref, vext_ref, sgu_o_ref, attn_o_ref, *, alpha, ts):
    s = pl.program_id(1)
    last = pl.num_programs(1) - 1
    nblk = ts // WINDOW

    x = x_ref[...]
    z = _dot(x.astype(BF16), w_in_ref[...])
    k = z[:, _OFF_K:_OFF_V]
    v = z[:, _OFF_V:_OFF_U]
    u = _gelu(z[:, _OFF_U:_OFF_SV])
    svn = _layer_norm(_gelu(z[:, _OFF_SV:_OFF_GA]), sgu_g_ref[...], sgu_b_ref[...])

    @pl.when(s == 0)
    def _():
        kext_ref[0:WINDOW, :] = jnp.zeros((WINDOW, D_KV), BF16)
        vext_ref[0:WINDOW, :] = jnp.zeros((WINDOW, D_KV), BF16)

    kext_ref[WINDOW:, :] = k.astype(BF16)
    vext_ref[WINDOW:, :] = v.astype(BF16)

    @pl.when(s == last)
    def _():
        k_ref[...] = k[ts - WINDOW:, :]
        v_ref[...] = v[ts - WINDOW:, :]
        sv_ref[...] = svn[ts - CHUNK:, :]

    row = lax.broadcasted_iota(jnp.int32, (CHUNK, CHUNK), 0)
    col = lax.broadcasted_iota(jnp.int32, (CHUNK, CHUNK), 1)
    for g in range(N_SGU_GROUPS):
        wt = jnp.where(col <= row, sgu_w_ref[g], 0.0).astype(BF16)
        bcol = sgu_bcol_ref[:, g:g + 1]
        gs = slice(g * SGU_GROUP_DIM, (g + 1) * SGU_GROUP_DIM)
        for c in range(ts // CHUNK):
            rs = slice(c * CHUNK, (c + 1) * CHUNK)
            mix = _dot(wt, svn[rs, gs].astype(BF16)) + bcol
            sgu_o_ref[rs, gs] = (u[rs, gs] * mix).astype(BF16)

    nq = Q_PER_KV * WINDOW
    kj = lax.broadcasted_iota(jnp.int32, (2 * WINDOW, nq), 0)
    qi = lax.broadcasted_iota(jnp.int32, (2 * WINDOW, nq), 1) % WINDOW
    rel = qi + WINDOW - kj
    bias = jnp.where((rel >= 0) & (rel < WINDOW), 0.0, NEG_INF).astype(F32)
    bias_first = jnp.where(s > 0, bias, jnp.where(kj < WINDOW, NEG_INF, bias))
    scale = HEAD_DIM ** -0.5
    for i in range(nblk):
        qs = slice(i * WINDOW, (i + 1) * WINDOW)
        outs = []
        for g in range(N_KV_HEADS):
            ds = slice(g * HEAD_DIM, (g + 1) * HEAD_DIM)
            kb = kext_ref[i * WINDOW:(i + 2) * WINDOW, ds]
            vb = vext_ref[i * WINDOW:(i + 2) * WINDOW, ds]
            q4 = jnp.concatenate(
                [z[qs, (g * Q_PER_KV + r) * HEAD_DIM:(g * Q_PER_KV + r + 1) * HEAD_DIM]
                 for r in range(Q_PER_KV)], axis=0).astype(BF16)
            logits = _dot_nt(kb, q4) * scale + (bias_first if i == 0 else bias)
            sink = jnp.concatenate(
                [jnp.full((1, WINDOW), sinks_ref[g * Q_PER_KV + r], F32) for r in range(Q_PER_KV)], axis=1)
            m = jnp.maximum(jnp.max(logits, 0, keepdims=True), sink)
            p = jnp.exp(logits - m)
            denom = jnp.sum(p, 0, keepdims=True) + jnp.exp(sink - m)
            pv = lax.dot_general(vb, p.astype(BF16), (((0,), (0,)), ((), ())), preferred_element_type=F32)
            outs.append(pv / denom)
        o_all = jnp.concatenate(outs, axis=0).T
        heads = [o_all[r * WINDOW:(r + 1) * WINDOW, g * HEAD_DIM:(g + 1) * HEAD_DIM]
                 for g in range(N_KV_HEADS) for r in range(Q_PER_KV)]
        attn_o_ref[qs, :] = jnp.concatenate(heads, axis=1).astype(BF16)

    kext_ref[0:WINDOW, :] = kext_ref[ts:ts + WINDOW, :]
    vext_ref[0:WINDOW, :] = vext_ref[ts:ts + WINDOW, :]

    merged = (_sigmoid(z[:, _OFF_GA:_OFF_GB]) * _dot(sgu_o_ref[...], w_bs_ref[...])
              + _sigmoid(z[:, _OFF_GB:]) * _dot(attn_o_ref[...], w_ba_ref[...]))
    y = _dot(merged.astype(BF16), w_out_ref[...])
    h_ref[...] = _layer_norm(alpha * x + y, ln1_g_ref[...], ln1_b_ref[...])


def _full(shape):
    return pl.BlockSpec(shape, lambda *_: (0,) * len(shape), pipeline_mode=pl.Buffered(1))


def _prompt_mixer(x, p, alpha, ts=PROMPT_BLOCK):
    B, S, D = x.shape
    ts = min(ts, S)
    assert S % ts == 0 and ts % WINDOW == 0 and ts % CHUNK == 0
    grid = (B, S // ts)
    smem = pl.BlockSpec(memory_space=pltpu.SMEM)
    in_specs = [
        smem,
        pl.BlockSpec((None, ts, D), lambda b, s: (b, s, 0)),
        _full((D, D_IN)), _full((1, D_SGU)), _full((1, D_SGU)),
        _full((N_SGU_GROUPS, CHUNK, CHUNK)), _full((CHUNK, N_SGU_GROUPS)),
        _full((D_SGU, D)), _full((D_ATTN, D)), _full((D, D)), _full((1, D)), _full((1, D)),
    ]
    out_specs = [
        pl.BlockSpec((None, ts, D), lambda b, s: (b, s, 0)),
        pl.BlockSpec((None, WINDOW, D_KV), lambda b, s: (b, 0, 0)),
        pl.BlockSpec((None, WINDOW, D_KV), lambda b, s: (b, 0, 0)),
        pl.BlockSpec((None, CHUNK, D_SGU), lambda b, s: (b, 0, 0)),
    ]
    out_shape = [
        jax.ShapeDtypeStruct((B, S, D), F32),
        jax.ShapeDtypeStruct((B, WINDOW, D_KV), F32),
        jax.ShapeDtypeStruct((B, WINDOW, D_KV), F32),
        jax.ShapeDtypeStruct((B, CHUNK, D_SGU), F32),
    ]
    scratch = [
        pltpu.VMEM((ts + WINDOW, D_KV), BF16),
        pltpu.VMEM((ts + WINDOW, D_KV), BF16),
        pltpu.VMEM((ts, D_SGU), BF16),
        pltpu.VMEM((ts, D_ATTN), BF16),
    ]
    return pl.pallas_call(
        functools.partial(_prompt_mixer_kernel, alpha=alpha, ts=ts),
        grid=grid, in_specs=in_specs, out_specs=out_specs, out_shape=out_shape,
        scratch_shapes=scratch, name="prompt_mixer",
        compiler_params=pltpu.CompilerParams(
            dimension_semantics=("arbitrary", "arbitrary"), vmem_limit_bytes=VMEM_LIMIT_BYTES),
    )(p["sinks"], x, p["w_in"], p["sgu_ln_g"], p["sgu_ln_b"], p["sgu_w"], p["sgu_bcol"],
      p["w_bs"], p["w_ba"], p["w_out"], p["ln1_g"], p["ln1_b"])


def _sample_pre_kernel(sgu_w_ref, sgu_b_ref, x_ref, w_in_ref, sgu_g_ref, sgu_bb_ref, w_bs_ref,
                       q_ref, k_ref, v_ref, sv_ref, part_ref, gate_b_ref, *, t_new):
    x = x_ref[...]
    n = x.shape[0]
    z = _dot(x.astype(BF16), w_in_ref[...])
    q_ref[...] = z[:, _OFF_Q:_OFF_K]
    k_ref[...] = z[:, _OFF_K:_OFF_V]
    v_ref[...] = z[:, _OFF_V:_OFF_U]
    u = _gelu(z[:, _OFF_U:_OFF_SV])
    svn = _layer_norm(_gelu(z[:, _OFF_SV:_OFF_GA]), sgu_g_ref[...], sgu_bb_ref[...])
    sv_ref[...] = svn
    t_of_row = lax.broadcasted_iota(jnp.int32, (n, SGU_GROUP_DIM), 0) % t_new
    pieces = []
    for g in range(N_SGU_GROUPS):
        vg = svn[:, g * SGU_GROUP_DIM:(g + 1) * SGU_GROUP_DIM]
        mix = jnp.zeros((n, SGU_GROUP_DIM), F32)
        for t in range(t_new):
            mix = jnp.where(t_of_row == t, sgu_b_ref[g, t], mix)
        for d in range(t_new):
            coef = jnp.zeros((n, SGU_GROUP_DIM), F32)
            for t in range(d, t_new):
                coef = jnp.where(t_of_row == t, sgu_w_ref[g, t * t_new + t - d], coef)
            shifted = vg if d == 0 else pltpu.roll(vg, d, 0)
            mix = mix + coef * shifted
        pieces.append(u[:, g * SGU_GROUP_DIM:(g + 1) * SGU_GROUP_DIM] * mix)
    sgu_o = jnp.concatenate(pieces, axis=1).astype(BF16)
    part_ref[...] = _sigmoid(z[:, _OFF_GA:_OFF_GB]) * _dot(sgu_o, w_bs_ref[...])
    gate_b_ref[...] = _sigmoid(z[:, _OFF_GB:])


def _sample_pre(x2d, p, t_new):
    n, D = x2d.shape
    smem = pl.BlockSpec(memory_space=pltpu.SMEM)
    out_shape = [
        jax.ShapeDtypeStruct((n, D_ATTN), F32),
        jax.ShapeDtypeStruct((n, D_KV), F32),
        jax.ShapeDtypeStruct((n, D_KV), F32),
        jax.ShapeDtypeStruct((n, D_SGU), F32),
        jax.ShapeDtypeStruct((n, D), F32),
        jax.ShapeDtypeStruct((n, D), F32),
    ]
    return pl.pallas_call(
        functools.partial(_sample_pre_kernel, t_new=t_new),
        in_specs=[smem, smem] + [pl.BlockSpec(memory_space=pltpu.VMEM)] * 5,
        out_specs=[pl.BlockSpec(memory_space=pltpu.VMEM)] * 6,
        out_shape=out_shape, name="sample_pre",
        compiler_params=pltpu.CompilerParams(vmem_limit_bytes=VMEM_LIMIT_BYTES),
    )(p["sgu_w"][:, :t_new, :t_new].reshape(N_SGU_GROUPS, t_new * t_new), p["sgu_b"],
      x2d, p["w_in"], p["sgu_ln_g"], p["sgu_ln_b"], p["w_bs"])


def _sample_attn_kernel(sinks_ref, q_ref, kn_ref, vn_ref, ck_ref, cv_ref,
                        o_ref, nk_ref, nv_ref, kbuf_ref, vbuf_ref, *, t_new, bb):
    nq = Q_PER_KV * t_new
    nkeys = WINDOW + 8
    row = lax.broadcasted_iota(jnp.int32, (nq, nkeys), 0)
    kj = lax.broadcasted_iota(jnp.int32, (nq, nkeys), 1)
    t_q = row % t_new
    bias = jnp.where((kj > t_q) & (kj <= t_q + WINDOW), 0.0, NEG_INF).astype(F32)
    r_of_row = lax.broadcasted_iota(jnp.int32, (nq, 1), 0) // t_new
    scale = HEAD_DIM ** -0.5
    pad = jnp.zeros((8 - t_new, D_KV), F32)
    for b in range(bb):
        kbuf_ref[0:WINDOW, :] = ck_ref[b]
        kbuf_ref[WINDOW:WINDOW + t_new, :] = kn_ref[b]
        kbuf_ref[WINDOW + t_new:, :] = pad
        vbuf_ref[0:WINDOW, :] = cv_ref[b]
        vbuf_ref[WINDOW:WINDOW + t_new, :] = vn_ref[b]
        vbuf_ref[WINDOW + t_new:, :] = pad
        nk_ref[b] = kbuf_ref[t_new:t_new + WINDOW, :]
        nv_ref[b] = vbuf_ref[t_new:t_new + WINDOW, :]
        for g in range(N_KV_HEADS):
            ds = slice(g * HEAD_DIM, (g + 1) * HEAD_DIM)
            kg = kbuf_ref[:, ds].astype(BF16)
            vg = vbuf_ref[:, ds].astype(BF16)
            logits = _dot_nt(q_ref[b, g].astype(BF16), kg) * scale + bias
            sink_col = jnp.zeros((nq, 1), F32)
            for r in range(Q_PER_KV):
                sink_col = jnp.where(r_of_row == r, sinks_ref[g * Q_PER_KV + r], sink_col)
            o_ref[b, g] = _sink_softmax_pv(logits, sink_col, vg)


def _sample_attn(q_grouped, k_new, v_new, cache_k, cache_v, sinks, t_new, bb=SAMPLE_ATTN_BATCH):
    nb = q_grouped.shape[0]
    bb = min(bb, nb)
    assert nb % bb == 0 and t_new <= 8
    nq = Q_PER_KV * t_new
    smem = pl.BlockSpec(memory_space=pltpu.SMEM)
    in_specs = [
        smem,
        pl.BlockSpec((bb, N_KV_HEADS, nq, HEAD_DIM), lambda i: (i, 0, 0, 0)),
        pl.BlockSpec((bb, t_new, D_KV), lambda i: (i, 0, 0)),
        pl.BlockSpec((bb, t_new, D_KV), lambda i: (i, 0, 0)),
        pl.BlockSpec((bb, WINDOW, D_KV), lambda i: (i, 0, 0)),
        pl.BlockSpec((bb, WINDOW, D_KV), lambda i: (i, 0, 0)),
    ]
    out_specs = [
        pl.BlockSpec((bb, N_KV_HEADS, nq, HEAD_DIM), lambda i: (i, 0, 0, 0)),
        pl.BlockSpec((bb, WINDOW, D_KV), lambda i: (i, 0, 0)),
        pl.BlockSpec((bb, WINDOW, D_KV), lambda i: (i, 0, 0)),
    ]
    out_shape = [
        jax.ShapeDtypeStruct((nb, N_KV_HEADS, nq, HEAD_DIM), F32),
        jax.ShapeDtypeStruct((nb, WINDOW, D_KV), F32),
        jax.ShapeDtypeStruct((nb, WINDOW, D_KV), F32),
    ]
    return pl.pallas_call(
        functools.partial(_sample_attn_kernel, t_new=t_new, bb=bb),
        grid=(nb // bb,), in_specs=in_specs, out_specs=out_specs, out_shape=out_shape,
        scratch_shapes=[pltpu.VMEM((WINDOW + 8, D_KV), F32), pltpu.VMEM((WINDOW + 8, D_KV), F32)],
        name="sample_attn",
        compiler_params=pltpu.CompilerParams(dimension_semantics=("arbitrary",)),
    )(sinks, q_grouped, k_new, v_new, cache_k, cache_v)


def _sample_post_kernel(x_ref, part_ref, gate_b_ref, o_ref, w_ba_ref, w_out_ref, ln1_g_ref, ln1_b_ref,
                        h_ref, *, alpha):
    merged = part_ref[...] + gate_b_ref[...] * _dot(o_ref[...].astype(BF16), w_ba_ref[...])
    y = _dot(merged.astype(BF16), w_out_ref[...])
    h_ref[...] = _layer_norm(alpha * x_ref[...] + y, ln1_g_ref[...], ln1_b_ref[...])


def _sample_post(x2d, part, gate_b, o2d, p, alpha):
    return pl.pallas_call(
        functools.partial(_sample_post_kernel, alpha=alpha),
        in_specs=[pl.BlockSpec(memory_space=pltpu.VMEM)] * 8,
        out_specs=pl.BlockSpec(memory_space=pltpu.VMEM),
        out_shape=jax.ShapeDtypeStruct(x2d.shape, F32), name="sample_post",
        compiler_params=pltpu.CompilerParams(vmem_limit_bytes=VMEM_LIMIT_BYTES),
    )(x2d, part, gate_b, o2d, p["w_ba"], p["w_out"], p["ln1_g"], p["ln1_b"])


def _top_values(work, count):
    vals = []
    for i in range(count):
        top = jnp.max(work, axis=0, keepdims=True)
        vals.append(top)
        if i + 1 < count:
            work = jnp.where(work == top, -jnp.inf, work)
    return vals


def _sort_pair(v, i, j):
    v[i], v[j] = jnp.maximum(v[i], v[j]), jnp.minimum(v[i], v[j])


def _bitonic_merge(v):
    j = len(v) // 2
    while j >= 1:
        for i in range(len(v)):
            if i ^ j > i:
                _sort_pair(v, i, i ^ j)
        j //= 2


def _top16_rows(s):
    n = PEER_TOPK
    assert s.shape[0] == 8 * n
    v = [s[k * 8:(k + 1) * 8] for k in range(n)]
    k = 2
    while k <= n:
        j = k // 2
        while j >= 1:
            for i in range(n):
                l = i ^ j
                if l > i:
                    if i & k == 0:
                        _sort_pair(v, i, l)
                    else:
                        _sort_pair(v, l, i)
            j //= 2
        k *= 2
    for shift in (4, 2, 1):
        v = [jnp.maximum(v[i], pltpu.roll(v[n - 1 - i], shift, 0)) for i in range(n)]
        _bitonic_merge(v)
    return [x[0:1] for x in v]


def _candidate_rows(a_rows, b_rows):
    assert PEER_TOPK == 16
    a_all = jnp.concatenate(a_rows, axis=0)
    a_low = a_all[0:8]
    row = lax.broadcasted_iota(jnp.int32, a_low.shape, 0)

    def pair(j):
        return a_low + b_rows[j]

    def corner(j):
        return a_rows[0] + b_rows[j]

    t3 = jnp.where(row >= 5, pltpu.roll(pair(4), 5, 0), pair(2))
    t4 = jnp.where(row >= 6, pltpu.roll(pair(6), 6, 0),
                   jnp.where(row >= 4, pltpu.roll(pair(5), 4, 0), pair(3)))
    t5 = pair(7)
    for k in range(6):
        t5 = jnp.where(row >= 2 + k, corner(8 + k), t5)
    t6 = jnp.where(row >= 2, -jnp.inf, jnp.where(row >= 1, corner(15), corner(14)))
    return jnp.concatenate([a_all + b_rows[0], pair(1), t3, t4, t5, t6], axis=0)


def _peer_select(s1, s2):
    a_rows = _top16_rows(s1)
    b_rows = _top16_rows(s2)
    rank2 = jnp.zeros_like(s2)
    for jj in range(PEER_TOPK):
        rank2 = jnp.where(s2 < b_rows[jj], float(jj + 1), rank2)
    best = _top_values(_candidate_rows(a_rows, b_rows), PEER_TOPK)
    norm = jnp.zeros_like(best[0])
    for i in range(PEER_TOPK):
        norm = norm + jnp.exp(best[i] - best[0])
    count = jnp.zeros_like(s1)
    for jj in range(PEER_TOPK):
        count = jnp.where(s1 + b_rows[jj] >= best[PEER_TOPK - 1], float(jj + 1), count)
    e1 = jnp.exp(s1 - a_rows[0]) / norm
    e2 = jnp.exp(s2 - b_rows[0])
    return rank2, e2, count, e1


def _peer_gate_tile(act_ref, coef_ref, first_n1, ns, t, sel_refs):
    rank2_ref, e2_ref, count_ref, e1_ref = sel_refs
    ls = slice(t * LANES, (t + 1) * LANES)
    gates = [None for _ in ns]
    for hd in range(PEER_HEADS):
        rank2 = rank2_ref[hd, t]
        e2 = e2_ref[hd, t]
        for i, n in enumerate(ns):
            count = count_ref[hd, t, pl.ds(first_n1 + n, 1), :]
            e1 = e1_ref[hd, t, pl.ds(first_n1 + n, 1), :]
            term = jnp.where(rank2 < count, e2, 0.0) * e1
            gates[i] = term if gates[i] is None else gates[i] + term
    for i, n in enumerate(ns):
        rs = slice(n * N_SUBKEYS, (n + 1) * N_SUBKEYS)
        x = act_ref[rs, ls]
        inner = x * (GELU_C0 + (GELU_C0 * GELU_C1) * (x * x))
        coef_ref[rs, ls] = ((gates[i] * x) * (1.0 + jnp.tanh(inner))).astype(BF16)


def _peer_kernel(h_ref, wq_ref, sk_ref, u_ref, vt_ref, ln_g_ref, ln_b_ref, y_ref,
                 ht_ref, qt_ref, rank2_ref, e2_ref, count_ref, e1_ref, acc_ref,
                 act0_ref, act1_ref, coef0_ref, coef1_ref, *, alpha, tb, se):
    j = pl.program_id(1)
    ntb = tb // LANES
    nblocks = N_EXPERTS // se
    sel_refs = (rank2_ref, e2_ref, count_ref, e1_ref)
    act_refs = (act0_ref, act1_ref)
    coef_refs = (coef0_ref, coef1_ref)

    @pl.when(j == 0)
    def _():
        ht = h_ref[...].T.astype(BF16)
        ht_ref[...] = ht
        qt_ref[...] = _dot(wq_ref[...], ht).astype(BF16)

        def per_head(hd, carry):
            for t in range(ntb):
                ls = slice(t * LANES, (t + 1) * LANES)
                s = []
                for c in range(2):
                    r0 = pl.multiple_of((hd * 2 + c) * PEER_HALF, PEER_HALF)
                    s.append(_dot(sk_ref[hd * 2 + c], qt_ref[pl.ds(r0, PEER_HALF), ls]))
                rank2, e2, count, e1 = _peer_select(s[0], s[1])
                rank2_ref[hd, t] = rank2
                e2_ref[hd, t] = e2
                count_ref[hd, t] = count
                e1_ref[hd, t] = 0.5 * e1
            return carry

        lax.fori_loop(0, PEER_HEADS, per_head, 0)
        acc_ref[...] = jnp.zeros_like(acc_ref)

    def step_body(q, with_act, with_gate, with_combine):
        piece_t, piece_e, piece_d = tb // PEER_TOKEN_PIECES, se // PEER_ROW_PIECES, acc_ref.shape[0] // PEER_ROW_PIECES
        pieces = []
        for tp in range(PEER_TOKEN_PIECES):
            hs = slice(tp * piece_t, (tp + 1) * piece_t)
            for lo in range(PEER_ROW_PIECES):
                if with_act:
                    pieces.append(("act", hs, slice(lo * piece_e, (lo + 1) * piece_e)))
                if with_combine:
                    pieces.append(("combine", hs, slice(lo * piece_d, (lo + 1) * piece_d)))
        n_sub = se // N_SUBKEYS
        groups = [tuple(range(g, min(g + PEER_GATE_GROUP, n_sub))) for g in range(0, n_sub, PEER_GATE_GROUP)]
        tiles = [(ns, t) for t in range(ntb) for ns in groups] if with_gate else []
        per_piece = -(-len(tiles) // max(len(pieces), 1))
        for k, (kind, hs, es) in enumerate(pieces):
            for n, t in tiles[k * per_piece:(k + 1) * per_piece]:
                _peer_gate_tile(act_refs[1 - q], coef_refs[1 - q], (j - 1) * (se // N_SUBKEYS), n, t, sel_refs)
            if kind == "act":
                act_refs[q][es, hs] = _dot(u_ref[es, :], ht_ref[:, hs])
            else:
                acc_ref[es, hs] += _dot(vt_ref[es, :], coef_refs[q][:, hs])
        for n, t in tiles[len(pieces) * per_piece:]:
            _peer_gate_tile(act_refs[1 - q], coef_refs[1 - q], (j - 1) * (se // N_SUBKEYS), n, t, sel_refs)

    assert nblocks % 2 == 0 and nblocks >= 4
    pl.when(j == 0)(functools.partial(step_body, 0, True, False, False))
    pl.when(j == 1)(functools.partial(step_body, 1, True, True, False))
    for q in range(2):
        pl.when((j >= 2) & (j < nblocks) & (j % 2 == q))(functools.partial(step_body, q, True, True, True))
    pl.when(j == nblocks)(functools.partial(step_body, 0, False, True, True))
    pl.when(j == nblocks + 1)(functools.partial(step_body, 1, False, False, True))

    @pl.when(j == pl.num_programs(1) - 1)
    def _():
        y_ref[...] = _layer_norm(alpha * h_ref[...] + acc_ref[...].T, ln_g_ref[...], ln_b_ref[...])


def _peer(h2d, p, alpha, tb=PEER_TOKEN_BLOCK):
    n, D = h2d.shape
    tb = min(tb, n)
    se = p["peer_vt"].shape[2]
    assert n % tb == 0 and tb % LANES == 0 and N_EXPERTS % se == 0 and se % N_SUBKEYS == 0
    ntb = tb // LANES
    qdim = PEER_HEADS * PEER_KEY_DIM
    nblocks = N_EXPERTS // se
    in_specs = [
        pl.BlockSpec((tb, D), lambda i, j: (i, 0)),
        _full((qdim, D)),
        _full((PEER_HEADS * 2, N_SUBKEYS, PEER_HALF)),
        pl.BlockSpec((se, D), lambda i, j: (jnp.minimum(j, nblocks - 1), 0)),
        pl.BlockSpec((None, D, se), lambda i, j: (jnp.clip(j - 2, 0, nblocks - 1), 0, 0)),
        _full((1, D)), _full((1, D)),
    ]
    per_head = (PEER_HEADS, ntb, N_SUBKEYS, LANES)
    scratch = [
        pltpu.VMEM((D, tb), BF16),
        pltpu.VMEM((qdim, tb), BF16),
        pltpu.VMEM(per_head, F32), pltpu.VMEM(per_head, F32),
        pltpu.VMEM(per_head, F32), pltpu.VMEM(per_head, F32),
        pltpu.VMEM((D, tb), F32),
    ] + [pltpu.VMEM((se, tb), F32)] * 2 + [pltpu.VMEM((se, tb), BF16)] * 2
    return pl.pallas_call(
        functools.partial(_peer_kernel, alpha=alpha, tb=tb, se=se),
        grid=(n // tb, nblocks + 2), in_specs=in_specs,
        out_specs=pl.BlockSpec((tb, D), lambda i, j: (i, 0), pipeline_mode=pl.Buffered(1)),
        out_shape=jax.ShapeDtypeStruct((n, D), F32), scratch_shapes=scratch, name="peer",
        compiler_params=pltpu.CompilerParams(
            dimension_semantics=("arbitrary", "arbitrary"), vmem_limit_bytes=VMEM_LIMIT_BYTES),
    )(h2d, p["peer_wq_t"], p["peer_sk"], p["peer_u"], p["peer_vt"], p["ln2_g"], p["ln2_b"])


def _layer_params(l, w_in, sinks, sgu_ln_g, sgu_ln_b, sgu_w, sgu_b, w_branch_sgu, w_branch_attn, w_out,
                  ln1_g, ln1_b, peer_w_q, peer_sub_keys, peer_u, peer_v, ln2_g, ln2_b):
    return {
        "w_in": w_in[l].astype(BF16),
        "sinks": sinks[l],
        "sgu_ln_g": sgu_ln_g[l][None, :], "sgu_ln_b": sgu_ln_b[l][None, :],
        "sgu_w": sgu_w[l],
        "sgu_b": sgu_b[l], "sgu_bcol": sgu_b[l].T,
        "w_bs": w_branch_sgu[l].astype(BF16), "w_ba": w_branch_attn[l].astype(BF16),
        "w_out": w_out[l].astype(BF16),
        "ln1_g": ln1_g[l][None, :], "ln1_b": ln1_b[l][None, :],
        "peer_wq_t": peer_w_q[l].T.astype(BF16),
        "peer_sk": peer_sub_keys[l].reshape(PEER_HEADS * 2, N_SUBKEYS, PEER_HALF).astype(BF16),
        "peer_u": peer_u[l].astype(BF16),
        "peer_vt": peer_v[l].reshape(N_EXPERTS // PEER_EXPERT_BLOCK, PEER_EXPERT_BLOCK, D_MODEL)
                            .transpose(0, 2, 1).astype(BF16),
        "ln2_g": ln2_g[l][None, :], "ln2_b": ln2_b[l][None, :],
    }


def _prompt_layer(x, p, alpha):
    B, S, D = x.shape
    h, k_last, v_last, sv_last = _prompt_mixer(x, p, alpha)
    y = _peer(h.reshape(B * S, D), p, alpha).reshape(B, S, D)
    return (y, k_last.reshape(B, WINDOW, N_KV_HEADS, HEAD_DIM), v_last.reshape(B, WINDOW, N_KV_HEADS, HEAD_DIM),
            sv_last.reshape(B, CHUNK, N_SGU_GROUPS, SGU_GROUP_DIM))


def _sample_layer(x, cache_k, cache_v, p, alpha):
    B, T, D = x.shape
    x2d = x.reshape(B * T, D)
    q, k_new, v_new, svn, part, gate_b = _sample_pre(x2d, p, T)
    qg = q.reshape(B, T, N_KV_HEADS, Q_PER_KV, HEAD_DIM).transpose(0, 2, 3, 1, 4)
    qg = qg.reshape(B, N_KV_HEADS, Q_PER_KV * T, HEAD_DIM)
    og, new_k, new_v = _sample_attn(qg, k_new.reshape(B, T, D_KV), v_new.reshape(B, T, D_KV),
                                    cache_k.reshape(B, WINDOW, D_KV), cache_v.reshape(B, WINDOW, D_KV),
                                    p["sinks"], T)
    o2d = og.reshape(B, N_KV_HEADS, Q_PER_KV, T, HEAD_DIM).transpose(0, 3, 1, 2, 4).reshape(B * T, D_ATTN)
    h = _sample_post(x2d, part, gate_b, o2d, p, alpha)
    y = _peer(h, p, alpha).reshape(B, T, D)
    return (y, new_k.reshape(B, WINDOW, N_KV_HEADS, HEAD_DIM), new_v.reshape(B, WINDOW, N_KV_HEADS, HEAD_DIM),
            svn.reshape(B, T, N_SGU_GROUPS, SGU_GROUP_DIM))


def kernel(x_prompt, x_sample, cache_k, cache_v, w_in, sinks, sgu_ln_g, sgu_ln_b, sgu_w, sgu_b, w_branch_sgu, w_branch_attn, w_out, ln1_g, ln1_b, peer_w_q, peer_sub_keys, peer_u, peer_v, ln2_g, ln2_b):
    depth = w_in.shape[0]
    alpha = (2.0 * depth) ** 0.25
    yp, ys = x_prompt, x_sample
    outs = [[] for _ in range(6)]
    for l in range(depth):
        p = _layer_params(l, w_in, sinks, sgu_ln_g, sgu_ln_b, sgu_w, sgu_b, w_branch_sgu, w_branch_attn,
                          w_out, ln1_g, ln1_b, peer_w_q, peer_sub_keys, peer_u, peer_v, ln2_g, ln2_b)
        yp, kp, vp, svp = _prompt_layer(yp, p, alpha)
        ys, ksm, vsm, svs = _sample_layer(ys, cache_k[l], cache_v[l], p, alpha)
        for acc, val in zip(outs, (kp, vp, ksm, vsm, svp, svs)):
            acc.append(val)
    return (yp, ys) + tuple(jnp.stack(o) for o in outs)
```

```python
import functools

import jax
import jax.numpy as jnp
from jax import lax
from jax.experimental import pallas as pl
from jax.experimental.pallas import tpu as pltpu

F32 = jnp.float32
BF16 = jnp.bfloat16

D_MODEL = 1024
N_Q_HEADS = 8
N_KV_HEADS = 2
HEAD_DIM = 64
Q_PER_KV = N_Q_HEADS // N_KV_HEADS
D_ATTN = N_Q_HEADS * HEAD_DIM
D_KV = N_KV_HEADS * HEAD_DIM
WINDOW = 128
CHUNK = 128
D_SGU = D_MODEL // 2
N_SGU_GROUPS = 4
SGU_GROUP_DIM = D_SGU // N_SGU_GROUPS
PEER_HEADS = 8
N_SUBKEYS = 128
N_EXPERTS = N_SUBKEYS * N_SUBKEYS
PEER_TOPK = 16
PEER_KEY_DIM = 256
PEER_HALF = PEER_KEY_DIM // 2
LN_EPS = 1e-5
NEG_INF = -1e30
D_IN = D_ATTN + 2 * D_KV + 2 * D_SGU + 2 * D_MODEL
_OFF_Q = 0
_OFF_K = D_ATTN
_OFF_V = _OFF_K + D_KV
_OFF_U = _OFF_V + D_KV
_OFF_SV = _OFF_U + D_SGU
_OFF_GA = _OFF_SV + D_SGU
_OFF_GB = _OFF_GA + D_MODEL

LANES = 128
VMEM_LIMIT_BYTES = 56 * 1024 * 1024

PROMPT_BLOCK = 512
SAMPLE_ATTN_BATCH = 8
PEER_TOKEN_BLOCK = 512
PEER_EXPERT_BLOCK = 1024
PEER_GATE_GROUP = 1
PEER_TOKEN_PIECES = 1
PEER_ROW_PIECES = 1


GELU_C0 = 0.7978845608028654
GELU_C1 = 0.044715


def _gelu(x):
    return 0.5 * x * (1.0 + jnp.tanh(GELU_C0 * (x + GELU_C1 * (x * x * x))))


def _sigmoid(x):
    return 1.0 / (1.0 + jnp.exp(-x))


def _layer_norm(x, g, b):
    mu = jnp.mean(x, -1, keepdims=True)
    xc = x - mu
    var = jnp.mean(xc * xc, -1, keepdims=True)
    return xc * lax.rsqrt(var + LN_EPS) * g + b


def _dot(a, b):
    return jnp.dot(a, b, preferred_element_type=F32)


def _dot_nt(a, b):
    return lax.dot_general(a, b, (((1,), (1,)), ((), ())), preferred_element_type=F32)


def _sink_softmax_pv(logits, sink_col, vals):
    m = jnp.maximum(jnp.max(logits, -1, keepdims=True), sink_col)
    p = jnp.exp(logits - m)
    denom = jnp.sum(p, -1, keepdims=True) + jnp.exp(sink_col - m)
    return _dot(p.astype(BF16), vals) / denom


def _prompt_mixer_kernel(sinks_ref, x_ref, w_in_ref, sgu_g_ref, sgu_b_ref, sgu_w_ref, sgu_bcol_ref,
                         w_bs_ref, w_ba_ref, w_out_ref, ln1_g_ref, ln1_b_ref,
                         h_ref, k_ref, v_ref, sv_ref,
                         kext_ref, vext_ref, sgu_o_ref, attn_o_ref, *, alpha, ts):
    s = pl.program_id(1)
    last = pl.num_programs(1) - 1
    nblk = ts // WINDOW

    x = x_ref[...]
    z = _dot(x.astype(BF16), w_in_ref[...])
    k = z[:, _OFF_K:_OFF_V]
    v = z[:, _OFF_V:_OFF_U]
    u = _gelu(z[:, _OFF_U:_OFF_SV])
    svn = _layer_norm(_gelu(z[:, _OFF_SV:_OFF_GA]), sgu_g_ref[...], sgu_b_ref[...])

    @pl.when(s == 0)
    def _():
        kext_ref[0:WINDOW, :] = jnp.zeros((WINDOW, D_KV), BF16)
        vext_ref[0:WINDOW, :] = jnp.zeros((WINDOW, D_KV), BF16)

    kext_ref[WINDOW:, :] = k.astype(BF16)
    vext_ref[WINDOW:, :] = v.astype(BF16)

    @pl.when(s == last)
    def _():
        k_ref[...] = k[ts - WINDOW:, :]
        v_ref[...] = v[ts - WINDOW:, :]
        sv_ref[...] = svn[ts - CHUNK:, :]

    row = lax.broadcasted_iota(jnp.int32, (CHUNK, CHUNK), 0)
    col = lax.broadcasted_iota(jnp.int32, (CHUNK, CHUNK), 1)
    for g in range(N_SGU_GROUPS):
        wt = jnp.where(col <= row, sgu_w_ref[g], 0.0).astype(BF16)
        bcol = sgu_bcol_ref[:, g:g + 1]
        gs = slice(g * SGU_GROUP_DIM, (g + 1) * SGU_GROUP_DIM)
        for c in range(ts // CHUNK):
            rs = slice(c * CHUNK, (c + 1) * CHUNK)
            mix = _dot(wt, svn[rs, gs].astype(BF16)) + bcol
            sgu_o_ref[rs, gs] = (u[rs, gs] * mix).astype(BF16)

    nq = Q_PER_KV * WINDOW
    kj = lax.broadcasted_iota(jnp.int32, (2 * WINDOW, nq), 0)
    qi = lax.broadcasted_iota(jnp.int32, (2 * WINDOW, nq), 1) % WINDOW
    rel = qi + WINDOW - kj
    bias = jnp.where((rel >= 0) & (rel < WINDOW), 0.0, NEG_INF).astype(F32)
    bias_first = jnp.where(s > 0, bias, jnp.where(kj < WINDOW, NEG_INF, bias))
    scale = HEAD_DIM ** -0.5
    for i in range(nblk):
        qs = slice(i * WINDOW, (i + 1) * WINDOW)
        outs = []
        for g in range(N_KV_HEADS):
            ds = slice(g * HEAD_DIM, (g + 1) * HEAD_DIM)
            kb = kext_ref[i * WINDOW:(i + 2) * WINDOW, ds]
            vb = vext_ref[i * WINDOW:(i + 2) * WINDOW, ds]
            q4 = jnp.concatenate(
                [z[qs, (g * Q_PER_KV + r) * HEAD_DIM:(g * Q_PER_KV + r + 1) * HEAD_DIM]
                 for r in range(Q_PER_KV)], axis=0).astype(BF16)
            logits = _dot_nt(kb, q4) * scale + (bias_first if i == 0 else bias)
            sink = jnp.concatenate(
                [jnp.full((1, WINDOW), sinks_ref[g * Q_PER_KV + r], F32) for r in range(Q_PER_KV)], axis=1)
            m = jnp.maximum(jnp.max(logits, 0, keepdims=True), sink)
            p = jnp.exp(logits - m)
            denom = jnp.sum(p, 0, keepdims=True) + jnp.exp(sink - m)
            pv = lax.dot_general(vb, p.astype(BF16), (((0,), (0,)), ((), ())), preferred_element_type=F32)
            outs.append(pv / denom)
        o_all = jnp.concatenate(outs, axis=0).T
        heads = [o_all[r * WINDOW:(r + 1) * WINDOW, g * HEAD_DIM:(g + 1) * HEAD_DIM]
                 for g in range(N_KV_HEADS) for r in range(Q_PER_KV)]
        attn_o_ref[qs, :] = jnp.concatenate(heads, axis=1).astype(BF16)

    kext_ref[0:WINDOW, :] = kext_ref[ts:ts + WINDOW, :]
    vext_ref[0:WINDOW, :] = vext_ref[ts:ts + WINDOW, :]

    merged = (_sigmoid(z[:, _OFF_GA:_OFF_GB]) * _dot(sgu_o_ref[...], w_bs_ref[...])
              + _sigmoid(z[:, _OFF_GB:]) * _dot(attn_o_ref[...], w_ba_ref[...]))
    y = _dot(merged.astype(BF16), w_out_ref[...])
    h_ref[...] = _layer_norm(alpha * x + y, ln1_g_ref[...], ln1_b_ref[...])


def _full(shape):
    return pl.BlockSpec(shape, lambda *_: (0,) * len(shape), pipeline_mode=pl.Buffered(1))


def _prompt_mixer(x, p, alpha, ts=PROMPT_BLOCK):
    B, S, D = x.shape
    ts = min(ts, S)
    assert S % ts == 0 and ts % WINDOW == 0 and ts % CHUNK == 0
    grid = (B, S // ts)
    smem = pl.BlockSpec(memory_space=pltpu.SMEM)
    in_specs = [
        smem,
        pl.BlockSpec((None, ts, D), lambda b, s: (b, s, 0)),
        _full((D, D_IN)), _full((1, D_SGU)), _full((1, D_SGU)),
        _full((N_SGU_GROUPS, CHUNK, CHUNK)), _full((CHUNK, N_SGU_GROUPS)),
        _full((D_SGU, D)), _full((D_ATTN, D)), _full((D, D)), _full((1, D)), _full((1, D)),
    ]
    out_specs = [
        pl.BlockSpec((None, ts, D), lambda b, s: (b, s, 0)),
        pl.BlockSpec((None, WINDOW, D_KV), lambda b, s: (b, 0, 0)),
        pl.BlockSpec((None, WINDOW, D_KV), lambda b, s: (b, 0, 0)),
        pl.BlockSpec((None, CHUNK, D_SGU), lambda b, s: (b, 0, 0)),
    ]
    out_shape = [
        jax.ShapeDtypeStruct((B, S, D), F32),
        jax.ShapeDtypeStruct((B, WINDOW, D_KV), F32),
        jax.ShapeDtypeStruct((B, WINDOW, D_KV), F32),
        jax.ShapeDtypeStruct((B, CHUNK, D_SGU), F32),
    ]
    scratch = [
        pltpu.VMEM((ts + WINDOW, D_KV), BF16),
        pltpu.VMEM((ts + WINDOW, D_KV), BF16),
        pltpu.VMEM((ts, D_SGU), BF16),
        pltpu.VMEM((ts, D_ATTN), BF16),
    ]
    return pl.pallas_call(
        functools.partial(_prompt_mixer_kernel, alpha=alpha, ts=ts),
        grid=grid, in_specs=in_specs, out_specs=out_specs, out_shape=out_shape,
        scratch_shapes=scratch, name="prompt_mixer",
        compiler_params=pltpu.CompilerParams(
            dimension_semantics=("arbitrary", "arbitrary"), vmem_limit_bytes=VMEM_LIMIT_BYTES),
    )(p["sinks"], x, p["w_in"], p["sgu_ln_g"], p["sgu_ln_b"], p["sgu_w"], p["sgu_bcol"],
      p["w_bs"], p["w_ba"], p["w_out"], p["ln1_g"], p["ln1_b"])


def _sample_pre_kernel(sgu_w_ref, sgu_b_ref, x_ref, w_in_ref, sgu_g_ref, sgu_bb_ref, w_bs_ref,
                       q_ref, k_ref, v_ref, sv_ref, part_ref, gate_b_ref, *, t_new):
    x = x_ref[...]
    n = x.shape[0]
    z = _dot(x.astype(BF16), w_in_ref[...])
    q_ref[...] = z[:, _OFF_Q:_OFF_K]
    k_ref[...] = z[:, _OFF_K:_OFF_V]
    v_ref[...] = z[:, _OFF_V:_OFF_U]
    u = _gelu(z[:, _OFF_U:_OFF_SV])
    svn = _layer_norm(_gelu(z[:, _OFF_SV:_OFF_GA]), sgu_g_ref[...], sgu_bb_ref[...])
    sv_ref[...] = svn
    t_of_row = lax.broadcasted_iota(jnp.int32, (n, SGU_GROUP_DIM), 0) % t_new
    pieces = []
    for g in range(N_SGU_GROUPS):
        vg = svn[:, g * SGU_GROUP_DIM:(g + 1) * SGU_GROUP_DIM]
        mix = jnp.zeros((n, SGU_GROUP_DIM), F32)
        for t in range(t_new):
            mix = jnp.where(t_of_row == t, sgu_b_ref[g, t], mix)
        for d in range(t_new):
            coef = jnp.zeros((n, SGU_GROUP_DIM), F32)
            for t in range(d, t_new):
                coef = jnp.where(t_of_row == t, sgu_w_ref[g, t * t_new + t - d], coef)
            shifted = vg if d == 0 else pltpu.roll(vg, d, 0)
            mix = mix + coef * shifted
        pieces.append(u[:, g * SGU_GROUP_DIM:(g + 1) * SGU_GROUP_DIM] * mix)
    sgu_o = jnp.concatenate(pieces, axis=1).astype(BF16)
    part_ref[...] = _sigmoid(z[:, _OFF_GA:_OFF_GB]) * _dot(sgu_o, w_bs_ref[...])
    gate_b_ref[...] = _sigmoid(z[:, _OFF_GB:])


def _sample_pre(x2d, p, t_new):
    n, D = x2d.shape
    smem = pl.BlockSpec(memory_space=pltpu.SMEM)
    out_shape = [
        jax.ShapeDtypeStruct((n, D_ATTN), F32),
        jax.ShapeDtypeStruct((n, D_KV), F32),
        jax.ShapeDtypeStruct((n, D_KV), F32),
        jax.ShapeDtypeStruct((n, D_SGU), F32),
        jax.ShapeDtypeStruct((n, D), F32),
        jax.ShapeDtypeStruct((n, D), F32),
    ]
    return pl.pallas_call(
        functools.partial(_sample_pre_kernel, t_new=t_new),
        in_specs=[smem, smem] + [pl.BlockSpec(memory_space=pltpu.VMEM)] * 5,
        out_specs=[pl.BlockSpec(memory_space=pltpu.VMEM)] * 6,
        out_shape=out_shape, name="sample_pre",
        compiler_params=pltpu.CompilerParams(vmem_limit_bytes=VMEM_LIMIT_BYTES),
    )(p["sgu_w"][:, :t_new, :t_new].reshape(N_SGU_GROUPS, t_new * t_new), p["sgu_b"],
      x2d, p["w_in"], p["sgu_ln_g"], p["sgu_ln_b"], p["w_bs"])


def _sample_attn_kernel(sinks_ref, q_ref, kn_ref, vn_ref, ck_ref, cv_ref,
                        o_ref, nk_ref, nv_ref, kbuf_ref, vbuf_ref, *, t_new, bb):
    nq = Q_PER_KV * t_new
    nkeys = WINDOW + 8
    row = lax.broadcasted_iota(jnp.int32, (nq, nkeys), 0)
    kj = lax.broadcasted_iota(jnp.int32, (nq, nkeys), 1)
    t_q = row % t_new
    bias = jnp.where((kj > t_q) & (kj <= t_q + WINDOW), 0.0, NEG_INF).astype(F32)
    r_of_row = lax.broadcasted_iota(jnp.int32, (nq, 1), 0) // t_new
    scale = HEAD_DIM ** -0.5
    pad = jnp.zeros((8 - t_new, D_KV), F32)
    for b in range(bb):
        kbuf_ref[0:WINDOW, :] = ck_ref[b]
        kbuf_ref[WINDOW:WINDOW + t_new, :] = kn_ref[b]
        kbuf_ref[WINDOW + t_new:, :] = pad
        vbuf_ref[0:WINDOW, :] = cv_ref[b]
        vbuf_ref[WINDOW:WINDOW + t_new, :] = vn_ref[b]
        vbuf_ref[WINDOW + t_new:, :] = pad
        nk_ref[b] = kbuf_ref[t_new:t_new + WINDOW, :]
        nv_ref[b] = vbuf_ref[t_new:t_new + WINDOW, :]
        for g in range(N_KV_HEADS):
            ds = slice(g * HEAD_DIM, (g + 1) * HEAD_DIM)
            kg = kbuf_ref[:, ds].astype(BF16)
            vg = vbuf_ref[:, ds].astype(BF16)
            logits = _dot_nt(q_ref[b, g].astype(BF16), kg) * scale + bias
            sink_col = jnp.zeros((nq, 1), F32)
            for r in range(Q_PER_KV):
                sink_col = jnp.where(r_of_row == r, sinks_ref[g * Q_PER_KV + r], sink_col)
            o_ref[b, g] = _sink_softmax_pv(logits, sink_col, vg)


def _sample_attn(q_grouped, k_new, v_new, cache_k, cache_v, sinks, t_new, bb=SAMPLE_ATTN_BATCH):
    nb = q_grouped.shape[0]
    bb = min(bb, nb)
    assert nb % bb == 0 and t_new <= 8
    nq = Q_PER_KV * t_new
    smem = pl.BlockSpec(memory_space=pltpu.SMEM)
    in_specs = [
        smem,
        pl.BlockSpec((bb, N_KV_HEADS, nq, HEAD_DIM), lambda i: (i, 0, 0, 0)),
        pl.BlockSpec((bb, t_new, D_KV), lambda i: (i, 0, 0)),
        pl.BlockSpec((bb, t_new, D_KV), lambda i: (i, 0, 0)),
        pl.BlockSpec((bb, WINDOW, D_KV), lambda i: (i, 0, 0)),
        pl.BlockSpec((bb, WINDOW, D_KV), lambda i: (i, 0, 0)),
    ]
    out_specs = [
        pl.BlockSpec((bb, N_KV_HEADS, nq, HEAD_DIM), lambda i: (i, 0, 0, 0)),
        pl.BlockSpec((bb, WINDOW, D_KV), lambda i: (i, 0, 0)),
        pl.BlockSpec((bb, WINDOW, D_KV), lambda i: (i, 0, 0)),
    ]
    out_shape = [
        jax.ShapeDtypeStruct((nb, N_KV_HEADS, nq, HEAD_DIM), F32),
        jax.ShapeDtypeStruct((nb, WINDOW, D_KV), F32),
        jax.ShapeDtypeStruct((nb, WINDOW, D_KV), F32),
    ]
    return pl.pallas_call(
        functools.partial(_sample_attn_kernel, t_new=t_new, bb=bb),
        grid=(nb // bb,), in_specs=in_specs, out_specs=out_specs, out_shape=out_shape,
        scratch_shapes=[pltpu.VMEM((WINDOW + 8, D_KV), F32), pltpu.VMEM((WINDOW + 8, D_KV), F32)],
        name="sample_attn",
        compiler_params=pltpu.CompilerParams(dimension_semantics=("arbitrary",)),
    )(sinks, q_grouped, k_new, v_new, cache_k, cache_v)


def _sample_post_kernel(x_ref, part_ref, gate_b_ref, o_ref, w_ba_ref, w_out_ref, ln1_g_ref, ln1_b_ref,
                        h_ref, *, alpha):
    merged = part_ref[...] + gate_b_ref[...] * _dot(o_ref[...].astype(BF16), w_ba_ref[...])
    y = _dot(merged.astype(BF16), w_out_ref[...])
    h_ref[...] = _layer_norm(alpha * x_ref[...] + y, ln1_g_ref[...], ln1_b_ref[...])


def _sample_post(x2d, part, gate_b, o2d, p, alpha):
    return pl.pallas_call(
        functools.partial(_sample_post_kernel, alpha=alpha),
        in_specs=[pl.BlockSpec(memory_space=pltpu.VMEM)] * 8,
        out_specs=pl.BlockSpec(memory_space=pltpu.VMEM),
        out_shape=jax.ShapeDtypeStruct(x2d.shape, F32), name="sample_post",
        compiler_params=pltpu.CompilerParams(vmem_limit_bytes=VMEM_LIMIT_BYTES),
    )(x2d, part, gate_b, o2d, p["w_ba"], p["w_out"], p["ln1_g"], p["ln1_b"])


def _top_values(work, count):
    vals = []
    for i in range(count):
        top = jnp.max(work, axis=0, keepdims=True)
        vals.append(top)
        if i + 1 < count:
            work = jnp.where(work == top, -jnp.inf, work)
    return vals


def _sort_pair(v, i, j):
    v[i], v[j] = jnp.maximum(v[i], v[j]), jnp.minimum(v[i], v[j])


def _bitonic_merge(v):
    j = len(v) // 2
    while j >= 1:
        for i in range(len(v)):
            if i ^ j > i:
                _sort_pair(v, i, i ^ j)
        j //= 2


def _top16_rows(s):
    n = PEER_TOPK
    assert s.shape[0] == 8 * n
    v = [s[k * 8:(k + 1) * 8] for k in range(n)]
    k = 2
    while k <= n:
        j = k // 2
        while j >= 1:
            for i in range(n):
                l = i ^ j
                if l > i:
                    if i & k == 0:
                        _sort_pair(v, i, l)
                    else:
                        _sort_pair(v, l, i)
            j //= 2
        k *= 2
    for shift in (4, 2, 1):
        v = [jnp.maximum(v[i], pltpu.roll(v[n - 1 - i], shift, 0)) for i in range(n)]
        _bitonic_merge(v)
    return [x[0:1] for x in v]


def _candidate_rows(a_rows, b_rows):
    assert PEER_TOPK == 16
    a_all = jnp.concatenate(a_rows, axis=0)
    a_low = a_all[0:8]
    row = lax.broadcasted_iota(jnp.int32, a_low.shape, 0)

    def pair(j):
        return a_low + b_rows[j]

    def corner(j):
        return a_rows[0] + b_rows[j]

    t3 = jnp.where(row >= 5, pltpu.roll(pair(4), 5, 0), pair(2))
    t4 = jnp.where(row >= 6, pltpu.roll(pair(6), 6, 0),
                   jnp.where(row >= 4, pltpu.roll(pair(5), 4, 0), pair(3)))
    t5 = pair(7)
    for k in range(6):
        t5 = jnp.where(row >= 2 + k, corner(8 + k), t5)
    t6 = jnp.where(row >= 2, -jnp.inf, jnp.where(row >= 1, corner(15), corner(14)))
    return jnp.concatenate([a_all + b_rows[0], pair(1), t3, t4, t5, t6], axis=0)


def _peer_select(s1, s2):
    a_rows = _top16_rows(s1)
    b_rows = _top16_rows(s2)
    rank2 = jnp.zeros_like(s2)
    for jj in range(PEER_TOPK):
        rank2 = jnp.where(s2 < b_rows[jj], float(jj + 1), rank2)
    best = _top_values(_candidate_rows(a_rows, b_rows), PEER_TOPK)
    norm = jnp.zeros_like(best[0])
    for i in range(PEER_TOPK):
        norm = norm + jnp.exp(best[i] - best[0])
    count = jnp.zeros_like(s1)
    for jj in range(PEER_TOPK):
        count = jnp.where(s1 + b_rows[jj] >= best[PEER_TOPK - 1], float(jj + 1), count)
    e1 = jnp.exp(s1 - a_rows[0]) / norm
    e2 = jnp.exp(s2 - b_rows[0])
    return rank2, e2, count, e1


def _peer_gate_tile(act_ref, coef_ref, first_n1, ns, t, sel_refs):
    rank2_ref, e2_ref, count_ref, e1_ref = sel_refs
    ls = slice(t * LANES, (t + 1) * LANES)
    gates = [None for _ in ns]
    for hd in range(PEER_HEADS):
        rank2 = rank2_ref[hd, t]
        e2 = e2_ref[hd, t]
        for i, n in enumerate(ns):
            count = count_ref[hd, t, pl.ds(first_n1 + n, 1), :]
            e1 = e1_ref[hd, t, pl.ds(first_n1 + n, 1), :]
            term = jnp.where(rank2 < count, e2, 0.0) * e1
            gates[i] = term if gates[i] is None else gates[i] + term
    for i, n in enumerate(ns):
        rs = slice(n * N_SUBKEYS, (n + 1) * N_SUBKEYS)
        x = act_ref[rs, ls]
        inner = x * (GELU_C0 + (GELU_C0 * GELU_C1) * (x * x))
        coef_ref[rs, ls] = ((gates[i] * x) * (1.0 + jnp.tanh(inner))).astype(BF16)


def _peer_kernel(h_ref, wq_ref, sk_ref, u_ref, vt_ref, ln_g_ref, ln_b_ref, y_ref,
                 ht_ref, qt_ref, rank2_ref, e2_ref, count_ref, e1_ref, acc_ref,
                 act0_ref, act1_ref, coef0_ref, coef1_ref, *, alpha, tb, se):
    j = pl.program_id(1)
    ntb = tb // LANES
    nblocks = N_EXPERTS // se
    sel_refs = (rank2_ref, e2_ref, count_ref, e1_ref)
    act_refs = (act0_ref, act1_ref)
    coef_refs = (coef0_ref, coef1_ref)

    @pl.when(j == 0)
    def _():
        ht = h_ref[...].T.astype(BF16)
        ht_ref[...] = ht
        qt_ref[...] = _dot(wq_ref[...], ht).astype(BF16)

        def per_head(hd, carry):
            for t in range(ntb):
                ls = slice(t * LANES, (t + 1) * LANES)
                s = []
                for c in range(2):
                    r0 = pl.multiple_of((hd * 2 + c) * PEER_HALF, PEER_HALF)
                    s.append(_dot(sk_ref[hd * 2 + c], qt_ref[pl.ds(r0, PEER_HALF), ls]))
                rank2, e2, count, e1 = _peer_select(s[0], s[1])
                rank2_ref[hd, t] = rank2
                e2_ref[hd, t] = e2
                count_ref[hd, t] = count
                e1_ref[hd, t] = 0.5 * e1
            return carry

        lax.fori_loop(0, PEER_HEADS, per_head, 0)
        acc_ref[...] = jnp.zeros_like(acc_ref)

    def step_body(q, with_act, with_gate, with_combine):
        piece_t, piece_e, piece_d = tb // PEER_TOKEN_PIECES, se // PEER_ROW_PIECES, acc_ref.shape[0] // PEER_ROW_PIECES
        pieces = []
        for tp in range(PEER_TOKEN_PIECES):
            hs = slice(tp * piece_t, (tp + 1) * piece_t)
            for lo in range(PEER_ROW_PIECES):
                if with_act:
                    pieces.append(("act", hs, slice(lo * piece_e, (lo + 1) * piece_e)))
                if with_combine:
                    pieces.append(("combine", hs, slice(lo * piece_d, (lo + 1) * piece_d)))
        n_sub = se // N_SUBKEYS
        groups = [tuple(range(g, min(g + PEER_GATE_GROUP, n_sub))) for g in range(0, n_sub, PEER_GATE_GROUP)]
        tiles = [(ns, t) for t in range(ntb) for ns in groups] if with_gate else []
        per_piece = -(-len(tiles) // max(len(pieces), 1))
        for k, (kind, hs, es) in enumerate(pieces):
            for n, t in tiles[k * per_piece:(k + 1) * per_piece]:
                _peer_gate_tile(act_refs[1 - q], coef_refs[1 - q], (j - 1) * (se // N_SUBKEYS), n, t, sel_refs)
            if kind == "act":
                act_refs[q][es, hs] = _dot(u_ref[es, :], ht_ref[:, hs])
            else:
                acc_ref[es, hs] += _dot(vt_ref[es, :], coef_refs[q][:, hs])
        for n, t in tiles[len(pieces) * per_piece:]:
            _peer_gate_tile(act_refs[1 - q], coef_refs[1 - q], (j - 1) * (se // N_SUBKEYS), n, t, sel_refs)

    assert nblocks % 2 == 0 and nblocks >= 4
    pl.when(j == 0)(functools.partial(step_body, 0, True, False, False))
    pl.when(j == 1)(functools.partial(step_body, 1, True, True, False))
    for q in range(2):
        pl.when((j >= 2) & (j < nblocks) & (j % 2 == q))(functools.partial(step_body, q, True, True, True))
    pl.when(j == nblocks)(functools.partial(step_body, 0, False, True, True))
    pl.when(j == nblocks + 1)(functools.partial(step_body, 1, False, False, True))

    @pl.when(j == pl.num_programs(1) - 1)
    def _():
        y_ref[...] = _layer_norm(alpha * h_ref[...] + acc_ref[...].T, ln_g_ref[...], ln_b_ref[...])


def _peer(h2d, p, alpha, tb=PEER_TOKEN_BLOCK):
    n, D = h2d.shape
    tb = min(tb, n)
    se = p["peer_vt"].shape[2]
    assert n % tb == 0 and tb % LANES == 0 and N_EXPERTS % se == 0 and se % N_SUBKEYS == 0
    ntb = tb // LANES
    qdim = PEER_HEADS * PEER_KEY_DIM
    nblocks = N_EXPERTS // se
    in_specs = [
        pl.BlockSpec((tb, D), lambda i, j: (i, 0)),
        _full((qdim, D)),
        _full((PEER_HEADS * 2, N_SUBKEYS, PEER_HALF)),
        pl.BlockSpec((se, D), lambda i, j: (jnp.minimum(j, nblocks - 1), 0)),
        pl.BlockSpec((None, D, se), lambda i, j: (jnp.clip(j - 2, 0, nblocks - 1), 0, 0)),
        _full((1, D)), _full((1, D)),
    ]
    per_head = (PEER_HEADS, ntb, N_SUBKEYS, LANES)
    scratch = [
        pltpu.VMEM((D, tb), BF16),
        pltpu.VMEM((qdim, tb), BF16),
        pltpu.VMEM(per_head, F32), pltpu.VMEM(per_head, F32),
        pltpu.VMEM(per_head, F32), pltpu.VMEM(per_head, F32),
        pltpu.VMEM((D, tb), F32),
    ] + [pltpu.VMEM((se, tb), F32)] * 2 + [pltpu.VMEM((se, tb), BF16)] * 2
    return pl.pallas_call(
        functools.partial(_peer_kernel, alpha=alpha, tb=tb, se=se),
        grid=(n // tb, nblocks + 2), in_specs=in_specs,
        out_specs=pl.BlockSpec((tb, D), lambda i, j: (i, 0), pipeline_mode=pl.Buffered(1)),
        out_shape=jax.ShapeDtypeStruct((n, D), F32), scratch_shapes=scratch, name="peer",
        compiler_params=pltpu.CompilerParams(
            dimension_semantics=("arbitrary", "arbitrary"), vmem_limit_bytes=VMEM_LIMIT_BYTES),
    )(h2d, p["peer_wq_t"], p["peer_sk"], p["peer_u"], p["peer_vt"], p["ln2_g"], p["ln2_b"])


def _layer_params(l, w_in, sinks, sgu_ln_g, sgu_ln_b, sgu_w, sgu_b, w_branch_sgu, w_branch_attn, w_out,
                  ln1_g, ln1_b, peer_w_q, peer_sub_keys, peer_u, peer_v, ln2_g, ln2_b):
    return {
        "w_in": w_in[l].astype(BF16),
        "sinks": sinks[l],
        "sgu_ln_g": sgu_ln_g[l][None, :], "sgu_ln_b": sgu_ln_b[l][None, :],
        "sgu_w": sgu_w[l],
        "sgu_b": sgu_b[l], "sgu_bcol": sgu_b[l].T,
        "w_bs": w_branch_sgu[l].astype(BF16), "w_ba": w_branch_attn[l].astype(BF16),
        "w_out": w_out[l].astype(BF16),
        "ln1_g": ln1_g[l][None, :], "ln1_b": ln1_b[l][None, :],
        "peer_wq_t": peer_w_q[l].T.astype(BF16),
        "peer_sk": peer_sub_keys[l].reshape(PEER_HEADS * 2, N_SUBKEYS, PEER_HALF).astype(BF16),
        "peer_u": peer_u[l].astype(BF16),
        "peer_vt": peer_v[l].reshape(N_EXPERTS // PEER_EXPERT_BLOCK, PEER_EXPERT_BLOCK, D_MODEL)
                            .transpose(0, 2, 1).astype(BF16),
        "ln2_g": ln2_g[l][None, :], "ln2_b": ln2_b[l][None, :],
    }


def _prompt_layer(x, p, alpha):
    B, S, D = x.shape
    h, k_last, v_last, sv_last = _prompt_mixer(x, p, alpha)
    y = _peer(h.reshape(B * S, D), p, alpha).reshape(B, S, D)
    return (y, k_last.reshape(B, WINDOW, N_KV_HEADS, HEAD_DIM), v_last.reshape(B, WINDOW, N_KV_HEADS, HEAD_DIM),
            sv_last.reshape(B, CHUNK, N_SGU_GROUPS, SGU_GROUP_DIM))


def _sample_layer(x, cache_k, cache_v, p, alpha):
    B, T, D = x.shape
    x2d = x.reshape(B * T, D)
    q, k_new, v_new, svn, part, gate_b = _sample_pre(x2d, p, T)
    qg = q.reshape(B, T, N_KV_HEADS, Q_PER_KV, HEAD_DIM).transpose(0, 2, 3, 1, 4)
    qg = qg.reshape(B, N_KV_HEADS, Q_PER_KV * T, HEAD_DIM)
    og, new_k, new_v = _sample_attn(qg, k_new.reshape(B, T, D_KV), v_new.reshape(B, T, D_KV),
                                    cache_k.reshape(B, WINDOW, D_KV), cache_v.reshape(B, WINDOW, D_KV),
                                    p["sinks"], T)
    o2d = og.reshape(B, N_KV_HEADS, Q_PER_KV, T, HEAD_DIM).transpose(0, 3, 1, 2, 4).reshape(B * T, D_ATTN)
    h = _sample_post(x2d, part, gate_b, o2d, p, alpha)
    y = _peer(h, p, alpha).reshape(B, T, D)
    return (y, new_k.reshape(B, WINDOW, N_KV_HEADS, HEAD_DIM), new_v.reshape(B, WINDOW, N_KV_HEADS, HEAD_DIM),
            svn.reshape(B, T, N_SGU_GROUPS, SGU_GROUP_DIM))


def kernel(x_prompt, x_sample, cache_k, cache_v, w_in, sinks, sgu_ln_g, sgu_ln_b, sgu_w, sgu_b, w_branch_sgu, w_branch_attn, w_out, ln1_g, ln1_b, peer_w_q, peer_sub_keys, peer_u, peer_v, ln2_g, ln2_b):
    depth = w_in.shape[0]
    alpha = (2.0 * depth) ** 0.25
    yp, ys = x_prompt, x_sample
    outs = [[] for _ in range(6)]
    for l in range(depth):
        p = _layer_params(l, w_in, sinks, sgu_ln_g, sgu_ln_b, sgu_w, sgu_b, w_branch_sgu, w_branch_attn,
                          w_out, ln1_g, ln1_b, peer_w_q, peer_sub_keys, peer_u, peer_v, ln2_g, ln2_b)
        yp, kp, vp, svp = _prompt_layer(yp, p, alpha)
        ys, ksm, vsm, svs = _sample_layer(ys, cache_k[l], cache_v[l], p, alpha)
        for acc, val in zip(outs, (kp, vp, ksm, vsm, svp, svs)):
            acc.append(val)
    return (yp, ys) + tuple(jnp.stack(o) for o in outs)
```

```python
import functools

import jax
import jax.numpy as jnp
from jax import lax
from jax.experimental import pallas as pl
from jax.experimental.pallas import tpu as pltpu

F32 = jnp.float32
BF16 = jnp.bfloat16

D_MODEL = 1024
N_Q_HEADS = 8
N_KV_HEADS = 2
HEAD_DIM = 64
Q_PER_KV = N_Q_HEADS // N_KV_HEADS
D_ATTN = N_Q_HEADS * HEAD_DIM
D_KV = N_KV_HEADS * HEAD_DIM
WINDOW = 128
CHUNK = 128
D_SGU = D_MODEL // 2
N_SGU_GROUPS = 4
SGU_GROUP_DIM = D_SGU // N_SGU_GROUPS
PEER_HEADS = 8
N_SUBKEYS = 128
N_EXPERTS = N_SUBKEYS * N_SUBKEYS
PEER_TOPK = 16
PEER_KEY_DIM = 256
PEER_HALF = PEER_KEY_DIM // 2
LN_EPS = 1e-5
NEG_INF = -1e30
D_IN = D_ATTN + 2 * D_KV + 2 * D_SGU + 2 * D_MODEL
_OFF_Q = 0
_OFF_K = D_ATTN
_OFF_V = _OFF_K + D_KV
_OFF_U = _OFF_V + D_KV
_OFF_SV = _OFF_U + D_SGU
_OFF_GA = _OFF_SV + D_SGU
_OFF_GB = _OFF_GA + D_MODEL

LANES = 128
VMEM_LIMIT_BYTES = 56 * 1024 * 1024

PROMPT_BLOCK = 512
SAMPLE_ATTN_BATCH = 8
PEER_TOKEN_BLOCK = 512
PEER_EXPERT_BLOCK = 1024
PEER_GATE_GROUP = 2
PEER_TOKEN_PIECES = 2
PEER_ROW_PIECES = 1


GELU_C0 = 0.7978845608028654
GELU_C1 = 0.044715


def _gelu(x):
    return 0.5 * x * (1.0 + jnp.tanh(GELU_C0 * (x + GELU_C1 * (x * x * x))))


def _sigmoid(x):
    return 1.0 / (1.0 + jnp.exp(-x))


def _layer_norm(x, g, b):
    mu = jnp.mean(x, -1, keepdims=True)
    xc = x - mu
    var = jnp.mean(xc * xc, -1, keepdims=True)
    return xc * lax.rsqrt(var + LN_EPS) * g + b


def _dot(a, b):
    return jnp.dot(a, b, preferred_element_type=F32)


def _dot_nt(a, b):
    return lax.dot_general(a, b, (((1,), (1,)), ((), ())), preferred_element_type=F32)


def _sink_softmax_pv(logits, sink_col, vals):
    m = jnp.maximum(jnp.max(logits, -1, keepdims=True), sink_col)
    p = jnp.exp(logits - m)
    denom = jnp.sum(p, -1, keepdims=True) + jnp.exp(sink_col - m)
    return _dot(p.astype(BF16), vals) / denom


def _prompt_mixer_kernel(sinks_ref, x_ref, w_in_ref, sgu_g_ref, sgu_b_ref, sgu_w_ref, sgu_bcol_ref,
                         w_bs_ref, w_ba_ref, w_out_ref, ln1_g_ref, ln1_b_ref,
                         h_ref, k_ref, v_ref, sv_ref,
                         kext_ref, vext_ref, sgu_o_ref, attn_o_ref, *, alpha, ts):
    s = pl.program_id(1)
    last = pl.num_programs(1) - 1
    nblk = ts // WINDOW

    x = x_ref[...]
    z = _dot(x.astype(BF16), w_in_ref[...])
    k = z[:, _OFF_K:_OFF_V]
    v = z[:, _OFF_V:_OFF_U]
    u = _gelu(z[:, _OFF_U:_OFF_SV])
    svn = _layer_norm(_gelu(z[:, _OFF_SV:_OFF_GA]), sgu_g_ref[...], sgu_b_ref[...])

    @pl.when(s == 0)
    def _():
        kext_ref[0:WINDOW, :] = jnp.zeros((WINDOW, D_KV), BF16)
        vext_ref[0:WINDOW, :] = jnp.zeros((WINDOW, D_KV), BF16)

    kext_ref[WINDOW:, :] = k.astype(BF16)
    vext_ref[WINDOW:, :] = v.astype(BF16)

    @pl.when(s == last)
    def _():
        k_ref[...] = k[ts - WINDOW:, :]
        v_ref[...] = v[ts - WINDOW:, :]
        sv_ref[...] = svn[ts - CHUNK:, :]

    row = lax.broadcasted_iota(jnp.int32, (CHUNK, CHUNK), 0)
    col = lax.broadcasted_iota(jnp.int32, (CHUNK, CHUNK), 1)
    for g in range(N_SGU_GROUPS):
        wt = jnp.where(col <= row, sgu_w_ref[g], 0.0).astype(BF16)
        bcol = sgu_bcol_ref[:, g:g + 1]
        gs = slice(g * SGU_GROUP_DIM, (g + 1) * SGU_GROUP_DIM)
        for c in range(ts // CHUNK):
            rs = slice(c * CHUNK, (c + 1) * CHUNK)
            mix = _dot(wt, svn[rs, gs].astype(BF16)) + bcol
            sgu_o_ref[rs, gs] = (u[rs, gs] * mix).astype(BF16)

    nq = Q_PER_KV * WINDOW
    kj = lax.broadcasted_iota(jnp.int32, (2 * WINDOW, nq), 0)
    qi = lax.broadcasted_iota(jnp.int32, (2 * WINDOW, nq), 1) % WINDOW
    rel = qi + WINDOW - kj
    bias = jnp.where((rel >= 0) & (rel < WINDOW), 0.0, NEG_INF).astype(F32)
    bias_first = jnp.where(s > 0, bias, jnp.where(kj < WINDOW, NEG_INF, bias))
    scale = HEAD_DIM ** -0.5
    for i in range(nblk):
        qs = slice(i * WINDOW, (i + 1) * WINDOW)
        outs = []
        for g in range(N_KV_HEADS):
            ds = slice(g * HEAD_DIM, (g + 1) * HEAD_DIM)
            kb = kext_ref[i * WINDOW:(i + 2) * WINDOW, ds]
            vb = vext_ref[i * WINDOW:(i + 2) * WINDOW, ds]
            q4 = jnp.concatenate(
                [z[qs, (g * Q_PER_KV + r) * HEAD_DIM:(g * Q_PER_KV + r + 1) * HEAD_DIM]
                 for r in range(Q_PER_KV)], axis=0).astype(BF16)
            logits = _dot_nt(kb, q4) * scale + (bias_first if i == 0 else bias)
            sink = jnp.concatenate(
                [jnp.full((1, WINDOW), sinks_ref[g * Q_PER_KV + r], F32) for r in range(Q_PER_KV)], axis=1)
            m = jnp.maximum(jnp.max(logits, 0, keepdims=True), sink)
            p = jnp.exp(logits - m)
            denom = jnp.sum(p, 0, keepdims=True) + jnp.exp(sink - m)
            pv = lax.dot_general(vb, p.astype(BF16), (((0,), (0,)), ((), ())), preferred_element_type=F32)
            outs.append(pv / denom)
        o_all = jnp.concatenate(outs, axis=0).T
        heads = [o_all[r * WINDOW:(r + 1) * WINDOW, g * HEAD_DIM:(g + 1) * HEAD_DIM]
                 for g in range(N_KV_HEADS) for r in range(Q_PER_KV)]
        attn_o_ref[qs, :] = jnp.concatenate(heads, axis=1).astype(BF16)

    kext_ref[0:WINDOW, :] = kext_ref[ts:ts + WINDOW, :]
    vext_ref[0:WINDOW, :] = vext_ref[ts:ts + WINDOW, :]

    merged = (_sigmoid(z[:, _OFF_GA:_OFF_GB]) * _dot(sgu_o_ref[...], w_bs_ref[...])
              + _sigmoid(z[:, _OFF_GB:]) * _dot(attn_o_ref[...], w_ba_ref[...]))
    y = _dot(merged.astype(BF16), w_out_ref[...])
    h_ref[...] = _layer_norm(alpha * x + y, ln1_g_ref[...], ln1_b_ref[...])


def _full(shape):
    return pl.BlockSpec(shape, lambda *_: (0,) * len(shape), pipeline_mode=pl.Buffered(1))


def _prompt_mixer(x, p, alpha, ts=PROMPT_BLOCK):
    B, S, D = x.shape
    ts = min(ts, S)
    assert S % ts == 0 and ts % WINDOW == 0 and ts % CHUNK == 0
    grid = (B, S // ts)
    smem = pl.BlockSpec(memory_space=pltpu.SMEM)
    in_specs = [
        smem,
        pl.BlockSpec((None, ts, D), lambda b, s: (b, s, 0)),
        _full((D, D_IN)), _full((1, D_SGU)), _full((1, D_SGU)),
        _full((N_SGU_GROUPS, CHUNK, CHUNK)), _full((CHUNK, N_SGU_GROUPS)),
        _full((D_SGU, D)), _full((D_ATTN, D)), _full((D, D)), _full((1, D)), _full((1, D)),
    ]
    out_specs = [
        pl.BlockSpec((None, ts, D), lambda b, s: (b, s, 0)),
        pl.BlockSpec((None, WINDOW, D_KV), lambda b, s: (b, 0, 0)),
        pl.BlockSpec((None, WINDOW, D_KV), lambda b, s: (b, 0, 0)),
        pl.BlockSpec((None, CHUNK, D_SGU), lambda b, s: (b, 0, 0)),
    ]
    out_shape = [
        jax.ShapeDtypeStruct((B, S, D), F32),
        jax.ShapeDtypeStruct((B, WINDOW, D_KV), F32),
        jax.ShapeDtypeStruct((B, WINDOW, D_KV), F32),
        jax.ShapeDtypeStruct((B, CHUNK, D_SGU), F32),
    ]
    scratch = [
        pltpu.VMEM((ts + WINDOW, D_KV), BF16),
        pltpu.VMEM((ts + WINDOW, D_KV), BF16),
        pltpu.VMEM((ts, D_SGU), BF16),
        pltpu.VMEM((ts, D_ATTN), BF16),
    ]
    return pl.pallas_call(
        functools.partial(_prompt_mixer_kernel, alpha=alpha, ts=ts),
        grid=grid, in_specs=in_specs, out_specs=out_specs, out_shape=out_shape,
        scratch_shapes=scratch, name="prompt_mixer",
        compiler_params=pltpu.CompilerParams(
            dimension_semantics=("arbitrary", "arbitrary"), vmem_limit_bytes=VMEM_LIMIT_BYTES),
    )(p["sinks"], x, p["w_in"], p["sgu_ln_g"], p["sgu_ln_b"], p["sgu_w"], p["sgu_bcol"],
      p["w_bs"], p["w_ba"], p["w_out"], p["ln1_g"], p["ln1_b"])


def _sample_pre_kernel(sgu_w_ref, sgu_b_ref, x_ref, w_in_ref, sgu_g_ref, sgu_bb_ref, w_bs_ref,
                       q_ref, k_ref, v_ref, sv_ref, part_ref, gate_b_ref, *, t_new):
    x = x_ref[...]
    n = x.shape[0]
    z = _dot(x.astype(BF16), w_in_ref[...])
    q_ref[...] = z[:, _OFF_Q:_OFF_K]
    k_ref[...] = z[:, _OFF_K:_OFF_V]
    v_ref[...] = z[:, _OFF_V:_OFF_U]
    u = _gelu(z[:, _OFF_U:_OFF_SV])
    svn = _layer_norm(_gelu(z[:, _OFF_SV:_OFF_GA]), sgu_g_ref[...], sgu_bb_ref[...])
    sv_ref[...] = svn
    t_of_row = lax.broadcasted_iota(jnp.int32, (n, SGU_GROUP_DIM), 0) % t_new
    pieces = []
    for g in range(N_SGU_GROUPS):
        vg = svn[:, g * SGU_GROUP_DIM:(g + 1) * SGU_GROUP_DIM]
        mix = jnp.zeros((n, SGU_GROUP_DIM), F32)
        for t in range(t_new):
            mix = jnp.where(t_of_row == t, sgu_b_ref[g, t], mix)
        for d in range(t_new):
            coef = jnp.zeros((n, SGU_GROUP_DIM), F32)
            for t in range(d, t_new):
                coef = jnp.where(t_of_row == t, sgu_w_ref[g, t * t_new + t - d], coef)
            shifted = vg if d == 0 else pltpu.roll(vg, d, 0)
            mix = mix + coef * shifted
        pieces.append(u[:, g * SGU_GROUP_DIM:(g + 1) * SGU_GROUP_DIM] * mix)
    sgu_o = jnp.concatenate(pieces, axis=1).astype(BF16)
    part_ref[...] = _sigmoid(z[:, _OFF_GA:_OFF_GB]) * _dot(sgu_o, w_bs_ref[...])
    gate_b_ref[...] = _sigmoid(z[:, _OFF_GB:])


def _sample_pre(x2d, p, t_new):
    n, D = x2d.shape
    smem = pl.BlockSpec(memory_space=pltpu.SMEM)
    out_shape = [
        jax.ShapeDtypeStruct((n, D_ATTN), F32),
        jax.ShapeDtypeStruct((n, D_KV), F32),
        jax.ShapeDtypeStruct((n, D_KV), F32),
        jax.ShapeDtypeStruct((n, D_SGU), F32),
        jax.ShapeDtypeStruct((n, D), F32),
        jax.ShapeDtypeStruct((n, D), F32),
    ]
    return pl.pallas_call(
        functools.partial(_sample_pre_kernel, t_new=t_new),
        in_specs=[smem, smem] + [pl.BlockSpec(memory_space=pltpu.VMEM)] * 5,
        out_specs=[pl.BlockSpec(memory_space=pltpu.VMEM)] * 6,
        out_shape=out_shape, name="sample_pre",
        compiler_params=pltpu.CompilerParams(vmem_limit_bytes=VMEM_LIMIT_BYTES),
    )(p["sgu_w"][:, :t_new, :t_new].reshape(N_SGU_GROUPS, t_new * t_new), p["sgu_b"],
      x2d, p["w_in"], p["sgu_ln_g"], p["sgu_ln_b"], p["w_bs"])


def _sample_attn_kernel(sinks_ref, q_ref, kn_ref, vn_ref, ck_ref, cv_ref,
                        o_ref, nk_ref, nv_ref, kbuf_ref, vbuf_ref, *, t_new, bb):
    nq = Q_PER_KV * t_new
    nkeys = WINDOW + 8
    row = lax.broadcasted_iota(jnp.int32, (nq, nkeys), 0)
    kj = lax.broadcasted_iota(jnp.int32, (nq, nkeys), 1)
    t_q = row % t_new
    bias = jnp.where((kj > t_q) & (kj <= t_q + WINDOW), 0.0, NEG_INF).astype(F32)
    r_of_row = lax.broadcasted_iota(jnp.int32, (nq, 1), 0) // t_new
    scale = HEAD_DIM ** -0.5
    pad = jnp.zeros((8 - t_new, D_KV), F32)
    for b in range(bb):
        kbuf_ref[0:WINDOW, :] = ck_ref[b]
        kbuf_ref[WINDOW:WINDOW + t_new, :] = kn_ref[b]
        kbuf_ref[WINDOW + t_new:, :] = pad
        vbuf_ref[0:WINDOW, :] = cv_ref[b]
        vbuf_ref[WINDOW:WINDOW + t_new, :] = vn_ref[b]
        vbuf_ref[WINDOW + t_new:, :] = pad
        nk_ref[b] = kbuf_ref[t_new:t_new + WINDOW, :]
        nv_ref[b] = vbuf_ref[t_new:t_new + WINDOW, :]
        for g in range(N_KV_HEADS):
            ds = slice(g * HEAD_DIM, (g + 1) * HEAD_DIM)
            kg = kbuf_ref[:, ds].astype(BF16)
            vg = vbuf_ref[:, ds].astype(BF16)
            logits = _dot_nt(q_ref[b, g].astype(BF16), kg) * scale + bias
            sink_col = jnp.zeros((nq, 1), F32)
            for r in range(Q_PER_KV):
                sink_col = jnp.where(r_of_row == r, sinks_ref[g * Q_PER_KV + r], sink_col)
            o_ref[b, g] = _sink_softmax_pv(logits, sink_col, vg)


def _sample_attn(q_grouped, k_new, v_new, cache_k, cache_v, sinks, t_new, bb=SAMPLE_ATTN_BATCH):
    nb = q_grouped.shape[0]
    bb = min(bb, nb)
    assert nb % bb == 0 and t_new <= 8
    nq = Q_PER_KV * t_new
    smem = pl.BlockSpec(memory_space=pltpu.SMEM)
    in_specs = [
        smem,
        pl.BlockSpec((bb, N_KV_HEADS, nq, HEAD_DIM), lambda i: (i, 0, 0, 0)),
        pl.BlockSpec((bb, t_new, D_KV), lambda i: (i, 0, 0)),
        pl.BlockSpec((bb, t_new, D_KV), lambda i: (i, 0, 0)),
        pl.BlockSpec((bb, WINDOW, D_KV), lambda i: (i, 0, 0)),
        pl.BlockSpec((bb, WINDOW, D_KV), lambda i: (i, 0, 0)),
    ]
    out_specs = [
        pl.BlockSpec((bb, N_KV_HEADS, nq, HEAD_DIM), lambda i: (i, 0, 0, 0)),
        pl.BlockSpec((bb, WINDOW, D_KV), lambda i: (i, 0, 0)),
        pl.BlockSpec((bb, WINDOW, D_KV), lambda i: (i, 0, 0)),
    ]
    out_shape = [
        jax.ShapeDtypeStruct((nb, N_KV_HEADS, nq, HEAD_DIM), F32),
        jax.ShapeDtypeStruct((nb, WINDOW, D_KV), F32),
        jax.ShapeDtypeStruct((nb, WINDOW, D_KV), F32),
    ]
    return pl.pallas_call(
        functools.partial(_sample_attn_kernel, t_new=t_new, bb=bb),
        grid=(nb // bb,), in_specs=in_specs, out_specs=out_specs, out_shape=out_shape,
        scratch_shapes=[pltpu.VMEM((WINDOW + 8, D_KV), F32), pltpu.VMEM((WINDOW + 8, D_KV), F32)],
        name="sample_attn",
        compiler_params=pltpu.CompilerParams(dimension_semantics=("arbitrary",)),
    )(sinks, q_grouped, k_new, v_new, cache_k, cache_v)


def _sample_post_kernel(x_ref, part_ref, gate_b_ref, o_ref, w_ba_ref, w_out_ref, ln1_g_ref, ln1_b_ref,
                        h_ref, *, alpha):
    merged = part_ref[...] + gate_b_ref[...] * _dot(o_ref[...].astype(BF16), w_ba_ref[...])
    y = _dot(merged.astype(BF16), w_out_ref[...])
    h_ref[...] = _layer_norm(alpha * x_ref[...] + y, ln1_g_ref[...], ln1_b_ref[...])


def _sample_post(x2d, part, gate_b, o2d, p, alpha):
    return pl.pallas_call(
        functools.partial(_sample_post_kernel, alpha=alpha),
        in_specs=[pl.BlockSpec(memory_space=pltpu.VMEM)] * 8,
        out_specs=pl.BlockSpec(memory_space=pltpu.VMEM),
        out_shape=jax.ShapeDtypeStruct(x2d.shape, F32), name="sample_post",
        compiler_params=pltpu.CompilerParams(vmem_limit_bytes=VMEM_LIMIT_BYTES),
    )(x2d, part, gate_b, o2d, p["w_ba"], p["w_out"], p["ln1_g"], p["ln1_b"])


def _top_values(work, count):
    vals = []
    for i in range(count):
        top = jnp.max(work, axis=0, keepdims=True)
        vals.append(top)
        if i + 1 < count:
            work = jnp.where(work == top, -jnp.inf, work)
    return vals


def _sort_pair(v, i, j):
    v[i], v[j] = jnp.maximum(v[i], v[j]), jnp.minimum(v[i], v[j])


def _bitonic_merge(v):
    j = len(v) // 2
    while j >= 1:
        for i in range(len(v)):
            if i ^ j > i:
                _sort_pair(v, i, i ^ j)
        j //= 2


def _top16_rows(s):
    n = PEER_TOPK
    assert s.shape[0] == 8 * n
    v = [s[k * 8:(k + 1) * 8] for k in range(n)]
    k = 2
    while k <= n:
        j = k // 2
        while j >= 1:
            for i in range(n):
                l = i ^ j
                if l > i:
                    if i & k == 0:
                        _sort_pair(v, i, l)
                    else:
                        _sort_pair(v, l, i)
            j //= 2
        k *= 2
    for shift in (4, 2, 1):
        v = [jnp.maximum(v[i], pltpu.roll(v[n - 1 - i], shift, 0)) for i in range(n)]
        _bitonic_merge(v)
    return [x[0:1] for x in v]


def _candidate_rows(a_rows, b_rows):
    assert PEER_TOPK == 16
    a_all = jnp.concatenate(a_rows, axis=0)
    a_low = a_all[0:8]
    row = lax.broadcasted_iota(jnp.int32, a_low.shape, 0)

    def pair(j):
        return a_low + b_rows[j]

    def corner(j):
        return a_rows[0] + b_rows[j]

    t3 = jnp.where(row >= 5, pltpu.roll(pair(4), 5, 0), pair(2))
    t4 = jnp.where(row >= 6, pltpu.roll(pair(6), 6, 0),
                   jnp.where(row >= 4, pltpu.roll(pair(5), 4, 0), pair(3)))
    t5 = pair(7)
    for k in range(6):
        t5 = jnp.where(row >= 2 + k, corner(8 + k), t5)
    t6 = jnp.where(row >= 2, -jnp.inf, jnp.where(row >= 1, corner(15), corner(14)))
    return jnp.concatenate([a_all + b_rows[0], pair(1), t3, t4, t5, t6], axis=0)


def _peer_select(s1, s2):
    a_rows = _top16_rows(s1)
    b_rows = _top16_rows(s2)
    rank2 = jnp.zeros_like(s2)
    for jj in range(PEER_TOPK):
        rank2 = jnp.where(s2 < b_rows[jj], float(jj + 1), rank2)
    best = _top_values(_candidate_rows(a_rows, b_rows), PEER_TOPK)
    norm = jnp.zeros_like(best[0])
    for i in range(PEER_TOPK):
        norm = norm + jnp.exp(best[i] - best[0])
    count = jnp.zeros_like(s1)
    for jj in range(PEER_TOPK):
        count = jnp.where(s1 + b_rows[jj] >= best[PEER_TOPK - 1], float(jj + 1), count)
    e1 = jnp.exp(s1 - a_rows[0]) / norm
    e2 = jnp.exp(s2 - b_rows[0])
    return rank2, e2, count, e1


def _peer_gate_tile(act_ref, coef_ref, first_n1, ns, t, sel_refs):
    rank2_ref, e2_ref, count_ref, e1_ref = sel_refs
    ls = slice(t * LANES, (t + 1) * LANES)
    gates = [None for _ in ns]
    for hd in range(PEER_HEADS):
        rank2 = rank2_ref[hd, t]
        e2 = e2_ref[hd, t]
        for i, n in enumerate(ns):
            count = count_ref[hd, t, pl.ds(first_n1 + n, 1), :]
            e1 = e1_ref[hd, t, pl.ds(first_n1 + n, 1), :]
            term = jnp.where(rank2 < count, e2, 0.0) * e1
            gates[i] = term if gates[i] is None else gates[i] + term
    for i, n in enumerate(ns):
        rs = slice(n * N_SUBKEYS, (n + 1) * N_SUBKEYS)
        x = act_ref[rs, ls]
        inner = x * (GELU_C0 + (GELU_C0 * GELU_C1) * (x * x))
        coef_ref[rs, ls] = ((gates[i] * x) * (1.0 + jnp.tanh(inner))).astype(BF16)


def _peer_kernel(h_ref, wq_ref, sk_ref, u_ref, vt_ref, ln_g_ref, ln_b_ref, y_ref,
                 ht_ref, qt_ref, rank2_ref, e2_ref, count_ref, e1_ref, acc_ref,
                 act0_ref, act1_ref, coef0_ref, coef1_ref, *, alpha, tb, se):
    j = pl.program_id(1)
    ntb = tb // LANES
    nblocks = N_EXPERTS // se
    sel_refs = (rank2_ref, e2_ref, count_ref, e1_ref)
    act_refs = (act0_ref, act1_ref)
    coef_refs = (coef0_ref, coef1_ref)

    @pl.when(j == 0)
    def _():
        ht = h_ref[...].T.astype(BF16)
        ht_ref[...] = ht
        qt_ref[...] = _dot(wq_ref[...], ht).astype(BF16)

        def per_head(hd, carry):
            for t in range(ntb):
                ls = slice(t * LANES, (t + 1) * LANES)
                s = []
                for c in range(2):
                    r0 = pl.multiple_of((hd * 2 + c) * PEER_HALF, PEER_HALF)
                    s.append(_dot(sk_ref[hd * 2 + c], qt_ref[pl.ds(r0, PEER_HALF), ls]))
                rank2, e2, count, e1 = _peer_select(s[0], s[1])
                rank2_ref[hd, t] = rank2
                e2_ref[hd, t] = e2
                count_ref[hd, t] = count
                e1_ref[hd, t] = 0.5 * e1
            return carry

        lax.fori_loop(0, PEER_HEADS, per_head, 0)
        acc_ref[...] = jnp.zeros_like(acc_ref)

    def step_body(q, with_act, with_gate, with_combine):
        piece_t, piece_e, piece_d = tb // PEER_TOKEN_PIECES, se // PEER_ROW_PIECES, acc_ref.shape[0] // PEER_ROW_PIECES
        pieces = []
        for tp in range(PEER_TOKEN_PIECES):
            hs = slice(tp * piece_t, (tp + 1) * piece_t)
            for lo in range(PEER_ROW_PIECES):
                if with_act:
                    pieces.append(("act", hs, slice(lo * piece_e, (lo + 1) * piece_e)))
                if with_combine:
                    pieces.append(("combine", hs, slice(lo * piece_d, (lo + 1) * piece_d)))
        n_sub = se // N_SUBKEYS
        groups = [tuple(range(g, min(g + PEER_GATE_GROUP, n_sub))) for g in range(0, n_sub, PEER_GATE_GROUP)]
        tiles = [(ns, t) for t in range(ntb) for ns in groups] if with_gate else []
        per_piece = -(-len(tiles) // max(len(pieces), 1))
        for k, (kind, hs, es) in enumerate(pieces):
            for n, t in tiles[k * per_piece:(k + 1) * per_piece]:
                _peer_gate_tile(act_refs[1 - q], coef_refs[1 - q], (j - 1) * (se // N_SUBKEYS), n, t, sel_refs)
            if kind == "act":
                act_refs[q][es, hs] = _dot(u_ref[es, :], ht_ref[:, hs])
            else:
                acc_ref[es, hs] += _dot(vt_ref[es, :], coef_refs[q][:, hs])
        for n, t in tiles[len(pieces) * per_piece:]:
            _peer_gate_tile(act_refs[1 - q], coef_refs[1 - q], (j - 1) * (se // N_SUBKEYS), n, t, sel_refs)

    assert nblocks % 2 == 0 and nblocks >= 4
    pl.when(j == 0)(functools.partial(step_body, 0, True, False, False))
    pl.when(j == 1)(functools.partial(step_body, 1, True, True, False))
    for q in range(2):
        pl.when((j >= 2) & (j < nblocks) & (j % 2 == q))(functools.partial(step_body, q, True, True, True))
    pl.when(j == nblocks)(functools.partial(step_body, 0, False, True, True))
    pl.when(j == nblocks + 1)(functools.partial(step_body, 1, False, False, True))

    @pl.when(j == pl.num_programs(1) - 1)
    def _():
        y_ref[...] = _layer_norm(alpha * h_ref[...] + acc_ref[...].T, ln_g_ref[...], ln_b_ref[...])


def _peer(h2d, p, alpha, tb=PEER_TOKEN_BLOCK):
    n, D = h2d.shape
    tb = min(tb, n)
    se = p["peer_vt"].shape[2]
    assert n % tb == 0 and tb % LANES == 0 and N_EXPERTS % se == 0 and se % N_SUBKEYS == 0
    ntb = tb // LANES
    qdim = PEER_HEADS * PEER_KEY_DIM
    nblocks = N_EXPERTS // se
    in_specs = [
        pl.BlockSpec((tb, D), lambda i, j: (i, 0)),
        _full((qdim, D)),
        _full((PEER_HEADS * 2, N_SUBKEYS, PEER_HALF)),
        pl.BlockSpec((se, D), lambda i, j: (jnp.minimum(j, nblocks - 1), 0)),
        pl.BlockSpec((None, D, se), lambda i, j: (jnp.clip(j - 2, 0, nblocks - 1), 0, 0)),
        _full((1, D)), _full((1, D)),
    ]
    per_head = (PEER_HEADS, ntb, N_SUBKEYS, LANES)
    scratch = [
        pltpu.VMEM((D, tb), BF16),
        pltpu.VMEM((qdim, tb), BF16),
        pltpu.VMEM(per_head, F32), pltpu.VMEM(per_head, F32),
        pltpu.VMEM(per_head, F32), pltpu.VMEM(per_head, F32),
        pltpu.VMEM((D, tb), F32),
    ] + [pltpu.VMEM((se, tb), F32)] * 2 + [pltpu.VMEM((se, tb), BF16)] * 2
    return pl.pallas_call(
        functools.partial(_peer_kernel, alpha=alpha, tb=tb, se=se),
        grid=(n // tb, nblocks + 2), in_specs=in_specs,
        out_specs=pl.BlockSpec((tb, D), lambda i, j: (i, 0), pipeline_mode=pl.Buffered(1)),
        out_shape=jax.ShapeDtypeStruct((n, D), F32), scratch_shapes=scratch, name="peer",
        compiler_params=pltpu.CompilerParams(
            dimension_semantics=("arbitrary", "arbitrary"), vmem_limit_bytes=VMEM_LIMIT_BYTES),
    )(h2d, p["peer_wq_t"], p["peer_sk"], p["peer_u"], p["peer_vt"], p["ln2_g"], p["ln2_b"])


def _layer_params(l, w_in, sinks, sgu_ln_g, sgu_ln_b, sgu_w, sgu_b, w_branch_sgu, w_branch_attn, w_out,
                  ln1_g, ln1_b, peer_w_q, peer_sub_keys, peer_u, peer_v, ln2_g, ln2_b):
    return {
        "w_in": w_in[l].astype(BF16),
        "sinks": sinks[l],
        "sgu_ln_g": sgu_ln_g[l][None, :], "sgu_ln_b": sgu_ln_b[l][None, :],
        "sgu_w": sgu_w[l],
        "sgu_b": sgu_b[l], "sgu_bcol": sgu_b[l].T,
        "w_bs": w_branch_sgu[l].astype(BF16), "w_ba": w_branch_attn[l].astype(BF16),
        "w_out": w_out[l].astype(BF16),
        "ln1_g": ln1_g[l][None, :], "ln1_b": ln1_b[l][None, :],
        "peer_wq_t": peer_w_q[l].T.astype(BF16),
        "peer_sk": peer_sub_keys[l].reshape(PEER_HEADS * 2, N_SUBKEYS, PEER_HALF).astype(BF16),
        "peer_u": peer_u[l].astype(BF16),
        "peer_vt": peer_v[l].reshape(N_EXPERTS // PEER_EXPERT_BLOCK, PEER_EXPERT_BLOCK, D_MODEL)
                            .transpose(0, 2, 1).astype(BF16),
        "ln2_g": ln2_g[l][None, :], "ln2_b": ln2_b[l][None, :],
    }


def _prompt_layer(x, p, alpha):
    B, S, D = x.shape
    h, k_last, v_last, sv_last = _prompt_mixer(x, p, alpha)
    y = _peer(h.reshape(B * S, D), p, alpha).reshape(B, S, D)
    return (y, k_last.reshape(B, WINDOW, N_KV_HEADS, HEAD_DIM), v_last.reshape(B, WINDOW, N_KV_HEADS, HEAD_DIM),
            sv_last.reshape(B, CHUNK, N_SGU_GROUPS, SGU_GROUP_DIM))


def _sample_layer(x, cache_k, cache_v, p, alpha):
    B, T, D = x.shape
    x2d = x.reshape(B * T, D)
    q, k_new, v_new, svn, part, gate_b = _sample_pre(x2d, p, T)
    qg = q.reshape(B, T, N_KV_HEADS, Q_PER_KV, HEAD_DIM).transpose(0, 2, 3, 1, 4)
    qg = qg.reshape(B, N_KV_HEADS, Q_PER_KV * T, HEAD_DIM)
    og, new_k, new_v = _sample_attn(qg, k_new.reshape(B, T, D_KV), v_new.reshape(B, T, D_KV),
                                    cache_k.reshape(B, WINDOW, D_KV), cache_v.reshape(B, WINDOW, D_KV),
                                    p["sinks"], T)
    o2d = og.reshape(B, N_KV_HEADS, Q_PER_KV, T, HEAD_DIM).transpose(0, 3, 1, 2, 4).reshape(B * T, D_ATTN)
    h = _sample_post(x2d, part, gate_b, o2d, p, alpha)
    y = _peer(h, p, alpha).reshape(B, T, D)
    return (y, new_k.reshape(B, WINDOW, N_KV_HEADS, HEAD_DIM), new_v.reshape(B, WINDOW, N_KV_HEADS, HEAD_DIM),
            svn.reshape(B, T, N_SGU_GROUPS, SGU_GROUP_DIM))


def kernel(x_prompt, x_sample, cache_k, cache_v, w_in, sinks, sgu_ln_g, sgu_ln_b, sgu_w, sgu_b, w_branch_sgu, w_branch_attn, w_out, ln1_g, ln1_b, peer_w_q, peer_sub_keys, peer_u, peer_v, ln2_g, ln2_b):
    depth = w_in.shape[0]
    alpha = (2.0 * depth) ** 0.25
    yp, ys = x_prompt, x_sample
    outs = [[] for _ in range(6)]
    for l in range(depth):
        p = _layer_params(l, w_in, sinks, sgu_ln_g, sgu_ln_b, sgu_w, sgu_b, w_branch_sgu, w_branch_attn,
                          w_out, ln1_g, ln1_b, peer_w_q, peer_sub_keys, peer_u, peer_v, ln2_g, ln2_b)
        yp, kp, vp, svp = _prompt_layer(yp, p, alpha)
        ys, ksm, vsm, svs = _sample_layer(ys, cache_k[l], cache_v[l], p, alpha)
        for acc, val in zip(outs, (kp, vp, ksm, vsm, svp, svs)):
            acc.append(val)
    return (yp, ys) + tuple(jnp.stack(o) for o in outs)
```

```python
import functools

import jax
import jax.numpy as jnp
from jax import lax
from jax.experimental import pallas as pl
from jax.experimental.pallas import tpu as pltpu

F32 = jnp.float32
BF16 = jnp.bfloat16

D_MODEL = 1024
N_Q_HEADS = 8
N_KV_HEADS = 2
HEAD_DIM = 64
Q_PER_KV = N_Q_HEADS // N_KV_HEADS
D_ATTN = N_Q_HEADS * HEAD_DIM
D_KV = N_KV_HEADS * HEAD_DIM
WINDOW = 128
CHUNK = 128
D_SGU = D_MODEL // 2
N_SGU_GROUPS = 4
SGU_GROUP_DIM = D_SGU // N_SGU_GROUPS
PEER_HEADS = 8
N_SUBKEYS = 128
N_EXPERTS = N_SUBKEYS * N_SUBKEYS
PEER_TOPK = 16
PEER_KEY_DIM = 256
PEER_HALF = PEER_KEY_DIM // 2
LN_EPS = 1e-5
NEG_INF = -1e30
D_IN = D_ATTN + 2 * D_KV + 2 * D_SGU + 2 * D_MODEL
_OFF_Q = 0
_OFF_K = D_ATTN
_OFF_V = _OFF_K + D_KV
_OFF_U = _OFF_V + D_KV
_OFF_SV = _OFF_U + D_SGU
_OFF_GA = _OFF_SV + D_SGU
_OFF_GB = _OFF_GA + D_MODEL

LANES = 128
VMEM_LIMIT_BYTES = 56 * 1024 * 1024

PROMPT_BLOCK = 1024
SAMPLE_ATTN_BATCH = 8
PEER_TOKEN_BLOCK = 512
PEER_EXPERT_BLOCK = 1024
PEER_GATE_GROUP = 1
PEER_TOKEN_PIECES = 2
PEER_ROW_PIECES = 1


GELU_C0 = 0.7978845608028654
GELU_C1 = 0.044715


def _gelu(x):
    return 0.5 * x * (1.0 + jnp.tanh(GELU_C0 * (x + GELU_C1 * (x * x * x))))


def _sigmoid(x):
    return 1.0 / (1.0 + jnp.exp(-x))


def _layer_norm(x, g, b):
    mu = jnp.mean(x, -1, keepdims=True)
    xc = x - mu
    var = jnp.mean(xc * xc, -1, keepdims=True)
    return xc * lax.rsqrt(var + LN_EPS) * g + b


def _dot(a, b):
    return jnp.dot(a, b, preferred_element_type=F32)


def _dot_nt(a, b):
    return lax.dot_general(a, b, (((1,), (1,)), ((), ())), preferred_element_type=F32)


def _sink_softmax_pv(logits, sink_col, vals):
    m = jnp.maximum(jnp.max(logits, -1, keepdims=True), sink_col)
    p = jnp.exp(logits - m)
    denom = jnp.sum(p, -1, keepdims=True) + jnp.exp(sink_col - m)
    return _dot(p.astype(BF16), vals) / denom


def _prompt_mixer_kernel(sinks_ref, x_ref, w_in_ref, sgu_g_ref, sgu_b_ref, sgu_w_ref, sgu_bcol_ref,
                         w_bs_ref, w_ba_ref, w_out_ref, ln1_g_ref, ln1_b_ref,
                         h_ref, k_ref, v_ref, sv_ref,
                         kext_ref, vext_ref, sgu_o_ref, attn_o_ref, *, alpha, ts):
    s = pl.program_id(1)
    last = pl.num_programs(1) - 1
    nblk = ts // WINDOW

    x = x_ref[...]
    z = _dot(x.astype(BF16), w_in_ref[...])
    k = z[:, _OFF_K:_OFF_V]
    v = z[:, _OFF_V:_OFF_U]
    u = _gelu(z[:, _OFF_U:_OFF_SV])
    svn = _layer_norm(_gelu(z[:, _OFF_SV:_OFF_GA]), sgu_g_ref[...], sgu_b_ref[...])

    @pl.when(s == 0)
    def _():
        kext_ref[0:WINDOW, :] = jnp.zeros((WINDOW, D_KV), BF16)
        vext_ref[0:WINDOW, :] = jnp.zeros((WINDOW, D_KV), BF16)

    kext_ref[WINDOW:, :] = k.astype(BF16)
    vext_ref[WINDOW:, :] = v.astype(BF16)

    @pl.when(s == last)
    def _():
        k_ref[...] = k[ts - WINDOW:, :]
        v_ref[...] = v[ts - WINDOW:, :]
        sv_ref[...] = svn[ts - CHUNK:, :]

    row = lax.broadcasted_iota(jnp.int32, (CHUNK, CHUNK), 0)
    col = lax.broadcasted_iota(jnp.int32, (CHUNK, CHUNK), 1)
    for g in range(N_SGU_GROUPS):
        wt = jnp.where(col <= row, sgu_w_ref[g], 0.0).astype(BF16)
        bcol = sgu_bcol_ref[:, g:g + 1]
        gs = slice(g * SGU_GROUP_DIM, (g + 1) * SGU_GROUP_DIM)
        for c in range(ts // CHUNK):
            rs = slice(c * CHUNK, (c + 1) * CHUNK)
            mix = _dot(wt, svn[rs, gs].astype(BF16)) + bcol
            sgu_o_ref[rs, gs] = (u[rs, gs] * mix).astype(BF16)

    nq = Q_PER_KV * WINDOW
    kj = lax.broadcasted_iota(jnp.int32, (2 * WINDOW, nq), 0)
    qi = lax.broadcasted_iota(jnp.int32, (2 * WINDOW, nq), 1) % WINDOW
    rel = qi + WINDOW - kj
    bias = jnp.where((rel >= 0) & (rel < WINDOW), 0.0, NEG_INF).astype(F32)
    bias_first = jnp.where(s > 0, bias, jnp.where(kj < WINDOW, NEG_INF, bias))
    scale = HEAD_DIM ** -0.5
    for i in range(nblk):
        qs = slice(i * WINDOW, (i + 1) * WINDOW)
        outs = []
        for g in range(N_KV_HEADS):
            ds = slice(g * HEAD_DIM, (g + 1) * HEAD_DIM)
            kb = kext_ref[i * WINDOW:(i + 2) * WINDOW, ds]
            vb = vext_ref[i * WINDOW:(i + 2) * WINDOW, ds]
            q4 = jnp.concatenate(
                [z[qs, (g * Q_PER_KV + r) * HEAD_DIM:(g * Q_PER_KV + r + 1) * HEAD_DIM]
                 for r in range(Q_PER_KV)], axis=0).astype(BF16)
            logits = _dot_nt(kb, q4) * scale + (bias_first if i == 0 else bias)
            sink = jnp.concatenate(
                [jnp.full((1, WINDOW), sinks_ref[g * Q_PER_KV + r], F32) for r in range(Q_PER_KV)], axis=1)
            m = jnp.maximum(jnp.max(logits, 0, keepdims=True), sink)
            p = jnp.exp(logits - m)
            denom = jnp.sum(p, 0, keepdims=True) + jnp.exp(sink - m)
            pv = lax.dot_general(vb, p.astype(BF16), (((0,), (0,)), ((), ())), preferred_element_type=F32)
            outs.append(pv / denom)
        o_all = jnp.concatenate(outs, axis=0).T
        heads = [o_all[r * WINDOW:(r + 1) * WINDOW, g * HEAD_DIM:(g + 1) * HEAD_DIM]
                 for g in range(N_KV_HEADS) for r in range(Q_PER_KV)]
        attn_o_ref[qs, :] = jnp.concatenate(heads, axis=1).astype(BF16)

    kext_ref[0:WINDOW, :] = kext_ref[ts:ts + WINDOW, :]
    vext_ref[0:WINDOW, :] = vext_ref[ts:ts + WINDOW, :]

    merged = (_sigmoid(z[:, _OFF_GA:_OFF_GB]) * _dot(sgu_o_ref[...], w_bs_ref[...])
              + _sigmoid(z[:, _OFF_GB:]) * _dot(attn_o_ref[...], w_ba_ref[...]))
    y = _dot(merged.astype(BF16), w_out_ref[...])
    h_ref[...] = _layer_norm(alpha * x + y, ln1_g_ref[...], ln1_b_ref[...])


def _full(shape):
    return pl.BlockSpec(shape, lambda *_: (0,) * len(shape), pipeline_mode=pl.Buffered(1))


def _prompt_mixer(x, p, alpha, ts=PROMPT_BLOCK):
    B, S, D = x.shape
    ts = min(ts, S)
    assert S % ts == 0 and ts % WINDOW == 0 and ts % CHUNK == 0
    grid = (B, S // ts)
    smem = pl.BlockSpec(memory_space=pltpu.SMEM)
    in_specs = [
        smem,
        pl.BlockSpec((None, ts, D), lambda b, s: (b, s, 0)),
        _full((D, D_IN)), _full((1, D_SGU)), _full((1, D_SGU)),
        _full((N_SGU_GROUPS, CHUNK, CHUNK)), _full((CHUNK, N_SGU_GROUPS)),
        _full((D_SGU, D)), _full((D_ATTN, D)), _full((D, D)), _full((1, D)), _full((1, D)),
    ]
    out_specs = [
        pl.BlockSpec((None, ts, D), lambda b, s: (b, s, 0)),
        pl.BlockSpec((None, WINDOW, D_KV), lambda b, s: (b, 0, 0)),
        pl.BlockSpec((None, WINDOW, D_KV), lambda b, s: (b, 0, 0)),
        pl.BlockSpec((None, CHUNK, D_SGU), lambda b, s: (b, 0, 0)),
    ]
    out_shape = [
        jax.ShapeDtypeStruct((B, S, D), F32),
        jax.ShapeDtypeStruct((B, WINDOW, D_KV), F32),
        jax.ShapeDtypeStruct((B, WINDOW, D_KV), F32),
        jax.ShapeDtypeStruct((B, CHUNK, D_SGU), F32),
    ]
    scratch = [
        pltpu.VMEM((ts + WINDOW, D_KV), BF16),
        pltpu.VMEM((ts + WINDOW, D_KV), BF16),
        pltpu.VMEM((ts, D_SGU), BF16),
        pltpu.VMEM((ts, D_ATTN), BF16),
    ]
    return pl.pallas_call(
        functools.partial(_prompt_mixer_kernel, alpha=alpha, ts=ts),
        grid=grid, in_specs=in_specs, out_specs=out_specs, out_shape=out_shape,
        scratch_shapes=scratch, name="prompt_mixer",
        compiler_params=pltpu.CompilerParams(
            dimension_semantics=("arbitrary", "arbitrary"), vmem_limit_bytes=VMEM_LIMIT_BYTES),
    )(p["sinks"], x, p["w_in"], p["sgu_ln_g"], p["sgu_ln_b"], p["sgu_w"], p["sgu_bcol"],
      p["w_bs"], p["w_ba"], p["w_out"], p["ln1_g"], p["ln1_b"])


def _sample_pre_kernel(sgu_w_ref, sgu_b_ref, x_ref, w_in_ref, sgu_g_ref, sgu_bb_ref, w_bs_ref,
                       q_ref, k_ref, v_ref, sv_ref, part_ref, gate_b_ref, *, t_new):
    x = x_ref[...]
    n = x.shape[0]
    z = _dot(x.astype(BF16), w_in_ref[...])
    q_ref[...] = z[:, _OFF_Q:_OFF_K]
    k_ref[...] = z[:, _OFF_K:_OFF_V]
    v_ref[...] = z[:, _OFF_V:_OFF_U]
    u = _gelu(z[:, _OFF_U:_OFF_SV])
    svn = _layer_norm(_gelu(z[:, _OFF_SV:_OFF_GA]), sgu_g_ref[...], sgu_bb_ref[...])
    sv_ref[...] = svn
    t_of_row = lax.broadcasted_iota(jnp.int32, (n, SGU_GROUP_DIM), 0) % t_new
    pieces = []
    for g in range(N_SGU_GROUPS):
        vg = svn[:, g * SGU_GROUP_DIM:(g + 1) * SGU_GROUP_DIM]
        mix = jnp.zeros((n, SGU_GROUP_DIM), F32)
        for t in range(t_new):
            mix = jnp.where(t_of_row == t, sgu_b_ref[g, t], mix)
        for d in range(t_new):
            coef = jnp.zeros((n, SGU_GROUP_DIM), F32)
            for t in range(d, t_new):
                coef = jnp.where(t_of_row == t, sgu_w_ref[g, t * t_new + t - d], coef)
            shifted = vg if d == 0 else pltpu.roll(vg, d, 0)
            mix = mix + coef * shifted
        pieces.append(u[:, g * SGU_GROUP_DIM:(g + 1) * SGU_GROUP_DIM] * mix)
    sgu_o = jnp.concatenate(pieces, axis=1).astype(BF16)
    part_ref[...] = _sigmoid(z[:, _OFF_GA:_OFF_GB]) * _dot(sgu_o, w_bs_ref[...])
    gate_b_ref[...] = _sigmoid(z[:, _OFF_GB:])


def _sample_pre(x2d, p, t_new):
    n, D = x2d.shape
    smem = pl.BlockSpec(memory_space=pltpu.SMEM)
    out_shape = [
        jax.ShapeDtypeStruct((n, D_ATTN), F32),
        jax.ShapeDtypeStruct((n, D_KV), F32),
        jax.ShapeDtypeStruct((n, D_KV), F32),
        jax.ShapeDtypeStruct((n, D_SGU), F32),
        jax.ShapeDtypeStruct((n, D), F32),
        jax.ShapeDtypeStruct((n, D), F32),
    ]
    return pl.pallas_call(
        functools.partial(_sample_pre_kernel, t_new=t_new),
        in_specs=[smem, smem] + [pl.BlockSpec(memory_space=pltpu.VMEM)] * 5,
        out_specs=[pl.BlockSpec(memory_space=pltpu.VMEM)] * 6,
        out_shape=out_shape, name="sample_pre",
        compiler_params=pltpu.CompilerParams(vmem_limit_bytes=VMEM_LIMIT_BYTES),
    )(p["sgu_w"][:, :t_new, :t_new].reshape(N_SGU_GROUPS, t_new * t_new), p["sgu_b"],
      x2d, p["w_in"], p["sgu_ln_g"], p["sgu_ln_b"], p["w_bs"])


def _sample_attn_kernel(sinks_ref, q_ref, kn_ref, vn_ref, ck_ref, cv_ref,
                        o_ref, nk_ref, nv_ref, kbuf_ref, vbuf_ref, *, t_new, bb):
    nq = Q_PER_KV * t_new
    nkeys = WINDOW + 8
    row = lax.broadcasted_iota(jnp.int32, (nq, nkeys), 0)
    kj = lax.broadcasted_iota(jnp.int32, (nq, nkeys), 1)
    t_q = row % t_new
    bias = jnp.where((kj > t_q) & (kj <= t_q + WINDOW), 0.0, NEG_INF).astype(F32)
    r_of_row = lax.broadcasted_iota(jnp.int32, (nq, 1), 0) // t_new
    scale = HEAD_DIM ** -0.5
    pad = jnp.zeros((8 - t_new, D_KV), F32)
    for b in range(bb):
        kbuf_ref[0:WINDOW, :] = ck_ref[b]
        kbuf_ref[WINDOW:WINDOW + t_new, :] = kn_ref[b]
        kbuf_ref[WINDOW + t_new:, :] = pad
        vbuf_ref[0:WINDOW, :] = cv_ref[b]
        vbuf_ref[WINDOW:WINDOW + t_new, :] = vn_ref[b]
        vbuf_ref[WINDOW + t_new:, :] = pad
        nk_ref[b] = kbuf_ref[t_new:t_new + WINDOW, :]
        nv_ref[b] = vbuf_ref[t_new:t_new + WINDOW, :]
        for g in range(N_KV_HEADS):
            ds = slice(g * HEAD_DIM, (g + 1) * HEAD_DIM)
            kg = kbuf_ref[:, ds].astype(BF16)
            vg = vbuf_ref[:, ds].astype(BF16)
            logits = _dot_nt(q_ref[b, g].astype(BF16), kg) * scale + bias
            sink_col = jnp.zeros((nq, 1), F32)
            for r in range(Q_PER_KV):
                sink_col = jnp.where(r_of_row == r, sinks_ref[g * Q_PER_KV + r], sink_col)
            o_ref[b, g] = _sink_softmax_pv(logits, sink_col, vg)


def _sample_attn(q_grouped, k_new, v_new, cache_k, cache_v, sinks, t_new, bb=SAMPLE_ATTN_BATCH):
    nb = q_grouped.shape[0]
    bb = min(bb, nb)
    assert nb % bb == 0 and t_new <= 8
    nq = Q_PER_KV * t_new
    smem = pl.BlockSpec(memory_space=pltpu.SMEM)
    in_specs = [
        smem,
        pl.BlockSpec((bb, N_KV_HEADS, nq, HEAD_DIM), lambda i: (i, 0, 0, 0)),
        pl.BlockSpec((bb, t_new, D_KV), lambda i: (i, 0, 0)),
        pl.BlockSpec((bb, t_new, D_KV), lambda i: (i, 0, 0)),
        pl.BlockSpec((bb, WINDOW, D_KV), lambda i: (i, 0, 0)),
        pl.BlockSpec((bb, WINDOW, D_KV), lambda i: (i, 0, 0)),
    ]
    out_specs = [
        pl.BlockSpec((bb, N_KV_HEADS, nq, HEAD_DIM), lambda i: (i, 0, 0, 0)),
        pl.BlockSpec((bb, WINDOW, D_KV), lambda i: (i, 0, 0)),
        pl.BlockSpec((bb, WINDOW, D_KV), lambda i: (i, 0, 0)),
    ]
    out_shape = [
        jax.ShapeDtypeStruct((nb, N_KV_HEADS, nq, HEAD_DIM), F32),
        jax.ShapeDtypeStruct((nb, WINDOW, D_KV), F32),
        jax.ShapeDtypeStruct((nb, WINDOW, D_KV), F32),
    ]
    return pl.pallas_call(
        functools.partial(_sample_attn_kernel, t_new=t_new, bb=bb),
        grid=(nb // bb,), in_specs=in_specs, out_specs=out_specs, out_shape=out_shape,
        scratch_shapes=[pltpu.VMEM((WINDOW + 8, D_KV), F32), pltpu.VMEM((WINDOW + 8, D_KV), F32)],
        name="sample_attn",
        compiler_params=pltpu.CompilerParams(dimension_semantics=("arbitrary",)),
    )(sinks, q_grouped, k_new, v_new, cache_k, cache_v)


def _sample_post_kernel(x_ref, part_ref, gate_b_ref, o_ref, w_ba_ref, w_out_ref, ln1_g_ref, ln1_b_ref,
                        h_ref, *, alpha):
    merged = part_ref[...] + gate_b_ref[...] * _dot(o_ref[...].astype(BF16), w_ba_ref[...])
    y = _dot(merged.astype(BF16), w_out_ref[...])
    h_ref[...] = _layer_norm(alpha * x_ref[...] + y, ln1_g_ref[...], ln1_b_ref[...])


def _sample_post(x2d, part, gate_b, o2d, p, alpha):
    return pl.pallas_call(
        functools.partial(_sample_post_kernel, alpha=alpha),
        in_specs=[pl.BlockSpec(memory_space=pltpu.VMEM)] * 8,
        out_specs=pl.BlockSpec(memory_space=pltpu.VMEM),
        out_shape=jax.ShapeDtypeStruct(x2d.shape, F32), name="sample_post",
        compiler_params=pltpu.CompilerParams(vmem_limit_bytes=VMEM_LIMIT_BYTES),
    )(x2d, part, gate_b, o2d, p["w_ba"], p["w_out"], p["ln1_g"], p["ln1_b"])


def _top_values(work, count):
    vals = []
    for i in range(count):
        top = jnp.max(work, axis=0, keepdims=True)
        vals.append(top)
        if i + 1 < count:
            work = jnp.where(work == top, -jnp.inf, work)
    return vals


def _sort_pair(v, i, j):
    v[i], v[j] = jnp.maximum(v[i], v[j]), jnp.minimum(v[i], v[j])


def _bitonic_merge(v):
    j = len(v) // 2
    while j >= 1:
        for i in range(len(v)):
            if i ^ j > i:
                _sort_pair(v, i, i ^ j)
        j //= 2


def _top16_rows(s):
    n = PEER_TOPK
    assert s.shape[0] == 8 * n
    v = [s[k * 8:(k + 1) * 8] for k in range(n)]
    k = 2
    while k <= n:
        j = k // 2
        while j >= 1:
            for i in range(n):
                l = i ^ j
                if l > i:
                    if i & k == 0:
                        _sort_pair(v, i, l)
                    else:
                        _sort_pair(v, l, i)
            j //= 2
        k *= 2
    for shift in (4, 2, 1):
        v = [jnp.maximum(v[i], pltpu.roll(v[n - 1 - i], shift, 0)) for i in range(n)]
        _bitonic_merge(v)
    return [x[0:1] for x in v]


def _candidate_rows(a_rows, b_rows):
    assert PEER_TOPK == 16
    a_all = jnp.concatenate(a_rows, axis=0)
    a_low = a_all[0:8]
    row = lax.broadcasted_iota(jnp.int32, a_low.shape, 0)

    def pair(j):
        return a_low + b_rows[j]

    def corner(j):
        return a_rows[0] + b_rows[j]

    t3 = jnp.where(row >= 5, pltpu.roll(pair(4), 5, 0), pair(2))
    t4 = jnp.where(row >= 6, pltpu.roll(pair(6), 6, 0),
                   jnp.where(row >= 4, pltpu.roll(pair(5), 4, 0), pair(3)))
    t5 = pair(7)
    for k in range(6):
        t5 = jnp.where(row >= 2 + k, corner(8 + k), t5)
    t6 = jnp.where(row >= 2, -jnp.inf, jnp.where(row >= 1, corner(15), corner(14)))
    return jnp.concatenate([a_all + b_rows[0], pair(1), t3, t4, t5, t6], axis=0)


def _peer_select(s1, s2):
    a_rows = _top16_rows(s1)
    b_rows = _top16_rows(s2)
    rank2 = jnp.zeros_like(s2)
    for jj in range(PEER_TOPK):
        rank2 = jnp.where(s2 < b_rows[jj], float(jj + 1), rank2)
    best = _top_values(_candidate_rows(a_rows, b_rows), PEER_TOPK)
    norm = jnp.zeros_like(best[0])
    for i in range(PEER_TOPK):
        norm = norm + jnp.exp(best[i] - best[0])
    count = jnp.zeros_like(s1)
    for jj in range(PEER_TOPK):
        count = jnp.where(s1 + b_rows[jj] >= best[PEER_TOPK - 1], float(jj + 1), count)
    e1 = jnp.exp(s1 - a_rows[0]) / norm
    e2 = jnp.exp(s2 - b_rows[0])
    return rank2, e2, count, e1


def _peer_gate_tile(act_ref, coef_ref, first_n1, ns, t, sel_refs):
    rank2_ref, e2_ref, count_ref, e1_ref = sel_refs
    ls = slice(t * LANES, (t + 1) * LANES)
    gates = [None for _ in ns]
    for hd in range(PEER_HEADS):
        rank2 = rank2_ref[hd, t]
        e2 = e2_ref[hd, t]
        for i, n in enumerate(ns):
            count = count_ref[hd, t, pl.ds(first_n1 + n, 1), :]
            e1 = e1_ref[hd, t, pl.ds(first_n1 + n, 1), :]
            term = jnp.where(rank2 < count, e2, 0.0) * e1
            gates[i] = term if gates[i] is None else gates[i] + term
    for i, n in enumerate(ns):
        rs = slice(n * N_SUBKEYS, (n + 1) * N_SUBKEYS)
        x = act_ref[rs, ls]
        inner = x * (GELU_C0 + (GELU_C0 * GELU_C1) * (x * x))
        coef_ref[rs, ls] = ((gates[i] * x) * (1.0 + jnp.tanh(inner))).astype(BF16)


def _peer_kernel(h_ref, wq_ref, sk_ref, u_ref, vt_ref, ln_g_ref, ln_b_ref, y_ref,
                 ht_ref, qt_ref, rank2_ref, e2_ref, count_ref, e1_ref, acc_ref,
                 act0_ref, act1_ref, coef0_ref, coef1_ref, *, alpha, tb, se):
    j = pl.program_id(1)
    ntb = tb // LANES
    nblocks = N_EXPERTS // se
    sel_refs = (rank2_ref, e2_ref, count_ref, e1_ref)
    act_refs = (act0_ref, act1_ref)
    coef_refs = (coef0_ref, coef1_ref)

    @pl.when(j == 0)
    def _():
        ht = h_ref[...].T.astype(BF16)
        ht_ref[...] = ht
        qt_ref[...] = _dot(wq_ref[...], ht).astype(BF16)

        def per_head(hd, carry):
            for t in range(ntb):
                ls = slice(t * LANES, (t + 1) * LANES)
                s = []
                for c in range(2):
                    r0 = pl.multiple_of((hd * 2 + c) * PEER_HALF, PEER_HALF)
                    s.append(_dot(sk_ref[hd * 2 + c], qt_ref[pl.ds(r0, PEER_HALF), ls]))
                rank2, e2, count, e1 = _peer_select(s[0], s[1])
                rank2_ref[hd, t] = rank2
                e2_ref[hd, t] = e2
                count_ref[hd, t] = count
                e1_ref[hd, t] = 0.5 * e1
            return carry

        lax.fori_loop(0, PEER_HEADS, per_head, 0)
        acc_ref[...] = jnp.zeros_like(acc_ref)

    def step_body(q, with_act, with_gate, with_combine):
        piece_t, piece_e, piece_d = tb // PEER_TOKEN_PIECES, se // PEER_ROW_PIECES, acc_ref.shape[0] // PEER_ROW_PIECES
        pieces = []
        for tp in range(PEER_TOKEN_PIECES):
            hs = slice(tp * piece_t, (tp + 1) * piece_t)
            for lo in range(PEER_ROW_PIECES):
                if with_act:
                    pieces.append(("act", hs, slice(lo * piece_e, (lo + 1) * piece_e)))
                if with_combine:
                    pieces.append(("combine", hs, slice(lo * piece_d, (lo + 1) * piece_d)))
        n_sub = se // N_SUBKEYS
        groups = [tuple(range(g, min(g + PEER_GATE_GROUP, n_sub))) for g in range(0, n_sub, PEER_GATE_GROUP)]
        tiles = [(ns, t) for t in range(ntb) for ns in groups] if with_gate else []
        per_piece = -(-len(tiles) // max(len(pieces), 1))
        for k, (kind, hs, es) in enumerate(pieces):
            for n, t in tiles[k * per_piece:(k + 1) * per_piece]:
                _peer_gate_tile(act_refs[1 - q], coef_refs[1 - q], (j - 1) * (se // N_SUBKEYS), n, t, sel_refs)
            if kind == "act":
                act_refs[q][es, hs] = _dot(u_ref[es, :], ht_ref[:, hs])
            else:
                acc_ref[es, hs] += _dot(vt_ref[es, :], coef_refs[q][:, hs])
        for n, t in tiles[len(pieces) * per_piece:]:
            _peer_gate_tile(act_refs[1 - q], coef_refs[1 - q], (j - 1) * (se // N_SUBKEYS), n, t, sel_refs)

    assert nblocks % 2 == 0 and nblocks >= 4
    pl.when(j == 0)(functools.partial(step_body, 0, True, False, False))
    pl.when(j == 1)(functools.partial(step_body, 1, True, True, False))
    for q in range(2):
        pl.when((j >= 2) & (j < nblocks) & (j % 2 == q))(functools.partial(step_body, q, True, True, True))
    pl.when(j == nblocks)(functools.partial(step_body, 0, False, True, True))
    pl.when(j == nblocks + 1)(functools.partial(step_body, 1, False, False, True))

    @pl.when(j == pl.num_programs(1) - 1)
    def _():
        y_ref[...] = _layer_norm(alpha * h_ref[...] + acc_ref[...].T, ln_g_ref[...], ln_b_ref[...])


def _peer(h2d, p, alpha, tb=PEER_TOKEN_BLOCK):
    n, D = h2d.shape
    tb = min(tb, n)
    se = p["peer_vt"].shape[2]
    assert n % tb == 0 and tb % LANES == 0 and N_EXPERTS % se == 0 and se % N_SUBKEYS == 0
    ntb = tb // LANES
    qdim = PEER_HEADS * PEER_KEY_DIM
    nblocks = N_EXPERTS // se
    in_specs = [
        pl.BlockSpec((tb, D), lambda i, j: (i, 0)),
        _full((qdim, D)),
        _full((PEER_HEADS * 2, N_SUBKEYS, PEER_HALF)),
        pl.BlockSpec((se, D), lambda i, j: (jnp.minimum(j, nblocks - 1), 0)),
        pl.BlockSpec((None, D, se), lambda i, j: (jnp.clip(j - 2, 0, nblocks - 1), 0, 0)),
        _full((1, D)), _full((1, D)),
    ]
    per_head = (PEER_HEADS, ntb, N_SUBKEYS, LANES)
    scratch = [
        pltpu.VMEM((D, tb), BF16),
        pltpu.VMEM((qdim, tb), BF16),
        pltpu.VMEM(per_head, F32), pltpu.VMEM(per_head, F32),
        pltpu.VMEM(per_head, F32), pltpu.VMEM(per_head, F32),
        pltpu.VMEM((D, tb), F32),
    ] + [pltpu.VMEM((se, tb), F32)] * 2 + [pltpu.VMEM((se, tb), BF16)] * 2
    return pl.pallas_call(
        functools.partial(_peer_kernel, alpha=alpha, tb=tb, se=se),
        grid=(n // tb, nblocks + 2), in_specs=in_specs,
        out_specs=pl.BlockSpec((tb, D), lambda i, j: (i, 0), pipeline_mode=pl.Buffered(1)),
        out_shape=jax.ShapeDtypeStruct((n, D), F32), scratch_shapes=scratch, name="peer",
        compiler_params=pltpu.CompilerParams(
            dimension_semantics=("arbitrary", "arbitrary"), vmem_limit_bytes=VMEM_LIMIT_BYTES),
    )(h2d, p["peer_wq_t"], p["peer_sk"], p["peer_u"], p["peer_vt"], p["ln2_g"], p["ln2_b"])


def _layer_params(l, w_in, sinks, sgu_ln_g, sgu_ln_b, sgu_w, sgu_b, w_branch_sgu, w_branch_attn, w_out,
                  ln1_g, ln1_b, peer_w_q, peer_sub_keys, peer_u, peer_v, ln2_g, ln2_b):
    return {
        "w_in": w_in[l].astype(BF16),
        "sinks": sinks[l],
        "sgu_ln_g": sgu_ln_g[l][None, :], "sgu_ln_b": sgu_ln_b[l][None, :],
        "sgu_w": sgu_w[l],
        "sgu_b": sgu_b[l], "sgu_bcol": sgu_b[l].T,
        "w_bs": w_branch_sgu[l].astype(BF16), "w_ba": w_branch_attn[l].astype(BF16),
        "w_out": w_out[l].astype(BF16),
        "ln1_g": ln1_g[l][None, :], "ln1_b": ln1_b[l][None, :],
        "peer_wq_t": peer_w_q[l].T.astype(BF16),
        "peer_sk": peer_sub_keys[l].reshape(PEER_HEADS * 2, N_SUBKEYS, PEER_HALF).astype(BF16),
        "peer_u": peer_u[l].astype(BF16),
        "peer_vt": peer_v[l].reshape(N_EXPERTS // PEER_EXPERT_BLOCK, PEER_EXPERT_BLOCK, D_MODEL)
                            .transpose(0, 2, 1).astype(BF16),
        "ln2_g": ln2_g[l][None, :], "ln2_b": ln2_b[l][None, :],
    }


def _prompt_layer(x, p, alpha):
    B, S, D = x.shape
    h, k_last, v_last, sv_last = _prompt_mixer(x, p, alpha)
    y = _peer(h.reshape(B * S, D), p, alpha).reshape(B, S, D)
    return (y, k_last.reshape(B, WINDOW, N_KV_HEADS, HEAD_DIM), v_last.reshape(B, WINDOW, N_KV_HEADS, HEAD_DIM),
            sv_last.reshape(B, CHUNK, N_SGU_GROUPS, SGU_GROUP_DIM))


def _sample_layer(x, cache_k, cache_v, p, alpha):
    B, T, D = x.shape
    x2d = x.reshape(B * T, D)
    q, k_new, v_new, svn, part, gate_b = _sample_pre(x2d, p, T)
    qg = q.reshape(B, T, N_KV_HEADS, Q_PER_KV, HEAD_DIM).transpose(0, 2, 3, 1, 4)
    qg = qg.reshape(B, N_KV_HEADS, Q_PER_KV * T, HEAD_DIM)
    og, new_k, new_v = _sample_attn(qg, k_new.reshape(B, T, D_KV), v_new.reshape(B, T, D_KV),
                                    cache_k.reshape(B, WINDOW, D_KV), cache_v.reshape(B, WINDOW, D_KV),
                                    p["sinks"], T)
    o2d = og.reshape(B, N_KV_HEADS, Q_PER_KV, T, HEAD_DIM).transpose(0, 3, 1, 2, 4).reshape(B * T, D_ATTN)
    h = _sample_post(x2d, part, gate_b, o2d, p, alpha)
    y = _peer(h, p, alpha).reshape(B, T, D)
    return (y, new_k.reshape(B, WINDOW, N_KV_HEADS, HEAD_DIM), new_v.reshape(B, WINDOW, N_KV_HEADS, HEAD_DIM),
            svn.reshape(B, T, N_SGU_GROUPS, SGU_GROUP_DIM))


def kernel(x_prompt, x_sample, cache_k, cache_v, w_in, sinks, sgu_ln_g, sgu_ln_b, sgu_w, sgu_b, w_branch_sgu, w_branch_attn, w_out, ln1_g, ln1_b, peer_w_q, peer_sub_keys, peer_u, peer_v, ln2_g, ln2_b):
    depth = w_in.shape[0]
    alpha = (2.0 * depth) ** 0.25
    yp, ys = x_prompt, x_sample
    outs = [[] for _ in range(6)]
    for l in range(depth):
        p = _layer_params(l, w_in, sinks, sgu_ln_g, sgu_ln_b, sgu_w, sgu_b, w_branch_sgu, w_branch_attn,
                          w_out, ln1_g, ln1_b, peer_w_q, peer_sub_keys, peer_u, peer_v, ln2_g, ln2_b)
        yp, kp, vp, svp = _prompt_layer(yp, p, alpha)
        ys, ksm, vsm, svs = _sample_layer(ys, cache_k[l], cache_v[l], p, alpha)
        for acc, val in zip(outs, (kp, vp, ksm, vsm, svp, svs)):
            acc.append(val)
    return (yp, ys) + tuple(jnp.stack(o) for o in outs)
```

```python
import functools

import jax
import jax.numpy as jnp
from jax import lax
from jax.experimental import pallas as pl
from jax.experimental.pallas import tpu as pltpu

F32 = jnp.float32
BF16 = jnp.bfloat16

D_MODEL = 1024
N_Q_HEADS = 8
N_KV_HEADS = 2
HEAD_DIM = 64
Q_PER_KV = N_Q_HEADS // N_KV_HEADS
D_ATTN = N_Q_HEADS * HEAD_DIM
D_KV = N_KV_HEADS * HEAD_DIM
WINDOW = 128
CHUNK = 128
D_SGU = D_MODEL // 2
N_SGU_GROUPS = 4
SGU_GROUP_DIM = D_SGU // N_SGU_GROUPS
PEER_HEADS = 8
N_SUBKEYS = 128
N_EXPERTS = N_SUBKEYS * N_SUBKEYS
PEER_TOPK = 16
PEER_KEY_DIM = 256
PEER_HALF = PEER_KEY_DIM // 2
LN_EPS = 1e-5
NEG_INF = -1e30
D_IN = D_ATTN + 2 * D_KV + 2 * D_SGU + 2 * D_MODEL
_OFF_Q = 0
_OFF_K = D_ATTN
_OFF_V = _OFF_K + D_KV
_OFF_U = _OFF_V + D_KV
_OFF_SV = _OFF_U + D_SGU
_OFF_GA = _OFF_SV + D_SGU
_OFF_GB = _OFF_GA + D_MODEL

LANES = 128
VMEM_LIMIT_BYTES = 56 * 1024 * 1024

PROMPT_BLOCK = 512
SAMPLE_ATTN_BATCH = 8
PEER_TOKEN_BLOCK = 512
PEER_EXPERT_BLOCK = 1024
PEER_GATE_GROUP = 1
PEER_TOKEN_PIECES = 2
PEER_ROW_PIECES = 1


GELU_C0 = 0.7978845608028654
GELU_C1 = 0.044715


def _gelu(x):
    return 0.5 * x * (1.0 + jnp.tanh(GELU_C0 * (x + GELU_C1 * (x * x * x))))


def _sigmoid(x):
    return 1.0 / (1.0 + jnp.exp(-x))


def _layer_norm(x, g, b):
    mu = jnp.mean(x, -1, keepdims=True)
    xc = x - mu
    var = jnp.mean(xc * xc, -1, keepdims=True)
    return xc * lax.rsqrt(var + LN_EPS) * g + b


def _dot(a, b):
    return jnp.dot(a, b, preferred_element_type=F32)


def _dot_nt(a, b):
    return lax.dot_general(a, b, (((1,), (1,)), ((), ())), preferred_element_type=F32)


def _sink_softmax_pv(logits, sink_col, vals):
    m = jnp.maximum(jnp.max(logits, -1, keepdims=True), sink_col)
    p = jnp.exp(logits - m)
    denom = jnp.sum(p, -1, keepdims=True) + jnp.exp(sink_col - m)
    return _dot(p.astype(BF16), vals) / denom


def _prompt_mixer_kernel(sinks_ref, x_ref, w_in_ref, sgu_g_ref, sgu_b_ref, sgu_w_ref, sgu_bcol_ref,
                         w_bs_ref, w_ba_ref, w_out_ref, ln1_g_ref, ln1_b_ref,
                         h_ref, k_ref, v_ref, sv_ref,
                         kext_ref, vext_ref, sgu_o_ref, attn_o_ref, *, alpha, ts):
    s = pl.program_id(1)
    last = pl.num_programs(1) - 1
    nblk = ts // WINDOW

    x = x_ref[...]
    z = _dot(x.astype(BF16), w_in_ref[...])
    k = z[:, _OFF_K:_OFF_V]
    v = z[:, _OFF_V:_OFF_U]
    u = _gelu(z[:, _OFF_U:_OFF_SV])
    svn = _layer_norm(_gelu(z[:, _OFF_SV:_OFF_GA]), sgu_g_ref[...], sgu_b_ref[...])

    @pl.when(s == 0)
    def _():
        kext_ref[0:WINDOW, :] = jnp.zeros((WINDOW, D_KV), BF16)
        vext_ref[0:WINDOW, :] = jnp.zeros((WINDOW, D_KV), BF16)

    kext_ref[WINDOW:, :] = k.astype(BF16)
    vext_ref[WINDOW:, :] = v.astype(BF16)

    @pl.when(s == last)
    def _():
        k_ref[...] = k[ts - WINDOW:, :]
        v_ref[...] = v[ts - WINDOW:, :]
        sv_ref[...] = svn[ts - CHUNK:, :]

    row = lax.broadcasted_iota(jnp.int32, (CHUNK, CHUNK), 0)
    col = lax.broadcasted_iota(jnp.int32, (CHUNK, CHUNK), 1)
    for g in range(N_SGU_GROUPS):
        wt = jnp.where(col <= row, sgu_w_ref[g], 0.0).astype(BF16)
        bcol = sgu_bcol_ref[:, g:g + 1]
        gs = slice(g * SGU_GROUP_DIM, (g + 1) * SGU_GROUP_DIM)
        for c in range(ts // CHUNK):
            rs = slice(c * CHUNK, (c + 1) * CHUNK)
            mix = _dot(wt, svn[rs, gs].astype(BF16)) + bcol
            sgu_o_ref[rs, gs] = (u[rs, gs] * mix).astype(BF16)

    nq = Q_PER_KV * WINDOW
    kj = lax.broadcasted_iota(jnp.int32, (2 * WINDOW, nq), 0)
    qi = lax.broadcasted_iota(jnp.int32, (2 * WINDOW, nq), 1) % WINDOW
    rel = qi + WINDOW - kj
    bias = jnp.where((rel >= 0) & (rel < WINDOW), 0.0, NEG_INF).astype(F32)
    bias_first = jnp.where(s > 0, bias, jnp.where(kj < WINDOW, NEG_INF, bias))
    scale = HEAD_DIM ** -0.5
    for i in range(nblk):
        qs = slice(i * WINDOW, (i + 1) * WINDOW)
        outs = []
        for g in range(N_KV_HEADS):
            ds = slice(g * HEAD_DIM, (g + 1) * HEAD_DIM)
            kb = kext_ref[i * WINDOW:(i + 2) * WINDOW, ds]
            vb = vext_ref[i * WINDOW:(i + 2) * WINDOW, ds]
            q4 = jnp.concatenate(
                [z[qs, (g * Q_PER_KV + r) * HEAD_DIM:(g * Q_PER_KV + r + 1) * HEAD_DIM]
                 for r in range(Q_PER_KV)], axis=0).astype(BF16)
            logits = _dot_nt(kb, q4) * scale + (bias_first if i == 0 else bias)
            sink = jnp.concatenate(
                [jnp.full((1, WINDOW), sinks_ref[g * Q_PER_KV + r], F32) for r in range(Q_PER_KV)], axis=1)
            m = jnp.maximum(jnp.max(logits, 0, keepdims=True), sink)
            p = jnp.exp(logits - m)
            denom = jnp.sum(p, 0, keepdims=True) + jnp.exp(sink - m)
            pv = lax.dot_general(vb, p.astype(BF16), (((0,), (0,)), ((), ())), preferred_element_type=F32)
            outs.append(pv / denom)
        o_all = jnp.concatenate(outs, axis=0).T
        heads = [o_all[r * WINDOW:(r + 1) * WINDOW, g * HEAD_DIM:(g + 1) * HEAD_DIM]
                 for g in range(N_KV_HEADS) for r in range(Q_PER_KV)]
        attn_o_ref[qs, :] = jnp.concatenate(heads, axis=1).astype(BF16)

    kext_ref[0:WINDOW, :] = kext_ref[ts:ts + WINDOW, :]
    vext_ref[0:WINDOW, :] = vext_ref[ts:ts + WINDOW, :]

    merged = (_sigmoid(z[:, _OFF_GA:_OFF_GB]) * _dot(sgu_o_ref[...], w_bs_ref[...])
              + _sigmoid(z[:, _OFF_GB:]) * _dot(attn_o_ref[...], w_ba_ref[...]))
    y = _dot(merged.astype(BF16), w_out_ref[...])
    h_ref[...] = _layer_norm(alpha * x + y, ln1_g_ref[...], ln1_b_ref[...])


def _full(shape):
    return pl.BlockSpec(shape, lambda *_: (0,) * len(shape), pipeline_mode=pl.Buffered(1))


def _prompt_mixer(x, p, alpha, ts=PROMPT_BLOCK):
    B, S, D = x.shape
    ts = min(ts, S)
    assert S % ts == 0 and ts % WINDOW == 0 and ts % CHUNK == 0
    grid = (B, S // ts)
    smem = pl.BlockSpec(memory_space=pltpu.SMEM)
    in_specs = [
        smem,
        pl.BlockSpec((None, ts, D), lambda b, s: (b, s, 0)),
        _full((D, D_IN)), _full((1, D_SGU)), _full((1, D_SGU)),
        _full((N_SGU_GROUPS, CHUNK, CHUNK)), _full((CHUNK, N_SGU_GROUPS)),
        _full((D_SGU, D)), _full((D_ATTN, D)), _full((D, D)), _full((1, D)), _full((1, D)),
    ]
    out_specs = [
        pl.BlockSpec((None, ts, D), lambda b, s: (b, s, 0)),
        pl.BlockSpec((None, WINDOW, D_KV), lambda b, s: (b, 0, 0)),
        pl.BlockSpec((None, WINDOW, D_KV), lambda b, s: (b, 0, 0)),
        pl.BlockSpec((None, CHUNK, D_SGU), lambda b, s: (b, 0, 0)),
    ]
    out_shape = [
        jax.ShapeDtypeStruct((B, S, D), F32),
        jax.ShapeDtypeStruct((B, WINDOW, D_KV), F32),
        jax.ShapeDtypeStruct((B, WINDOW, D_KV), F32),
        jax.ShapeDtypeStruct((B, CHUNK, D_SGU), F32),
    ]
    scratch = [
        pltpu.VMEM((ts + WINDOW, D_KV), BF16),
        pltpu.VMEM((ts + WINDOW, D_KV), BF16),
        pltpu.VMEM((ts, D_SGU), BF16),
        pltpu.VMEM((ts, D_ATTN), BF16),
    ]
    return pl.pallas_call(
        functools.partial(_prompt_mixer_kernel, alpha=alpha, ts=ts),
        grid=grid, in_specs=in_specs, out_specs=out_specs, out_shape=out_shape,
        scratch_shapes=scratch, name="prompt_mixer",
        compiler_params=pltpu.CompilerParams(
            dimension_semantics=("arbitrary", "arbitrary"), vmem_limit_bytes=VMEM_LIMIT_BYTES),
    )(p["sinks"], x, p["w_in"], p["sgu_ln_g"], p["sgu_ln_b"], p["sgu_w"], p["sgu_bcol"],
      p["w_bs"], p["w_ba"], p["w_out"], p["ln1_g"], p["ln1_b"])


def _sample_pre_kernel(sgu_w_ref, sgu_b_ref, x_ref, w_in_ref, sgu_g_ref, sgu_bb_ref, w_bs_ref,
                       q_ref, k_ref, v_ref, sv_ref, part_ref, gate_b_ref, *, t_new):
    x = x_ref[...]
    n = x.shape[0]
    z = _dot(x.astype(BF16), w_in_ref[...])
    q_ref[...] = z[:, _OFF_Q:_OFF_K]
    k_ref[...] = z[:, _OFF_K:_OFF_V]
    v_ref[...] = z[:, _OFF_V:_OFF_U]
    u = _gelu(z[:, _OFF_U:_OFF_SV])
    svn = _layer_norm(_gelu(z[:, _OFF_SV:_OFF_GA]), sgu_g_ref[...], sgu_bb_ref[...])
    sv_ref[...] = svn
    t_of_row = lax.broadcasted_iota(jnp.int32, (n, SGU_GROUP_DIM), 0) % t_new
    pieces = []
    for g in range(N_SGU_GROUPS):
        vg = svn[:, g * SGU_GROUP_DIM:(g + 1) * SGU_GROUP_DIM]
        mix = jnp.zeros((n, SGU_GROUP_DIM), F32)
        for t in range(t_new):
            mix = jnp.where(t_of_row == t, sgu_b_ref[g, t], mix)
        for d in range(t_new):
            coef = jnp.zeros((n, SGU_GROUP_DIM), F32)
            for t in range(d, t_new):
                coef = jnp.where(t_of_row == t, sgu_w_ref[g, t * t_new + t - d], coef)
            shifted = vg if d == 0 else pltpu.roll(vg, d, 0)
            mix = mix + coef * shifted
        pieces.append(u[:, g * SGU_GROUP_DIM:(g + 1) * SGU_GROUP_DIM] * mix)
    sgu_o = jnp.concatenate(pieces, axis=1).astype(BF16)
    part_ref[...] = _sigmoid(z[:, _OFF_GA:_OFF_GB]) * _dot(sgu_o, w_bs_ref[...])
    gate_b_ref[...] = _sigmoid(z[:, _OFF_GB:])


def _sample_pre(x2d, p, t_new):
    n, D = x2d.shape
    smem = pl.BlockSpec(memory_space=pltpu.SMEM)
    out_shape = [
        jax.ShapeDtypeStruct((n, D_ATTN), F32),
        jax.ShapeDtypeStruct((n, D_KV), F32),
        jax.ShapeDtypeStruct((n, D_KV), F32),
        jax.ShapeDtypeStruct((n, D_SGU), F32),
        jax.ShapeDtypeStruct((n, D), F32),
        jax.ShapeDtypeStruct((n, D), F32),
    ]
    return pl.pallas_call(
        functools.partial(_sample_pre_kernel, t_new=t_new),
        in_specs=[smem, smem] + [pl.BlockSpec(memory_space=pltpu.VMEM)] * 5,
        out_specs=[pl.BlockSpec(memory_space=pltpu.VMEM)] * 6,
        out_shape=out_shape, name="sample_pre",
        compiler_params=pltpu.CompilerParams(vmem_limit_bytes=VMEM_LIMIT_BYTES),
    )(p["sgu_w"][:, :t_new, :t_new].reshape(N_SGU_GROUPS, t_new * t_new), p["sgu_b"],
      x2d, p["w_in"], p["sgu_ln_g"], p["sgu_ln_b"], p["w_bs"])


def _sample_attn_kernel(sinks_ref, q_ref, kn_ref, vn_ref, ck_ref, cv_ref,
                        o_ref, nk_ref, nv_ref, kbuf_ref, vbuf_ref, *, t_new, bb):
    nq = Q_PER_KV * t_new
    nkeys = WINDOW + 8
    row = lax.broadcasted_iota(jnp.int32, (nq, nkeys), 0)
    kj = lax.broadcasted_iota(jnp.int32, (nq, nkeys), 1)
    t_q = row % t_new
    bias = jnp.where((kj > t_q) & (kj <= t_q + WINDOW), 0.0, NEG_INF).astype(F32)
    r_of_row = lax.broadcasted_iota(jnp.int32, (nq, 1), 0) // t_new
    scale = HEAD_DIM ** -0.5
    pad = jnp.zeros((8 - t_new, D_KV), F32)
    for b in range(bb):
        kbuf_ref[0:WINDOW, :] = ck_ref[b]
        kbuf_ref[WINDOW:WINDOW + t_new, :] = kn_ref[b]
        kbuf_ref[WINDOW + t_new:, :] = pad
        vbuf_ref[0:WINDOW, :] = cv_ref[b]
        vbuf_ref[WINDOW:WINDOW + t_new, :] = vn_ref[b]
        vbuf_ref[WINDOW + t_new:, :] = pad
        nk_ref[b] = kbuf_ref[t_new:t_new + WINDOW, :]
        nv_ref[b] = vbuf_ref[t_new:t_new + WINDOW, :]
        for g in range(N_KV_HEADS):
            ds = slice(g * HEAD_DIM, (g + 1) * HEAD_DIM)
            kg = kbuf_ref[:, ds].astype(BF16)
            vg = vbuf_ref[:, ds].astype(BF16)
            logits = _dot_nt(q_ref[b, g].astype(BF16), kg) * scale + bias
            sink_col = jnp.zeros((nq, 1), F32)
            for r in range(Q_PER_KV):
                sink_col = jnp.where(r_of_row == r, sinks_ref[g * Q_PER_KV + r], sink_col)
            o_ref[b, g] = _sink_softmax_pv(logits, sink_col, vg)


def _sample_attn(q_grouped, k_new, v_new, cache_k, cache_v, sinks, t_new, bb=SAMPLE_ATTN_BATCH):
    nb = q_grouped.shape[0]
    bb = min(bb, nb)
    assert nb % bb == 0 and t_new <= 8
    nq = Q_PER_KV * t_new
    smem = pl.BlockSpec(memory_space=pltpu.SMEM)
    in_specs = [
        smem,
        pl.BlockSpec((bb, N_KV_HEADS, nq, HEAD_DIM), lambda i: (i, 0, 0, 0)),
        pl.BlockSpec((bb, t_new, D_KV), lambda i: (i, 0, 0)),
        pl.BlockSpec((bb, t_new, D_KV), lambda i: (i, 0, 0)),
        pl.BlockSpec((bb, WINDOW, D_KV), lambda i: (i, 0, 0)),
        pl.BlockSpec((bb, WINDOW, D_KV), lambda i: (i, 0, 0)),
    ]
    out_specs = [
        pl.BlockSpec((bb, N_KV_HEADS, nq, HEAD_DIM), lambda i: (i, 0, 0, 0)),
        pl.BlockSpec((bb, WINDOW, D_KV), lambda i: (i, 0, 0)),
        pl.BlockSpec((bb, WINDOW, D_KV), lambda i: (i, 0, 0)),
    ]
    out_shape = [
        jax.ShapeDtypeStruct((nb, N_KV_HEADS, nq, HEAD_DIM), F32),
        jax.ShapeDtypeStruct((nb, WINDOW, D_KV), F32),
        jax.ShapeDtypeStruct((nb, WINDOW, D_KV), F32),
    ]
    return pl.pallas_call(
        functools.partial(_sample_attn_kernel, t_new=t_new, bb=bb),
        grid=(nb // bb,), in_specs=in_specs, out_specs=out_specs, out_shape=out_shape,
        scratch_shapes=[pltpu.VMEM((WINDOW + 8, D_KV), F32), pltpu.VMEM((WINDOW + 8, D_KV), F32)],
        name="sample_attn",
        compiler_params=pltpu.CompilerParams(dimension_semantics=("arbitrary",)),
    )(sinks, q_grouped, k_new, v_new, cache_k, cache_v)


def _sample_post_kernel(x_ref, part_ref, gate_b_ref, o_ref, w_ba_ref, w_out_ref, ln1_g_ref, ln1_b_ref,
                        h_ref, *, alpha):
    merged = part_ref[...] + gate_b_ref[...] * _dot(o_ref[...].astype(BF16), w_ba_ref[...])
    y = _dot(merged.astype(BF16), w_out_ref[...])
    h_ref[...] = _layer_norm(alpha * x_ref[...] + y, ln1_g_ref[...], ln1_b_ref[...])


def _sample_post(x2d, part, gate_b, o2d, p, alpha):
    return pl.pallas_call(
        functools.partial(_sample_post_kernel, alpha=alpha),
        in_specs=[pl.BlockSpec(memory_space=pltpu.VMEM)] * 8,
        out_specs=pl.BlockSpec(memory_space=pltpu.VMEM),
        out_shape=jax.ShapeDtypeStruct(x2d.shape, F32), name="sample_post",
        compiler_params=pltpu.CompilerParams(vmem_limit_bytes=VMEM_LIMIT_BYTES),
    )(x2d, part, gate_b, o2d, p["w_ba"], p["w_out"], p["ln1_g"], p["ln1_b"])


def _top_values(work, count):
    vals = []
    for i in range(count):
        top = jnp.max(work, axis=0, keepdims=True)
        vals.append(top)
        if i + 1 < count:
            work = jnp.where(work == top, -jnp.inf, work)
    return vals


def _sort_pair(v, i, j):
    v[i], v[j] = jnp.maximum(v[i], v[j]), jnp.minimum(v[i], v[j])


def _bitonic_merge(v):
    j = len(v) // 2
    while j >= 1:
        for i in range(len(v)):
            if i ^ j > i:
                _sort_pair(v, i, i ^ j)
        j //= 2


def _top16_rows(s):
    n = PEER_TOPK
    assert s.shape[0] == 8 * n
    v = [s[k * 8:(k + 1) * 8] for k in range(n)]
    k = 2
    while k <= n:
        j = k // 2
        while j >= 1:
            for i in range(n):
                l = i ^ j
                if l > i:
                    if i & k == 0:
                        _sort_pair(v, i, l)
                    else:
                        _sort_pair(v, l, i)
            j //= 2
        k *= 2
    for shift in (4, 2, 1):
        v = [jnp.maximum(v[i], pltpu.roll(v[n - 1 - i], shift, 0)) for i in range(n)]
        _bitonic_merge(v)
    return [x[0:1] for x in v]


def _candidate_rows(a_rows, b_rows):
    assert PEER_TOPK == 16
    a_all = jnp.concatenate(a_rows, axis=0)
    a_low = a_all[0:8]
    row = lax.broadcasted_iota(jnp.int32, a_low.shape, 0)

    def pair(j):
        return a_low + b_rows[j]

    def corner(j):
        return a_rows[0] + b_rows[j]

    t3 = jnp.where(row >= 5, pltpu.roll(pair(4), 5, 0), pair(2))
    t4 = jnp.where(row >= 6, pltpu.roll(pair(6), 6, 0),
                   jnp.where(row >= 4, pltpu.roll(pair(5), 4, 0), pair(3)))
    t5 = pair(7)
    for k in range(6):
        t5 = jnp.where(row >= 2 + k, corner(8 + k), t5)
    t6 = jnp.where(row >= 2, -jnp.inf, jnp.where(row >= 1, corner(15), corner(14)))
    return jnp.concatenate([a_all + b_rows[0], pair(1), t3, t4, t5, t6], axis=0)


def _peer_select(s1, s2):
    a_rows = _top16_rows(s1)
    b_rows = _top16_rows(s2)
    rank2 = jnp.zeros_like(s2)
    for jj in range(PEER_TOPK):
        rank2 = jnp.where(s2 < b_rows[jj], float(jj + 1), rank2)
    best = _top_values(_candidate_rows(a_rows, b_rows), PEER_TOPK)
    norm = jnp.zeros_like(best[0])
    for i in range(PEER_TOPK):
        norm = norm + jnp.exp(best[i] - best[0])
    count = jnp.zeros_like(s1)
    for jj in range(PEER_TOPK):
        count = jnp.where(s1 + b_rows[jj] >= best[PEER_TOPK - 1], float(jj + 1), count)
    e1 = jnp.exp(s1 - a_rows[0]) / norm
    e2 = jnp.exp(s2 - b_rows[0])
    return rank2, e2, count, e1


def _peer_gate_tile(act_ref, coef_ref, first_n1, ns, t, sel_refs):
    rank2_ref, e2_ref, count_ref, e1_ref = sel_refs
    ls = slice(t * LANES, (t + 1) * LANES)
    gates = [None for _ in ns]
    for hd in range(PEER_HEADS):
        rank2 = rank2_ref[hd, t]
        e2 = e2_ref[hd, t]
        for i, n in enumerate(ns):
            count = count_ref[hd, t, pl.ds(first_n1 + n, 1), :]
            e1 = e1_ref[hd, t, pl.ds(first_n1 + n, 1), :]
            hit = rank2 < count
            gates[i] = (jnp.where(hit, e2 * e1, 0.0) if gates[i] is None
                        else jnp.where(hit, gates[i] + e2 * e1, gates[i]))
    for i, n in enumerate(ns):
        rs = slice(n * N_SUBKEYS, (n + 1) * N_SUBKEYS)
        x = act_ref[rs, ls]
        inner = x * (GELU_C0 + (GELU_C0 * GELU_C1) * (x * x))
        coef_ref[rs, ls] = ((gates[i] * x) * (1.0 + jnp.tanh(inner))).astype(BF16)


def _peer_kernel(h_ref, wq_ref, sk_ref, u_ref, vt_ref, ln_g_ref, ln_b_ref, y_ref,
                 ht_ref, qt_ref, rank2_ref, e2_ref, count_ref, e1_ref, acc_ref,
                 act0_ref, act1_ref, coef0_ref, coef1_ref, *, alpha, tb, se):
    j = pl.program_id(1)
    ntb = tb // LANES
    nblocks = N_EXPERTS // se
    sel_refs = (rank2_ref, e2_ref, count_ref, e1_ref)
    act_refs = (act0_ref, act1_ref)
    coef_refs = (coef0_ref, coef1_ref)

    @pl.when(j == 0)
    def _():
        ht = h_ref[...].T.astype(BF16)
        ht_ref[...] = ht
        qt_ref[...] = _dot(wq_ref[...], ht).astype(BF16)

        def per_head(hd, carry):
            for t in range(ntb):
                ls = slice(t * LANES, (t + 1) * LANES)
                s = []
                for c in range(2):
                    r0 = pl.multiple_of((hd * 2 + c) * PEER_HALF, PEER_HALF)
                    s.append(_dot(sk_ref[hd * 2 + c], qt_ref[pl.ds(r0, PEER_HALF), ls]))
                rank2, e2, count, e1 = _peer_select(s[0], s[1])
                rank2_ref[hd, t] = rank2
                e2_ref[hd, t] = e2
                count_ref[hd, t] = count
                e1_ref[hd, t] = 0.5 * e1
            return carry

        lax.fori_loop(0, PEER_HEADS, per_head, 0)
        acc_ref[...] = jnp.zeros_like(acc_ref)

    def step_body(q, with_act, with_gate, with_combine):
        piece_t, piece_e, piece_d = tb // PEER_TOKEN_PIECES, se // PEER_ROW_PIECES, acc_ref.shape[0] // PEER_ROW_PIECES
        pieces = []
        for tp in range(PEER_TOKEN_PIECES):
            hs = slice(tp * piece_t, (tp + 1) * piece_t)
            for lo in range(PEER_ROW_PIECES):
                if with_act:
                    pieces.append(("act", hs, slice(lo * piece_e, (lo + 1) * piece_e)))
                if with_combine:
                    pieces.append(("combine", hs, slice(lo * piece_d, (lo + 1) * piece_d)))
        n_sub = se // N_SUBKEYS
        groups = [tuple(range(g, min(g + PEER_GATE_GROUP, n_sub))) for g in range(0, n_sub, PEER_GATE_GROUP)]
        tiles = [(ns, t) for t in range(ntb) for ns in groups] if with_gate else []
        per_piece = -(-len(tiles) // max(len(pieces), 1))
        for k, (kind, hs, es) in enumerate(pieces):
            for n, t in tiles[k * per_piece:(k + 1) * per_piece]:
                _peer_gate_tile(act_refs[1 - q], coef_refs[1 - q], (j - 1) * (se // N_SUBKEYS), n, t, sel_refs)
            if kind == "act":
                act_refs[q][es, hs] = _dot(u_ref[es, :], ht_ref[:, hs])
            else:
                acc_ref[es, hs] += _dot(vt_ref[es, :], coef_refs[q][:, hs])
        for n, t in tiles[len(pieces) * per_piece:]:
            _peer_gate_tile(act_refs[1 - q], coef_refs[1 - q], (j - 1) * (se // N_SUBKEYS), n, t, sel_refs)

    assert nblocks % 2 == 0 and nblocks >= 4
    pl.when(j == 0)(functools.partial(step_body, 0, True, False, False))
    pl.when(j == 1)(functools.partial(step_body, 1, True, True, False))
    for q in range(2):
        pl.when((j >= 2) & (j < nblocks) & (j % 2 == q))(functools.partial(step_body, q, True, True, True))
    pl.when(j == nblocks)(functools.partial(step_body, 0, False, True, True))
    pl.when(j == nblocks + 1)(functools.partial(step_body, 1, False, False, True))

    @pl.when(j == pl.num_programs(1) - 1)
    def _():
        y_ref[...] = _layer_norm(alpha * h_ref[...] + acc_ref[...].T, ln_g_ref[...], ln_b_ref[...])


def _peer(h2d, p, alpha, tb=PEER_TOKEN_BLOCK):
    n, D = h2d.shape
    tb = min(tb, n)
    se = p["peer_vt"].shape[2]
    assert n % tb == 0 and tb % LANES == 0 and N_EXPERTS % se == 0 and se % N_SUBKEYS == 0
    ntb = tb // LANES
    qdim = PEER_HEADS * PEER_KEY_DIM
    nblocks = N_EXPERTS // se
    in_specs = [
        pl.BlockSpec((tb, D), lambda i, j: (i, 0)),
        _full((qdim, D)),
        _full((PEER_HEADS * 2, N_SUBKEYS, PEER_HALF)),
        pl.BlockSpec((se, D), lambda i, j: (jnp.minimum(j, nblocks - 1), 0)),
        pl.BlockSpec((None, D, se), lambda i, j: (jnp.clip(j - 2, 0, nblocks - 1), 0, 0)),
        _full((1, D)), _full((1, D)),
    ]
    per_head = (PEER_HEADS, ntb, N_SUBKEYS, LANES)
    scratch = [
        pltpu.VMEM((D, tb), BF16),
        pltpu.VMEM((qdim, tb), BF16),
        pltpu.VMEM(per_head, F32), pltpu.VMEM(per_head, F32),
        pltpu.VMEM(per_head, F32), pltpu.VMEM(per_head, F32),
        pltpu.VMEM((D, tb), F32),
    ] + [pltpu.VMEM((se, tb), F32)] * 2 + [pltpu.VMEM((se, tb), BF16)] * 2
    return pl.pallas_call(
        functools.partial(_peer_kernel, alpha=alpha, tb=tb, se=se),
        grid=(n // tb, nblocks + 2), in_specs=in_specs,
        out_specs=pl.BlockSpec((tb, D), lambda i, j: (i, 0), pipeline_mode=pl.Buffered(1)),
        out_shape=jax.ShapeDtypeStruct((n, D), F32), scratch_shapes=scratch, name="peer",
        compiler_params=pltpu.CompilerParams(
            dimension_semantics=("arbitrary", "arbitrary"), vmem_limit_bytes=VMEM_LIMIT_BYTES),
    )(h2d, p["peer_wq_t"], p["peer_sk"], p["peer_u"], p["peer_vt"], p["ln2_g"], p["ln2_b"])


def _layer_params(l, w_in, sinks, sgu_ln_g, sgu_ln_b, sgu_w, sgu_b, w_branch_sgu, w_branch_attn, w_out,
                  ln1_g, ln1_b, peer_w_q, peer_sub_keys, peer_u, peer_v, ln2_g, ln2_b):
    return {
        "w_in": w_in[l].astype(BF16),
        "sinks": sinks[l],
        "sgu_ln_g": sgu_ln_g[l][None, :], "sgu_ln_b": sgu_ln_b[l][None, :],
        "sgu_w": sgu_w[l],
        "sgu_b": sgu_b[l], "sgu_bcol": sgu_b[l].T,
        "w_bs": w_branch_sgu[l].astype(BF16), "w_ba": w_branch_attn[l].astype(BF16),
        "w_out": w_out[l].astype(BF16),
        "ln1_g": ln1_g[l][None, :], "ln1_b": ln1_b[l][None, :],
        "peer_wq_t": peer_w_q[l].T.astype(BF16),
        "peer_sk": peer_sub_keys[l].reshape(PEER_HEADS * 2, N_SUBKEYS, PEER_HALF).astype(BF16),
        "peer_u": peer_u[l].astype(BF16),
        "peer_vt": peer_v[l].reshape(N_EXPERTS // PEER_EXPERT_BLOCK, PEER_EXPERT_BLOCK, D_MODEL)
                            .transpose(0, 2, 1).astype(BF16),
        "ln2_g": ln2_g[l][None, :], "ln2_b": ln2_b[l][None, :],
    }


def _prompt_layer(x, p, alpha):
    B, S, D = x.shape
    h, k_last, v_last, sv_last = _prompt_mixer(x, p, alpha)
    y = _peer(h.reshape(B * S, D), p, alpha).reshape(B, S, D)
    return (y, k_last.reshape(B, WINDOW, N_KV_HEADS, HEAD_DIM), v_last.reshape(B, WINDOW, N_KV_HEADS, HEAD_DIM),
            sv_last.reshape(B, CHUNK, N_SGU_GROUPS, SGU_GROUP_DIM))


def _sample_layer(x, cache_k, cache_v, p, alpha):
    B, T, D = x.shape
    x2d = x.reshape(B * T, D)
    q, k_new, v_new, svn, part, gate_b = _sample_pre(x2d, p, T)
    qg = q.reshape(B, T, N_KV_HEADS, Q_PER_KV, HEAD_DIM).transpose(0, 2, 3, 1, 4)
    qg = qg.reshape(B, N_KV_HEADS, Q_PER_KV * T, HEAD_DIM)
    og, new_k, new_v = _sample_attn(qg, k_new.reshape(B, T, D_KV), v_new.reshape(B, T, D_KV),
                                    cache_k.reshape(B, WINDOW, D_KV), cache_v.reshape(B, WINDOW, D_KV),
                                    p["sinks"], T)
    o2d = og.reshape(B, N_KV_HEADS, Q_PER_KV, T, HEAD_DIM).transpose(0, 3, 1, 2, 4).reshape(B * T, D_ATTN)
    h = _sample_post(x2d, part, gate_b, o2d, p, alpha)
    y = _peer(h, p, alpha).reshape(B, T, D)
    return (y, new_k.reshape(B, WINDOW, N_KV_HEADS, HEAD_DIM), new_v.reshape(B, WINDOW, N_KV_HEADS, HEAD_DIM),
            svn.reshape(B, T, N_SGU_GROUPS, SGU_GROUP_DIM))


def kernel(x_prompt, x_sample, cache_k, cache_v, w_in, sinks, sgu_ln_g, sgu_ln_b, sgu_w, sgu_b, w_branch_sgu, w_branch_attn, w_out, ln1_g, ln1_b, peer_w_q, peer_sub_keys, peer_u, peer_v, ln2_g, ln2_b):
    depth = w_in.shape[0]
    alpha = (2.0 * depth) ** 0.25
    yp, ys = x_prompt, x_sample
    outs = [[] for _ in range(6)]
    for l in range(depth):
        p = _layer_params(l, w_in, sinks, sgu_ln_g, sgu_ln_b, sgu_w, sgu_b, w_branch_sgu, w_branch_attn,
                          w_out, ln1_g, ln1_b, peer_w_q, peer_sub_keys, peer_u, peer_v, ln2_g, ln2_b)
        yp, kp, vp, svp = _prompt_layer(yp, p, alpha)
        ys, ksm, vsm, svs = _sample_layer(ys, cache_k[l], cache_v[l], p, alpha)
        for acc, val in zip(outs, (kp, vp, ksm, vsm, svp, svs)):
            acc.append(val)
    return (yp, ys) + tuple(jnp.stack(o) for o in outs)
```

```python
import functools

import jax
import jax.numpy as jnp
from jax import lax
from jax.experimental import pallas as pl
from jax.experimental.pallas import tpu as pltpu

F32 = jnp.float32
BF16 = jnp.bfloat16

D_MODEL = 1024
N_Q_HEADS = 8
N_KV_HEADS = 2
HEAD_DIM = 64
Q_PER_KV = N_Q_HEADS // N_KV_HEADS
D_ATTN = N_Q_HEADS * HEAD_DIM
D_KV = N_KV_HEADS * HEAD_DIM
WINDOW = 128
CHUNK = 128
D_SGU = D_MODEL // 2
N_SGU_GROUPS = 4
SGU_GROUP_DIM = D_SGU // N_SGU_GROUPS
PEER_HEADS = 8
N_SUBKEYS = 128
N_EXPERTS = N_SUBKEYS * N_SUBKEYS
PEER_TOPK = 16
PEER_KEY_DIM = 256
PEER_HALF = PEER_KEY_DIM // 2
LN_EPS = 1e-5
NEG_INF = -1e30
D_IN = D_ATTN + 2 * D_KV + 2 * D_SGU + 2 * D_MODEL
_OFF_Q = 0
_OFF_K = D_ATTN
_OFF_V = _OFF_K + D_KV
_OFF_U = _OFF_V + D_KV
_OFF_SV = _OFF_U + D_SGU
_OFF_GA = _OFF_SV + D_SGU
_OFF_GB = _OFF_GA + D_MODEL

LANES = 128
VMEM_LIMIT_BYTES = 56 * 1024 * 1024

PROMPT_BLOCK = 512
SAMPLE_ATTN_BATCH = 8
PEER_TOKEN_BLOCK = 512
PEER_EXPERT_BLOCK = 1024
PEER_GATE_GROUP = 1
PEER_TOKEN_PIECES = 2
PEER_ROW_PIECES = 1


GELU_C0 = 0.7978845608028654
GELU_C1 = 0.044715


def _gelu(x):
    return 0.5 * x * (1.0 + jnp.tanh(GELU_C0 * (x + GELU_C1 * (x * x * x))))


def _sigmoid(x):
    return 1.0 / (1.0 + jnp.exp(-x))


def _layer_norm(x, g, b):
    mu = jnp.mean(x, -1, keepdims=True)
    xc = x - mu
    var = jnp.mean(xc * xc, -1, keepdims=True)
    return xc * lax.rsqrt(var + LN_EPS) * g + b


def _dot(a, b):
    return jnp.dot(a, b, preferred_element_type=F32)


def _dot_nt(a, b):
    return lax.dot_general(a, b, (((1,), (1,)), ((), ())), preferred_element_type=F32)


def _sink_softmax_pv(logits, sink_col, vals):
    m = jnp.maximum(jnp.max(logits, -1, keepdims=True), sink_col)
    p = jnp.exp(logits - m)
    denom = jnp.sum(p, -1, keepdims=True) + jnp.exp(sink_col - m)
    return _dot(p.astype(BF16), vals) / denom


def _prompt_mixer_kernel(sinks_ref, x_ref, w_in_ref, sgu_g_ref, sgu_b_ref, sgu_w_ref, sgu_bcol_ref,
                         w_bs_ref, w_ba_ref, w_out_ref, ln1_g_ref, ln1_b_ref,
                         h_ref, k_ref, v_ref, sv_ref,
                         kext_ref, vext_ref, sgu_o_ref, attn_o_ref, *, alpha, ts):
    s = pl.program_id(1)
    last = pl.num_programs(1) - 1
    nblk = ts // WINDOW

    x = x_ref[...]
    z = _dot(x.astype(BF16), w_in_ref[...])
    k = z[:, _OFF_K:_OFF_V]
    v = z[:, _OFF_V:_OFF_U]
    u = _gelu(z[:, _OFF_U:_OFF_SV])
    svn = _layer_norm(_gelu(z[:, _OFF_SV:_OFF_GA]), sgu_g_ref[...], sgu_b_ref[...])

    @pl.when(s == 0)
    def _():
        kext_ref[0:WINDOW, :] = jnp.zeros((WINDOW, D_KV), BF16)
        vext_ref[0:WINDOW, :] = jnp.zeros((WINDOW, D_KV), BF16)

    kext_ref[WINDOW:, :] = k.astype(BF16)
    vext_ref[WINDOW:, :] = v.astype(BF16)

    @pl.when(s == last)
    def _():
        k_ref[...] = k[ts - WINDOW:, :]
        v_ref[...] = v[ts - WINDOW:, :]
        sv_ref[...] = svn[ts - CHUNK:, :]

    row = lax.broadcasted_iota(jnp.int32, (CHUNK, CHUNK), 0)
    col = lax.broadcasted_iota(jnp.int32, (CHUNK, CHUNK), 1)
    for g in range(N_SGU_GROUPS):
        wt = jnp.where(col <= row, sgu_w_ref[g], 0.0).astype(BF16)
        bcol = sgu_bcol_ref[:, g:g + 1]
        gs = slice(g * SGU_GROUP_DIM, (g + 1) * SGU_GROUP_DIM)
        for c in range(ts // CHUNK):
            rs = slice(c * CHUNK, (c + 1) * CHUNK)
            mix = _dot(wt, svn[rs, gs].astype(BF16)) + bcol
            sgu_o_ref[rs, gs] = (u[rs, gs] * mix).astype(BF16)

    nq = Q_PER_KV * WINDOW
    kj = lax.broadcasted_iota(jnp.int32, (2 * WINDOW, nq), 0)
    qi = lax.broadcasted_iota(jnp.int32, (2 * WINDOW, nq), 1) % WINDOW
    rel = qi + WINDOW - kj
    bias = jnp.where((rel >= 0) & (rel < WINDOW), 0.0, NEG_INF).astype(F32)
    bias_first = jnp.where(s > 0, bias, jnp.where(kj < WINDOW, NEG_INF, bias))
    scale = HEAD_DIM ** -0.5
    for i in range(nblk):
        qs = slice(i * WINDOW, (i + 1) * WINDOW)
        outs = []
        for g in range(N_KV_HEADS):
            ds = slice(g * HEAD_DIM, (g + 1) * HEAD_DIM)
            kb = kext_ref[i * WINDOW:(i + 2) * WINDOW, ds]
            vb = vext_ref[i * WINDOW:(i + 2) * WINDOW, ds]
            q4 = jnp.concatenate(
                [z[qs, (g * Q_PER_KV + r) * HEAD_DIM:(g * Q_PER_KV + r + 1) * HEAD_DIM]
                 for r in range(Q_PER_KV)], axis=0).astype(BF16)
            logits = _dot_nt(kb, q4) * scale + (bias_first if i == 0 else bias)
            sink = jnp.concatenate(
                [jnp.full((1, WINDOW), sinks_ref[g * Q_PER_KV + r], F32) for r in range(Q_PER_KV)], axis=1)
            m = jnp.maximum(jnp.max(logits, 0, keepdims=True), sink)
            p = jnp.exp(logits - m)
            denom = jnp.sum(p, 0, keepdims=True) + jnp.exp(sink - m)
            pv = lax.dot_general(vb, p.astype(BF16), (((0,), (0,)), ((), ())), preferred_element_type=F32)
            outs.append(pv / denom)
        o_all = jnp.concatenate(outs, axis=0).T
        heads = [o_all[r * WINDOW:(r + 1) * WINDOW, g * HEAD_DIM:(g + 1) * HEAD_DIM]
                 for g in range(N_KV_HEADS) for r in range(Q_PER_KV)]
        attn_o_ref[qs, :] = jnp.concatenate(heads, axis=1).astype(BF16)

    kext_ref[0:WINDOW, :] = kext_ref[ts:ts + WINDOW, :]
    vext_ref[0:WINDOW, :] = vext_ref[ts:ts + WINDOW, :]

    merged = (_sigmoid(z[:, _OFF_GA:_OFF_GB]) * _dot(sgu_o_ref[...], w_bs_ref[...])
              + _sigmoid(z[:, _OFF_GB:]) * _dot(attn_o_ref[...], w_ba_ref[...]))
    y = _dot(merged.astype(BF16), w_out_ref[...])
    h_ref[...] = _layer_norm(alpha * x + y, ln1_g_ref[...], ln1_b_ref[...])


def _full(shape):
    return pl.BlockSpec(shape, lambda *_: (0,) * len(shape), pipeline_mode=pl.Buffered(1))


def _prompt_mixer(x, p, alpha, ts=PROMPT_BLOCK):
    B, S, D = x.shape
    ts = min(ts, S)
    assert S % ts == 0 and ts % WINDOW == 0 and ts % CHUNK == 0
    grid = (B, S // ts)
    smem = pl.BlockSpec(memory_space=pltpu.SMEM)
    in_specs = [
        smem,
        pl.BlockSpec((None, ts, D), lambda b, s: (b, s, 0)),
        _full((D, D_IN)), _full((1, D_SGU)), _full((1, D_SGU)),
        _full((N_SGU_GROUPS, CHUNK, CHUNK)), _full((CHUNK, N_SGU_GROUPS)),
        _full((D_SGU, D)), _full((D_ATTN, D)), _full((D, D)), _full((1, D)), _full((1, D)),
    ]
    out_specs = [
        pl.BlockSpec((None, ts, D), lambda b, s: (b, s, 0)),
        pl.BlockSpec((None, WINDOW, D_KV), lambda b, s: (b, 0, 0)),
        pl.BlockSpec((None, WINDOW, D_KV), lambda b, s: (b, 0, 0)),
        pl.BlockSpec((None, CHUNK, D_SGU), lambda b, s: (b, 0, 0)),
    ]
    out_shape = [
        jax.ShapeDtypeStruct((B, S, D), F32),
        jax.ShapeDtypeStruct((B, WINDOW, D_KV), F32),
        jax.ShapeDtypeStruct((B, WINDOW, D_KV), F32),
        jax.ShapeDtypeStruct((B, CHUNK, D_SGU), F32),
    ]
    scratch = [
        pltpu.VMEM((ts + WINDOW, D_KV), BF16),
        pltpu.VMEM((ts + WINDOW, D_KV), BF16),
        pltpu.VMEM((ts, D_SGU), BF16),
        pltpu.VMEM((ts, D_ATTN), BF16),
    ]
    return pl.pallas_call(
        functools.partial(_prompt_mixer_kernel, alpha=alpha, ts=ts),
        grid=grid, in_specs=in_specs, out_specs=out_specs, out_shape=out_shape,
        scratch_shapes=scratch, name="prompt_mixer",
        compiler_params=pltpu.CompilerParams(
            dimension_semantics=("arbitrary", "arbitrary"), vmem_limit_bytes=VMEM_LIMIT_BYTES),
    )(p["sinks"], x, p["w_in"], p["sgu_ln_g"], p["sgu_ln_b"], p["sgu_w"], p["sgu_bcol"],
      p["w_bs"], p["w_ba"], p["w_out"], p["ln1_g"], p["ln1_b"])


def _sample_pre_kernel(sgu_w_ref, sgu_b_ref, x_ref, w_in_ref, sgu_g_ref, sgu_bb_ref, w_bs_ref,
                       q_ref, k_ref, v_ref, sv_ref, part_ref, gate_b_ref, *, t_new):
    x = x_ref[...]
    n = x.shape[0]
    z = _dot(x.astype(BF16), w_in_ref[...])
    q_ref[...] = z[:, _OFF_Q:_OFF_K]
    k_ref[...] = z[:, _OFF_K:_OFF_V]
    v_ref[...] = z[:, _OFF_V:_OFF_U]
    u = _gelu(z[:, _OFF_U:_OFF_SV])
    svn = _layer_norm(_gelu(z[:, _OFF_SV:_OFF_GA]), sgu_g_ref[...], sgu_bb_ref[...])
    sv_ref[...] = svn
    t_of_row = lax.broadcasted_iota(jnp.int32, (n, SGU_GROUP_DIM), 0) % t_new
    pieces = []
    for g in range(N_SGU_GROUPS):
        vg = svn[:, g * SGU_GROUP_DIM:(g + 1) * SGU_GROUP_DIM]
        mix = jnp.zeros((n, SGU_GROUP_DIM), F32)
        for t in range(t_new):
            mix = jnp.where(t_of_row == t, sgu_b_ref[g, t], mix)
        for d in range(t_new):
            coef = jnp.zeros((n, SGU_GROUP_DIM), F32)
            for t in range(d, t_new):
                coef = jnp.where(t_of_row == t, sgu_w_ref[g, t * t_new + t - d], coef)
            shifted = vg if d == 0 else pltpu.roll(vg, d, 0)
            mix = mix + coef * shifted
        pieces.append(u[:, g * SGU_GROUP_DIM:(g + 1) * SGU_GROUP_DIM] * mix)
    sgu_o = jnp.concatenate(pieces, axis=1).astype(BF16)
    part_ref[...] = _sigmoid(z[:, _OFF_GA:_OFF_GB]) * _dot(sgu_o, w_bs_ref[...])
    gate_b_ref[...] = _sigmoid(z[:, _OFF_GB:])


def _sample_pre(x2d, p, t_new):
    n, D = x2d.shape
    smem = pl.BlockSpec(memory_space=pltpu.SMEM)
    out_shape = [
        jax.ShapeDtypeStruct((n, D_ATTN), F32),
        jax.ShapeDtypeStruct((n, D_KV), F32),
        jax.ShapeDtypeStruct((n, D_KV), F32),
        jax.ShapeDtypeStruct((n, D_SGU), F32),
        jax.ShapeDtypeStruct((n, D), F32),
        jax.ShapeDtypeStruct((n, D), F32),
    ]
    return pl.pallas_call(
        functools.partial(_sample_pre_kernel, t_new=t_new),
        in_specs=[smem, smem] + [pl.BlockSpec(memory_space=pltpu.VMEM)] * 5,
        out_specs=[pl.BlockSpec(memory_space=pltpu.VMEM)] * 6,
        out_shape=out_shape, name="sample_pre",
        compiler_params=pltpu.CompilerParams(vmem_limit_bytes=VMEM_LIMIT_BYTES),
    )(p["sgu_w"][:, :t_new, :t_new].reshape(N_SGU_GROUPS, t_new * t_new), p["sgu_b"],
      x2d, p["w_in"], p["sgu_ln_g"], p["sgu_ln_b"], p["w_bs"])


def _sample_attn_kernel(sinks_ref, q_ref, kn_ref, vn_ref, ck_ref, cv_ref,
                        o_ref, nk_ref, nv_ref, kbuf_ref, vbuf_ref, *, t_new, bb):
    nq = Q_PER_KV * t_new
    nkeys = WINDOW + 8
    row = lax.broadcasted_iota(jnp.int32, (nq, nkeys), 0)
    kj = lax.broadcasted_iota(jnp.int32, (nq, nkeys), 1)
    t_q = row % t_new
    bias = jnp.where((kj > t_q) & (kj <= t_q + WINDOW), 0.0, NEG_INF).astype(F32)
    r_of_row = lax.broadcasted_iota(jnp.int32, (nq, 1), 0) // t_new
    scale = HEAD_DIM ** -0.5
    pad = jnp.zeros((8 - t_new, D_KV), F32)
    for b in range(bb):
        kbuf_ref[0:WINDOW, :] = ck_ref[b]
        kbuf_ref[WINDOW:WINDOW + t_new, :] = kn_ref[b]
        kbuf_ref[WINDOW + t_new:, :] = pad
        vbuf_ref[0:WINDOW, :] = cv_ref[b]
        vbuf_ref[WINDOW:WINDOW + t_new, :] = vn_ref[b]
        vbuf_ref[WINDOW + t_new:, :] = pad
        nk_ref[b] = kbuf_ref[t_new:t_new + WINDOW, :]
        nv_ref[b] = vbuf_ref[t_new:t_new + WINDOW, :]
        for g in range(N_KV_HEADS):
            ds = slice(g * HEAD_DIM, (g + 1) * HEAD_DIM)
            kg = kbuf_ref[:, ds].astype(BF16)
            vg = vbuf_ref[:, ds].astype(BF16)
            logits = _dot_nt(q_ref[b, g].astype(BF16), kg) * scale + bias
            sink_col = jnp.zeros((nq, 1), F32)
            for r in range(Q_PER_KV):
                sink_col = jnp.where(r_of_row == r, sinks_ref[g * Q_PER_KV + r], sink_col)
            o_ref[b, g] = _sink_softmax_pv(logits, sink_col, vg)


def _sample_attn(q_grouped, k_new, v_new, cache_k, cache_v, sinks, t_new, bb=SAMPLE_ATTN_BATCH):
    nb = q_grouped.shape[0]
    bb = min(bb, nb)
    assert nb % bb == 0 and t_new <= 8
    nq = Q_PER_KV * t_new
    smem = pl.BlockSpec(memory_space=pltpu.SMEM)
    in_specs = [
        smem,
        pl.BlockSpec((bb, N_KV_HEADS, nq, HEAD_DIM), lambda i: (i, 0, 0, 0)),
        pl.BlockSpec((bb, t_new, D_KV), lambda i: (i, 0, 0)),
        pl.BlockSpec((bb, t_new, D_KV), lambda i: (i, 0, 0)),
        pl.BlockSpec((bb, WINDOW, D_KV), lambda i: (i, 0, 0)),
        pl.BlockSpec((bb, WINDOW, D_KV), lambda i: (i, 0, 0)),
    ]
    out_specs = [
        pl.BlockSpec((bb, N_KV_HEADS, nq, HEAD_DIM), lambda i: (i, 0, 0, 0)),
        pl.BlockSpec((bb, WINDOW, D_KV), lambda i: (i, 0, 0)),
        pl.BlockSpec((bb, WINDOW, D_KV), lambda i: (i, 0, 0)),
    ]
    out_shape = [
        jax.ShapeDtypeStruct((nb, N_KV_HEADS, nq, HEAD_DIM), F32),
        jax.ShapeDtypeStruct((nb, WINDOW, D_KV), F32),
        jax.ShapeDtypeStruct((nb, WINDOW, D_KV), F32),
    ]
    return pl.pallas_call(
        functools.partial(_sample_attn_kernel, t_new=t_new, bb=bb),
        grid=(nb // bb,), in_specs=in_specs, out_specs=out_specs, out_shape=out_shape,
        scratch_shapes=[pltpu.VMEM((WINDOW + 8, D_KV), F32), pltpu.VMEM((WINDOW + 8, D_KV), F32)],
        name="sample_attn",
        compiler_params=pltpu.CompilerParams(dimension_semantics=("arbitrary",)),
    )(sinks, q_grouped, k_new, v_new, cache_k, cache_v)


def _sample_post_kernel(x_ref, part_ref, gate_b_ref, o_ref, w_ba_ref, w_out_ref, ln1_g_ref, ln1_b_ref,
                        h_ref, *, alpha):
    merged = part_ref[...] + gate_b_ref[...] * _dot(o_ref[...].astype(BF16), w_ba_ref[...])
    y = _dot(merged.astype(BF16), w_out_ref[...])
    h_ref[...] = _layer_norm(alpha * x_ref[...] + y, ln1_g_ref[...], ln1_b_ref[...])


def _sample_post(x2d, part, gate_b, o2d, p, alpha):
    return pl.pallas_call(
        functools.partial(_sample_post_kernel, alpha=alpha),
        in_specs=[pl.BlockSpec(memory_space=pltpu.VMEM)] * 8,
        out_specs=pl.BlockSpec(memory_space=pltpu.VMEM),
        out_shape=jax.ShapeDtypeStruct(x2d.shape, F32), name="sample_post",
        compiler_params=pltpu.CompilerParams(vmem_limit_bytes=VMEM_LIMIT_BYTES),
    )(x2d, part, gate_b, o2d, p["w_ba"], p["w_out"], p["ln1_g"], p["ln1_b"])


def _top_values(work, count):
    vals = []
    for i in range(count):
        top = jnp.max(work, axis=0, keepdims=True)
        vals.append(top)
        if i + 1 < count:
            work = jnp.where(work == top, -jnp.inf, work)
    return vals


def _sort_pair(v, i, j):
    v[i], v[j] = jnp.maximum(v[i], v[j]), jnp.minimum(v[i], v[j])


def _bitonic_merge(v):
    j = len(v) // 2
    while j >= 1:
        for i in range(len(v)):
            if i ^ j > i:
                _sort_pair(v, i, i ^ j)
        j //= 2


def _top16_rows(s):
    n = PEER_TOPK
    assert s.shape[0] == 8 * n
    v = [s[k * 8:(k + 1) * 8] for k in range(n)]
    k = 2
    while k <= n:
        j = k // 2
        while j >= 1:
            for i in range(n):
                l = i ^ j
                if l > i:
                    if i & k == 0:
                        _sort_pair(v, i, l)
                    else:
                        _sort_pair(v, l, i)
            j //= 2
        k *= 2
    for shift in (4, 2, 1):
        v = [jnp.maximum(v[i], pltpu.roll(v[n - 1 - i], shift, 0)) for i in range(n)]
        _bitonic_merge(v)
    return [x[0:1] for x in v]


def _candidate_rows(a_rows, b_rows):
    assert PEER_TOPK == 16
    a_all = jnp.concatenate(a_rows, axis=0)
    a_low = a_all[0:8]
    row = lax.broadcasted_iota(jnp.int32, a_low.shape, 0)

    def pair(j):
        return a_low + b_rows[j]

    def corner(j):
        return a_rows[0] + b_rows[j]

    t3 = jnp.where(row >= 5, pltpu.roll(pair(4), 5, 0), pair(2))
    t4 = jnp.where(row >= 6, pltpu.roll(pair(6), 6, 0),
                   jnp.where(row >= 4, pltpu.roll(pair(5), 4, 0), pair(3)))
    t5 = pair(7)
    for k in range(6):
        t5 = jnp.where(row >= 2 + k, corner(8 + k), t5)
    t6 = jnp.where(row >= 2, -jnp.inf, jnp.where(row >= 1, corner(15), corner(14)))
    return jnp.concatenate([a_all + b_rows[0], pair(1), t3, t4, t5, t6], axis=0)


def _peer_select(s1, s2):
    a_rows = _top16_rows(s1)
    b_rows = _top16_rows(s2)
    rank2 = jnp.zeros_like(s2)
    for jj in range(PEER_TOPK):
        rank2 = jnp.where(s2 < b_rows[jj], float(jj + 1), rank2)
    best = _top_values(_candidate_rows(a_rows, b_rows), PEER_TOPK)
    norm = jnp.zeros_like(best[0])
    for i in range(PEER_TOPK):
        norm = norm + jnp.exp(best[i] - best[0])
    count = jnp.zeros_like(s1)
    for jj in range(PEER_TOPK):
        count = jnp.where(s1 + b_rows[jj] >= best[PEER_TOPK - 1], float(jj + 1), count)
    e1 = jnp.exp(s1 - a_rows[0]) / norm
    e2 = jnp.exp(s2 - b_rows[0])
    return rank2, e2, count, e1


def _peer_gate_tile(act_ref, coef_ref, first_n1, ns, t, sel_refs):
    rank2_ref, e2_ref, count_ref, e1_ref = sel_refs
    ls = slice(t * LANES, (t + 1) * LANES)
    half = N_SUBKEYS // 2
    for hh in range(2):
        rr = slice(hh * half, (hh + 1) * half)
        gates = [None for _ in ns]
        for hd in range(PEER_HEADS):
            rank2 = rank2_ref[hd, t, rr, :]
            e2 = e2_ref[hd, t, rr, :]
            for i, n in enumerate(ns):
                count = count_ref[hd, t, pl.ds(first_n1 + n, 1), :]
                e1 = e1_ref[hd, t, pl.ds(first_n1 + n, 1), :]
                term = jnp.where(rank2 < count, e2, 0.0) * e1
                gates[i] = term if gates[i] is None else gates[i] + term
        for i, n in enumerate(ns):
            rs = slice(n * N_SUBKEYS + hh * half, n * N_SUBKEYS + (hh + 1) * half)
            x = act_ref[rs, ls]
            inner = x * (GELU_C0 + (GELU_C0 * GELU_C1) * (x * x))
            coef_ref[rs, ls] = ((gates[i] * x) * (1.0 + jnp.tanh(inner))).astype(BF16)


def _peer_kernel(h_ref, wq_ref, sk_ref, u_ref, vt_ref, ln_g_ref, ln_b_ref, y_ref,
                 ht_ref, qt_ref, rank2_ref, e2_ref, count_ref, e1_ref, acc_ref,
                 act0_ref, act1_ref, coef0_ref, coef1_ref, *, alpha, tb, se):
    j = pl.program_id(1)
    ntb = tb // LANES
    nblocks = N_EXPERTS // se
    sel_refs = (rank2_ref, e2_ref, count_ref, e1_ref)
    act_refs = (act0_ref, act1_ref)
    coef_refs = (coef0_ref, coef1_ref)

    @pl.when(j == 0)
    def _():
        ht = h_ref[...].T.astype(BF16)
        ht_ref[...] = ht
        qt_ref[...] = _dot(wq_ref[...], ht).astype(BF16)

        def per_head(hd, carry):
            for t in range(ntb):
                ls = slice(t * LANES, (t + 1) * LANES)
                s = []
                for c in range(2):
                    r0 = pl.multiple_of((hd * 2 + c) * PEER_HALF, PEER_HALF)
                    s.append(_dot(sk_ref[hd * 2 + c], qt_ref[pl.ds(r0, PEER_HALF), ls]))
                rank2, e2, count, e1 = _peer_select(s[0], s[1])
                rank2_ref[hd, t] = rank2
                e2_ref[hd, t] = e2
                count_ref[hd, t] = count
                e1_ref[hd, t] = 0.5 * e1
            return carry

        lax.fori_loop(0, PEER_HEADS, per_head, 0)
        acc_ref[...] = jnp.zeros_like(acc_ref)

    def step_body(q, with_act, with_gate, with_combine):
        piece_t, piece_e, piece_d = tb // PEER_TOKEN_PIECES, se // PEER_ROW_PIECES, acc_ref.shape[0] // PEER_ROW_PIECES
        pieces = []
        for tp in range(PEER_TOKEN_PIECES):
            hs = slice(tp * piece_t, (tp + 1) * piece_t)
            for lo in range(PEER_ROW_PIECES):
                if with_act:
                    pieces.append(("act", hs, slice(lo * piece_e, (lo + 1) * piece_e)))
                if with_combine:
                    pieces.append(("combine", hs, slice(lo * piece_d, (lo + 1) * piece_d)))
        n_sub = se // N_SUBKEYS
        groups = [tuple(range(g, min(g + PEER_GATE_GROUP, n_sub))) for g in range(0, n_sub, PEER_GATE_GROUP)]
        tiles = [(ns, t) for t in range(ntb) for ns in groups] if with_gate else []
        per_piece = -(-len(tiles) // max(len(pieces), 1))
        for k, (kind, hs, es) in enumerate(pieces):
            for n, t in tiles[k * per_piece:(k + 1) * per_piece]:
                _peer_gate_tile(act_refs[1 - q], coef_refs[1 - q], (j - 1) * (se // N_SUBKEYS), n, t, sel_refs)
            if kind == "act":
                act_refs[q][es, hs] = _dot(u_ref[es, :], ht_ref[:, hs])
            else:
                acc_ref[es, hs] += _dot(vt_ref[es, :], coef_refs[q][:, hs])
        for n, t in tiles[len(pieces) * per_piece:]:
            _peer_gate_tile(act_refs[1 - q], coef_refs[1 - q], (j - 1) * (se // N_SUBKEYS), n, t, sel_refs)

    assert nblocks % 2 == 0 and nblocks >= 4
    pl.when(j == 0)(functools.partial(step_body, 0, True, False, False))
    pl.when(j == 1)(functools.partial(step_body, 1, True, True, False))
    for q in range(2):
        pl.when((j >= 2) & (j < nblocks) & (j % 2 == q))(functools.partial(step_body, q, True, True, True))
    pl.when(j == nblocks)(functools.partial(step_body, 0, False, True, True))
    pl.when(j == nblocks + 1)(functools.partial(step_body, 1, False, False, True))

    @pl.when(j == pl.num_programs(1) - 1)
    def _():
        y_ref[...] = _layer_norm(alpha * h_ref[...] + acc_ref[...].T, ln_g_ref[...], ln_b_ref[...])


def _peer(h2d, p, alpha, tb=PEER_TOKEN_BLOCK):
    n, D = h2d.shape
    tb = min(tb, n)
    se = p["peer_vt"].shape[2]
    assert n % tb == 0 and tb % LANES == 0 and N_EXPERTS % se == 0 and se % N_SUBKEYS == 0
    ntb = tb // LANES
    qdim = PEER_HEADS * PEER_KEY_DIM
    nblocks = N_EXPERTS // se
    in_specs = [
        pl.BlockSpec((tb, D), lambda i, j: (i, 0)),
        _full((qdim, D)),
        _full((PEER_HEADS * 2, N_SUBKEYS, PEER_HALF)),
        pl.BlockSpec((se, D), lambda i, j: (jnp.minimum(j, nblocks - 1), 0)),
        pl.BlockSpec((None, D, se), lambda i, j: (jnp.clip(j - 2, 0, nblocks - 1), 0, 0)),
        _full((1, D)), _full((1, D)),
    ]
    per_head = (PEER_HEADS, ntb, N_SUBKEYS, LANES)
    scratch = [
        pltpu.VMEM((D, tb), BF16),
        pltpu.VMEM((qdim, tb), BF16),
        pltpu.VMEM(per_head, F32), pltpu.VMEM(per_head, F32),
        pltpu.VMEM(per_head, F32), pltpu.VMEM(per_head, F32),
        pltpu.VMEM((D, tb), F32),
    ] + [pltpu.VMEM((se, tb), F32)] * 2 + [pltpu.VMEM((se, tb), BF16)] * 2
    return pl.pallas_call(
        functools.partial(_peer_kernel, alpha=alpha, tb=tb, se=se),
        grid=(n // tb, nblocks + 2), in_specs=in_specs,
        out_specs=pl.BlockSpec((tb, D), lambda i, j: (i, 0), pipeline_mode=pl.Buffered(1)),
        out_shape=jax.ShapeDtypeStruct((n, D), F32), scratch_shapes=scratch, name="peer",
        compiler_params=pltpu.CompilerParams(
            dimension_semantics=("arbitrary", "arbitrary"), vmem_limit_bytes=VMEM_LIMIT_BYTES),
    )(h2d, p["peer_wq_t"], p["peer_sk"], p["peer_u"], p["peer_vt"], p["ln2_g"], p["ln2_b"])


def _layer_params(l, w_in, sinks, sgu_ln_g, sgu_ln_b, sgu_w, sgu_b, w_branch_sgu, w_branch_attn, w_out,
                  ln1_g, ln1_b, peer_w_q, peer_sub_keys, peer_u, peer_v, ln2_g, ln2_b):
    return {
        "w_in": w_in[l].astype(BF16),
        "sinks": sinks[l],
        "sgu_ln_g": sgu_ln_g[l][None, :], "sgu_ln_b": sgu_ln_b[l][None, :],
        "sgu_w": sgu_w[l],
        "sgu_b": sgu_b[l], "sgu_bcol": sgu_b[l].T,
        "w_bs": w_branch_sgu[l].astype(BF16), "w_ba": w_branch_attn[l].astype(BF16),
        "w_out": w_out[l].astype(BF16),
        "ln1_g": ln1_g[l][None, :], "ln1_b": ln1_b[l][None, :],
        "peer_wq_t": peer_w_q[l].T.astype(BF16),
        "peer_sk": peer_sub_keys[l].reshape(PEER_HEADS * 2, N_SUBKEYS, PEER_HALF).astype(BF16),
        "peer_u": peer_u[l].astype(BF16),
        "peer_vt": peer_v[l].reshape(N_EXPERTS // PEER_EXPERT_BLOCK, PEER_EXPERT_BLOCK, D_MODEL)
                            .transpose(0, 2, 1).astype(BF16),
        "ln2_g": ln2_g[l][None, :], "ln2_b": ln2_b[l][None, :],
    }


def _prompt_layer(x, p, alpha):
    B, S, D = x.shape
    h, k_last, v_last, sv_last = _prompt_mixer(x, p, alpha)
    y = _peer(h.reshape(B * S, D), p, alpha).reshape(B, S, D)
    return (y, k_last.reshape(B, WINDOW, N_KV_HEADS, HEAD_DIM), v_last.reshape(B, WINDOW, N_KV_HEADS, HEAD_DIM),
            sv_last.reshape(B, CHUNK, N_SGU_GROUPS, SGU_GROUP_DIM))


def _sample_layer(x, cache_k, cache_v, p, alpha):
    B, T, D = x.shape
    x2d = x.reshape(B * T, D)
    q, k_new, v_new, svn, part, gate_b = _sample_pre(x2d, p, T)
    qg = q.reshape(B, T, N_KV_HEADS, Q_PER_KV, HEAD_DIM).transpose(0, 2, 3, 1, 4)
    qg = qg.reshape(B, N_KV_HEADS, Q_PER_KV * T, HEAD_DIM)
    og, new_k, new_v = _sample_attn(qg, k_new.reshape(B, T, D_KV), v_new.reshape(B, T, D_KV),
                                    cache_k.reshape(B, WINDOW, D_KV), cache_v.reshape(B, WINDOW, D_KV),
                                    p["sinks"], T)
    o2d = og.reshape(B, N_KV_HEADS, Q_PER_KV, T, HEAD_DIM).transpose(0, 3, 1, 2, 4).reshape(B * T, D_ATTN)
    h = _sample_post(x2d, part, gate_b, o2d, p, alpha)
    y = _peer(h, p, alpha).reshape(B, T, D)
    return (y, new_k.reshape(B, WINDOW, N_KV_HEADS, HEAD_DIM), new_v.reshape(B, WINDOW, N_KV_HEADS, HEAD_DIM),
            svn.reshape(B, T, N_SGU_GROUPS, SGU_GROUP_DIM))


def kernel(x_prompt, x_sample, cache_k, cache_v, w_in, sinks, sgu_ln_g, sgu_ln_b, sgu_w, sgu_b, w_branch_sgu, w_branch_attn, w_out, ln1_g, ln1_b, peer_w_q, peer_sub_keys, peer_u, peer_v, ln2_g, ln2_b):
    depth = w_in.shape[0]
    alpha = (2.0 * depth) ** 0.25
    yp, ys = x_prompt, x_sample
    outs = [[] for _ in range(6)]
    for l in range(depth):
        p = _layer_params(l, w_in, sinks, sgu_ln_g, sgu_ln_b, sgu_w, sgu_b, w_branch_sgu, w_branch_attn,
                          w_out, ln1_g, ln1_b, peer_w_q, peer_sub_keys, peer_u, peer_v, ln2_g, ln2_b)
        yp, kp, vp, svp = _prompt_layer(yp, p, alpha)
        ys, ksm, vsm, svs = _sample_layer(ys, cache_k[l], cache_v[l], p, alpha)
        for acc, val in zip(outs, (kp, vp, ksm, vsm, svp, svs)):
            acc.append(val)
    return (yp, ys) + tuple(jnp.stack(o) for o in outs)
```

```python
import functools

import jax
import jax.numpy as jnp
from jax import lax
from jax.experimental import pallas as pl
from jax.experimental.pallas import tpu as pltpu

F32 = jnp.float32
BF16 = jnp.bfloat16

D_MODEL = 1024
N_Q_HEADS = 8
N_KV_HEADS = 2
HEAD_DIM = 64
Q_PER_KV = N_Q_HEADS // N_KV_HEADS
D_ATTN = N_Q_HEADS * HEAD_DIM
D_KV = N_KV_HEADS * HEAD_DIM
WINDOW = 128
CHUNK = 128
D_SGU = D_MODEL // 2
N_SGU_GROUPS = 4
SGU_GROUP_DIM = D_SGU // N_SGU_GROUPS
PEER_HEADS = 8
N_SUBKEYS = 128
N_EXPERTS = N_SUBKEYS * N_SUBKEYS
PEER_TOPK = 16
PEER_KEY_DIM = 256
PEER_HALF = PEER_KEY_DIM // 2
LN_EPS = 1e-5
NEG_INF = -1e30
D_IN = D_ATTN + 2 * D_KV + 2 * D_SGU + 2 * D_MODEL
_OFF_Q = 0
_OFF_K = D_ATTN
_OFF_V = _OFF_K + D_KV
_OFF_U = _OFF_V + D_KV
_OFF_SV = _OFF_U + D_SGU
_OFF_GA = _OFF_SV + D_SGU
_OFF_GB = _OFF_GA + D_MODEL

LANES = 128
VMEM_LIMIT_BYTES = 56 * 1024 * 1024

PROMPT_BLOCK = 512
SAMPLE_ATTN_BATCH = 8
PEER_TOKEN_BLOCK = 512
PEER_EXPERT_BLOCK = 2048
PEER_GATE_GROUP = 1
PEER_TOKEN_PIECES = 2
PEER_ROW_PIECES = 1


GELU_C0 = 0.7978845608028654
GELU_C1 = 0.044715


def _gelu(x):
    return 0.5 * x * (1.0 + jnp.tanh(GELU_C0 * (x + GELU_C1 * (x * x * x))))


def _sigmoid(x):
    return 1.0 / (1.0 + jnp.exp(-x))


def _layer_norm(x, g, b):
    mu = jnp.mean(x, -1, keepdims=True)
    xc = x - mu
    var = jnp.mean(xc * xc, -1, keepdims=True)
    return xc * lax.rsqrt(var + LN_EPS) * g + b


def _dot(a, b):
    return jnp.dot(a, b, preferred_element_type=F32)


def _dot_nt(a, b):
    return lax.dot_general(a, b, (((1,), (1,)), ((), ())), preferred_element_type=F32)


def _sink_softmax_pv(logits, sink_col, vals):
    m = jnp.maximum(jnp.max(logits, -1, keepdims=True), sink_col)
    p = jnp.exp(logits - m)
    denom = jnp.sum(p, -1, keepdims=True) + jnp.exp(sink_col - m)
    return _dot(p.astype(BF16), vals) / denom


def _prompt_mixer_kernel(sinks_ref, x_ref, w_in_ref, sgu_g_ref, sgu_b_ref, sgu_w_ref, sgu_bcol_ref,
                         w_bs_ref, w_ba_ref, w_out_ref, ln1_g_ref, ln1_b_ref,
                         h_ref, k_ref, v_ref, sv_ref,
                         kext_ref, vext_ref, sgu_o_ref, attn_o_ref, *, alpha, ts):
    s = pl.program_id(1)
    last = pl.num_programs(1) - 1
    nblk = ts // WINDOW

    x = x_ref[...]
    z = _dot(x.astype(BF16), w_in_ref[...])
    k = z[:, _OFF_K:_OFF_V]
    v = z[:, _OFF_V:_OFF_U]
    u = _gelu(z[:, _OFF_U:_OFF_SV])
    svn = _layer_norm(_gelu(z[:, _OFF_SV:_OFF_GA]), sgu_g_ref[...], sgu_b_ref[...])

    @pl.when(s == 0)
    def _():
        kext_ref[0:WINDOW, :] = jnp.zeros((WINDOW, D_KV), BF16)
        vext_ref[0:WINDOW, :] = jnp.zeros((WINDOW, D_KV), BF16)

    kext_ref[WINDOW:, :] = k.astype(BF16)
    vext_ref[WINDOW:, :] = v.astype(BF16)

    @pl.when(s == last)
    def _():
        k_ref[...] = k[ts - WINDOW:, :]
        v_ref[...] = v[ts - WINDOW:, :]
        sv_ref[...] = svn[ts - CHUNK:, :]

    row = lax.broadcasted_iota(jnp.int32, (CHUNK, CHUNK), 0)
    col = lax.broadcasted_iota(jnp.int32, (CHUNK, CHUNK), 1)
    for g in range(N_SGU_GROUPS):
        wt = jnp.where(col <= row, sgu_w_ref[g], 0.0).astype(BF16)
        bcol = sgu_bcol_ref[:, g:g + 1]
        gs = slice(g * SGU_GROUP_DIM, (g + 1) * SGU_GROUP_DIM)
        for c in range(ts // CHUNK):
            rs = slice(c * CHUNK, (c + 1) * CHUNK)
            mix = _dot(wt, svn[rs, gs].astype(BF16)) + bcol
            sgu_o_ref[rs, gs] = (u[rs, gs] * mix).astype(BF16)

    nq = Q_PER_KV * WINDOW
    kj = lax.broadcasted_iota(jnp.int32, (2 * WINDOW, nq), 0)
    qi = lax.broadcasted_iota(jnp.int32, (2 * WINDOW, nq), 1) % WINDOW
    rel = qi + WINDOW - kj
    bias = jnp.where((rel >= 0) & (rel < WINDOW), 0.0, NEG_INF).astype(F32)
    bias_first = jnp.where(s > 0, bias, jnp.where(kj < WINDOW, NEG_INF, bias))
    scale = HEAD_DIM ** -0.5
    for i in range(nblk):
        qs = slice(i * WINDOW, (i + 1) * WINDOW)
        outs = []
        for g in range(N_KV_HEADS):
            ds = slice(g * HEAD_DIM, (g + 1) * HEAD_DIM)
            kb = kext_ref[i * WINDOW:(i + 2) * WINDOW, ds]
            vb = vext_ref[i * WINDOW:(i + 2) * WINDOW, ds]
            q4 = jnp.concatenate(
                [z[qs, (g * Q_PER_KV + r) * HEAD_DIM:(g * Q_PER_KV + r + 1) * HEAD_DIM]
                 for r in range(Q_PER_KV)], axis=0).astype(BF16)
            logits = _dot_nt(kb, q4) * scale + (bias_first if i == 0 else bias)
            sink = jnp.concatenate(
                [jnp.full((1, WINDOW), sinks_ref[g * Q_PER_KV + r], F32) for r in range(Q_PER_KV)], axis=1)
            m = jnp.maximum(jnp.max(logits, 0, keepdims=True), sink)
            p = jnp.exp(logits - m)
            denom = jnp.sum(p, 0, keepdims=True) + jnp.exp(sink - m)
            pv = lax.dot_general(vb, p.astype(BF16), (((0,), (0,)), ((), ())), preferred_element_type=F32)
            outs.append(pv / denom)
        o_all = jnp.concatenate(outs, axis=0).T
        heads = [o_all[r * WINDOW:(r + 1) * WINDOW, g * HEAD_DIM:(g + 1) * HEAD_DIM]
                 for g in range(N_KV_HEADS) for r in range(Q_PER_KV)]
        attn_o_ref[qs, :] = jnp.concatenate(heads, axis=1).astype(BF16)

    kext_ref[0:WINDOW, :] = kext_ref[ts:ts + WINDOW, :]
    vext_ref[0:WINDOW, :] = vext_ref[ts:ts + WINDOW, :]

    merged = (_sigmoid(z[:, _OFF_GA:_OFF_GB]) * _dot(sgu_o_ref[...], w_bs_ref[...])
              + _sigmoid(z[:, _OFF_GB:]) * _dot(attn_o_ref[...], w_ba_ref[...]))
    y = _dot(merged.astype(BF16), w_out_ref[...])
    h_ref[...] = _layer_norm(alpha * x + y, ln1_g_ref[...], ln1_b_ref[...])


def _full(shape):
    return pl.BlockSpec(shape, lambda *_: (0,) * len(shape), pipeline_mode=pl.Buffered(1))


def _prompt_mixer(x, p, alpha, ts=PROMPT_BLOCK):
    B, S, D = x.shape
    ts = min(ts, S)
    assert S % ts == 0 and ts % WINDOW == 0 and ts % CHUNK == 0
    grid = (B, S // ts)
    smem = pl.BlockSpec(memory_space=pltpu.SMEM)
    in_specs = [
        smem,
        pl.BlockSpec((None, ts, D), lambda b, s: (b, s, 0)),
        _full((D, D_IN)), _full((1, D_SGU)), _full((1, D_SGU)),
        _full((N_SGU_GROUPS, CHUNK, CHUNK)), _full((CHUNK, N_SGU_GROUPS)),
        _full((D_SGU, D)), _full((D_ATTN, D)), _full((D, D)), _full((1, D)), _full((1, D)),
    ]
    out_specs = [
        pl.BlockSpec((None, ts, D), lambda b, s: (b, s, 0)),
        pl.BlockSpec((None, WINDOW, D_KV), lambda b, s: (b, 0, 0)),
        pl.BlockSpec((None, WINDOW, D_KV), lambda b, s: (b, 0, 0)),
        pl.BlockSpec((None, CHUNK, D_SGU), lambda b, s: (b, 0, 0)),
    ]
    out_shape = [
        jax.ShapeDtypeStruct((B, S, D), F32),
        jax.ShapeDtypeStruct((B, WINDOW, D_KV), F32),
        jax.ShapeDtypeStruct((B, WINDOW, D_KV), F32),
        jax.ShapeDtypeStruct((B, CHUNK, D_SGU), F32),
    ]
    scratch = [
        pltpu.VMEM((ts + WINDOW, D_KV), BF16),
        pltpu.VMEM((ts + WINDOW, D_KV), BF16),
        pltpu.VMEM((ts, D_SGU), BF16),
        pltpu.VMEM((ts, D_ATTN), BF16),
    ]
    return pl.pallas_call(
        functools.partial(_prompt_mixer_kernel, alpha=alpha, ts=ts),
        grid=grid, in_specs=in_specs, out_specs=out_specs, out_shape=out_shape,
        scratch_shapes=scratch, name="prompt_mixer",
        compiler_params=pltpu.CompilerParams(
            dimension_semantics=("arbitrary", "arbitrary"), vmem_limit_bytes=VMEM_LIMIT_BYTES),
    )(p["sinks"], x, p["w_in"], p["sgu_ln_g"], p["sgu_ln_b"], p["sgu_w"], p["sgu_bcol"],
      p["w_bs"], p["w_ba"], p["w_out"], p["ln1_g"], p["ln1_b"])


def _sample_pre_kernel(sgu_w_ref, sgu_b_ref, x_ref, w_in_ref, sgu_g_ref, sgu_bb_ref, w_bs_ref,
                       q_ref, k_ref, v_ref, sv_ref, part_ref, gate_b_ref, *, t_new):
    x = x_ref[...]
    n = x.shape[0]
    z = _dot(x.astype(BF16), w_in_ref[...])
    q_ref[...] = z[:, _OFF_Q:_OFF_K]
    k_ref[...] = z[:, _OFF_K:_OFF_V]
    v_ref[...] = z[:, _OFF_V:_OFF_U]
    u = _gelu(z[:, _OFF_U:_OFF_SV])
    svn = _layer_norm(_gelu(z[:, _OFF_SV:_OFF_GA]), sgu_g_ref[...], sgu_bb_ref[...])
    sv_ref[...] = svn
    t_of_row = lax.broadcasted_iota(jnp.int32, (n, SGU_GROUP_DIM), 0) % t_new
    pieces = []
    for g in range(N_SGU_GROUPS):
        vg = svn[:, g * SGU_GROUP_DIM:(g + 1) * SGU_GROUP_DIM]
        mix = jnp.zeros((n, SGU_GROUP_DIM), F32)
        for t in range(t_new):
            mix = jnp.where(t_of_row == t, sgu_b_ref[g, t], mix)
        for d in range(t_new):
            coef = jnp.zeros((n, SGU_GROUP_DIM), F32)
            for t in range(d, t_new):
                coef = jnp.where(t_of_row == t, sgu_w_ref[g, t * t_new + t - d], coef)
            shifted = vg if d == 0 else pltpu.roll(vg, d, 0)
            mix = mix + coef * shifted
        pieces.append(u[:, g * SGU_GROUP_DIM:(g + 1) * SGU_GROUP_DIM] * mix)
    sgu_o = jnp.concatenate(pieces, axis=1).astype(BF16)
    part_ref[...] = _sigmoid(z[:, _OFF_GA:_OFF_GB]) * _dot(sgu_o, w_bs_ref[...])
    gate_b_ref[...] = _sigmoid(z[:, _OFF_GB:])


def _sample_pre(x2d, p, t_new):
    n, D = x2d.shape
    smem = pl.BlockSpec(memory_space=pltpu.SMEM)
    out_shape = [
        jax.ShapeDtypeStruct((n, D_ATTN), F32),
        jax.ShapeDtypeStruct((n, D_KV), F32),
        jax.ShapeDtypeStruct((n, D_KV), F32),
        jax.ShapeDtypeStruct((n, D_SGU), F32),
        jax.ShapeDtypeStruct((n, D), F32),
        jax.ShapeDtypeStruct((n, D), F32),
    ]
    return pl.pallas_call(
        functools.partial(_sample_pre_kernel, t_new=t_new),
        in_specs=[smem, smem] + [pl.BlockSpec(memory_space=pltpu.VMEM)] * 5,
        out_specs=[pl.BlockSpec(memory_space=pltpu.VMEM)] * 6,
        out_shape=out_shape, name="sample_pre",
        compiler_params=pltpu.CompilerParams(vmem_limit_bytes=VMEM_LIMIT_BYTES),
    )(p["sgu_w"][:, :t_new, :t_new].reshape(N_SGU_GROUPS, t_new * t_new), p["sgu_b"],
      x2d, p["w_in"], p["sgu_ln_g"], p["sgu_ln_b"], p["w_bs"])


def _sample_attn_kernel(sinks_ref, q_ref, kn_ref, vn_ref, ck_ref, cv_ref,
                        o_ref, nk_ref, nv_ref, kbuf_ref, vbuf_ref, *, t_new, bb):
    nq = Q_PER_KV * t_new
    nkeys = WINDOW + 8
    row = lax.broadcasted_iota(jnp.int32, (nq, nkeys), 0)
    kj = lax.broadcasted_iota(jnp.int32, (nq, nkeys), 1)
    t_q = row % t_new
    bias = jnp.where((kj > t_q) & (kj <= t_q + WINDOW), 0.0, NEG_INF).astype(F32)
    r_of_row = lax.broadcasted_iota(jnp.int32, (nq, 1), 0) // t_new
    scale = HEAD_DIM ** -0.5
    pad = jnp.zeros((8 - t_new, D_KV), F32)
    for b in range(bb):
        kbuf_ref[0:WINDOW, :] = ck_ref[b]
        kbuf_ref[WINDOW:WINDOW + t_new, :] = kn_ref[b]
        kbuf_ref[WINDOW + t_new:, :] = pad
        vbuf_ref[0:WINDOW, :] = cv_ref[b]
        vbuf_ref[WINDOW:WINDOW + t_new, :] = vn_ref[b]
        vbuf_ref[WINDOW + t_new:, :] = pad
        nk_ref[b] = kbuf_ref[t_new:t_new + WINDOW, :]
        nv_ref[b] = vbuf_ref[t_new:t_new + WINDOW, :]
        for g in range(N_KV_HEADS):
            ds = slice(g * HEAD_DIM, (g + 1) * HEAD_DIM)
            kg = kbuf_ref[:, ds].astype(BF16)
            vg = vbuf_ref[:, ds].astype(BF16)
            logits = _dot_nt(q_ref[b, g].astype(BF16), kg) * scale + bias
            sink_col = jnp.zeros((nq, 1), F32)
            for r in range(Q_PER_KV):
                sink_col = jnp.where(r_of_row == r, sinks_ref[g * Q_PER_KV + r], sink_col)
            o_ref[b, g] = _sink_softmax_pv(logits, sink_col, vg)


def _sample_attn(q_grouped, k_new, v_new, cache_k, cache_v, sinks, t_new, bb=SAMPLE_ATTN_BATCH):
    nb = q_grouped.shape[0]
    bb = min(bb, nb)
    assert nb % bb == 0 and t_new <= 8
    nq = Q_PER_KV * t_new
    smem = pl.BlockSpec(memory_space=pltpu.SMEM)
    in_specs = [
        smem,
        pl.BlockSpec((bb, N_KV_HEADS, nq, HEAD_DIM), lambda i: (i, 0, 0, 0)),
        pl.BlockSpec((bb, t_new, D_KV), lambda i: (i, 0, 0)),
        pl.BlockSpec((bb, t_new, D_KV), lambda i: (i, 0, 0)),
        pl.BlockSpec((bb, WINDOW, D_KV), lambda i: (i, 0, 0)),
        pl.BlockSpec((bb, WINDOW, D_KV), lambda i: (i, 0, 0)),
    ]
    out_specs = [
        pl.BlockSpec((bb, N_KV_HEADS, nq, HEAD_DIM), lambda i: (i, 0, 0, 0)),
        pl.BlockSpec((bb, WINDOW, D_KV), lambda i: (i, 0, 0)),
        pl.BlockSpec((bb, WINDOW, D_KV), lambda i: (i, 0, 0)),
    ]
    out_shape = [
        jax.ShapeDtypeStruct((nb, N_KV_HEADS, nq, HEAD_DIM), F32),
        jax.ShapeDtypeStruct((nb, WINDOW, D_KV), F32),
        jax.ShapeDtypeStruct((nb, WINDOW, D_KV), F32),
    ]
    return pl.pallas_call(
        functools.partial(_sample_attn_kernel, t_new=t_new, bb=bb),
        grid=(nb // bb,), in_specs=in_specs, out_specs=out_specs, out_shape=out_shape,
        scratch_shapes=[pltpu.VMEM((WINDOW + 8, D_KV), F32), pltpu.VMEM((WINDOW + 8, D_KV), F32)],
        name="sample_attn",
        compiler_params=pltpu.CompilerParams(dimension_semantics=("arbitrary",)),
    )(sinks, q_grouped, k_new, v_new, cache_k, cache_v)


def _sample_post_kernel(x_ref, part_ref, gate_b_ref, o_ref, w_ba_ref, w_out_ref, ln1_g_ref, ln1_b_ref,
                        h_ref, *, alpha):
    merged = part_ref[...] + gate_b_ref[...] * _dot(o_ref[...].astype(BF16), w_ba_ref[...])
    y = _dot(merged.astype(BF16), w_out_ref[...])
    h_ref[...] = _layer_norm(alpha * x_ref[...] + y, ln1_g_ref[...], ln1_b_ref[...])


def _sample_post(x2d, part, gate_b, o2d, p, alpha):
    return pl.pallas_call(
        functools.partial(_sample_post_kernel, alpha=alpha),
        in_specs=[pl.BlockSpec(memory_space=pltpu.VMEM)] * 8,
        out_specs=pl.BlockSpec(memory_space=pltpu.VMEM),
        out_shape=jax.ShapeDtypeStruct(x2d.shape, F32), name="sample_post",
        compiler_params=pltpu.CompilerParams(vmem_limit_bytes=VMEM_LIMIT_BYTES),
    )(x2d, part, gate_b, o2d, p["w_ba"], p["w_out"], p["ln1_g"], p["ln1_b"])


def _top_values(work, count):
    vals = []
    for i in range(count):
        top = jnp.max(work, axis=0, keepdims=True)
        vals.append(top)
        if i + 1 < count:
            work = jnp.where(work == top, -jnp.inf, work)
    return vals


def _sort_pair(v, i, j):
    v[i], v[j] = jnp.maximum(v[i], v[j]), jnp.minimum(v[i], v[j])


def _bitonic_merge(v):
    j = len(v) // 2
    while j >= 1:
        for i in range(len(v)):
            if i ^ j > i:
                _sort_pair(v, i, i ^ j)
        j //= 2


def _top16_rows(s):
    n = PEER_TOPK
    assert s.shape[0] == 8 * n
    v = [s[k * 8:(k + 1) * 8] for k in range(n)]
    k = 2
    while k <= n:
        j = k // 2
        while j >= 1:
            for i in range(n):
                l = i ^ j
                if l > i:
                    if i & k == 0:
                        _sort_pair(v, i, l)
                    else:
                        _sort_pair(v, l, i)
            j //= 2
        k *= 2
    for shift in (4, 2, 1):
        v = [jnp.maximum(v[i], pltpu.roll(v[n - 1 - i], shift, 0)) for i in range(n)]
        _bitonic_merge(v)
    return [x[0:1] for x in v]


def _candidate_rows(a_rows, b_rows):
    assert PEER_TOPK == 16
    a_all = jnp.concatenate(a_rows, axis=0)
    a_low = a_all[0:8]
    row = lax.broadcasted_iota(jnp.int32, a_low.shape, 0)

    def pair(j):
        return a_low + b_rows[j]

    def corner(j):
        return a_rows[0] + b_rows[j]

    t3 = jnp.where(row >= 5, pltpu.roll(pair(4), 5, 0), pair(2))
    t4 = jnp.where(row >= 6, pltpu.roll(pair(6), 6, 0),
                   jnp.where(row >= 4, pltpu.roll(pair(5), 4, 0), pair(3)))
    t5 = pair(7)
    for k in range(6):
        t5 = jnp.where(row >= 2 + k, corner(8 + k), t5)
    t6 = jnp.where(row >= 2, -jnp.inf, jnp.where(row >= 1, corner(15), corner(14)))
    return jnp.concatenate([a_all + b_rows[0], pair(1), t3, t4, t5, t6], axis=0)


def _peer_select(s1, s2):
    a_rows = _top16_rows(s1)
    b_rows = _top16_rows(s2)
    rank2 = jnp.zeros_like(s2)
    for jj in range(PEER_TOPK):
        rank2 = jnp.where(s2 < b_rows[jj], float(jj + 1), rank2)
    best = _top_values(_candidate_rows(a_rows, b_rows), PEER_TOPK)
    norm = jnp.zeros_like(best[0])
    for i in range(PEER_TOPK):
        norm = norm + jnp.exp(best[i] - best[0])
    count = jnp.zeros_like(s1)
    for jj in range(PEER_TOPK):
        count = jnp.where(s1 + b_rows[jj] >= best[PEER_TOPK - 1], float(jj + 1), count)
    e1 = jnp.exp(s1 - a_rows[0]) / norm
    e2 = jnp.exp(s2 - b_rows[0])
    return rank2, e2, count, e1


def _peer_gate_tile(act_ref, coef_ref, first_n1, ns, t, sel_refs):
    rank2_ref, e2_ref, count_ref, e1_ref = sel_refs
    ls = slice(t * LANES, (t + 1) * LANES)
    gates = [None for _ in ns]
    for hd in range(PEER_HEADS):
        rank2 = rank2_ref[hd, t]
        e2 = e2_ref[hd, t]
        for i, n in enumerate(ns):
            count = count_ref[hd, t, pl.ds(first_n1 + n, 1), :]
            e1 = e1_ref[hd, t, pl.ds(first_n1 + n, 1), :]
            term = jnp.where(rank2 < count, e2, 0.0) * e1
            gates[i] = term if gates[i] is None else gates[i] + term
    for i, n in enumerate(ns):
        rs = slice(n * N_SUBKEYS, (n + 1) * N_SUBKEYS)
        x = act_ref[rs, ls]
        inner = x * (GELU_C0 + (GELU_C0 * GELU_C1) * (x * x))
        coef_ref[rs, ls] = ((gates[i] * x) * (1.0 + jnp.tanh(inner))).astype(BF16)


def _peer_kernel(h_ref, wq_ref, sk_ref, u_ref, vt_ref, ln_g_ref, ln_b_ref, y_ref,
                 ht_ref, qt_ref, rank2_ref, e2_ref, count_ref, e1_ref, acc_ref,
                 act0_ref, coef0_ref, *, alpha, tb, se):
    j = pl.program_id(1)
    ntb = tb // LANES
    sel_refs = (rank2_ref, e2_ref, count_ref, e1_ref)

    @pl.when(j == 0)
    def _():
        ht = h_ref[...].T.astype(BF16)
        ht_ref[...] = ht
        qt_ref[...] = _dot(wq_ref[...], ht).astype(BF16)

        def per_head(hd, carry):
            for t in range(ntb):
                ls = slice(t * LANES, (t + 1) * LANES)
                s = []
                for c in range(2):
                    r0 = pl.multiple_of((hd * 2 + c) * PEER_HALF, PEER_HALF)
                    s.append(_dot(sk_ref[hd * 2 + c], qt_ref[pl.ds(r0, PEER_HALF), ls]))
                rank2, e2, count, e1 = _peer_select(s[0], s[1])
                rank2_ref[hd, t] = rank2
                e2_ref[hd, t] = e2
                count_ref[hd, t] = count
                e1_ref[hd, t] = 0.5 * e1
            return carry

        lax.fori_loop(0, PEER_HEADS, per_head, 0)
        acc_ref[...] = jnp.zeros_like(acc_ref)

    token_pieces = PEER_TOKEN_PIECES if ntb % PEER_TOKEN_PIECES == 0 else 1
    piece_t = tb // token_pieces
    n_sub = se // N_SUBKEYS
    for tp in range(token_pieces):
        hs = slice(tp * piece_t, (tp + 1) * piece_t)
        act0_ref[:, hs] = _dot(u_ref[...], ht_ref[:, hs])
        for t in range(tp * piece_t // LANES, (tp + 1) * piece_t // LANES):
            for n in range(n_sub):
                _peer_gate_tile(act0_ref, coef0_ref, j * n_sub, (n,), t, sel_refs)
        acc_ref[:, hs] += _dot(vt_ref[...], coef0_ref[:, hs])

    @pl.when(j == pl.num_programs(1) - 1)
    def _():
        y_ref[...] = _layer_norm(alpha * h_ref[...] + acc_ref[...].T, ln_g_ref[...], ln_b_ref[...])


def _peer(h2d, p, alpha, tb=PEER_TOKEN_BLOCK):
    n, D = h2d.shape
    tb = min(tb, n)
    se = p["peer_vt"].shape[2]
    assert n % tb == 0 and tb % LANES == 0 and N_EXPERTS % se == 0 and se % N_SUBKEYS == 0
    ntb = tb // LANES
    qdim = PEER_HEADS * PEER_KEY_DIM
    nblocks = N_EXPERTS // se
    in_specs = [
        pl.BlockSpec((tb, D), lambda i, j: (i, 0)),
        _full((qdim, D)),
        _full((PEER_HEADS * 2, N_SUBKEYS, PEER_HALF)),
        pl.BlockSpec((se, D), lambda i, j: (j, 0)),
        pl.BlockSpec((None, D, se), lambda i, j: (j, 0, 0)),
        _full((1, D)), _full((1, D)),
    ]
    per_head = (PEER_HEADS, ntb, N_SUBKEYS, LANES)
    scratch = [
        pltpu.VMEM((D, tb), BF16),
        pltpu.VMEM((qdim, tb), BF16),
        pltpu.VMEM(per_head, F32), pltpu.VMEM(per_head, F32),
        pltpu.VMEM(per_head, F32), pltpu.VMEM(per_head, F32),
        pltpu.VMEM((D, tb), F32),
        pltpu.VMEM((se, tb), F32), pltpu.VMEM((se, tb), BF16),
    ]
    return pl.pallas_call(
        functools.partial(_peer_kernel, alpha=alpha, tb=tb, se=se),
        grid=(n // tb, nblocks), in_specs=in_specs,
        out_specs=pl.BlockSpec((tb, D), lambda i, j: (i, 0), pipeline_mode=pl.Buffered(1)),
        out_shape=jax.ShapeDtypeStruct((n, D), F32), scratch_shapes=scratch, name="peer",
        compiler_params=pltpu.CompilerParams(
            dimension_semantics=("arbitrary", "arbitrary"), vmem_limit_bytes=VMEM_LIMIT_BYTES),
    )(h2d, p["peer_wq_t"], p["peer_sk"], p["peer_u"], p["peer_vt"], p["ln2_g"], p["ln2_b"])


def _layer_params(l, w_in, sinks, sgu_ln_g, sgu_ln_b, sgu_w, sgu_b, w_branch_sgu, w_branch_attn, w_out,
                  ln1_g, ln1_b, peer_w_q, peer_sub_keys, peer_u, peer_v, ln2_g, ln2_b):
    return {
        "w_in": w_in[l].astype(BF16),
        "sinks": sinks[l],
        "sgu_ln_g": sgu_ln_g[l][None, :], "sgu_ln_b": sgu_ln_b[l][None, :],
        "sgu_w": sgu_w[l],
        "sgu_b": sgu_b[l], "sgu_bcol": sgu_b[l].T,
        "w_bs": w_branch_sgu[l].astype(BF16), "w_ba": w_branch_attn[l].astype(BF16),
        "w_out": w_out[l].astype(BF16),
        "ln1_g": ln1_g[l][None, :], "ln1_b": ln1_b[l][None, :],
        "peer_wq_t": peer_w_q[l].T.astype(BF16),
        "peer_sk": peer_sub_keys[l].reshape(PEER_HEADS * 2, N_SUBKEYS, PEER_HALF).astype(BF16),
        "peer_u": peer_u[l].astype(BF16),
        "peer_vt": peer_v[l].reshape(N_EXPERTS // PEER_EXPERT_BLOCK, PEER_EXPERT_BLOCK, D_MODEL)
                            .transpose(0, 2, 1).astype(BF16),
        "ln2_g": ln2_g[l][None, :], "ln2_b": ln2_b[l][None, :],
    }


def _prompt_layer(x, p, alpha):
    B, S, D = x.shape
    h, k_last, v_last, sv_last = _prompt_mixer(x, p, alpha)
    y = _peer(h.reshape(B * S, D), p, alpha).reshape(B, S, D)
    return (y, k_last.reshape(B, WINDOW, N_KV_HEADS, HEAD_DIM), v_last.reshape(B, WINDOW, N_KV_HEADS, HEAD_DIM),
            sv_last.reshape(B, CHUNK, N_SGU_GROUPS, SGU_GROUP_DIM))


def _sample_layer(x, cache_k, cache_v, p, alpha):
    B, T, D = x.shape
    x2d = x.reshape(B * T, D)
    q, k_new, v_new, svn, part, gate_b = _sample_pre(x2d, p, T)
    qg = q.reshape(B, T, N_KV_HEADS, Q_PER_KV, HEAD_DIM).transpose(0, 2, 3, 1, 4)
    qg = qg.reshape(B, N_KV_HEADS, Q_PER_KV * T, HEAD_DIM)
    og, new_k, new_v = _sample_attn(qg, k_new.reshape(B, T, D_KV), v_new.reshape(B, T, D_KV),
                                    cache_k.reshape(B, WINDOW, D_KV), cache_v.reshape(B, WINDOW, D_KV),
                                    p["sinks"], T)
    o2d = og.reshape(B, N_KV_HEADS, Q_PER_KV, T, HEAD_DIM).transpose(0, 3, 1, 2, 4).reshape(B * T, D_ATTN)
    h = _sample_post(x2d, part, gate_b, o2d, p, alpha)
    y = _peer(h, p, alpha).reshape(B, T, D)
    return (y, new_k.reshape(B, WINDOW, N_KV_HEADS, HEAD_DIM), new_v.reshape(B, WINDOW, N_KV_HEADS, HEAD_DIM),
            svn.reshape(B, T, N_SGU_GROUPS, SGU_GROUP_DIM))


def kernel(x_prompt, x_sample, cache_k, cache_v, w_in, sinks, sgu_ln_g, sgu_ln_b, sgu_w, sgu_b, w_branch_sgu, w_branch_attn, w_out, ln1_g, ln1_b, peer_w_q, peer_sub_keys, peer_u, peer_v, ln2_g, ln2_b):
    depth = w_in.shape[0]
    alpha = (2.0 * depth) ** 0.25
    yp, ys = x_prompt, x_sample
    outs = [[] for _ in range(6)]
    for l in range(depth):
        p = _layer_params(l, w_in, sinks, sgu_ln_g, sgu_ln_b, sgu_w, sgu_b, w_branch_sgu, w_branch_attn,
                          w_out, ln1_g, ln1_b, peer_w_q, peer_sub_keys, peer_u, peer_v, ln2_g, ln2_b)
        yp, kp, vp, svp = _prompt_layer(yp, p, alpha)
        ys, ksm, vsm, svs = _sample_layer(ys, cache_k[l], cache_v[l], p, alpha)
        for acc, val in zip(outs, (kp, vp, ksm, vsm, svp, svs)):
            acc.append(val)
    return (yp, ys) + tuple(jnp.stack(o) for o in outs)
```
